```python
import math
import jax, jax.numpy as jnp
from jax import lax
import numpy as np

D_MODEL = 1024
BATCH = 8
SEQ = 2048
DEPTH = 1
DEC_BATCH = 32
DEC_SEQ = 1
PAST_LEN = 16384
PAGE_SIZE = 128

RET_HEADS = 8
RET_DK = 64
RET_DV = 128
RET_CHUNK = 128
ROPE_BASE = 10000.0
ATT_WINDOWS = (128, 512, 2048)
ATT_DILATIONS = (1, 4, 16)
ATT_GROUPS = 3
ATT_HEADS_PER_GROUP = 4
ATT_HEADS = ATT_GROUPS * ATT_HEADS_PER_GROUP
ATT_HD = 128
REL_BUCKETS = 32
REL_MAX_DIST = 2048
D_FF = 4 * D_MODEL
D_PLE = 256
NORM_EPS = 1e-6

RET_QK_W = RET_HEADS * RET_DK
RET_V_W = RET_HEADS * RET_DV
ATT_W = ATT_HEADS * ATT_HD
ATT_OUT_W = ATT_HEADS_PER_GROUP * ATT_HD
IN_SPLITS = (RET_QK_W, RET_QK_W, RET_V_W, RET_V_W, ATT_W, ATT_W, ATT_W, D_MODEL, D_MODEL)
N_IN = sum(IN_SPLITS)

kernel_name = "retention_dilated_swa_hybrid_step"


def rms_norm(x, g):
    xf = x.astype(jnp.float32)
    y = xf * lax.rsqrt(jnp.mean(xf * xf, axis=-1, keepdims=True) + NORM_EPS)
    return (y * g.astype(jnp.float32)).astype(x.dtype)


def rotary(x, pos):
    half = x.shape[-1] // 2
    inv = ROPE_BASE ** (-jnp.arange(half, dtype=jnp.float32) / half)
    ang = pos.astype(jnp.float32)[:, None] * inv[None, :]
    cos = jnp.cos(ang)[None, :, None, :]
    sin = jnp.sin(ang)[None, :, None, :]
    x1, x2 = x[..., :half], x[..., half:]
    return jnp.concatenate([x1 * cos - x2 * sin, x1 * sin + x2 * cos], axis=-1)


def ret_log_decay():
    return jnp.log1p(-jnp.exp2(-5.0 - jnp.arange(RET_HEADS, dtype=jnp.float32)))


def retention_chunk(state, q, k, v, log_gamma):
    C = q.shape[1]
    i = jnp.arange(C, dtype=jnp.float32)
    diff = i[:, None] - i[None, :]
    decay = jnp.where(diff[None] >= 0,
                      jnp.exp(jnp.maximum(diff, 0.0)[None] * log_gamma[:, None, None]), 0.0)
    scores = jnp.einsum('bihk,bjhk->bhij', q, k) * decay[None]
    o_in = jnp.einsum('bhij,bjhv->bihv', scores, v)
    q_decay = jnp.exp((i + 1.0)[:, None] * log_gamma[None, :])
    o_cross = jnp.einsum('bihk,bhkv->bihv', q, state) * q_decay[None, :, :, None]
    k_decay = jnp.exp((C - 1.0 - i)[:, None] * log_gamma[None, :])
    new_state = (state * jnp.exp(C * log_gamma)[None, :, None, None]
                 + jnp.einsum('bjhk,bjhv->bhkv', k * k_decay[None, :, :, None], v))
    return new_state, o_in + o_cross


def retention_scan(q, k, v, state0, log_gamma):
    B, S, H, _ = q.shape
    nc = S // RET_CHUNK
    xs = tuple(a.reshape(B, nc, RET_CHUNK, H, a.shape[-1]).swapaxes(0, 1) for a in (q, k, v))

    def step(st, chunk):
        qc, kc, vc = chunk
        return retention_chunk(st, qc, kc, vc, log_gamma)

    st, o = lax.scan(step, state0, xs)
    return o.swapaxes(0, 1).reshape(B, S, H, v.shape[-1]), st


def retention_readout(o, g, gn_g):
    B, T = o.shape[0], o.shape[1]
    mu = jnp.mean(o, axis=-1, keepdims=True)
    var = jnp.mean(jnp.square(o - mu), axis=-1, keepdims=True)
    on = ((o - mu) * lax.rsqrt(var + NORM_EPS)).reshape(B, T, RET_V_W) * gn_g.astype(jnp.float32)
    return jax.nn.silu(g.astype(jnp.float32)) * on


def rel_bucket(dist):
    max_exact = REL_BUCKETS // 2
    d = dist.astype(jnp.int32)
    log_ratio = jnp.log(jnp.maximum(d, 1).astype(jnp.float32) / max_exact) / math.log(REL_MAX_DIST / max_exact)
    large = max_exact + (log_ratio * (REL_BUCKETS - max_exact)).astype(jnp.int32)
    large = jnp.minimum(large, REL_BUCKETS - 1)
    return jnp.where(d < max_exact, d, large)


def group_bias(rel_bias, g):
    dil = ATT_DILATIONS[g]
    nk = ATT_WINDOWS[g] // dil
    b = rel_bucket(jnp.arange(nk) * dil)
    tab = rel_bias[b][:, g * ATT_HEADS_PER_GROUP:(g + 1) * ATT_HEADS_PER_GROUP]
    return tab.T.astype(jnp.float32)


def dilated_attn_prompt(q, k, v, dil, nk, bias):
    B, S, H, E = q.shape
    L = S // dil
    nb = -(-L // nk)
    Lp = nb * nk

    def to_stream(a):
        a = a.reshape(B, L, dil, H, E).transpose(0, 2, 1, 3, 4)
        a = jnp.pad(a, ((0, 0), (0, 0), (0, Lp - L), (0, 0), (0, 0)))
        return a.reshape(B, dil, nb, nk, H, E)

    def with_prev(a):
        prev = jnp.pad(a, ((0, 0), (0, 0), (1, 0), (0, 0), (0, 0), (0, 0)))[:, :, :nb]
        return jnp.concatenate([prev, a], axis=3)

    qs = to_stream(q)
    kk = with_prev(to_stream(k))
    vv = with_prev(to_stream(v))
    i = jnp.arange(nk)[:, None]
    c = jnp.arange(2 * nk)[None, :]
    jd = i + nk - c
    rel_ok = (jd >= 0) & (jd < nk)
    blk = jnp.arange(nb)[:, None, None]
    ok = rel_ok[None] & ((blk > 0) | (c[None] >= nk))
    b_full = bias[:, jnp.clip(jd, 0, nk - 1)]
    s = jnp.einsum('bdnihe,bdnche->bdnhic', qs, kk) * (E ** -0.5) + b_full
    s = jnp.where(ok[:, None], s, -jnp.inf)
    m = jnp.max(s, axis=-1, keepdims=True)
    p = jnp.exp(s - m)
    den = jnp.sum(p, axis=-1, keepdims=True)
    o = jnp.einsum('bdnhic,bdnche->bdnihe', p, vv) / jnp.swapaxes(den, 3, 4)
    lse = jnp.swapaxes((m + jnp.log(den))[..., 0], 3, 4)

    def from_stream(a):
        a = a.reshape((B, dil, Lp) + a.shape[4:])[:, :, :L]
        a = jnp.swapaxes(a, 1, 2)
        return a.reshape((B, S) + a.shape[3:])

    return from_stream(o), from_stream(lse)


def dilated_attn_sample(q, k_new, v_new, k_buf, v_buf, dil, nk, bias, window):
    T = q.shape[1]
    Lb = k_buf.shape[1]
    kf = jnp.concatenate([k_buf.astype(jnp.float32), k_new], axis=1)
    vf = jnp.concatenate([v_buf.astype(jnp.float32), v_new], axis=1)
    idx = Lb + jnp.arange(T)[:, None] - jnp.arange(nk)[None, :] * dil
    ok = idx >= 0
    kg = jnp.take(kf, jnp.maximum(idx, 0), axis=1)
    vg = jnp.take(vf, jnp.maximum(idx, 0), axis=1)
    s = jnp.einsum('bthe,btjhe->bthj', q, kg) * (q.shape[-1] ** -0.5) + bias.T[None, None] .T.T if False else jnp.einsum('bthe,btjhe->bthj', q, kg) * (q.shape[-1] ** -0.5) + bias[None, None]
    s = jnp.where(ok[None, :, None, :], s, -jnp.inf)
    m = jnp.max(s, axis=-1, keepdims=True)
    p = jnp.exp(s - m)
    den = jnp.sum(p, axis=-1, keepdims=True)
    o = jnp.einsum('bthj,btjhe->bthe', p, vg) / den
    lse = (m + jnp.log(den))[..., 0]
    n_keep = min(window, Lb + T)
    return o, lse, kf[:, kf.shape[1] - n_keep:], vf[:, vf.shape[1] - n_keep:]


def decoder_layer(x, ple, pos, ret_state, kv_cache, lw, rel_bias):
    ln1, w_in_l, gn_g, w_rb, w_ab, w_o, ln2, w_u, w_d, w_pl, w_pg = lw
    f32 = jnp.float32
    B, T, _ = x.shape
    h = rms_norm(x, ln1)
    split_idx = [int(s) for s in np.cumsum(IN_SPLITS)[:-1]]
    rq, rk, rv, rg, aq, ak, av, gr, ga = jnp.split(h @ w_in_l, split_idx, axis=-1)
    rq = rotary(rq.reshape(B, T, RET_HEADS, RET_DK).astype(f32), pos)
    rk = rotary(rk.reshape(B, T, RET_HEADS, RET_DK).astype(f32), pos) * (RET_DK ** -0.5)
    rv = rv.reshape(B, T, RET_HEADS, RET_DV).astype(f32)
    lg = ret_log_decay()
    if ret_state is None:
        ro, new_ret = retention_scan(rq, rk, rv, jnp.zeros((B, RET_HEADS, RET_DK, RET_DV), f32), lg)
    else:
        new_ret, ro = retention_chunk(ret_state.astype(f32), rq, rk, rv, lg)
    ret_out = retention_readout(ro, rg, gn_g)
    aq = aq.reshape(B, T, ATT_GROUPS, ATT_HEADS_PER_GROUP, ATT_HD).astype(f32)
    ak = ak.reshape(B, T, ATT_GROUPS, ATT_HEADS_PER_GROUP, ATT_HD).astype(f32)
    av = av.reshape(B, T, ATT_GROUPS, ATT_HEADS_PER_GROUP, ATT_HD).astype(f32)
    outs, lses, new_kv = [], [], []
    for g in range(ATT_GROUPS):
        bias = group_bias(rel_bias, g)
        dil = ATT_DILATIONS[g]
        nk = ATT_WINDOWS[g] // dil
        if kv_cache is None:
            o, lse = dilated_attn_prompt(aq[:, :, g], ak[:, :, g], av[:, :, g], dil, nk, bias)
            keep = min(ATT_WINDOWS[g], T)
            kb, vb = ak[:, T - keep:, g], av[:, T - keep:, g]
        else:
            o, lse, kb, vb = dilated_attn_sample(aq[:, :, g], ak[:, :, g], av[:, :, g],
                                                 kv_cache[g][0], kv_cache[g][1], dil, nk, bias, ATT_WINDOWS[g])
        outs.append(o)
        lses.append(lse)
        new_kv.append((kb.astype(x.dtype), vb.astype(x.dtype)))
    alpha = jax.nn.softmax(jnp.stack(lses, axis=0), axis=0)
    att_out = jnp.sum(alpha[..., None] * jnp.stack(outs, axis=0), axis=0).reshape(B, T, ATT_OUT_W)
    mixed = (jax.nn.sigmoid(gr.astype(f32)) * (ret_out.astype(x.dtype) @ w_rb).astype(f32)
             + jax.nn.sigmoid(ga.astype(f32)) * (att_out.astype(x.dtype) @ w_ab).astype(f32))
    x = x + mixed.astype(x.dtype) @ w_o
    u = rms_norm(x, ln2) @ w_u
    x = x + jnp.square(jax.nn.relu(u)) @ w_d
    x = x + jax.nn.sigmoid(x @ w_pg) * (ple.astype(x.dtype) @ w_pl)
    return x, new_ret.astype(x.dtype), new_kv


def setup_inputs(seed: int = 0) -> dict:
    key = jax.random.key(seed)
    ks = iter(jax.random.split(key, 40))
    f32 = jnp.float32

    def nrm(shape, scale):
        return jax.random.normal(next(ks), shape, f32) * scale

    def gain(shape):
        return 1.0 + 0.05 * jax.random.normal(next(ks), shape, f32)

    out = {
        "x_prompt": nrm((BATCH, SEQ, D_MODEL), 1.0),
        "x_sample": nrm((DEC_BATCH, DEC_SEQ, D_MODEL), 1.0),
        "state_ret": nrm((DEPTH, DEC_BATCH, RET_HEADS, RET_DK, RET_DV), 0.5),
    }
    for w in ATT_WINDOWS:
        lb = min(w, PAST_LEN)
        out["cache_k_w%d" % w] = nrm((DEPTH, DEC_BATCH, lb, ATT_HEADS_PER_GROUP, ATT_HD), 1.0)
        out["cache_v_w%d" % w] = nrm((DEPTH, DEC_BATCH, lb, ATT_HEADS_PER_GROUP, ATT_HD), 1.0)
    out.update({
        "p_prompt": nrm((DEPTH, BATCH, SEQ, D_PLE), 1.0),
        "p_sample": nrm((DEPTH, DEC_BATCH, DEC_SEQ, D_PLE), 1.0),
        "ln1_g": gain((DEPTH, D_MODEL)),
        "w_in": nrm((DEPTH, D_MODEL, N_IN), D_MODEL ** -0.5),
        "ret_gn_g": gain((DEPTH, RET_V_W)),
        "w_ret_br": nrm((DEPTH, RET_V_W, D_MODEL), RET_V_W ** -0.5),
        "w_att_br": nrm((DEPTH, ATT_OUT_W, D_MODEL), ATT_OUT_W ** -0.5),
        "w_out": nrm((DEPTH, D_MODEL, D_MODEL), D_MODEL ** -0.5),
        "ln2_g": gain((DEPTH, D_MODEL)),
        "w_up": nrm((DEPTH, D_MODEL, D_FF), D_MODEL ** -0.5),
        "w_down": nrm((DEPTH, D_FF, D_MODEL), D_FF ** -0.5),
        "w_ple": nrm((DEPTH, D_PLE, D_MODEL), D_PLE ** -0.5),
        "w_ple_gate": nrm((DEPTH, D_MODEL, D_MODEL), D_MODEL ** -0.5),
        "rel_bias": nrm((REL_BUCKETS, ATT_HEADS), 0.5),
        "lnf_g": gain((D_MODEL,)),
    })
    return out


def reference(x_prompt, x_sample, state_ret, cache_k_w128, cache_v_w128, cache_k_w512, cache_v_w512,
              cache_k_w2048, cache_v_w2048, p_prompt, p_sample, ln1_g, w_in, ret_gn_g, w_ret_br, w_att_br,
              w_out, ln2_g, w_up, w_down, w_ple, w_ple_gate, rel_bias, lnf_g):
    cache_k = (cache_k_w128, cache_k_w512, cache_k_w2048)
    cache_v = (cache_v_w128, cache_v_w512, cache_v_w2048)
    xp, xs = x_prompt, x_sample
    pos_p = jnp.arange(xp.shape[1], dtype=jnp.int32)
    pos_s = PAST_LEN + jnp.arange(xs.shape[1], dtype=jnp.int32)
    ret_p, ret_s = [], []
    kv_p = [([], []) for _ in range(ATT_GROUPS)]
    kv_s = [([], []) for _ in range(ATT_GROUPS)]
    for l in range(DEPTH):
        lw = (ln1_g[l], w_in[l], ret_gn_g[l], w_ret_br[l], w_att_br[l], w_out[l], ln2_g[l],
              w_up[l], w_down[l], w_ple[l], w_ple_gate[l])
        xp, st_p, nkv_p = decoder_layer(xp, p_prompt[l], pos_p, None, None, lw, rel_bias)
        layer_cache = [(cache_k[g][l], cache_v[g][l]) for g in range(ATT_GROUPS)]
        xs, st_s, nkv_s = decoder_layer(xs, p_sample[l], pos_s, state_ret[l], layer_cache, lw, rel_bias)
        ret_p.append(st_p)
        ret_s.append(st_s)
        for g in range(ATT_GROUPS):
            kv_p[g][0].append(nkv_p[g][0])
            kv_p[g][1].append(nkv_p[g][1])
            kv_s[g][0].append(nkv_s[g][0])
            kv_s[g][1].append(nkv_s[g][1])
    y_prompt = rms_norm(xp, lnf_g)
    y_sample = rms_norm(xs, lnf_g)
    new_state_ret_prompt = jnp.stack(ret_p)
    new_k_w128_prompt = jnp.stack(kv_p[0][0])
    new_v_w128_prompt = jnp.stack(kv_p[0][1])
    new_k_w512_prompt = jnp.stack(kv_p[1][0])
    new_v_w512_prompt = jnp.stack(kv_p[1][1])
    new_k_w2048_prompt = jnp.stack(kv_p[2][0])
    new_v_w2048_prompt = jnp.stack(kv_p[2][1])
    new_state_ret_sample = jnp.stack(ret_s)
    new_k_w128_sample = jnp.stack(kv_s[0][0])
    new_v_w128_sample = jnp.stack(kv_s[0][1])
    new_k_w512_sample = jnp.stack(kv_s[1][0])
    new_v_w512_sample = jnp.stack(kv_s[1][1])
    new_k_w2048_sample = jnp.stack(kv_s[2][0])
    new_v_w2048_sample = jnp.stack(kv_s[2][1])
    return (y_prompt, y_sample,
            new_state_ret_prompt, new_k_w128_prompt, new_v_w128_prompt, new_k_w512_prompt, new_v_w512_prompt,
            new_k_w2048_prompt, new_v_w2048_prompt,
            new_state_ret_sample, new_k_w128_sample, new_v_w128_sample, new_k_w512_sample, new_v_w512_sample,
            new_k_w2048_sample, new_v_w2048_sample)
```

```python
import functools
import math

import jax
import jax.numpy as jnp
import numpy as np
from jax import lax
from jax.experimental import pallas as pl
from jax.experimental.pallas import tpu as pltpu

F32 = jnp.float32
BF16 = jnp.bfloat16

D_MODEL = 1024
RET_HEADS = 8
RET_DK = 64
RET_DV = 128
RET_PAIRS = RET_HEADS // 2
RET_CHUNK = 128
ROPE_BASE = 10000.0
ATT_WINDOWS = (128, 512, 2048)
ATT_DILATIONS = (1, 4, 16)
ATT_GROUPS = 3
ATT_HPG = 4
ATT_HD = 128
ATT_NK = 128
ATT_GW = ATT_HPG * ATT_HD
REL_BUCKETS = 32
REL_MAX_DIST = 2048
D_FF = 4 * D_MODEL
D_PLE = 256
NORM_EPS = 1e-6
RET_QK_W = RET_HEADS * RET_DK
RET_V_W = RET_HEADS * RET_DV
ATT_W = ATT_GROUPS * ATT_GW
COL_RET = 0
COL_ATT = 2 * RET_QK_W + 2 * RET_V_W
COL_GATE = COL_ATT + 3 * ATT_W
N_IN = COL_GATE + 2 * D_MODEL

VMEM_LIMIT_V7X = 56 * 1024 * 1024
LANES = 128


def _dot(a, b):
    return jnp.dot(a, b, preferred_element_type=F32)


def _dot_nt(a, b):
    return lax.dot_general(a, b, (((1,), (1,)), ((), ())), preferred_element_type=F32)


def _dot_tn(a, b):
    return lax.dot_general(a, b, (((0,), (0,)), ((), ())), preferred_element_type=F32)


def _rms(x, g):
    return x * lax.rsqrt(jnp.mean(x * x, axis=-1, keepdims=True) + NORM_EPS) * g


def _sigmoid(x):
    return 1.0 / (1.0 + jnp.exp(-x))


def _resident(shape):
    return pl.BlockSpec(shape, lambda *_: (0,) * len(shape), pipeline_mode=pl.Buffered(1))


def _params(n_axes):
    return pltpu.CompilerParams(dimension_semantics=("arbitrary",) * n_axes,
                                vmem_limit_bytes=VMEM_LIMIT_V7X)


def _rope_tables(pos):
    half = RET_DK // 2
    inv = ROPE_BASE ** (-jnp.arange(half, dtype=F32) / half)
    ang = pos.astype(F32)[:, None] * inv[None, :]
    cos = jnp.cos(ang)
    sin = jnp.sin(ang)
    cos_t = jnp.tile(cos, (1, LANES // half))
    sin_t = jnp.tile(jnp.concatenate([-sin, sin], axis=1), (1, LANES // RET_DK))
    return cos_t, sin_t


def _ret_log_decay():
    return jnp.log1p(-jnp.exp2(-5.0 - jnp.arange(RET_HEADS, dtype=F32)))


def _ret_tables():
    c = RET_CHUNK
    lg = _ret_log_decay()
    i = jnp.arange(c, dtype=F32)
    diff = i[:, None] - i[None, :]
    dmask = jnp.where(diff[None] >= 0, jnp.exp(jnp.maximum(diff, 0.0)[None] * lg[:, None, None]), 0.0)
    q_decay = jnp.exp((i + 1.0)[:, None] * lg[None, :])
    k_decay = jnp.exp((c - 1.0 - i)[:, None] * lg[None, :])
    qdec = jnp.broadcast_to(q_decay.T[:, :, None], (RET_HEADS, c, RET_DV))
    kdec = jnp.repeat(k_decay, RET_DK, axis=1).reshape(c, RET_PAIRS, 2 * RET_DK).transpose(1, 0, 2)
    gc = jnp.repeat(jnp.exp(c * lg), RET_DV).reshape(RET_PAIRS, 1, 2 * RET_DV)
    return dmask, qdec, kdec, gc


def _rel_bucket(dist):
    max_exact = REL_BUCKETS // 2
    d = dist.astype(jnp.int32)
    log_ratio = jnp.log(jnp.maximum(d, 1).astype(F32) / max_exact) / math.log(REL_MAX_DIST / max_exact)
    large = max_exact + (log_ratio * (REL_BUCKETS - max_exact)).astype(jnp.int32)
    large = jnp.minimum(large, REL_BUCKETS - 1)
    return jnp.where(d < max_exact, d, large)


def _group_bias(rel_bias, g):
    b = _rel_bucket(jnp.arange(ATT_NK) * ATT_DILATIONS[g])
    return rel_bias[b][:, g * ATT_HPG:(g + 1) * ATT_HPG].T.astype(F32)


def _band_tables(bias, with_prev):
    nk = ATT_NK
    i = jnp.arange(nk)[:, None]
    c = jnp.arange(2 * nk)[None, :]
    jd = i + nk - c
    ok = (jd >= 0) & (jd < nk)
    b_full = bias[:, jnp.clip(jd, 0, nk - 1)]
    neg = jnp.float32(-jnp.inf)
    rest = jnp.where(ok[None], b_full, neg)
    first = jnp.where((ok & (c >= nk))[None], b_full, neg)
    if with_prev:
        return jnp.stack([first, rest])
    return first[None, :, :, nk:]


def _inproj_ret_body(x_ref, ln_ref, w_ref, cos_ref, sin_ref, q_ref, k_ref, v_ref, g_ref, gr_ref, ga_ref):
    x = x_ref[...]
    h = _rms(x, ln_ref[...]).astype(BF16)
    cos = cos_ref[...]
    sin = sin_ref[...]
    qk = _dot(h, w_ref[:, 0:2 * RET_QK_W])
    lane = lax.broadcasted_iota(jnp.int32, cos.shape, 1)
    first_half = (lane % RET_DK) < (RET_DK // 2)
    n_q = RET_QK_W // LANES
    for c in range(2 * n_q):
        xc = qk[:, c * LANES:(c + 1) * LANES]
        swapped = jnp.where(first_half, pltpu.roll(xc, LANES - RET_DK // 2, 1), pltpu.roll(xc, RET_DK // 2, 1))
        r = xc * cos + swapped * sin
        if c < n_q:
            q_ref[:, c * LANES:(c + 1) * LANES] = r.astype(q_ref.dtype)
        else:
            k_ref[:, (c - n_q) * LANES:(c - n_q + 1) * LANES] = (r * (RET_DK ** -0.5)).astype(k_ref.dtype)
    o = 2 * RET_QK_W
    v_ref[...] = _dot(h, w_ref[:, o:o + RET_V_W]).astype(v_ref.dtype)
    o += RET_V_W
    g_ref[...] = _dot(h, w_ref[:, o:o + RET_V_W]).astype(g_ref.dtype)
    o += RET_V_W
    gr_ref[...] = _dot(h, w_ref[:, o:o + D_MODEL]).astype(gr_ref.dtype)
    o += D_MODEL
    ga_ref[...] = _dot(h, w_ref[:, o:o + D_MODEL]).astype(ga_ref.dtype)


def _inproj_ret(x2d, ln, w, cos_t, sin_t, tm, out_dtype):
    n = x2d.shape[0]
    t_rows = cos_t.shape[0]
    nt = t_rows // tm
    wcols = w.shape[1]
    row = lambda width: pl.BlockSpec((tm, width), lambda i: (i, 0))
    tab = pl.BlockSpec((tm, LANES), lambda i: (i % nt, 0))
    widths = (RET_QK_W, RET_QK_W, RET_V_W, RET_V_W, D_MODEL, D_MODEL)
    return pl.pallas_call(
        _inproj_ret_body,
        grid=(n // tm,),
        in_specs=[row(D_MODEL), _resident((1, D_MODEL)), _resident((D_MODEL, wcols)), tab, tab],
        out_specs=[row(wd) for wd in widths],
        out_shape=[jax.ShapeDtypeStruct((n, wd), out_dtype) for wd in widths],
        compiler_params=_params(1),
        name="inproj_ret",
    )(x2d, ln, w, cos_t, sin_t)


def _inproj_att_body(x_ref, ln_ref, w_ref, *out_refs, tails):
    x = x_ref[...]
    h = _rms(x, ln_ref[...]).astype(BF16)
    lowp = out_refs[:3 * ATT_GROUPS]
    full = out_refs[3 * ATT_GROUPS:]
    for kind in range(3):
        for g in range(ATT_GROUPS):
            o = kind * ATT_W + g * ATT_GW
            r = _dot(h, w_ref[:, o:o + ATT_GW])
            lowp[kind * ATT_GROUPS + g][...] = r.astype(lowp[0].dtype)
            if kind > 0 and tails is not None:
                dst = full[(kind - 1) * ATT_GROUPS + g]
                keep = tails[g]
                dst[...] = r[r.shape[0] - keep:, :]


def _inproj_att_prompt(x2d, ln, w, tm, seq):
    n = x2d.shape[0]
    batch = n // seq
    tiles_per_seq = seq // tm
    keeps = tuple(min(wd, seq) for wd in ATT_WINDOWS)
    assert all(kp <= tm or kp == seq for kp in keeps) and seq % tm == 0
    row = lambda width: pl.BlockSpec((tm, width), lambda i: (i, 0))
    out_specs = [row(ATT_GW) for _ in range(3 * ATT_GROUPS)]
    out_shape = [jax.ShapeDtypeStruct((n, ATT_GW), BF16) for _ in range(3 * ATT_GROUPS)]
    tails = []
    for _ in range(2):
        for g in range(ATT_GROUPS):
            if keeps[g] == seq:
                out_specs.append(row(ATT_GW))
                out_shape.append(jax.ShapeDtypeStruct((n, ATT_GW), F32))
            else:
                out_specs.append(pl.BlockSpec((keeps[g], ATT_GW), lambda i: (i // tiles_per_seq, 0)))
                out_shape.append(jax.ShapeDtypeStruct((batch * keeps[g], ATT_GW), F32))
    tails = tuple(tm if kp == seq else kp for kp in keeps)
    return pl.pallas_call(
        functools.partial(_inproj_att_body, tails=tails),
        grid=(n // tm,),
        in_specs=[row(D_MODEL), _resident((1, D_MODEL)), _resident((D_MODEL, 3 * ATT_W))],
        out_specs=out_specs,
        out_shape=out_shape,
        compiler_params=_params(1),
        name="inproj_att",
    )(x2d, ln, w)


def _inproj_att_sample(x2d, ln, w):
    n = x2d.shape[0]
    row = lambda width: pl.BlockSpec((n, width), lambda i: (0, 0))
    return pl.pallas_call(
        functools.partial(_inproj_att_body, tails=None),
        grid=(1,),
        in_specs=[row(D_MODEL), _resident((1, D_MODEL)), _resident((D_MODEL, 3 * ATT_W))],
        out_specs=[row(ATT_GW) for _ in range(3 * ATT_GROUPS)],
        out_shape=[jax.ShapeDtypeStruct((n, ATT_GW), F32) for _ in range(3 * ATT_GROUPS)],
        compiler_params=_params(1),
        name="inproj_att_sample",
    )(x2d, ln, w)


def _gn_swish(o, gate, gn):
    mu = jnp.mean(o, axis=-1, keepdims=True)
    d = o - mu
    var = jnp.mean(d * d, axis=-1, keepdims=True)
    on = d * lax.rsqrt(var + NORM_EPS) * gn
    return gate * _sigmoid(gate) * on


def _retention_body(q_ref, k_ref, v_ref, g_ref, gn_ref, dm_ref, qdec_ref, kdec_ref, gc_ref,
                    out_ref, st_ref, state, *, n_chunks):
    c = RET_CHUNK

    @pl.when(pl.program_id(1) == 0)
    def _():
        state[...] = jnp.zeros_like(state)

    lane = lax.broadcasted_iota(jnp.int32, (c, 2 * RET_DK), 1)
    head0 = lane < RET_DK

    def chunk(ci, carry):
        rows = pl.ds(pl.multiple_of(ci * c, c), c)
        for p in range(RET_PAIRS):
            q2 = q_ref[rows, p * 2 * RET_DK:(p + 1) * 2 * RET_DK]
            k2 = k_ref[rows, p * 2 * RET_DK:(p + 1) * 2 * RET_DK]
            v2 = v_ref[rows, p * 2 * RET_DV:(p + 1) * 2 * RET_DV]
            pst = state[p]
            pst_lo = pst.astype(BF16)
            zero = jnp.zeros_like(q2)
            for hh in range(2):
                h = 2 * p + hh
                qm = jnp.where(head0 if hh == 0 else jnp.logical_not(head0), q2, zero)
                vh = v2[:, hh * RET_DV:(hh + 1) * RET_DV]
                s = _dot_nt(qm, k2) * dm_ref[h]
                o = _dot(s.astype(BF16), vh)
                o = o + _dot(qm, pst_lo[:, hh * RET_DV:(hh + 1) * RET_DV]) * qdec_ref[h]
                gate = g_ref[rows, h * RET_DV:(h + 1) * RET_DV].astype(F32)
                res = _gn_swish(o, gate, gn_ref[:, h * RET_DV:(h + 1) * RET_DV])
                out_ref[rows, h * RET_DV:(h + 1) * RET_DV] = res.astype(out_ref.dtype)
            kd = (k2.astype(F32) * kdec_ref[p]).astype(BF16)
            state[p] = pst * gc_ref[p] + _dot_tn(kd, v2)
        return carry

    lax.fori_loop(0, n_chunks, chunk, 0)
    for p in range(RET_PAIRS):
        pst = state[p]
        for hh in range(2):
            st_ref[2 * p + hh] = pst[hh * RET_DK:(hh + 1) * RET_DK, hh * RET_DV:(hh + 1) * RET_DV]


def _retention_prompt(q, k, v, g, gn, batch, seq, ts):
    n = q.shape[0]
    steps = seq // ts
    dmask, qdec, kdec, gc = _ret_tables()
    row = lambda width: pl.BlockSpec((ts, width), lambda b, s: (b * steps + s, 0))
    const = lambda a: pl.BlockSpec(a.shape, lambda b, s: (0,) * a.ndim)
    return pl.pallas_call(
        functools.partial(_retention_body, n_chunks=ts // RET_CHUNK),
        grid=(batch, steps),
        in_specs=[row(RET_QK_W), row(RET_QK_W), row(RET_V_W), row(RET_V_W), const(gn),
                  const(dmask), const(qdec), const(kdec), const(gc)],
        out_specs=[row(RET_V_W),
                   pl.BlockSpec((None, RET_HEADS, RET_DK, RET_DV), lambda b, s: (b, 0, 0, 0))],
        out_shape=[jax.ShapeDtypeStruct((n, RET_V_W), BF16),
                   jax.ShapeDtypeStruct((batch, RET_HEADS, RET_DK, RET_DV), F32)],
        scratch_shapes=[pltpu.VMEM((RET_PAIRS, 2 * RET_DK, 2 * RET_DV), F32)],
        compiler_params=_params(2),
        name="retention",
    )(q, k, v, g, gn, dmask, qdec, kdec, gc)


def _attn_body(q_ref, k_ref, v_ref, tb_ref, o_ref, lse_ref, *, streams, n_blocks):
    nk = ATT_NK
    scale = ATT_HD ** -0.5
    lane = lax.broadcasted_iota(jnp.int32, (nk, LANES), 1)

    def block(rs, n, n_prev, sel):
        rows = pl.ds(pl.multiple_of(n * nk, nk), nk)
        cols = slice(rs * ATT_GW, (rs + 1) * ATT_GW)
        q = q_ref[rows, cols]
        k_cur = k_ref[rows, cols]
        v_cur = v_ref[rows, cols]
        if n_blocks > 1:
            prev_rows = pl.ds(pl.multiple_of(n_prev * nk, nk), nk)
            k_all = jnp.concatenate([k_ref[prev_rows, cols], k_cur], axis=0)
            v_all = jnp.concatenate([v_ref[prev_rows, cols], v_cur], axis=0)
        else:
            k_all, v_all = k_cur, v_cur
        lse_tile = jnp.zeros((nk, LANES), F32)
        for h in range(ATT_HPG):
            hc = slice(h * ATT_HD, (h + 1) * ATT_HD)
            s = _dot_nt(q[:, hc], k_all[:, hc]) * scale + tb_ref[sel, h]
            m = jnp.max(s, axis=-1, keepdims=True)
            p = jnp.exp(s - m)
            den = jnp.sum(p, axis=-1, keepdims=True)
            o = _dot(p.astype(BF16), v_all[:, hc]) / den
            o_ref[rows, rs * ATT_GW + h * ATT_HD:rs * ATT_GW + (h + 1) * ATT_HD] = o.astype(o_ref.dtype)
            lse_tile = jnp.where(lane == h, m + jnp.log(den), lse_tile)
        lse_ref[rows, rs * LANES:(rs + 1) * LANES] = lse_tile

    for rs in range(streams):
        if n_blocks == 1:
            block(rs, 0, 0, 0)
        else:
            def loop(n, carry, rs=rs):
                block(rs, n, jnp.maximum(n - 1, 0), jnp.minimum(n, 1))
                return carry
            lax.fori_loop(0, n_blocks, loop, 0)


def _attention_prompt(aq, ak, av, tb, batch, seq, g):
    dil = ATT_DILATIONS[g]
    length = seq // dil
    n_blocks = length // ATT_NK
    streams = min(dil, 4)
    steps = dil // streams
    view = lambda a: a.reshape(batch, length, dil * a.shape[1])
    blk = lambda width: pl.BlockSpec((None, length, streams * width), lambda b, s: (b, 0, s))
    o, lse = pl.pallas_call(
        functools.partial(_attn_body, streams=streams, n_blocks=n_blocks),
        grid=(batch, steps),
        in_specs=[blk(ATT_GW), blk(ATT_GW), blk(ATT_GW),
                  pl.BlockSpec(tb.shape, lambda b, s: (0, 0, 0, 0))],
        out_specs=[blk(ATT_GW), blk(LANES)],
        out_shape=[jax.ShapeDtypeStruct((batch, length, dil * ATT_GW), BF16),
                   jax.ShapeDtypeStruct((batch, length, dil * LANES), F32)],
        compiler_params=_params(2),
        name="attention_g%d" % g,
    )(view(aq), view(ak), view(av), tb)
    return o.reshape(batch * seq, ATT_GW), lse.reshape(batch * seq, LANES)


def _tail_body(*refs, combine):
    if combine:
        (ret_ref, o0_ref, o1_ref, o2_ref, l0_ref, l1_ref, l2_ref, gr_ref, ga_ref, x_ref, ple_ref,
         wrb_ref, wab_ref, wo_ref, wu_ref, wd_ref, wpl_ref, wpg_ref, ln2_ref, lnf_ref, y_ref) = refs
        lses = [l0_ref[...], l1_ref[...], l2_ref[...]]
        outs = [o0_ref, o1_ref, o2_ref]
        parts = []
        for h in range(ATT_HPG):
            lh = [l[:, h:h + 1] for l in lses]
            mx = jnp.maximum(jnp.maximum(lh[0], lh[1]), lh[2])
            e = [jnp.exp(l - mx) for l in lh]
            tot = e[0] + e[1] + e[2]
            acc = None
            for g in range(ATT_GROUPS):
                term = (e[g] / tot) * outs[g][:, h * ATT_HD:(h + 1) * ATT_HD].astype(F32)
                acc = term if acc is None else acc + term
            parts.append(acc)
        att = jnp.concatenate(parts, axis=1).astype(BF16)
    else:
        (ret_ref, att_ref, gr_ref, ga_ref, x_ref, ple_ref,
         wrb_ref, wab_ref, wo_ref, wu_ref, wd_ref, wpl_ref, wpg_ref, ln2_ref, lnf_ref, y_ref) = refs
        att = att_ref[...].astype(BF16)
    a = _dot(ret_ref[...].astype(BF16), wrb_ref[...])
    b = _dot(att, wab_ref[...])
    mixed = _sigmoid(gr_ref[...].astype(F32)) * a + _sigmoid(ga_ref[...].astype(F32)) * b
    x1 = x_ref[...] + _dot(mixed.astype(BF16), wo_ref[...])
    h2 = _rms(x1, ln2_ref[...]).astype(BF16)
    ff_chunk = D_MODEL
    acc = None
    for c in range(D_FF // ff_chunk):
        u = _dot(h2, wu_ref[:, c * ff_chunk:(c + 1) * ff_chunk])
        r = jnp.maximum(u, 0.0)
        t = _dot((r * r).astype(BF16), wd_ref[c * ff_chunk:(c + 1) * ff_chunk, :])
        acc = t if acc is None else acc + t
    x2 = x1 + acc
    gate = _sigmoid(_dot(x2.astype(BF16), wpg_ref[...]))
    x3 = x2 + gate * _dot(ple_ref[...].astype(BF16), wpl_ref[...])
    y_ref[...] = _rms(x3, lnf_ref[...])


def _tail(acts, weights, tm, combine):
    n = acts[0].shape[0]
    row = lambda a: pl.BlockSpec((tm, a.shape[1]), lambda i: (i, 0))
    return pl.pallas_call(
        functools.partial(_tail_body, combine=combine),
        grid=(n // tm,),
        in_specs=[row(a) for a in acts] + [_resident(w.shape) for w in weights],
        out_specs=pl.BlockSpec((tm, D_MODEL), lambda i: (i, 0)),
        out_shape=jax.ShapeDtypeStruct((n, D_MODEL), F32),
        compiler_params=_params(1),
        name="tail" if combine else "tail_sample",
    )(*acts, *weights)


def _decode_body(q_ref, k_ref, v_ref, g_ref, gn_ref, st_ref, gam_ref,
                 aq_ref, ak_ref, av_ref, bias_ref,
                 ck0_ref, cv0_ref, ck1_ref, cv1_ref, ck2_ref, cv2_ref,
                 ret_ref, att_ref, nst_ref):
    sub = lax.broadcasted_iota(jnp.int32, (8, 2 * RET_DK), 0)
    lane = lax.broadcasted_iota(jnp.int32, (8, 2 * RET_DK), 1)
    row0 = sub == 0
    srow = lax.broadcasted_iota(jnp.int32, (2 * RET_DK, RET_DV), 0)
    for p in range(RET_PAIRS):
        pc = slice(p * 2 * RET_DK, (p + 1) * 2 * RET_DK)
        q2 = jnp.where(row0, jnp.broadcast_to(q_ref[:, pc], (8, 2 * RET_DK)), 0.0)
        k2 = jnp.where(row0, jnp.broadcast_to(k_ref[:, pc], (8, 2 * RET_DK)), 0.0)
        pst = st_ref[p]
        gam = gam_ref[p]
        outer = []
        for hh in range(2):
            h = 2 * p + hh
            hsel = (lane < RET_DK) if hh == 0 else (lane >= RET_DK)
            qm = jnp.where(hsel, q2, 0.0)
            km = jnp.where(hsel, k2, 0.0)
            vh = v_ref[:, h * RET_DV:(h + 1) * RET_DV]
            v8 = jnp.where(row0[:, :RET_DV], jnp.broadcast_to(vh, (8, RET_DV)), 0.0)
            cross = _dot(qm.astype(BF16), (pst * gam).astype(BF16))[0:1, :]
            qk = jnp.sum(qm[0:1, :] * km[0:1, :], axis=-1, keepdims=True)
            o = cross + qk * vh
            res = _gn_swish(o, g_ref[:, h * RET_DV:(h + 1) * RET_DV], gn_ref[:, h * RET_DV:(h + 1) * RET_DV])
            ret_ref[:, h * RET_DV:(h + 1) * RET_DV] = res
            outer.append(_dot_tn(k2.astype(BF16), v8.astype(BF16)))
        nst_ref[p] = pst * gam + jnp.where(srow < RET_DK, outer[0], outer[1])
    scale = ATT_HD ** -0.5
    caches = ((ck0_ref, cv0_ref), (ck1_ref, cv1_ref), (ck2_ref, cv2_ref))
    slot = lax.broadcasted_iota(jnp.int32, (ATT_NK, ATT_HD), 0)
    is_new = slot == 0
    for h in range(ATT_HPG):
        o_g, lse_g = [], []
        for g in range(ATT_GROUPS):
            ck_ref, cv_ref = caches[g]
            cols = slice(g * ATT_GW + h * ATT_HD, g * ATT_GW + (h + 1) * ATT_HD)
            qh = aq_ref[:, cols]
            kk = jnp.where(is_new, ak_ref[:, cols], ck_ref[:, h, :])
            vv = jnp.where(is_new, av_ref[:, cols], cv_ref[:, h, :])
            s = jnp.sum(kk * qh, axis=-1, keepdims=True) * scale + bias_ref[g * ATT_HPG + h]
            m = jnp.max(s, axis=0, keepdims=True)
            pr = jnp.exp(s - m)
            den = jnp.sum(pr, axis=0, keepdims=True)
            o_g.append(jnp.sum(pr * vv, axis=0, keepdims=True) / den)
            lse_g.append(m + jnp.log(den))
        mx = jnp.maximum(jnp.maximum(lse_g[0], lse_g[1]), lse_g[2])
        e = [jnp.exp(l - mx) for l in lse_g]
        tot = e[0] + e[1] + e[2]
        att_ref[:, h * ATT_HD:(h + 1) * ATT_HD] = ((e[0] / tot) * o_g[0] + (e[1] / tot) * o_g[1]
                                                   + (e[2] / tot) * o_g[2])


def _decode(q, k, v, g, gn, state, aq, ak, av, bias_cols, caches):
    batch = q.shape[0]
    lg = _ret_log_decay()
    gam = jnp.broadcast_to(jnp.repeat(jnp.exp(lg), RET_DK).reshape(RET_PAIRS, 2 * RET_DK, 1),
                           (RET_PAIRS, 2 * RET_DK, RET_DV))
    vec = lambda a: a.reshape(batch, 1, a.shape[1])
    vspec = lambda width: pl.BlockSpec((None, 1, width), lambda b: (b, 0, 0))
    const = lambda a: pl.BlockSpec(a.shape, lambda b: (0,) * a.ndim)
    st_spec = pl.BlockSpec((None, RET_PAIRS, 2 * RET_DK, RET_DV), lambda b: (b, 0, 0, 0))
    cache_in, cache_specs = [], []
    for gi in range(ATT_GROUPS):
        dil = ATT_DILATIONS[gi]
        for c in caches[2 * gi:2 * gi + 2]:
            cache_in.append(c.reshape(batch, ATT_NK, dil, ATT_HPG, ATT_HD))
            cache_specs.append(pl.BlockSpec((None, ATT_NK, None, ATT_HPG, ATT_HD), lambda b: (b, 0, 0, 0, 0)))
    st_pairs = state.reshape(batch, RET_PAIRS, 2 * RET_DK, RET_DV)
    ret, att, nst = pl.pallas_call(
        _decode_body,
        grid=(batch,),
        in_specs=[vspec(RET_QK_W), vspec(RET_QK_W), vspec(RET_V_W), vspec(RET_V_W), const(gn), st_spec, const(gam),
                  vspec(ATT_W), vspec(ATT_W), vspec(ATT_W), const(bias_cols)] + cache_specs,
        out_specs=[vspec(RET_V_W), vspec(ATT_GW), st_spec],
        out_shape=[jax.ShapeDtypeStruct((batch, 1, RET_V_W), F32),
                   jax.ShapeDtypeStruct((batch, 1, ATT_GW), F32),
                   jax.ShapeDtypeStruct(st_pairs.shape, F32)],
        compiler_params=_params(1),
        name="decode",
    )(vec(q), vec(k), vec(v), vec(g), gn, st_pairs, gam, vec(aq), vec(ak), vec(av), bias_cols, *cache_in)
    return ret.reshape(batch, RET_V_W), att.reshape(batch, ATT_GW), nst.reshape(state.shape)


def _shift_body(*refs, n):
    new_refs = refs[:2]
    old = refs[2:2 + n]
    out = refs[2 + n:2 + 2 * n]
    rows = refs[2 + 2 * n:2 + 3 * n]
    sem = refs[2 + 3 * n]
    copies = []
    for i in range(n):
        width = old[i].shape[1]
        g, kind = divmod(i, 2)
        for h in range(ATT_HPG):
            c0 = g * ATT_GW + h * ATT_HD
            rows[i][:, h, :] = new_refs[kind][:, c0:c0 + ATT_HD]
        copies.append(pltpu.make_async_copy(old[i].at[:, pl.ds(1, width - 1)],
                                            out[i].at[:, pl.ds(0, width - 1)], sem.at[2 * i]))
        copies.append(pltpu.make_async_copy(rows[i], out[i].at[:, width - 1], sem.at[2 * i + 1]))
    for cp in copies:
        cp.start()
    for cp in copies:
        cp.wait()


def _shift_caches(ak, av, caches):
    n = len(caches)
    batch = ak.shape[0]
    anyspec = pl.BlockSpec(memory_space=pl.ANY)
    vm = pl.BlockSpec(memory_space=pltpu.VMEM)
    return pl.pallas_call(
        functools.partial(_shift_body, n=n),
        in_specs=[vm, vm] + [anyspec] * n,
        out_specs=[anyspec] * n,
        out_shape=[jax.ShapeDtypeStruct(c.shape, c.dtype) for c in caches],
        scratch_shapes=[pltpu.VMEM((batch, ATT_HPG, ATT_HD), F32) for _ in range(n)]
        + [pltpu.SemaphoreType.DMA((2 * n,))],
        compiler_params=pltpu.CompilerParams(vmem_limit_bytes=VMEM_LIMIT_V7X),
        name="cache_shift",
    )(ak, av, *caches)


def kernel(x_prompt, x_sample, state_ret, cache_k_w128, cache_v_w128, cache_k_w512, cache_v_w512,
           cache_k_w2048, cache_v_w2048, p_prompt, p_sample, ln1_g, w_in, ret_gn_g, w_ret_br, w_att_br,
           w_out, ln2_g, w_up, w_down, w_ple, w_ple_gate, rel_bias, lnf_g):
    depth = w_in.shape[0]
    assert depth == 1
    batch, seq, _ = x_prompt.shape
    dec_batch, dec_seq, _ = x_sample.shape
    assert dec_seq == 1
    past_len = 16384
    l = 0
    ln1 = ln1_g[l][None, :]
    ln2 = ln2_g[l][None, :]
    lnf = lnf_g[None, :]
    gn = ret_gn_g[l][None, :]
    w_in_l = w_in[l]
    w_ret = jnp.concatenate([w_in_l[:, :COL_ATT], w_in_l[:, COL_GATE:]], axis=1).astype(BF16)
    w_att = w_in_l[:, COL_ATT:COL_GATE].astype(BF16)
    tail_w = (w_ret_br[l].astype(BF16), w_att_br[l].astype(BF16), w_out[l].astype(BF16), w_up[l].astype(BF16),
              w_down[l].astype(BF16), w_ple[l].astype(BF16), w_ple_gate[l].astype(BF16), ln2, lnf)
    biases = [_group_bias(rel_bias, g) for g in range(ATT_GROUPS)]

    tm = 512
    xp = x_prompt.reshape(batch * seq, D_MODEL)
    cos_p, sin_p = _rope_tables(jnp.arange(seq, dtype=jnp.int32))
    rq, rk, rv, rg, gr, ga = _inproj_ret(xp, ln1, w_ret, cos_p, sin_p, tm, BF16)
    att_o = _inproj_att_prompt(xp, ln1, w_att, tm, seq)
    aqs, aks, avs = att_o[0:3], att_o[3:6], att_o[6:9]
    kfull, vfull = att_o[9:12], att_o[12:15]
    ret_out, st_p = _retention_prompt(rq, rk, rv, rg, gn, batch, seq, 512)
    outs, lses = [], []
    for g in range(ATT_GROUPS):
        n_blocks = seq // ATT_DILATIONS[g] // ATT_NK
        tb = _band_tables(biases[g], n_blocks > 1)
        o, lse = _attention_prompt(aqs[g], aks[g], avs[g], tb, batch, seq, g)
        outs.append(o)
        lses.append(lse)
    ple_p = p_prompt[l].reshape(batch * seq, D_PLE)
    y_p = _tail([ret_out] + outs + lses + [gr, ga, xp, ple_p], tail_w, tm, True)
    y_prompt = y_p.reshape(batch, seq, D_MODEL)
    new_state_p = st_p[None]
    kv_p = []
    for g in range(ATT_GROUPS):
        keep = min(ATT_WINDOWS[g], seq)
        kv_p.append(kfull[g].reshape(1, batch, keep, ATT_HPG, ATT_HD))
        kv_p.append(vfull[g].reshape(1, batch, keep, ATT_HPG, ATT_HD))

    xs = x_sample.reshape(dec_batch, D_MODEL)
    cos_s, sin_s = _rope_tables(jnp.full((dec_batch,), past_len, dtype=jnp.int32))
    sq, sk, sv, sg, sgr, sga = _inproj_ret(xs, ln1, w_ret, cos_s, sin_s, dec_batch, F32)
    s_att = _inproj_att_sample(xs, ln1, w_att)
    saq = jnp.concatenate(s_att[0:3], axis=1)
    sak = jnp.concatenate(s_att[3:6], axis=1)
    sav = jnp.concatenate(s_att[6:9], axis=1)
    caches = (cache_k_w128[l], cache_v_w128[l], cache_k_w512[l], cache_v_w512[l],
              cache_k_w2048[l], cache_v_w2048[l])
    slot_off = jnp.concatenate([jnp.zeros((1,), jnp.int32), ATT_NK - jnp.arange(1, ATT_NK)])
    bias_cols = jnp.concatenate([b[:, slot_off] for b in biases], axis=0)[:, :, None]
    s_ret, s_attn, new_st = _decode(sq, sk, sv, sg, gn, state_ret[l], saq, sak, sav, bias_cols, caches)
    ple_s = p_sample[l].reshape(dec_batch, D_PLE)
    y_s = _tail([s_ret, s_attn, sgr, sga, xs, ple_s], tail_w, dec_batch, False)
    y_sample = y_s.reshape(dec_batch, 1, D_MODEL)
    new_caches = _shift_caches(sak, sav, caches)
    kv_s = [c[None] for c in new_caches]

    return (y_prompt, y_sample, new_state_p, *kv_p, new_st[None], *kv_s)
```

```python
import functools
import math

import jax
import jax.numpy as jnp
import numpy as np
from jax import lax
from jax.experimental import pallas as pl
from jax.experimental.pallas import tpu as pltpu

F32 = jnp.float32
BF16 = jnp.bfloat16

D_MODEL = 1024
RET_HEADS = 8
RET_DK = 64
RET_DV = 128
RET_PAIRS = RET_HEADS // 2
RET_CHUNK = 128
ROPE_BASE = 10000.0
ATT_WINDOWS = (128, 512, 2048)
ATT_DILATIONS = (1, 4, 16)
ATT_GROUPS = 3
ATT_HPG = 4
ATT_HD = 128
ATT_NK = 128
ATT_GW = ATT_HPG * ATT_HD
REL_BUCKETS = 32
REL_MAX_DIST = 2048
D_FF = 4 * D_MODEL
D_PLE = 256
NORM_EPS = 1e-6
RET_QK_W = RET_HEADS * RET_DK
RET_V_W = RET_HEADS * RET_DV
ATT_W = ATT_GROUPS * ATT_GW
COL_RET = 0
COL_ATT = 2 * RET_QK_W + 2 * RET_V_W
COL_GATE = COL_ATT + 3 * ATT_W
N_IN = COL_GATE + 2 * D_MODEL

VMEM_LIMIT_V7X = 56 * 1024 * 1024
LANES = 128
SHIFT_ROWS = 512
SHIFT_SMALL_BATCH = 4
SHIFT_BUFFERS = 3


def _dot(a, b):
    return jnp.dot(a, b, preferred_element_type=F32)


def _dot_nt(a, b):
    return lax.dot_general(a, b, (((1,), (1,)), ((), ())), preferred_element_type=F32)


def _dot_tn(a, b):
    return lax.dot_general(a, b, (((0,), (0,)), ((), ())), preferred_element_type=F32)


def _rms(x, g):
    return x * lax.rsqrt(jnp.mean(x * x, axis=-1, keepdims=True) + NORM_EPS) * g


def _sigmoid(x):
    return 1.0 / (1.0 + jnp.exp(-x))


def _resident(shape):
    return pl.BlockSpec(shape, lambda *_: (0,) * len(shape), pipeline_mode=pl.Buffered(1))


def _params(n_axes):
    return pltpu.CompilerParams(dimension_semantics=("arbitrary",) * n_axes,
                                vmem_limit_bytes=VMEM_LIMIT_V7X)


def _rope_tables(pos):
    half = RET_DK // 2
    inv = ROPE_BASE ** (-jnp.arange(half, dtype=F32) / half)
    ang = pos.astype(F32)[:, None] * inv[None, :]
    cos = jnp.cos(ang)
    sin = jnp.sin(ang)
    cos_t = jnp.tile(cos, (1, LANES // half))
    sin_t = jnp.tile(jnp.concatenate([-sin, sin], axis=1), (1, LANES // RET_DK))
    return cos_t, sin_t


def _ret_log_decay():
    return jnp.log1p(-jnp.exp2(-5.0 - jnp.arange(RET_HEADS, dtype=F32)))


def _ret_tables():
    c = RET_CHUNK
    lg = _ret_log_decay()
    i = jnp.arange(c, dtype=F32)
    diff = i[:, None] - i[None, :]
    dmask = jnp.where(diff[None] >= 0, jnp.exp(jnp.maximum(diff, 0.0)[None] * lg[:, None, None]), 0.0)
    q_decay = jnp.exp((i + 1.0)[:, None] * lg[None, :])
    k_decay = jnp.exp((c - 1.0 - i)[:, None] * lg[None, :])
    qdec = jnp.broadcast_to(q_decay.T[:, :, None], (RET_HEADS, c, RET_DV))
    kdec = jnp.repeat(k_decay, RET_DK, axis=1).reshape(c, RET_PAIRS, 2 * RET_DK).transpose(1, 0, 2)
    gc = jnp.repeat(jnp.exp(c * lg), RET_DV).reshape(RET_PAIRS, 1, 2 * RET_DV)
    return dmask, qdec, kdec, gc


def _rel_bucket(dist):
    max_exact = REL_BUCKETS // 2
    d = dist.astype(jnp.int32)
    log_ratio = jnp.log(jnp.maximum(d, 1).astype(F32) / max_exact) / math.log(REL_MAX_DIST / max_exact)
    large = max_exact + (log_ratio * (REL_BUCKETS - max_exact)).astype(jnp.int32)
    large = jnp.minimum(large, REL_BUCKETS - 1)
    return jnp.where(d < max_exact, d, large)


def _group_bias(rel_bias, g):
    b = _rel_bucket(jnp.arange(ATT_NK) * ATT_DILATIONS[g])
    return rel_bias[b][:, g * ATT_HPG:(g + 1) * ATT_HPG].T.astype(F32)


def _band_tables(bias, with_prev):
    nk = ATT_NK
    hpg = bias.shape[0]
    neg = lambda n: jnp.full((hpg, n), -jnp.inf, F32)
    period = 3 * nk
    gvec = jnp.concatenate([neg(nk - 1), bias, neg(period - 2 * nk + 1)], axis=1)
    flat = jnp.tile(gvec, (1, nk + 1))[:, :nk * (period + 1)]
    hankel = flat.reshape(hpg, nk, period + 1)[:, :, :2 * nk]
    rest = hankel[:, :, ::-1]
    cur = rest[:, :, nk:]
    if with_prev:
        first = jnp.concatenate([jnp.full((hpg, nk, nk), -jnp.inf, F32), cur], axis=2)
        return jnp.stack([first, rest])
    return cur[None]


def _inproj_ret_body(x_ref, ln_ref, w_ref, cos_ref, sin_ref, q_ref, k_ref, v_ref, g_ref, gr_ref, ga_ref):
    x = x_ref[...]
    h = _rms(x, ln_ref[...]).astype(BF16)
    cos = cos_ref[...]
    sin = sin_ref[...]
    qk = _dot(h, w_ref[:, 0:2 * RET_QK_W])
    lane = lax.broadcasted_iota(jnp.int32, cos.shape, 1)
    first_half = (lane % RET_DK) < (RET_DK // 2)
    n_q = RET_QK_W // LANES
    for c in range(2 * n_q):
        xc = qk[:, c * LANES:(c + 1) * LANES]
        swapped = jnp.where(first_half, pltpu.roll(xc, LANES - RET_DK // 2, 1), pltpu.roll(xc, RET_DK // 2, 1))
        r = xc * cos + swapped * sin
        if c < n_q:
            q_ref[:, c * LANES:(c + 1) * LANES] = r.astype(q_ref.dtype)
        else:
            k_ref[:, (c - n_q) * LANES:(c - n_q + 1) * LANES] = (r * (RET_DK ** -0.5)).astype(k_ref.dtype)
    o = 2 * RET_QK_W
    v_ref[...] = _dot(h, w_ref[:, o:o + RET_V_W]).astype(v_ref.dtype)
    o += RET_V_W
    g_ref[...] = _dot(h, w_ref[:, o:o + RET_V_W]).astype(g_ref.dtype)
    o += RET_V_W
    gr_ref[...] = _dot(h, w_ref[:, o:o + D_MODEL]).astype(gr_ref.dtype)
    o += D_MODEL
    ga_ref[...] = _dot(h, w_ref[:, o:o + D_MODEL]).astype(ga_ref.dtype)


def _inproj_ret(x2d, ln, w, cos_t, sin_t, tm, out_dtype):
    n = x2d.shape[0]
    t_rows = cos_t.shape[0]
    nt = t_rows // tm
    wcols = w.shape[1]
    row = lambda width: pl.BlockSpec((tm, width), lambda i: (i, 0))
    tab = pl.BlockSpec((tm, LANES), lambda i: (i % nt, 0))
    widths = (RET_QK_W, RET_QK_W, RET_V_W, RET_V_W, D_MODEL, D_MODEL)
    return pl.pallas_call(
        _inproj_ret_body,
        grid=(n // tm,),
        in_specs=[row(D_MODEL), _resident((1, D_MODEL)), _resident((D_MODEL, wcols)), tab, tab],
        out_specs=[row(wd) for wd in widths],
        out_shape=[jax.ShapeDtypeStruct((n, wd), out_dtype) for wd in widths],
        compiler_params=_params(1),
        name="inproj_ret",
    )(x2d, ln, w, cos_t, sin_t)


def _inproj_att_body(x_ref, ln_ref, w_ref, *refs, tm, keeps, seq):
    lowp = refs[:3 * ATT_GROUPS]
    full = refs[3 * ATT_GROUPS:5 * ATT_GROUPS]
    res = refs[5 * ATT_GROUPS]
    x = x_ref[...]
    h = _rms(x, ln_ref[...]).astype(BF16)
    for kind in range(3):
        for g in range(ATT_GROUPS):
            o = kind * ATT_W + g * ATT_GW
            dil = ATT_DILATIONS[g]
            dst = lowp[kind * ATT_GROUPS + g]
            r = _dot(h, w_ref[:, o:o + ATT_GW])
            if dil == 1:
                dst[0] = r.astype(dst.dtype)
            else:
                for hh in range(ATT_HPG):
                    res[hh] = r[:, hh * ATT_HD:(hh + 1) * ATT_HD]
                for rr in range(dil):
                    for hh in range(ATT_HPG):
                        dst[rr, :, hh * ATT_HD:(hh + 1) * ATT_HD] = (
                            res[hh, pl.ds(rr, tm // dil, stride=dil), :].astype(dst.dtype))
            if kind > 0:
                cache = full[(kind - 1) * ATT_GROUPS + g]
                rows = tm if keeps[g] == seq else keeps[g]
                for hh in range(ATT_HPG):
                    cache[:, hh, :] = r[tm - rows:, hh * ATT_HD:(hh + 1) * ATT_HD]


def _inproj_att_prompt(x2d, ln, w, tm, seq):
    n = x2d.shape[0]
    batch = n // seq
    tiles = seq // tm
    keeps = tuple(min(wd, seq) for wd in ATT_WINDOWS)
    assert all(kp <= tm or kp == seq for kp in keeps) and seq % tm == 0
    out_specs, out_shape = [], []
    for _ in range(3):
        for g in range(ATT_GROUPS):
            dil = ATT_DILATIONS[g]
            out_specs.append(pl.BlockSpec((None, dil, tm // dil, ATT_GW), lambda i: (i // tiles, 0, i % tiles, 0)))
            out_shape.append(jax.ShapeDtypeStruct((batch, dil, seq // dil, ATT_GW), BF16))
    for _ in range(2):
        for g in range(ATT_GROUPS):
            if keeps[g] == seq:
                idx = lambda i: (i // tiles, i % tiles, 0, 0)
                rows = tm
            else:
                idx = lambda i: (i // tiles, 0, 0, 0)
                rows = keeps[g]
            out_specs.append(pl.BlockSpec((None, rows, ATT_HPG, ATT_HD), idx))
            out_shape.append(jax.ShapeDtypeStruct((batch, keeps[g], ATT_HPG, ATT_HD), F32))
    return pl.pallas_call(
        functools.partial(_inproj_att_body, tm=tm, keeps=keeps, seq=seq),
        grid=(n // tm,),
        in_specs=[pl.BlockSpec((tm, D_MODEL), lambda i: (i, 0)), _resident((1, D_MODEL)),
                  _resident((D_MODEL, 3 * ATT_W))],
        out_specs=out_specs,
        out_shape=out_shape,
        scratch_shapes=[pltpu.VMEM((ATT_HPG, tm, ATT_HD), F32)],
        compiler_params=_params(1),
        name="inproj_att",
    )(x2d, ln, w)


def _inproj_att_sample_body(x_ref, ln_ref, w_ref, o_ref):
    h = _rms(x_ref[...], ln_ref[...]).astype(BF16)
    for c in range(3 * ATT_GROUPS):
        o_ref[:, c * ATT_GW:(c + 1) * ATT_GW] = _dot(h, w_ref[:, c * ATT_GW:(c + 1) * ATT_GW])


def _inproj_att_sample(x2d, ln, w):
    n = x2d.shape[0]
    return pl.pallas_call(
        _inproj_att_sample_body,
        grid=(1,),
        in_specs=[pl.BlockSpec((n, D_MODEL), lambda i: (0, 0)), _resident((1, D_MODEL)),
                  _resident((D_MODEL, 3 * ATT_W))],
        out_specs=pl.BlockSpec((n, 3 * ATT_W), lambda i: (0, 0)),
        out_shape=jax.ShapeDtypeStruct((n, 3 * ATT_W), F32),
        compiler_params=_params(1),
        name="inproj_att_sample",
    )(x2d, ln, w)


def _gn_swish(o, gate, gn):
    mu = jnp.mean(o, axis=-1, keepdims=True)
    d = o - mu
    var = jnp.mean(d * d, axis=-1, keepdims=True)
    on = d * lax.rsqrt(var + NORM_EPS) * gn
    return gate * _sigmoid(gate) * on


def _retention_body(q_ref, k_ref, v_ref, g_ref, gn_ref, dm_ref, qdec_ref, kdec_ref, gc_ref,
                    out_ref, st_ref, state, *, n_chunks):
    c = RET_CHUNK

    @pl.when(pl.program_id(1) == 0)
    def _():
        state[...] = jnp.zeros_like(state)

    lane = lax.broadcasted_iota(jnp.int32, (c, 2 * RET_DK), 1)
    head0 = lane < RET_DK

    def chunk(ci, carry):
        rows = pl.ds(pl.multiple_of(ci * c, c), c)
        for p in range(RET_PAIRS):
            q2 = q_ref[rows, p * 2 * RET_DK:(p + 1) * 2 * RET_DK]
            k2 = k_ref[rows, p * 2 * RET_DK:(p + 1) * 2 * RET_DK]
            v2 = v_ref[rows, p * 2 * RET_DV:(p + 1) * 2 * RET_DV]
            pst = state[p]
            pst_lo = pst.astype(BF16)
            zero = jnp.zeros_like(q2)
            for hh in range(2):
                h = 2 * p + hh
                qm = jnp.where(head0 if hh == 0 else jnp.logical_not(head0), q2, zero)
                vh = v2[:, hh * RET_DV:(hh + 1) * RET_DV]
                s = _dot_nt(qm, k2) * dm_ref[h]
                o = _dot(s.astype(BF16), vh)
                o = o + _dot(qm, pst_lo[:, hh * RET_DV:(hh + 1) * RET_DV]) * qdec_ref[h]
                gate = g_ref[rows, h * RET_DV:(h + 1) * RET_DV].astype(F32)
                res = _gn_swish(o, gate, gn_ref[:, h * RET_DV:(h + 1) * RET_DV])
                out_ref[rows, h * RET_DV:(h + 1) * RET_DV] = res.astype(out_ref.dtype)
            kd = (k2.astype(F32) * kdec_ref[p]).astype(BF16)
            state[p] = pst * gc_ref[p] + _dot_tn(kd, v2)
        return carry

    lax.fori_loop(0, n_chunks, chunk, 0)
    for p in range(RET_PAIRS):
        pst = state[p]
        for hh in range(2):
            st_ref[2 * p + hh] = pst[hh * RET_DK:(hh + 1) * RET_DK, hh * RET_DV:(hh + 1) * RET_DV]


def _retention_prompt(q, k, v, g, gn, batch, seq, ts):
    n = q.shape[0]
    steps = seq // ts
    dmask, qdec, kdec, gc = _ret_tables()
    row = lambda width: pl.BlockSpec((ts, width), lambda b, s: (b * steps + s, 0))
    const = lambda a: pl.BlockSpec(a.shape, lambda b, s: (0,) * a.ndim)
    return pl.pallas_call(
        functools.partial(_retention_body, n_chunks=ts // RET_CHUNK),
        grid=(batch, steps),
        in_specs=[row(RET_QK_W), row(RET_QK_W), row(RET_V_W), row(RET_V_W), const(gn),
                  const(dmask), const(qdec), const(kdec), const(gc)],
        out_specs=[row(RET_V_W),
                   pl.BlockSpec((None, RET_HEADS, RET_DK, RET_DV), lambda b, s: (b, 0, 0, 0))],
        out_shape=[jax.ShapeDtypeStruct((n, RET_V_W), BF16),
                   jax.ShapeDtypeStruct((batch, RET_HEADS, RET_DK, RET_DV), F32)],
        scratch_shapes=[pltpu.VMEM((RET_PAIRS, 2 * RET_DK, 2 * RET_DV), F32)],
        compiler_params=_params(2),
        name="retention",
    )(q, k, v, g, gn, dmask, qdec, kdec, gc)


def _attn_body(q_ref, k_ref, v_ref, tb_ref, o_ref, lse_ref, *acc, dil, streams, n_blocks, n_steps):
    nk = ATT_NK
    scale = ATT_HD ** -0.5
    lane = lax.broadcasted_iota(jnp.int32, (nk, LANES), 1)
    step = pl.program_id(1)

    def block(rs, n, n_prev, sel):
        rows = pl.ds(pl.multiple_of(n * nk, nk), nk)
        q = q_ref[rs, rows, :]
        k_cur = k_ref[rs, rows, :]
        v_cur = v_ref[rs, rows, :]
        if n_blocks > 1:
            prev_rows = pl.ds(pl.multiple_of(n_prev * nk, nk), nk)
            k_all = jnp.concatenate([k_ref[rs, prev_rows, :], k_cur], axis=0)
            v_all = jnp.concatenate([v_ref[rs, prev_rows, :], v_cur], axis=0)
        else:
            k_all, v_all = k_cur, v_cur
        if dil == 1:
            pos_rows = rows
        else:
            pos_rows = pl.ds(n * (nk * dil) + step * streams + rs, nk, stride=dil)
        lse_tile = jnp.zeros((nk, LANES), F32)
        for h in range(ATT_HPG):
            hc = slice(h * ATT_HD, (h + 1) * ATT_HD)
            s = _dot_nt(q[:, hc], k_all[:, hc]) * scale + tb_ref[sel, h]
            m = jnp.max(s, axis=-1, keepdims=True)
            p = jnp.exp(s - m)
            den = jnp.sum(p, axis=-1, keepdims=True)
            o = _dot(p.astype(BF16), v_all[:, hc]) / den
            if dil == 1:
                o_ref[pos_rows, hc] = o.astype(o_ref.dtype)
            else:
                acc[0][h, pos_rows, :] = o
            lse_tile = jnp.where(lane == h, m + jnp.log(den), lse_tile)
        lse_ref[pos_rows, :] = lse_tile

    for rs in range(streams):
        if n_blocks == 1:
            block(rs, 0, 0, 0)
        else:
            def loop(n, carry, rs=rs):
                block(rs, n, jnp.maximum(n - 1, 0), jnp.minimum(n, 1))
                return carry
            lax.fori_loop(0, n_blocks, loop, 0)

    if dil > 1:
        @pl.when(step == n_steps - 1)
        def _():
            for h in range(ATT_HPG):
                o_ref[:, h * ATT_HD:(h + 1) * ATT_HD] = acc[0][h].astype(o_ref.dtype)


def _attention_prompt(aq, ak, av, tb, g):
    batch, dil, length, _ = aq.shape
    seq = dil * length
    n_blocks = length // ATT_NK
    streams = min(dil, 4)
    steps = dil // streams
    blk = pl.BlockSpec((None, streams, length, ATT_GW), lambda b, s: (b, s, 0, 0))
    o, lse = pl.pallas_call(
        functools.partial(_attn_body, dil=dil, streams=streams, n_blocks=n_blocks, n_steps=steps),
        grid=(batch, steps),
        in_specs=[blk, blk, blk, pl.BlockSpec(tb.shape, lambda b, s: (0, 0, 0, 0))],
        out_specs=[pl.BlockSpec((None, seq, ATT_GW), lambda b, s: (b, 0, 0)),
                   pl.BlockSpec((None, seq, LANES), lambda b, s: (b, 0, 0))],
        out_shape=[jax.ShapeDtypeStruct((batch, seq, ATT_GW), BF16),
                   jax.ShapeDtypeStruct((batch, seq, LANES), F32)],
        scratch_shapes=[pltpu.VMEM((ATT_HPG, seq, ATT_HD), F32)] if dil > 1 else [],
        compiler_params=_params(2),
        name="attention_g%d" % g,
    )(aq, ak, av, tb)
    return o.reshape(batch * seq, ATT_GW), lse.reshape(batch * seq, LANES)


def _tail_body(*refs, combine):
    if combine:
        (ret_ref, o0_ref, o1_ref, o2_ref, l0_ref, l1_ref, l2_ref, gr_ref, ga_ref, x_ref, ple_ref,
         wrb_ref, wab_ref, wo_ref, wu_ref, wd_ref, wpl_ref, wpg_ref, ln2_ref, lnf_ref, y_ref) = refs
        lses = [l0_ref[...], l1_ref[...], l2_ref[...]]
        outs = [o0_ref, o1_ref, o2_ref]
        parts = []
        for h in range(ATT_HPG):
            lh = [l[:, h:h + 1] for l in lses]
            mx = jnp.maximum(jnp.maximum(lh[0], lh[1]), lh[2])
            e = [jnp.exp(l - mx) for l in lh]
            tot = e[0] + e[1] + e[2]
            acc = None
            for g in range(ATT_GROUPS):
                term = (e[g] / tot) * outs[g][:, h * ATT_HD:(h + 1) * ATT_HD].astype(F32)
                acc = term if acc is None else acc + term
            parts.append(acc)
        att = jnp.concatenate(parts, axis=1).astype(BF16)
    else:
        (ret_ref, att_ref, gr_ref, ga_ref, x_ref, ple_ref,
         wrb_ref, wab_ref, wo_ref, wu_ref, wd_ref, wpl_ref, wpg_ref, ln2_ref, lnf_ref, y_ref) = refs
        att = att_ref[...].astype(BF16)
    a = _dot(ret_ref[...].astype(BF16), wrb_ref[...])
    b = _dot(att, wab_ref[...])
    mixed = _sigmoid(gr_ref[...].astype(F32)) * a + _sigmoid(ga_ref[...].astype(F32)) * b
    x1 = x_ref[...] + _dot(mixed.astype(BF16), wo_ref[...])
    h2 = _rms(x1, ln2_ref[...]).astype(BF16)
    ff_chunk = D_MODEL
    acc = None
    for c in range(D_FF // ff_chunk):
        u = _dot(h2, wu_ref[:, c * ff_chunk:(c + 1) * ff_chunk])
        r = jnp.maximum(u, 0.0)
        t = _dot((r * r).astype(BF16), wd_ref[c * ff_chunk:(c + 1) * ff_chunk, :])
        acc = t if acc is None else acc + t
    x2 = x1 + acc
    gate = _sigmoid(_dot(x2.astype(BF16), wpg_ref[...]))
    x3 = x2 + gate * _dot(ple_ref[...].astype(BF16), wpl_ref[...])
    y_ref[...] = _rms(x3, lnf_ref[...])


def _tail(acts, weights, tm, combine):
    n = acts[0].shape[0]
    row = lambda a: pl.BlockSpec((tm, a.shape[1]), lambda i: (i, 0))
    return pl.pallas_call(
        functools.partial(_tail_body, combine=combine),
        grid=(n // tm,),
        in_specs=[row(a) for a in acts] + [_resident(w.shape) for w in weights],
        out_specs=pl.BlockSpec((tm, D_MODEL), lambda i: (i, 0)),
        out_shape=jax.ShapeDtypeStruct((n, D_MODEL), F32),
        compiler_params=_params(1),
        name="tail" if combine else "tail_sample",
    )(*acts, *weights)


def _decode_body(q_ref, k_ref, v_ref, g_ref, gn_ref, st_ref, gam_ref,
                 qkv_ref, bias_ref,
                 ck0_ref, cv0_ref, ck1_ref, cv1_ref, ck2_ref, cv2_ref,
                 ret_ref, att_ref, nst_ref):
    sub = lax.broadcasted_iota(jnp.int32, (8, 2 * RET_DK), 0)
    lane = lax.broadcasted_iota(jnp.int32, (8, 2 * RET_DK), 1)
    row0 = sub == 0
    srow = lax.broadcasted_iota(jnp.int32, (2 * RET_DK, RET_DV), 0)
    for p in range(RET_PAIRS):
        pc = slice(p * 2 * RET_DK, (p + 1) * 2 * RET_DK)
        q2 = jnp.where(row0, jnp.broadcast_to(q_ref[:, pc], (8, 2 * RET_DK)), 0.0)
        k2 = jnp.where(row0, jnp.broadcast_to(k_ref[:, pc], (8, 2 * RET_DK)), 0.0)
        pst = st_ref[p]
        gam = gam_ref[p]
        outer = []
        for hh in range(2):
            h = 2 * p + hh
            hsel = (lane < RET_DK) if hh == 0 else (lane >= RET_DK)
            qm = jnp.where(hsel, q2, 0.0)
            km = jnp.where(hsel, k2, 0.0)
            vh = v_ref[:, h * RET_DV:(h + 1) * RET_DV]
            v8 = jnp.where(row0[:, :RET_DV], jnp.broadcast_to(vh, (8, RET_DV)), 0.0)
            cross = _dot(qm.astype(BF16), (pst * gam).astype(BF16))[0:1, :]
            qk = jnp.sum(qm[0:1, :] * km[0:1, :], axis=-1, keepdims=True)
            o = cross + qk * vh
            res = _gn_swish(o, g_ref[:, h * RET_DV:(h + 1) * RET_DV], gn_ref[:, h * RET_DV:(h + 1) * RET_DV])
            ret_ref[:, h * RET_DV:(h + 1) * RET_DV] = res
            outer.append(_dot_tn(k2.astype(BF16), v8.astype(BF16)))
        nst_ref[p] = pst * gam + jnp.where(srow < RET_DK, outer[0], outer[1])
    scale = ATT_HD ** -0.5
    caches = ((ck0_ref, cv0_ref), (ck1_ref, cv1_ref), (ck2_ref, cv2_ref))
    slot = lax.broadcasted_iota(jnp.int32, (ATT_NK, ATT_HD), 0)
    is_new = slot == 0
    for h in range(ATT_HPG):
        o_g, lse_g = [], []
        for g in range(ATT_GROUPS):
            ck_ref, cv_ref = caches[g]
            c0 = g * ATT_GW + h * ATT_HD
            qh = qkv_ref[:, c0:c0 + ATT_HD]
            kk = jnp.where(is_new, qkv_ref[:, ATT_W + c0:ATT_W + c0 + ATT_HD], ck_ref[:, h, :])
            vv = jnp.where(is_new, qkv_ref[:, 2 * ATT_W + c0:2 * ATT_W + c0 + ATT_HD], cv_ref[:, h, :])
            s = jnp.sum(kk * qh, axis=-1, keepdims=True) * scale + bias_ref[g * ATT_HPG + h]
            m = jnp.max(s, axis=0, keepdims=True)
            pr = jnp.exp(s - m)
            den = jnp.sum(pr, axis=0, keepdims=True)
            o_g.append(jnp.sum(pr * vv, axis=0, keepdims=True) / den)
            lse_g.append(m + jnp.log(den))
        mx = jnp.maximum(jnp.maximum(lse_g[0], lse_g[1]), lse_g[2])
        e = [jnp.exp(l - mx) for l in lse_g]
        tot = e[0] + e[1] + e[2]
        att_ref[:, h * ATT_HD:(h + 1) * ATT_HD] = ((e[0] / tot) * o_g[0] + (e[1] / tot) * o_g[1]
                                                   + (e[2] / tot) * o_g[2])


def _decode(q, k, v, g, gn, state, qkv, bias_cols, caches):
    batch = q.shape[0]
    lg = _ret_log_decay()
    gam = jnp.broadcast_to(jnp.repeat(jnp.exp(lg), RET_DK).reshape(RET_PAIRS, 2 * RET_DK, 1),
                           (RET_PAIRS, 2 * RET_DK, RET_DV))
    vec = lambda a: a.reshape(batch, 1, a.shape[1])
    vspec = lambda width: pl.BlockSpec((None, 1, width), lambda b: (b, 0, 0))
    const = lambda a: pl.BlockSpec(a.shape, lambda b: (0,) * a.ndim)
    st_spec = pl.BlockSpec((None, RET_PAIRS, 2 * RET_DK, RET_DV), lambda b: (b, 0, 0, 0))
    cache_in, cache_specs = [], []
    for gi in range(ATT_GROUPS):
        dil = ATT_DILATIONS[gi]
        for c in caches[2 * gi:2 * gi + 2]:
            cache_in.append(c.reshape(batch, ATT_NK, dil, ATT_HPG, ATT_HD))
            cache_specs.append(pl.BlockSpec((None, ATT_NK, None, ATT_HPG, ATT_HD), lambda b: (b, 0, 0, 0, 0)))
    st_pairs = state.reshape(batch, RET_PAIRS, 2 * RET_DK, RET_DV)
    ret, att, nst = pl.pallas_call(
        _decode_body,
        grid=(batch,),
        in_specs=[vspec(RET_QK_W), vspec(RET_QK_W), vspec(RET_V_W), vspec(RET_V_W), const(gn), st_spec, const(gam),
                  vspec(3 * ATT_W), const(bias_cols)] + cache_specs,
        out_specs=[vspec(RET_V_W), vspec(ATT_GW), st_spec],
        out_shape=[jax.ShapeDtypeStruct((batch, 1, RET_V_W), F32),
                   jax.ShapeDtypeStruct((batch, 1, ATT_GW), F32),
                   jax.ShapeDtypeStruct(st_pairs.shape, F32)],
        compiler_params=_params(1),
        name="decode",
    )(vec(q), vec(k), vec(v), vec(g), gn, st_pairs, gam, vec(qkv), bias_cols, *cache_in)
    return ret.reshape(batch, RET_V_W), att.reshape(batch, ATT_GW), nst.reshape(state.shape)


def _shift_plan(widths, batch):
    chunks = []
    counters = {}
    for i, width in enumerate(widths):
        if width <= SHIFT_ROWS:
            cls, bb, step = 0, SHIFT_SMALL_BATCH, width
        else:
            cls, bb, step = 1, 1, SHIFT_ROWS
        for b0 in range(0, batch, bb):
            for w0 in range(0, width - 1, step):
                wc = min(step, width - 1 - w0)
                cnt = counters.get(cls, 0)
                counters[cls] = cnt + 1
                chunks.append((i, b0, bb, w0, wc, cls, cnt % SHIFT_BUFFERS))
    return chunks


def _shift_body(*refs, n, plan):
    new_ref = refs[0]
    old = refs[1:1 + n]
    out = refs[1 + n:1 + 2 * n]
    rows = refs[1 + 2 * n:1 + 3 * n]
    bufs = refs[1 + 3 * n:3 + 3 * n]
    sem_row, sem_in, sem_out = refs[3 + 3 * n:]
    row_copies = []
    for i in range(n):
        width = old[i].shape[1]
        g, kind = divmod(i, 2)
        for h in range(ATT_HPG):
            c0 = (kind + 1) * ATT_W + g * ATT_GW + h * ATT_HD
            rows[i][:, h, :] = new_ref[:, c0:c0 + ATT_HD]
        cp = pltpu.make_async_copy(rows[i], out[i].at[:, width - 1], sem_row.at[i])
        cp.start()
        row_copies.append(cp)

    def copies(chunk):
        i, b0, bb, w0, wc, cls, slot = chunk
        stage = bufs[cls].at[slot, :, pl.ds(0, wc)]
        sem_idx = cls * SHIFT_BUFFERS + slot
        cin = pltpu.make_async_copy(old[i].at[pl.ds(b0, bb), pl.ds(w0 + 1, wc)], stage, sem_in.at[sem_idx])
        cout = pltpu.make_async_copy(stage, out[i].at[pl.ds(b0, bb), pl.ds(w0, wc)], sem_out.at[sem_idx])
        return cin, cout

    pairs = [copies(c) for c in plan]
    busy = {}

    def start_in(j):
        key = plan[j][5:7]
        if key in busy:
            busy.pop(key).wait()
        pairs[j][0].start()

    ahead = SHIFT_BUFFERS - 1
    for j in range(min(ahead, len(plan))):
        start_in(j)
    for j in range(len(plan)):
        if j + ahead < len(plan):
            start_in(j + ahead)
        pairs[j][0].wait()
        pairs[j][1].start()
        busy[plan[j][5:7]] = pairs[j][1]
    for cp in busy.values():
        cp.wait()
    for cp in row_copies:
        cp.wait()


def _shift_caches(new_qkv, caches):
    n = len(caches)
    batch = new_qkv.shape[0]
    widths = [c.shape[1] for c in caches]
    plan = _shift_plan(widths, batch)
    anyspec = pl.BlockSpec(memory_space=pl.ANY)
    small = max(wd - 1 for wd in widths if wd <= SHIFT_ROWS)
    return pl.pallas_call(
        functools.partial(_shift_body, n=n, plan=plan),
        in_specs=[pl.BlockSpec(memory_space=pltpu.VMEM)] + [anyspec] * n,
        out_specs=[anyspec] * n,
        out_shape=[jax.ShapeDtypeStruct(c.shape, c.dtype) for c in caches],
        scratch_shapes=[pltpu.VMEM((batch, ATT_HPG, ATT_HD), F32) for _ in range(n)]
        + [pltpu.VMEM((SHIFT_BUFFERS, SHIFT_SMALL_BATCH, small, ATT_HPG, ATT_HD), F32),
           pltpu.VMEM((SHIFT_BUFFERS, 1, SHIFT_ROWS, ATT_HPG, ATT_HD), F32),
           pltpu.SemaphoreType.DMA((n,)),
           pltpu.SemaphoreType.DMA((2 * SHIFT_BUFFERS,)),
           pltpu.SemaphoreType.DMA((2 * SHIFT_BUFFERS,))],
        compiler_params=pltpu.CompilerParams(vmem_limit_bytes=VMEM_LIMIT_V7X),
        name="cache_shift",
    )(new_qkv, *caches)


def kernel(x_prompt, x_sample, state_ret, cache_k_w128, cache_v_w128, cache_k_w512, cache_v_w512,
           cache_k_w2048, cache_v_w2048, p_prompt, p_sample, ln1_g, w_in, ret_gn_g, w_ret_br, w_att_br,
           w_out, ln2_g, w_up, w_down, w_ple, w_ple_gate, rel_bias, lnf_g):
    depth = w_in.shape[0]
    assert depth == 1
    batch, seq, _ = x_prompt.shape
    dec_batch, dec_seq, _ = x_sample.shape
    assert dec_seq == 1
    past_len = 16384
    l = 0
    ln1 = ln1_g[l][None, :]
    ln2 = ln2_g[l][None, :]
    lnf = lnf_g[None, :]
    gn = ret_gn_g[l][None, :]
    w_in_l = w_in[l]
    w_ret = jnp.concatenate([w_in_l[:, :COL_ATT], w_in_l[:, COL_GATE:]], axis=1).astype(BF16)
    w_att = w_in_l[:, COL_ATT:COL_GATE].astype(BF16)
    tail_w = (w_ret_br[l].astype(BF16), w_att_br[l].astype(BF16), w_out[l].astype(BF16), w_up[l].astype(BF16),
              w_down[l].astype(BF16), w_ple[l].astype(BF16), w_ple_gate[l].astype(BF16), ln2, lnf)
    biases = [_group_bias(rel_bias, g) for g in range(ATT_GROUPS)]

    tm = 512
    xp = x_prompt.reshape(batch * seq, D_MODEL)
    cos_p, sin_p = _rope_tables(jnp.arange(seq, dtype=jnp.int32))
    rq, rk, rv, rg, gr, ga = _inproj_ret(xp, ln1, w_ret, cos_p, sin_p, tm, BF16)
    att_o = _inproj_att_prompt(xp, ln1, w_att, tm, seq)
    aqs, aks, avs = att_o[0:3], att_o[3:6], att_o[6:9]
    kfull, vfull = att_o[9:12], att_o[12:15]
    ret_out, st_p = _retention_prompt(rq, rk, rv, rg, gn, batch, seq, 512)
    outs, lses = [], []
    for g in range(ATT_GROUPS):
        n_blocks = seq // ATT_DILATIONS[g] // ATT_NK
        tb = _band_tables(biases[g], n_blocks > 1)
        o, lse = _attention_prompt(aqs[g], aks[g], avs[g], tb, g)
        outs.append(o)
        lses.append(lse)
    ple_p = p_prompt[l].reshape(batch * seq, D_PLE)
    y_p = _tail([ret_out] + outs + lses + [gr, ga, xp, ple_p], tail_w, tm, True)
    y_prompt = y_p.reshape(batch, seq, D_MODEL)
    new_state_p = st_p[None]
    kv_p = []
    for g in range(ATT_GROUPS):
        kv_p.append(kfull[g][None])
        kv_p.append(vfull[g][None])

    xs = x_sample.reshape(dec_batch, D_MODEL)
    cos_s, sin_s = _rope_tables(jnp.full((dec_batch,), past_len, dtype=jnp.int32))
    sq, sk, sv, sg, sgr, sga = _inproj_ret(xs, ln1, w_ret, cos_s, sin_s, dec_batch, F32)
    s_qkv = _inproj_att_sample(xs, ln1, w_att)
    caches = (cache_k_w128[l], cache_v_w128[l], cache_k_w512[l], cache_v_w512[l],
              cache_k_w2048[l], cache_v_w2048[l])
    slot_off = jnp.concatenate([jnp.zeros((1,), jnp.int32), ATT_NK - jnp.arange(1, ATT_NK)])
    bias_cols = jnp.concatenate([b[:, slot_off] for b in biases], axis=0)[:, :, None]
    s_ret, s_attn, new_st = _decode(sq, sk, sv, sg, gn, state_ret[l], s_qkv, bias_cols, caches)
    ple_s = p_sample[l].reshape(dec_batch, D_PLE)
    y_s = _tail([s_ret, s_attn, sgr, sga, xs, ple_s], tail_w, dec_batch, False)
    y_sample = y_s.reshape(dec_batch, 1, D_MODEL)
    new_caches = _shift_caches(s_qkv, caches)
    kv_s = [c[None] for c in new_caches]

    return (y_prompt, y_sample, new_state_p, *kv_p, new_st[None], *kv_s)
```

```python
import functools
import math

import jax
import jax.numpy as jnp
from jax import lax
from jax.experimental import pallas as pl
from jax.experimental.pallas import tpu as pltpu

F32 = jnp.float32
BF16 = jnp.bfloat16

D_MODEL = 1024
RET_HEADS = 8
RET_DK = 64
RET_DV = 128
RET_PAIRS = RET_HEADS // 2
RET_CHUNK = 128
ROPE_BASE = 10000.0
ATT_WINDOWS = (128, 512, 2048)
ATT_DILATIONS = (1, 4, 16)
ATT_GROUPS = 3
ATT_HPG = 4
ATT_HD = 128
ATT_NK = 128
ATT_GW = ATT_HPG * ATT_HD
REL_BUCKETS = 32
REL_MAX_DIST = 2048
D_FF = 4 * D_MODEL
D_PLE = 256
NORM_EPS = 1e-6
RET_QK_W = RET_HEADS * RET_DK
RET_V_W = RET_HEADS * RET_DV
ATT_W = ATT_GROUPS * ATT_GW
COL_RET = 0
COL_ATT = 2 * RET_QK_W + 2 * RET_V_W
COL_GATE = COL_ATT + 3 * ATT_W
N_IN = COL_GATE + 2 * D_MODEL

VMEM_LIMIT_V7X = 56 * 1024 * 1024
LANES = 128
TM_INPROJ = 512
TS_RETENTION = 512


def _dot(a, b):
    return jnp.dot(a, b, preferred_element_type=F32)


def _dot_nt(a, b):
    return lax.dot_general(a, b, (((1,), (1,)), ((), ())), preferred_element_type=F32)


def _dot_tn(a, b):
    return lax.dot_general(a, b, (((0,), (0,)), ((), ())), preferred_element_type=F32)


def _rms(x, g):
    return x * lax.rsqrt(jnp.mean(x * x, axis=-1, keepdims=True) + NORM_EPS) * g


def _sigmoid(x):
    return 1.0 / (1.0 + jnp.exp(-x))


def _resident(shape):
    return pl.BlockSpec(shape, lambda *_: (0,) * len(shape), pipeline_mode=pl.Buffered(1))


def _params(n_axes):
    return pltpu.CompilerParams(dimension_semantics=("arbitrary",) * n_axes,
                                vmem_limit_bytes=VMEM_LIMIT_V7X)


def _rope_tables(pos):
    half = RET_DK // 2
    inv = ROPE_BASE ** (-jnp.arange(half, dtype=F32) / half)
    ang = pos.astype(F32)[:, None] * inv[None, :]
    cos = jnp.cos(ang)
    sin = jnp.sin(ang)
    cos_t = jnp.tile(cos, (1, LANES // half))
    sin_t = jnp.tile(jnp.concatenate([-sin, sin], axis=1), (1, LANES // RET_DK))
    return cos_t, sin_t


def _ret_log_decay():
    return jnp.log1p(-jnp.exp2(-5.0 - jnp.arange(RET_HEADS, dtype=F32)))


def _ret_tables():
    c = RET_CHUNK
    lg = _ret_log_decay()
    i = jnp.arange(c, dtype=F32)
    diff = i[:, None] - i[None, :]
    dmask = jnp.where(diff[None] >= 0, jnp.exp(jnp.maximum(diff, 0.0)[None] * lg[:, None, None]), 0.0)
    q_decay = jnp.exp((i + 1.0)[:, None] * lg[None, :])
    k_decay = jnp.exp((c - 1.0 - i)[:, None] * lg[None, :])
    qdec = jnp.broadcast_to(q_decay.T[:, :, None], (RET_HEADS, c, RET_DV))
    kdec = jnp.repeat(k_decay, RET_DK, axis=1).reshape(c, RET_PAIRS, 2 * RET_DK).transpose(1, 0, 2)
    gc = jnp.repeat(jnp.exp(c * lg), RET_DV).reshape(RET_PAIRS, 1, 2 * RET_DV)
    return dmask, qdec, kdec, gc


def _rel_bucket(dist):
    max_exact = REL_BUCKETS // 2
    d = dist.astype(jnp.int32)
    log_ratio = jnp.log(jnp.maximum(d, 1).astype(F32) / max_exact) / math.log(REL_MAX_DIST / max_exact)
    large = max_exact + (log_ratio * (REL_BUCKETS - max_exact)).astype(jnp.int32)
    large = jnp.minimum(large, REL_BUCKETS - 1)
    return jnp.where(d < max_exact, d, large)


def _group_bias(rel_bias, g):
    b = _rel_bucket(jnp.arange(ATT_NK) * ATT_DILATIONS[g])
    return rel_bias[b][:, g * ATT_HPG:(g + 1) * ATT_HPG].T.astype(F32)


def _band_tables(bias, with_prev):
    nk = ATT_NK
    hpg = bias.shape[0]
    neg = lambda n: jnp.full((hpg, n), -jnp.inf, F32)
    period = 3 * nk
    gvec = jnp.concatenate([neg(nk - 1), bias, neg(period - 2 * nk + 1)], axis=1)
    flat = jnp.tile(gvec, (1, nk + 1))[:, :nk * (period + 1)]
    hankel = flat.reshape(hpg, nk, period + 1)[:, :, :2 * nk]
    rest = hankel[:, :, ::-1]
    cur = rest[:, :, nk:]
    if with_prev:
        first = jnp.concatenate([jnp.full((hpg, nk, nk), -jnp.inf, F32), cur], axis=2)
        return jnp.stack([first, rest])
    return cur[None]


def _inproj_ret_body(x_ref, ln_ref, w_ref, cos_ref, sin_ref, q_ref, k_ref, v_ref, g_ref, gr_ref, ga_ref):
    x = x_ref[...]
    h = _rms(x, ln_ref[...]).astype(BF16)
    cos = cos_ref[...]
    sin = sin_ref[...]
    qk = _dot(h, w_ref[:, 0:2 * RET_QK_W])
    lane = lax.broadcasted_iota(jnp.int32, cos.shape, 1)
    first_half = (lane % RET_DK) < (RET_DK // 2)
    n_q = RET_QK_W // LANES
    for c in range(2 * n_q):
        xc = qk[:, c * LANES:(c + 1) * LANES]
        swapped = jnp.where(first_half, pltpu.roll(xc, LANES - RET_DK // 2, 1), pltpu.roll(xc, RET_DK // 2, 1))
        r = xc * cos + swapped * sin
        if c < n_q:
            q_ref[:, c * LANES:(c + 1) * LANES] = r.astype(q_ref.dtype)
        else:
            k_ref[:, (c - n_q) * LANES:(c - n_q + 1) * LANES] = (r * (RET_DK ** -0.5)).astype(k_ref.dtype)
    o = 2 * RET_QK_W
    v_ref[...] = _dot(h, w_ref[:, o:o + RET_V_W]).astype(v_ref.dtype)
    o += RET_V_W
    g_ref[...] = _dot(h, w_ref[:, o:o + RET_V_W]).astype(g_ref.dtype)
    o += RET_V_W
    gr_ref[...] = _dot(h, w_ref[:, o:o + D_MODEL]).astype(gr_ref.dtype)
    o += D_MODEL
    ga_ref[...] = _dot(h, w_ref[:, o:o + D_MODEL]).astype(ga_ref.dtype)


def _inproj_ret(x2d, ln, w, cos_t, sin_t, tm, out_dtype):
    n = x2d.shape[0]
    t_rows = cos_t.shape[0]
    nt = t_rows // tm
    wcols = w.shape[1]
    row = lambda width: pl.BlockSpec((tm, width), lambda i: (i, 0))
    tab = pl.BlockSpec((tm, LANES), lambda i: (i % nt, 0))
    widths = (RET_QK_W, RET_QK_W, RET_V_W, RET_V_W, D_MODEL, D_MODEL)
    return pl.pallas_call(
        _inproj_ret_body,
        grid=(n // tm,),
        in_specs=[row(D_MODEL), _resident((1, D_MODEL)), _resident((D_MODEL, wcols)), tab, tab],
        out_specs=[row(wd) for wd in widths],
        out_shape=[jax.ShapeDtypeStruct((n, wd), out_dtype) for wd in widths],
        compiler_params=_params(1),
        name="inproj_ret",
    )(x2d, ln, w, cos_t, sin_t)


def _inproj_att_body(x_ref, ln_ref, w_ref, *refs, tm, keeps, seq):
    lowp = refs[:3 * ATT_GROUPS]
    full = refs[3 * ATT_GROUPS:5 * ATT_GROUPS]
    res = refs[5 * ATT_GROUPS]
    x = x_ref[...]
    h = _rms(x, ln_ref[...]).astype(BF16)
    slot = 0
    for kind in range(3):
        for g in range(ATT_GROUPS):
            o = kind * ATT_W + g * ATT_GW
            dil = ATT_DILATIONS[g]
            dst = lowp[kind * ATT_GROUPS + g]
            r = _dot(h, w_ref[:, o:o + ATT_GW])
            if dil == 1:
                dst[0] = r.astype(dst.dtype)
            else:
                for hh in range(ATT_HPG):
                    res[slot, hh] = r[:, hh * ATT_HD:(hh + 1) * ATT_HD]
                for rr in range(dil):
                    for hh in range(ATT_HPG):
                        dst[rr, :, hh * ATT_HD:(hh + 1) * ATT_HD] = (
                            res[slot, hh, pl.ds(rr, tm // dil, stride=dil), :].astype(dst.dtype))
                slot += 1
            if kind > 0:
                cache = full[(kind - 1) * ATT_GROUPS + g]
                rows = tm if keeps[g] == seq else keeps[g]
                for hh in range(ATT_HPG):
                    cache[pl.ds(hh, rows, stride=ATT_HPG), :] = r[tm - rows:, hh * ATT_HD:(hh + 1) * ATT_HD]


def _inproj_att_prompt(x2d, ln, w, tm, seq):
    n = x2d.shape[0]
    batch = n // seq
    tiles = seq // tm
    keeps = tuple(min(wd, seq) for wd in ATT_WINDOWS)
    assert all(kp <= tm or kp == seq for kp in keeps) and seq % tm == 0
    out_specs, out_shape = [], []
    for _ in range(3):
        for g in range(ATT_GROUPS):
            dil = ATT_DILATIONS[g]
            out_specs.append(pl.BlockSpec((None, dil, tm // dil, ATT_GW), lambda i: (i // tiles, 0, i % tiles, 0)))
            out_shape.append(jax.ShapeDtypeStruct((batch, dil, seq // dil, ATT_GW), BF16))
    for _ in range(2):
        for g in range(ATT_GROUPS):
            if keeps[g] == seq:
                idx = lambda i: (i // tiles, i % tiles, 0)
                rows = tm
            else:
                idx = lambda i: (i // tiles, 0, 0)
                rows = keeps[g]
            out_specs.append(pl.BlockSpec((None, rows * ATT_HPG, ATT_HD), idx))
            out_shape.append(jax.ShapeDtypeStruct((batch, keeps[g] * ATT_HPG, ATT_HD), F32))
    return pl.pallas_call(
        functools.partial(_inproj_att_body, tm=tm, keeps=keeps, seq=seq),
        grid=(n // tm,),
        in_specs=[pl.BlockSpec((tm, D_MODEL), lambda i: (i, 0)), _resident((1, D_MODEL)),
                  _resident((D_MODEL, 3 * ATT_W))],
        out_specs=out_specs,
        out_shape=out_shape,
        scratch_shapes=[pltpu.VMEM((3 * sum(d > 1 for d in ATT_DILATIONS), ATT_HPG, tm, ATT_HD), F32)],
        compiler_params=_params(1),
        name="inproj_att",
    )(x2d, ln, w)


def _inproj_att_sample_body(x_ref, ln_ref, w_ref, o_ref):
    h = _rms(x_ref[...], ln_ref[...]).astype(BF16)
    for c in range(3 * ATT_GROUPS):
        o_ref[:, c * ATT_GW:(c + 1) * ATT_GW] = _dot(h, w_ref[:, c * ATT_GW:(c + 1) * ATT_GW])


def _inproj_att_sample(x2d, ln, w):
    n = x2d.shape[0]
    return pl.pallas_call(
        _inproj_att_sample_body,
        grid=(1,),
        in_specs=[pl.BlockSpec((n, D_MODEL), lambda i: (0, 0)), _resident((1, D_MODEL)),
                  _resident((D_MODEL, 3 * ATT_W))],
        out_specs=pl.BlockSpec((n, 3 * ATT_W), lambda i: (0, 0)),
        out_shape=jax.ShapeDtypeStruct((n, 3 * ATT_W), F32),
        compiler_params=_params(1),
        name="inproj_att_sample",
    )(x2d, ln, w)


def _gn_swish(o, gate, gn):
    mu = jnp.mean(o, axis=-1, keepdims=True)
    d = o - mu
    var = jnp.mean(d * d, axis=-1, keepdims=True)
    on = d * lax.rsqrt(var + NORM_EPS) * gn
    return gate * _sigmoid(gate) * on


def _retention_body(q_ref, k_ref, v_ref, g_ref, gn_ref, dm_ref, qdec_ref, kdec_ref, gc_ref,
                    out_ref, st_ref, state, *, n_chunks):
    c = RET_CHUNK

    @pl.when(pl.program_id(1) == 0)
    def _():
        state[...] = jnp.zeros_like(state)

    lane = lax.broadcasted_iota(jnp.int32, (c, 2 * RET_DK), 1)
    head0 = lane < RET_DK

    def chunk(ci, carry):
        rows = pl.ds(pl.multiple_of(ci * c, c), c)
        for p in range(RET_PAIRS):
            q2 = q_ref[rows, p * 2 * RET_DK:(p + 1) * 2 * RET_DK]
            k2 = k_ref[rows, p * 2 * RET_DK:(p + 1) * 2 * RET_DK]
            v2 = v_ref[rows, p * 2 * RET_DV:(p + 1) * 2 * RET_DV]
            pst = state[p]
            pst_lo = pst.astype(BF16)
            zero = jnp.zeros_like(q2)
            for hh in range(2):
                h = 2 * p + hh
                qm = jnp.where(head0 if hh == 0 else jnp.logical_not(head0), q2, zero)
                vh = v2[:, hh * RET_DV:(hh + 1) * RET_DV]
                s = _dot_nt(qm, k2) * dm_ref[h]
                o = _dot(s.astype(BF16), vh)
                o = o + _dot(qm, pst_lo[:, hh * RET_DV:(hh + 1) * RET_DV]) * qdec_ref[h]
                gate = g_ref[rows, h * RET_DV:(h + 1) * RET_DV].astype(F32)
                res = _gn_swish(o, gate, gn_ref[:, h * RET_DV:(h + 1) * RET_DV])
                out_ref[rows, h * RET_DV:(h + 1) * RET_DV] = res.astype(out_ref.dtype)
            kd = (k2.astype(F32) * kdec_ref[p]).astype(BF16)
            state[p] = pst * gc_ref[p] + _dot_tn(kd, v2)
        return carry

    lax.fori_loop(0, n_chunks, chunk, 0)
    for p in range(RET_PAIRS):
        pst = state[p]
        for hh in range(2):
            st_ref[2 * p + hh] = pst[hh * RET_DK:(hh + 1) * RET_DK, hh * RET_DV:(hh + 1) * RET_DV]


def _retention_prompt(q, k, v, g, gn, batch, seq, ts):
    n = q.shape[0]
    steps = seq // ts
    dmask, qdec, kdec, gc = _ret_tables()
    row = lambda width: pl.BlockSpec((ts, width), lambda b, s: (b * steps + s, 0))
    const = lambda a: pl.BlockSpec(a.shape, lambda b, s: (0,) * a.ndim)
    return pl.pallas_call(
        functools.partial(_retention_body, n_chunks=ts // RET_CHUNK),
        grid=(batch, steps),
        in_specs=[row(RET_QK_W), row(RET_QK_W), row(RET_V_W), row(RET_V_W), const(gn),
                  const(dmask), const(qdec), const(kdec), const(gc)],
        out_specs=[row(RET_V_W),
                   pl.BlockSpec((None, RET_HEADS, RET_DK, RET_DV), lambda b, s: (b, 0, 0, 0))],
        out_shape=[jax.ShapeDtypeStruct((n, RET_V_W), BF16),
                   jax.ShapeDtypeStruct((batch, RET_HEADS, RET_DK, RET_DV), F32)],
        scratch_shapes=[pltpu.VMEM((RET_PAIRS, 2 * RET_DK, 2 * RET_DV), F32)],
        compiler_params=_params(2),
        name="retention",
    )(q, k, v, g, gn, dmask, qdec, kdec, gc)


def _attn_body(q_ref, k_ref, v_ref, tb_ref, o_ref, lse_ref, *acc, dil, streams, n_blocks, n_steps):
    nk = ATT_NK
    scale = ATT_HD ** -0.5
    lane = lax.broadcasted_iota(jnp.int32, (nk, LANES), 1)
    step = pl.program_id(1)

    def block(rs, n, n_prev, sel):
        rows = pl.ds(pl.multiple_of(n * nk, nk), nk)
        q = q_ref[rs, rows, :]
        k_cur = k_ref[rs, rows, :]
        v_cur = v_ref[rs, rows, :]
        if n_blocks > 1:
            prev_rows = pl.ds(pl.multiple_of(n_prev * nk, nk), nk)
            k_all = jnp.concatenate([k_ref[rs, prev_rows, :], k_cur], axis=0)
            v_all = jnp.concatenate([v_ref[rs, prev_rows, :], v_cur], axis=0)
        else:
            k_all, v_all = k_cur, v_cur
        if dil == 1:
            pos_rows = rows
        else:
            pos_rows = pl.ds(n * (nk * dil) + step * streams + rs, nk, stride=dil)
        lse_tile = jnp.zeros((nk, LANES), F32)
        for h in range(ATT_HPG):
            hc = slice(h * ATT_HD, (h + 1) * ATT_HD)
            s = _dot_nt(q[:, hc], k_all[:, hc]) * scale + tb_ref[sel, h]
            m = jnp.max(s, axis=-1, keepdims=True)
            p = jnp.exp(s - m)
            den = jnp.sum(p, axis=-1, keepdims=True)
            o = _dot(p.astype(BF16), v_all[:, hc]) / den
            if dil == 1:
                o_ref[pos_rows, hc] = o.astype(o_ref.dtype)
            else:
                acc[0][h, pos_rows, :] = o
            lse_tile = jnp.where(lane == h, m + jnp.log(den), lse_tile)
        lse_ref[pos_rows, :] = lse_tile

    for rs in range(streams):
        if n_blocks == 1:
            block(rs, 0, 0, 0)
        else:
            def loop(n, carry, rs=rs):
                block(rs, n, jnp.maximum(n - 1, 0), jnp.minimum(n, 1))
                return carry
            lax.fori_loop(0, n_blocks, loop, 0)

    if dil > 1:
        @pl.when(step == n_steps - 1)
        def _():
            for h in range(ATT_HPG):
                o_ref[:, h * ATT_HD:(h + 1) * ATT_HD] = acc[0][h].astype(o_ref.dtype)


def _attention_prompt(aq, ak, av, tb, g):
    batch, dil, length, _ = aq.shape
    seq = dil * length
    n_blocks = length // ATT_NK
    streams = min(dil, 4)
    steps = dil // streams
    blk = pl.BlockSpec((None, streams, length, ATT_GW), lambda b, s: (b, s, 0, 0))
    o, lse = pl.pallas_call(
        functools.partial(_attn_body, dil=dil, streams=streams, n_blocks=n_blocks, n_steps=steps),
        grid=(batch, steps),
        in_specs=[blk, blk, blk, pl.BlockSpec(tb.shape, lambda b, s: (0, 0, 0, 0))],
        out_specs=[pl.BlockSpec((None, seq, ATT_GW), lambda b, s: (b, 0, 0)),
                   pl.BlockSpec((None, seq, LANES), lambda b, s: (b, 0, 0))],
        out_shape=[jax.ShapeDtypeStruct((batch, seq, ATT_GW), BF16),
                   jax.ShapeDtypeStruct((batch, seq, LANES), F32)],
        scratch_shapes=[pltpu.VMEM((ATT_HPG, seq, ATT_HD), F32)] if dil > 1 else [],
        compiler_params=_params(2),
        name="attention_g%d" % g,
    )(aq, ak, av, tb)
    return o.reshape(batch * seq, ATT_GW), lse.reshape(batch * seq, LANES)


def _tail_math(act_refs, w_refs, y_ref, combine):
    wrb_ref, wab_ref, wo_ref, wu_ref, wd_ref, wpl_ref, wpg_ref, ln2_ref, lnf_ref = w_refs
    if combine:
        ret_ref, o0_ref, o1_ref, o2_ref, l0_ref, l1_ref, l2_ref, gr_ref, ga_ref, x_ref, ple_ref = act_refs
        lses = [l0_ref[...], l1_ref[...], l2_ref[...]]
        outs = [o0_ref, o1_ref, o2_ref]
        parts = []
        for h in range(ATT_HPG):
            lh = [l[:, h:h + 1] for l in lses]
            mx = jnp.maximum(jnp.maximum(lh[0], lh[1]), lh[2])
            e = [jnp.exp(l - mx) for l in lh]
            tot = e[0] + e[1] + e[2]
            acc = None
            for g in range(ATT_GROUPS):
                term = (e[g] / tot) * outs[g][:, h * ATT_HD:(h + 1) * ATT_HD].astype(F32)
                acc = term if acc is None else acc + term
            parts.append(acc)
        att = jnp.concatenate(parts, axis=1).astype(BF16)
    else:
        ret_ref, att_ref, gr_ref, ga_ref, x_ref, ple_ref = act_refs
        att = att_ref[...].astype(BF16)
    a = _dot(ret_ref[...].astype(BF16), wrb_ref[...])
    b = _dot(att, wab_ref[...])
    mixed = _sigmoid(gr_ref[...].astype(F32)) * a + _sigmoid(ga_ref[...].astype(F32)) * b
    x1 = x_ref[...] + _dot(mixed.astype(BF16), wo_ref[...])
    h2 = _rms(x1, ln2_ref[...]).astype(BF16)
    ff_chunk = D_MODEL
    acc = None
    for c in range(D_FF // ff_chunk):
        u = _dot(h2, wu_ref[:, c * ff_chunk:(c + 1) * ff_chunk])
        r = jnp.maximum(u, 0.0)
        t = _dot((r * r).astype(BF16), wd_ref[c * ff_chunk:(c + 1) * ff_chunk, :])
        acc = t if acc is None else acc + t
    x2 = x1 + acc
    gate = _sigmoid(_dot(x2.astype(BF16), wpg_ref[...]))
    x3 = x2 + gate * _dot(ple_ref[...].astype(BF16), wpl_ref[...])
    y_ref[...] = _rms(x3, lnf_ref[...])


def _tail_body(*refs, n_act, combine):
    _tail_math(refs[:n_act], refs[n_act:-1], refs[-1], combine)


def _tail(acts, weights, tm, combine):
    n = acts[0].shape[0]
    row = lambda a: pl.BlockSpec((tm, a.shape[1]), lambda i: (i, 0))
    return pl.pallas_call(
        functools.partial(_tail_body, n_act=len(acts), combine=combine),
        grid=(n // tm,),
        in_specs=[row(a) for a in acts] + [_resident(w.shape) for w in weights],
        out_specs=pl.BlockSpec((tm, D_MODEL), lambda i: (i, 0)),
        out_shape=jax.ShapeDtypeStruct((n, D_MODEL), F32),
        compiler_params=_params(1),
        name="tail" if combine else "tail_sample",
    )(*acts, *weights)


def _tail_shift_body(*refs, n_act, n_w, n_steps):
    nc = 2 * ATT_GROUPS
    acts = refs[:n_act]
    ws = refs[n_act:n_act + n_w]
    new_ref = refs[n_act + n_w]
    base = n_act + n_w + 1
    old = refs[base:base + nc]
    y_ref = refs[base + nc]
    out = refs[base + nc + 1:base + 2 * nc + 1]
    stage = refs[base + 2 * nc + 1:base + 3 * nc + 1]
    sem_in, sem_out, sem_row = refs[base + 3 * nc + 1:]
    s = pl.program_id(0)
    b = lax.shift_right_logical(s, 1)
    parity = lax.rem(s, 2)

    def copy_in(i, bi):
        width = old[i].shape[1]
        return pltpu.make_async_copy(old[i].at[bi, pl.ds(1, width - 1)], stage[i], sem_in.at[i])

    def copy_out(i, bi):
        width = old[i].shape[1]
        return pltpu.make_async_copy(stage[i], out[i].at[bi, pl.ds(0, width - 1)], sem_out.at[i])

    def copy_row(g, kind):
        i = 2 * g + kind
        return pltpu.make_async_copy(new_ref.at[b, kind + 1, g], out[i].at[b, old[i].shape[1] - 1], sem_row.at[g])

    for kind in range(2):
        mine = [2 * g + kind for g in range(ATT_GROUPS)]
        other = [2 * g + 1 - kind for g in range(ATT_GROUPS)]

        @pl.when(parity == kind)
        def _(kind=kind, mine=mine, other=other):
            if kind == 0:
                @pl.when(s == 0)
                def _():
                    for i in mine:
                        copy_in(i, 0).start()
            for i in mine:
                copy_in(i, b).wait()
            for i in mine:
                copy_out(i, b).start()
            if kind == 0:
                @pl.when(s > 0)
                def _():
                    for i in other:
                        copy_out(i, b - 1).wait()
                for i in other:
                    copy_in(i, b).start()
            else:
                for i in other:
                    copy_out(i, b).wait()

                @pl.when(s < n_steps - 1)
                def _():
                    for i in other:
                        copy_in(i, b + 1).start()
            for g in range(ATT_GROUPS):
                copy_row(g, kind).start()

    _tail_math(acts, ws, y_ref, True)

    for kind in range(2):
        @pl.when(parity == kind)
        def _(kind=kind):
            for g in range(ATT_GROUPS):
                copy_row(g, kind).wait()

    @pl.when(s == n_steps - 1)
    def _():
        for g in range(ATT_GROUPS):
            copy_out(2 * g + 1, b).wait()


def _tail_shift(acts, weights, tm, new_qkv, caches):
    n = acts[0].shape[0]
    n_steps = n // tm
    assert n_steps == 2 * new_qkv.shape[0]
    row = lambda a: pl.BlockSpec((tm, a.shape[1]), lambda i: (i, 0))
    anyspec = pl.BlockSpec(memory_space=pl.ANY)
    nc = len(caches)
    new_rows = new_qkv.reshape(new_qkv.shape[0], 3, ATT_GROUPS, ATT_HPG, ATT_HD)
    res = pl.pallas_call(
        functools.partial(_tail_shift_body, n_act=len(acts), n_w=len(weights), n_steps=n_steps),
        grid=(n_steps,),
        in_specs=[row(a) for a in acts] + [_resident(w.shape) for w in weights] + [anyspec] * (nc + 1),
        out_specs=[pl.BlockSpec((tm, D_MODEL), lambda i: (i, 0))] + [anyspec] * nc,
        out_shape=[jax.ShapeDtypeStruct((n, D_MODEL), F32)]
        + [jax.ShapeDtypeStruct(c.shape, c.dtype) for c in caches],
        scratch_shapes=[pltpu.VMEM((c.shape[1] - 1, ATT_HPG, ATT_HD), F32) for c in caches]
        + [pltpu.SemaphoreType.DMA((nc,)), pltpu.SemaphoreType.DMA((nc,)),
           pltpu.SemaphoreType.DMA((ATT_GROUPS,))],
        compiler_params=_params(1),
        name="tail",
    )(*acts, *weights, new_rows, *caches)
    return res[0], res[1:]


def _decode_body(q_ref, k_ref, v_ref, g_ref, gn_ref, st_ref, gam_ref,
                 qkv_ref, bias_ref,
                 ck0_ref, cv0_ref, ck1_ref, cv1_ref, ck2_ref, cv2_ref,
                 ret_ref, att_ref, nst_ref):
    sub = lax.broadcasted_iota(jnp.int32, (8, 2 * RET_DK), 0)
    lane = lax.broadcasted_iota(jnp.int32, (8, 2 * RET_DK), 1)
    row0 = sub == 0
    srow = lax.broadcasted_iota(jnp.int32, (2 * RET_DK, RET_DV), 0)
    for p in range(RET_PAIRS):
        pc = slice(p * 2 * RET_DK, (p + 1) * 2 * RET_DK)
        q2 = jnp.where(row0, jnp.broadcast_to(q_ref[:, pc], (8, 2 * RET_DK)), 0.0)
        k2 = jnp.where(row0, jnp.broadcast_to(k_ref[:, pc], (8, 2 * RET_DK)), 0.0)
        pst = st_ref[p]
        gam = gam_ref[p]
        outer = []
        for hh in range(2):
            h = 2 * p + hh
            hsel = (lane < RET_DK) if hh == 0 else (lane >= RET_DK)
            qm = jnp.where(hsel, q2, 0.0)
            km = jnp.where(hsel, k2, 0.0)
            vh = v_ref[:, h * RET_DV:(h + 1) * RET_DV]
            v8 = jnp.where(row0[:, :RET_DV], jnp.broadcast_to(vh, (8, RET_DV)), 0.0)
            cross = _dot(qm.astype(BF16), (pst * gam).astype(BF16))[0:1, :]
            qk = jnp.sum(qm[0:1, :] * km[0:1, :], axis=-1, keepdims=True)
            o = cross + qk * vh
            res = _gn_swish(o, g_ref[:, h * RET_DV:(h + 1) * RET_DV], gn_ref[:, h * RET_DV:(h + 1) * RET_DV])
            ret_ref[:, h * RET_DV:(h + 1) * RET_DV] = res
            outer.append(_dot_tn(k2.astype(BF16), v8.astype(BF16)))
        nst_ref[p] = pst * gam + jnp.where(srow < RET_DK, outer[0], outer[1])
    scale = ATT_HD ** -0.5
    caches = ((ck0_ref, cv0_ref), (ck1_ref, cv1_ref), (ck2_ref, cv2_ref))
    slot = lax.broadcasted_iota(jnp.int32, (ATT_NK, ATT_HD), 0)
    is_new = slot == 0
    for h in range(ATT_HPG):
        o_g, lse_g = [], []
        for g in range(ATT_GROUPS):
            ck_ref, cv_ref = caches[g]
            c0 = g * ATT_GW + h * ATT_HD
            qh = qkv_ref[:, c0:c0 + ATT_HD]
            kk = jnp.where(is_new, qkv_ref[:, ATT_W + c0:ATT_W + c0 + ATT_HD], ck_ref[:, h, :])
            vv = jnp.where(is_new, qkv_ref[:, 2 * ATT_W + c0:2 * ATT_W + c0 + ATT_HD], cv_ref[:, h, :])
            s = jnp.sum(kk * qh, axis=-1, keepdims=True) * scale + bias_ref[g * ATT_HPG + h]
            m = jnp.max(s, axis=0, keepdims=True)
            pr = jnp.exp(s - m)
            den = jnp.sum(pr, axis=0, keepdims=True)
            o_g.append(jnp.sum(pr * vv, axis=0, keepdims=True) / den)
            lse_g.append(m + jnp.log(den))
        mx = jnp.maximum(jnp.maximum(lse_g[0], lse_g[1]), lse_g[2])
        e = [jnp.exp(l - mx) for l in lse_g]
        tot = e[0] + e[1] + e[2]
        att_ref[:, h * ATT_HD:(h + 1) * ATT_HD] = ((e[0] / tot) * o_g[0] + (e[1] / tot) * o_g[1]
                                                   + (e[2] / tot) * o_g[2])


def _decode(q, k, v, g, gn, state, qkv, bias_cols, caches):
    batch = q.shape[0]
    lg = _ret_log_decay()
    gam = jnp.broadcast_to(jnp.repeat(jnp.exp(lg), RET_DK).reshape(RET_PAIRS, 2 * RET_DK, 1),
                           (RET_PAIRS, 2 * RET_DK, RET_DV))
    vec = lambda a: a.reshape(batch, 1, a.shape[1])
    vspec = lambda width: pl.BlockSpec((None, 1, width), lambda b: (b, 0, 0))
    const = lambda a: pl.BlockSpec(a.shape, lambda b: (0,) * a.ndim)
    st_spec = pl.BlockSpec((None, RET_PAIRS, 2 * RET_DK, RET_DV), lambda b: (b, 0, 0, 0))
    cache_in, cache_specs = [], []
    for gi in range(ATT_GROUPS):
        dil = ATT_DILATIONS[gi]
        for c in caches[2 * gi:2 * gi + 2]:
            cache_in.append(c.reshape(batch, ATT_NK, dil, ATT_HPG, ATT_HD))
            cache_specs.append(pl.BlockSpec((None, ATT_NK, None, ATT_HPG, ATT_HD), lambda b: (b, 0, 0, 0, 0)))
    st_pairs = state.reshape(batch, RET_PAIRS, 2 * RET_DK, RET_DV)
    ret, att, nst = pl.pallas_call(
        _decode_body,
        grid=(batch,),
        in_specs=[vspec(RET_QK_W), vspec(RET_QK_W), vspec(RET_V_W), vspec(RET_V_W), const(gn), st_spec, const(gam),
                  vspec(3 * ATT_W), const(bias_cols)] + cache_specs,
        out_specs=[vspec(RET_V_W), vspec(ATT_GW), st_spec],
        out_shape=[jax.ShapeDtypeStruct((batch, 1, RET_V_W), F32),
                   jax.ShapeDtypeStruct((batch, 1, ATT_GW), F32),
                   jax.ShapeDtypeStruct(st_pairs.shape, F32)],
        compiler_params=_params(1),
        name="decode",
    )(vec(q), vec(k), vec(v), vec(g), gn, st_pairs, gam, vec(qkv), bias_cols, *cache_in)
    return ret.reshape(batch, RET_V_W), att.reshape(batch, ATT_GW), nst.reshape(state.shape)


def kernel(x_prompt, x_sample, state_ret, cache_k_w128, cache_v_w128, cache_k_w512, cache_v_w512,
           cache_k_w2048, cache_v_w2048, p_prompt, p_sample, ln1_g, w_in, ret_gn_g, w_ret_br, w_att_br,
           w_out, ln2_g, w_up, w_down, w_ple, w_ple_gate, rel_bias, lnf_g):
    depth = w_in.shape[0]
    assert depth == 1
    batch, seq, _ = x_prompt.shape
    dec_batch, dec_seq, _ = x_sample.shape
    assert dec_seq == 1
    past_len = 16384
    l = 0
    ln1 = ln1_g[l][None, :]
    ln2 = ln2_g[l][None, :]
    lnf = lnf_g[None, :]
    gn = ret_gn_g[l][None, :]
    w_in_l = w_in[l]
    w_ret = jnp.concatenate([w_in_l[:, :COL_ATT], w_in_l[:, COL_GATE:]], axis=1).astype(BF16)
    w_att = w_in_l[:, COL_ATT:COL_GATE].astype(BF16)
    tail_w = (w_ret_br[l].astype(BF16), w_att_br[l].astype(BF16), w_out[l].astype(BF16), w_up[l].astype(BF16),
              w_down[l].astype(BF16), w_ple[l].astype(BF16), w_ple_gate[l].astype(BF16), ln2, lnf)
    biases = [_group_bias(rel_bias, g) for g in range(ATT_GROUPS)]

    caches = (cache_k_w128[l], cache_v_w128[l], cache_k_w512[l], cache_v_w512[l],
              cache_k_w2048[l], cache_v_w2048[l])

    xs = x_sample.reshape(dec_batch, D_MODEL)
    cos_s, sin_s = _rope_tables(jnp.full((dec_batch,), past_len, dtype=jnp.int32))
    sq, sk, sv, sg, sgr, sga = _inproj_ret(xs, ln1, w_ret, cos_s, sin_s, dec_batch, F32)
    s_qkv = _inproj_att_sample(xs, ln1, w_att)
    slot_off = jnp.concatenate([jnp.zeros((1,), jnp.int32), ATT_NK - jnp.arange(1, ATT_NK)])
    bias_cols = jnp.concatenate([b[:, slot_off] for b in biases], axis=0)[:, :, None]
    s_ret, s_attn, new_st = _decode(sq, sk, sv, sg, gn, state_ret[l], s_qkv, bias_cols, caches)
    ple_s = p_sample[l].reshape(dec_batch, D_PLE)
    y_s = _tail([s_ret, s_attn, sgr, sga, xs, ple_s], tail_w, dec_batch, False)
    y_sample = y_s.reshape(dec_batch, 1, D_MODEL)

    xp = x_prompt.reshape(batch * seq, D_MODEL)
    cos_p, sin_p = _rope_tables(jnp.arange(seq, dtype=jnp.int32))
    rq, rk, rv, rg, gr, ga = _inproj_ret(xp, ln1, w_ret, cos_p, sin_p, TM_INPROJ, BF16)
    att_o = _inproj_att_prompt(xp, ln1, w_att, TM_INPROJ, seq)
    aqs, aks, avs = att_o[0:3], att_o[3:6], att_o[6:9]
    kfull, vfull = att_o[9:12], att_o[12:15]
    ret_out, st_p = _retention_prompt(rq, rk, rv, rg, gn, batch, seq, TS_RETENTION)
    outs, lses = [], []
    for g in range(ATT_GROUPS):
        n_blocks = seq // ATT_DILATIONS[g] // ATT_NK
        tb = _band_tables(biases[g], n_blocks > 1)
        o, lse = _attention_prompt(aqs[g], aks[g], avs[g], tb, g)
        outs.append(o)
        lses.append(lse)
    ple_p = p_prompt[l].reshape(batch * seq, D_PLE)
    tm_tail = batch * seq // (2 * dec_batch)
    y_p, new_caches = _tail_shift([ret_out] + outs + lses + [gr, ga, xp, ple_p], tail_w, tm_tail, s_qkv, caches)
    y_prompt = y_p.reshape(batch, seq, D_MODEL)
    new_state_p = st_p[None]
    kv_p = []
    for g in range(ATT_GROUPS):
        shape = (1, batch, min(ATT_WINDOWS[g], seq), ATT_HPG, ATT_HD)
        kv_p.append(kfull[g].reshape(shape))
        kv_p.append(vfull[g].reshape(shape))
    kv_s = [c[None] for c in new_caches]

    return (y_prompt, y_sample, new_state_p, *kv_p, new_st[None], *kv_s)
```

```python
import functools
import math

import jax
import jax.numpy as jnp
from jax import lax
from jax.experimental import pallas as pl
from jax.experimental.pallas import tpu as pltpu

F32 = jnp.float32
BF16 = jnp.bfloat16

D_MODEL = 1024
RET_HEADS = 8
RET_DK = 64
RET_DV = 128
RET_PAIRS = RET_HEADS // 2
RET_CHUNK = 128
ROPE_BASE = 10000.0
ATT_WINDOWS = (128, 512, 2048)
ATT_DILATIONS = (1, 4, 16)
ATT_GROUPS = 3
ATT_HPG = 4
ATT_HD = 128
ATT_NK = 128
ATT_GW = ATT_HPG * ATT_HD
REL_BUCKETS = 32
REL_MAX_DIST = 2048
D_FF = 4 * D_MODEL
D_PLE = 256
NORM_EPS = 1e-6
RET_QK_W = RET_HEADS * RET_DK
RET_V_W = RET_HEADS * RET_DV
ATT_W = ATT_GROUPS * ATT_GW
COL_RET = 0
COL_ATT = 2 * RET_QK_W + 2 * RET_V_W
COL_GATE = COL_ATT + 3 * ATT_W
N_IN = COL_GATE + 2 * D_MODEL

VMEM_LIMIT_V7X = 56 * 1024 * 1024
LANES = 128
TM_INPROJ = 512
TS_RETENTION = 512
ATTN_UNROLL = 4
PERM_ROWS = 256
SINGLE_BLOCK_GROUP = 4


def _dot(a, b):
    return jnp.dot(a, b, preferred_element_type=F32)


def _dot_nt(a, b):
    return lax.dot_general(a, b, (((1,), (1,)), ((), ())), preferred_element_type=F32)


def _dot_tn(a, b):
    return lax.dot_general(a, b, (((0,), (0,)), ((), ())), preferred_element_type=F32)


def _rms(x, g):
    return x * lax.rsqrt(jnp.mean(x * x, axis=-1, keepdims=True) + NORM_EPS) * g


def _sigmoid(x):
    return 1.0 / (1.0 + jnp.exp(-x))


def _resident(shape):
    return pl.BlockSpec(shape, lambda *_: (0,) * len(shape), pipeline_mode=pl.Buffered(1))


def _params(n_axes):
    return pltpu.CompilerParams(dimension_semantics=("arbitrary",) * n_axes,
                                vmem_limit_bytes=VMEM_LIMIT_V7X)


def _rope_tables(pos):
    half = RET_DK // 2
    inv = ROPE_BASE ** (-jnp.arange(half, dtype=F32) / half)
    ang = pos.astype(F32)[:, None] * inv[None, :]
    cos = jnp.cos(ang)
    sin = jnp.sin(ang)
    cos_t = jnp.tile(cos, (1, LANES // half))
    sin_t = jnp.tile(jnp.concatenate([-sin, sin], axis=1), (1, LANES // RET_DK))
    return cos_t, sin_t


def _ret_log_decay():
    return jnp.log1p(-jnp.exp2(-5.0 - jnp.arange(RET_HEADS, dtype=F32)))


def _ret_tables():
    c = RET_CHUNK
    lg = _ret_log_decay()
    i = jnp.arange(c, dtype=F32)
    diff = i[:, None] - i[None, :]
    dmask = jnp.where(diff[None] >= 0, jnp.exp(jnp.maximum(diff, 0.0)[None] * lg[:, None, None]), 0.0)
    q_decay = jnp.exp((i + 1.0)[:, None] * lg[None, :])
    k_decay = jnp.exp((c - 1.0 - i)[:, None] * lg[None, :])
    qdec = jnp.broadcast_to(q_decay.T[:, :, None], (RET_HEADS, c, RET_DV))
    kdec = jnp.repeat(k_decay, RET_DK, axis=1).reshape(c, RET_PAIRS, 2 * RET_DK).transpose(1, 0, 2)
    gc = jnp.repeat(jnp.exp(c * lg), RET_DV).reshape(RET_PAIRS, 1, 2 * RET_DV)
    return dmask, qdec, kdec, gc


def _rel_bucket(dist):
    max_exact = REL_BUCKETS // 2
    d = dist.astype(jnp.int32)
    log_ratio = jnp.log(jnp.maximum(d, 1).astype(F32) / max_exact) / math.log(REL_MAX_DIST / max_exact)
    large = max_exact + (log_ratio * (REL_BUCKETS - max_exact)).astype(jnp.int32)
    large = jnp.minimum(large, REL_BUCKETS - 1)
    return jnp.where(d < max_exact, d, large)


def _group_bias(rel_bias, g):
    b = _rel_bucket(jnp.arange(ATT_NK) * ATT_DILATIONS[g])
    return rel_bias[b][:, g * ATT_HPG:(g + 1) * ATT_HPG].T.astype(F32)


def _band_tables(bias, with_prev):
    nk = ATT_NK
    hpg = bias.shape[0]
    neg = lambda n: jnp.full((hpg, n), -jnp.inf, F32)
    period = 3 * nk
    gvec = jnp.concatenate([neg(nk - 1), bias, neg(period - 2 * nk + 1)], axis=1)
    flat = jnp.tile(gvec, (1, nk + 1))[:, :nk * (period + 1)]
    hankel = flat.reshape(hpg, nk, period + 1)[:, :, :2 * nk]
    rest = hankel[:, :, ::-1]
    cur = rest[:, :, nk:]
    if with_prev:
        first = jnp.concatenate([jnp.full((hpg, nk, nk), -jnp.inf, F32), cur], axis=2)
        return jnp.stack([first, rest])
    return cur[None]


def _inproj_ret_body(x_ref, ln_ref, w_ref, cos_ref, sin_ref, q_ref, k_ref, v_ref, g_ref, gr_ref, ga_ref):
    x = x_ref[...]
    h = _rms(x, ln_ref[...]).astype(BF16)
    cos = cos_ref[...]
    sin = sin_ref[...]
    qk = _dot(h, w_ref[:, 0:2 * RET_QK_W])
    lane = lax.broadcasted_iota(jnp.int32, cos.shape, 1)
    first_half = (lane % RET_DK) < (RET_DK // 2)
    n_q = RET_QK_W // LANES
    for c in range(2 * n_q):
        xc = qk[:, c * LANES:(c + 1) * LANES]
        swapped = jnp.where(first_half, pltpu.roll(xc, LANES - RET_DK // 2, 1), pltpu.roll(xc, RET_DK // 2, 1))
        r = xc * cos + swapped * sin
        if c < n_q:
            q_ref[:, c * LANES:(c + 1) * LANES] = r.astype(q_ref.dtype)
        else:
            k_ref[:, (c - n_q) * LANES:(c - n_q + 1) * LANES] = (r * (RET_DK ** -0.5)).astype(k_ref.dtype)
    o = 2 * RET_QK_W
    v_ref[...] = _dot(h, w_ref[:, o:o + RET_V_W]).astype(v_ref.dtype)
    o += RET_V_W
    g_ref[...] = _dot(h, w_ref[:, o:o + RET_V_W]).astype(g_ref.dtype)
    o += RET_V_W
    gr_ref[...] = _dot(h, w_ref[:, o:o + D_MODEL]).astype(gr_ref.dtype)
    o += D_MODEL
    ga_ref[...] = _dot(h, w_ref[:, o:o + D_MODEL]).astype(ga_ref.dtype)


def _inproj_ret(x2d, ln, w, cos_t, sin_t, tm, out_dtype):
    n = x2d.shape[0]
    t_rows = cos_t.shape[0]
    nt = t_rows // tm
    wcols = w.shape[1]
    row = lambda width: pl.BlockSpec((tm, width), lambda i: (i, 0))
    tab = pl.BlockSpec((tm, LANES), lambda i: (i % nt, 0))
    widths = (RET_QK_W, RET_QK_W, RET_V_W, RET_V_W, D_MODEL, D_MODEL)
    return pl.pallas_call(
        _inproj_ret_body,
        grid=(n // tm,),
        in_specs=[row(D_MODEL), _resident((1, D_MODEL)), _resident((D_MODEL, wcols)), tab, tab],
        out_specs=[row(wd) for wd in widths],
        out_shape=[jax.ShapeDtypeStruct((n, wd), out_dtype) for wd in widths],
        compiler_params=_params(1),
        name="inproj_ret",
    )(x2d, ln, w, cos_t, sin_t)


def _inproj_att_body(x_ref, ln_ref, w_ref, *refs, tm, keeps, seq):
    lowp = refs[:3 * ATT_GROUPS]
    full = refs[3 * ATT_GROUPS:5 * ATT_GROUPS]
    res = refs[5 * ATT_GROUPS]
    x = x_ref[...]
    h = _rms(x, ln_ref[...]).astype(BF16)
    slot = 0
    for kind in range(3):
        for g in range(ATT_GROUPS):
            o = kind * ATT_W + g * ATT_GW
            dil = ATT_DILATIONS[g]
            dst = lowp[kind * ATT_GROUPS + g]
            r = _dot(h, w_ref[:, o:o + ATT_GW])
            if dil == 1:
                dst[0] = r.astype(dst.dtype)
            else:
                for hh in range(ATT_HPG):
                    res[slot, hh] = r[:, hh * ATT_HD:(hh + 1) * ATT_HD]
                for rr in range(dil):
                    for hh in range(ATT_HPG):
                        dst[rr, :, hh * ATT_HD:(hh + 1) * ATT_HD] = (
                            res[slot, hh, pl.ds(rr, tm // dil, stride=dil), :].astype(dst.dtype))
                slot += 1
            if kind > 0:
                cache = full[(kind - 1) * ATT_GROUPS + g]
                rows = tm if keeps[g] == seq else keeps[g]
                for hh in range(ATT_HPG):
                    cache[pl.ds(hh, rows, stride=ATT_HPG), :] = r[tm - rows:, hh * ATT_HD:(hh + 1) * ATT_HD]


def _inproj_att_prompt(x2d, ln, w, tm, seq):
    n = x2d.shape[0]
    batch = n // seq
    tiles = seq // tm
    keeps = tuple(min(wd, seq) for wd in ATT_WINDOWS)
    assert all(kp <= tm or kp == seq for kp in keeps) and seq % tm == 0
    out_specs, out_shape = [], []
    for _ in range(3):
        for g in range(ATT_GROUPS):
            dil = ATT_DILATIONS[g]
            out_specs.append(pl.BlockSpec((None, dil, tm // dil, ATT_GW), lambda i: (i // tiles, 0, i % tiles, 0)))
            out_shape.append(jax.ShapeDtypeStruct((batch, dil, seq // dil, ATT_GW), BF16))
    for _ in range(2):
        for g in range(ATT_GROUPS):
            if keeps[g] == seq:
                idx = lambda i: (i // tiles, i % tiles, 0)
                rows = tm
            else:
                idx = lambda i: (i // tiles, 0, 0)
                rows = keeps[g]
            out_specs.append(pl.BlockSpec((None, rows * ATT_HPG, ATT_HD), idx))
            out_shape.append(jax.ShapeDtypeStruct((batch, keeps[g] * ATT_HPG, ATT_HD), F32))
    return pl.pallas_call(
        functools.partial(_inproj_att_body, tm=tm, keeps=keeps, seq=seq),
        grid=(n // tm,),
        in_specs=[pl.BlockSpec((tm, D_MODEL), lambda i: (i, 0)), _resident((1, D_MODEL)),
                  _resident((D_MODEL, 3 * ATT_W))],
        out_specs=out_specs,
        out_shape=out_shape,
        scratch_shapes=[pltpu.VMEM((3 * sum(d > 1 for d in ATT_DILATIONS), ATT_HPG, tm, ATT_HD), F32)],
        compiler_params=_params(1),
        name="inproj_att",
    )(x2d, ln, w)


def _inproj_att_sample_body(x_ref, ln_ref, w_ref, o_ref):
    h = _rms(x_ref[...], ln_ref[...]).astype(BF16)
    for c in range(3 * ATT_GROUPS):
        o_ref[:, c * ATT_GW:(c + 1) * ATT_GW] = _dot(h, w_ref[:, c * ATT_GW:(c + 1) * ATT_GW])


def _inproj_att_sample(x2d, ln, w):
    n = x2d.shape[0]
    return pl.pallas_call(
        _inproj_att_sample_body,
        grid=(1,),
        in_specs=[pl.BlockSpec((n, D_MODEL), lambda i: (0, 0)), _resident((1, D_MODEL)),
                  _resident((D_MODEL, 3 * ATT_W))],
        out_specs=pl.BlockSpec((n, 3 * ATT_W), lambda i: (0, 0)),
        out_shape=jax.ShapeDtypeStruct((n, 3 * ATT_W), F32),
        compiler_params=_params(1),
        name="inproj_att_sample",
    )(x2d, ln, w)


def _gn_swish(o, gate, gn):
    mu = jnp.mean(o, axis=-1, keepdims=True)
    d = o - mu
    var = jnp.mean(d * d, axis=-1, keepdims=True)
    on = d * lax.rsqrt(var + NORM_EPS) * gn
    return gate * _sigmoid(gate) * on


def _retention_body(q_ref, k_ref, v_ref, g_ref, gn_ref, dm_ref, qdec_ref, kdec_ref, gc_ref,
                    out_ref, st_ref, state, *, n_chunks):
    c = RET_CHUNK

    @pl.when(pl.program_id(1) == 0)
    def _():
        state[...] = jnp.zeros_like(state)

    lane = lax.broadcasted_iota(jnp.int32, (c, 2 * RET_DK), 1)
    head0 = lane < RET_DK

    def chunk(ci, carry):
        rows = pl.ds(pl.multiple_of(ci * c, c), c)
        for p in range(RET_PAIRS):
            q2 = q_ref[rows, p * 2 * RET_DK:(p + 1) * 2 * RET_DK]
            k2 = k_ref[rows, p * 2 * RET_DK:(p + 1) * 2 * RET_DK]
            v2 = v_ref[rows, p * 2 * RET_DV:(p + 1) * 2 * RET_DV]
            pst = state[p]
            pst_lo = pst.astype(BF16)
            zero = jnp.zeros_like(q2)
            for hh in range(2):
                h = 2 * p + hh
                qm = jnp.where(head0 if hh == 0 else jnp.logical_not(head0), q2, zero)
                vh = v2[:, hh * RET_DV:(hh + 1) * RET_DV]
                s = _dot_nt(qm, k2) * dm_ref[h]
                o = _dot(s.astype(BF16), vh)
                o = o + _dot(qm, pst_lo[:, hh * RET_DV:(hh + 1) * RET_DV]) * qdec_ref[h]
                gate = g_ref[rows, h * RET_DV:(h + 1) * RET_DV].astype(F32)
                res = _gn_swish(o, gate, gn_ref[:, h * RET_DV:(h + 1) * RET_DV])
                out_ref[rows, h * RET_DV:(h + 1) * RET_DV] = res.astype(out_ref.dtype)
            kd = (k2.astype(F32) * kdec_ref[p]).astype(BF16)
            state[p] = pst * gc_ref[p] + _dot_tn(kd, v2)
        return carry

    lax.fori_loop(0, n_chunks, chunk, 0, unroll=True)
    for p in range(RET_PAIRS):
        pst = state[p]
        for hh in range(2):
            st_ref[2 * p + hh] = pst[hh * RET_DK:(hh + 1) * RET_DK, hh * RET_DV:(hh + 1) * RET_DV]


def _retention_prompt(q, k, v, g, gn, batch, seq, ts):
    n = q.shape[0]
    steps = seq // ts
    dmask, qdec, kdec, gc = _ret_tables()
    row = lambda width: pl.BlockSpec((ts, width), lambda b, s: (b * steps + s, 0))
    const = lambda a: pl.BlockSpec(a.shape, lambda b, s: (0,) * a.ndim)
    return pl.pallas_call(
        functools.partial(_retention_body, n_chunks=ts // RET_CHUNK),
        grid=(batch, steps),
        in_specs=[row(RET_QK_W), row(RET_QK_W), row(RET_V_W), row(RET_V_W), const(gn),
                  const(dmask), const(qdec), const(kdec), const(gc)],
        out_specs=[row(RET_V_W),
                   pl.BlockSpec((None, RET_HEADS, RET_DK, RET_DV), lambda b, s: (b, 0, 0, 0))],
        out_shape=[jax.ShapeDtypeStruct((n, RET_V_W), BF16),
                   jax.ShapeDtypeStruct((batch, RET_HEADS, RET_DK, RET_DV), F32)],
        scratch_shapes=[pltpu.VMEM((RET_PAIRS, 2 * RET_DK, 2 * RET_DV), F32)],
        compiler_params=_params(2),
        name="retention",
    )(q, k, v, g, gn, dmask, qdec, kdec, gc)


def _attn_body(q_ref, k_ref, v_ref, tb_ref, o_ref, lse_ref, *scratch, dil, n_blocks):
    nk = ATT_NK
    scale = ATT_HD ** -0.5
    lane = lax.broadcasted_iota(jnp.int32, (nk, LANES), 1)
    if dil > 1:
        o_s, l_s = scratch

    def chain(s_raw, v_ext, sel, h):
        s = s_raw * scale + tb_ref[sel, h]
        m = jnp.max(s, axis=-1, keepdims=True)
        p = jnp.exp(s - m)
        pv = _dot(p.astype(BF16), v_ext)
        den = pv[:, ATT_HD:]
        return (pv[:, :ATT_HD] / den).astype(BF16), m + jnp.log(den)

    def write(rs, rows, outs):
        lse_tile = jnp.zeros((nk, LANES), F32)
        for h, (o, lse) in enumerate(outs):
            hc = slice(h * ATT_HD, (h + 1) * ATT_HD)
            if dil == 1:
                o_ref[rows, hc] = o
            else:
                o_s[rs, rows, hc] = o
            lse_tile = jnp.where(lane == h, lse, lse_tile)
        if dil == 1:
            lse_ref[rows, :] = lse_tile
        else:
            l_s[rs, rows, :] = lse_tile

    heads = [slice(h * ATT_HD, (h + 1) * ATT_HD) for h in range(ATT_HPG)]

    def block(rs, n, n_prev, sel):
        rows = pl.ds(pl.multiple_of(n * nk, nk), nk)
        prev_rows = pl.ds(pl.multiple_of(n_prev * nk, nk), nk)
        q = q_ref[rs, rows, :]
        k_all = jnp.concatenate([k_ref[rs, prev_rows, :], k_ref[rs, rows, :]], axis=0)
        v_all = jnp.concatenate([v_ref[rs, prev_rows, :], v_ref[rs, rows, :]], axis=0)
        ones = jnp.ones((2 * nk, ATT_HD), BF16)
        outs = []
        for h, hc in enumerate(heads):
            v_ext = jnp.concatenate([v_all[:, hc], ones], axis=1)
            outs.append(chain(_dot_nt(q[:, hc], k_all[:, hc]), v_ext, sel, h))
        write(rs, rows, outs)

    def single_blocks(streams):
        ones = jnp.ones((nk, ATT_HD), BF16)
        scores = [[_dot_nt(q_ref[rs, :, hc], k_ref[rs, :, hc]) for hc in heads] for rs in streams]
        for i, rs in enumerate(streams):
            outs = [chain(scores[i][h], jnp.concatenate([v_ref[rs, :, hc], ones], axis=1), 0, h)
                    for h, hc in enumerate(heads)]
            write(rs, slice(None), outs)

    if n_blocks == 1:
        for r0 in range(0, dil, SINGLE_BLOCK_GROUP):
            single_blocks(range(r0, min(r0 + SINGLE_BLOCK_GROUP, dil)))
    else:
        for rs in range(dil):
            def loop(n, carry, rs=rs):
                block(rs, n, jnp.maximum(n - 1, 0), jnp.minimum(n, 1))
                return carry
            lax.fori_loop(0, n_blocks, loop, 0, unroll=ATTN_UNROLL)

    if dil > 1:
        piece = PERM_ROWS // dil
        row = lax.broadcasted_iota(jnp.int32, (PERM_ROWS, PERM_ROWS), 0)
        col = lax.broadcasted_iota(jnp.int32, (PERM_ROWS, PERM_ROWS), 1)
        perm = jnp.where((col % piece) * dil + col // piece == row, 1.0, 0.0).astype(BF16)
        for j in range(o_ref.shape[0] // PERM_ROWS):
            src = slice(j * piece, (j + 1) * piece)
            dst = slice(j * PERM_ROWS, (j + 1) * PERM_ROWS)
            o_ref[dst, :] = _dot(perm, jnp.concatenate([o_s[r, src, :] for r in range(dil)], axis=0)).astype(BF16)
            lse = jnp.concatenate([l_s[r, src, :] for r in range(dil)], axis=0)
            hi = lse.astype(BF16)
            rem = lse - hi.astype(F32)
            mid = rem.astype(BF16)
            lo = (rem - mid.astype(F32)).astype(BF16)
            lse_ref[dst, :] = _dot(perm, hi) + _dot(perm, mid) + _dot(perm, lo)


def _attention_prompt(aq, ak, av, tb, g):
    batch, dil, length, _ = aq.shape
    seq = dil * length
    n_blocks = length // ATT_NK
    blk = pl.BlockSpec((None, dil, length, ATT_GW), lambda b: (b, 0, 0, 0))
    scratch = [pltpu.VMEM((dil, length, ATT_GW), BF16), pltpu.VMEM((dil, length, LANES), F32)] if dil > 1 else []
    o, lse = pl.pallas_call(
        functools.partial(_attn_body, dil=dil, n_blocks=n_blocks),
        grid=(batch,),
        in_specs=[blk, blk, blk, pl.BlockSpec(tb.shape, lambda b: (0, 0, 0, 0))],
        out_specs=[pl.BlockSpec((None, seq, ATT_GW), lambda b: (b, 0, 0)),
                   pl.BlockSpec((None, seq, LANES), lambda b: (b, 0, 0))],
        out_shape=[jax.ShapeDtypeStruct((batch, seq, ATT_GW), BF16),
                   jax.ShapeDtypeStruct((batch, seq, LANES), F32)],
        scratch_shapes=scratch,
        compiler_params=_params(1),
        name="attention_g%d" % g,
    )(aq, ak, av, tb)
    return o.reshape(batch * seq, ATT_GW), lse.reshape(batch * seq, LANES)


def _tail_math(act_refs, w_refs, y_ref, combine):
    wrb_ref, wab_ref, wo_ref, wu_ref, wd_ref, wpl_ref, wpg_ref, ln2_ref, lnf_ref = w_refs
    if combine:
        ret_ref, o0_ref, o1_ref, o2_ref, l0_ref, l1_ref, l2_ref, gr_ref, ga_ref, x_ref, ple_ref = act_refs
        lses = [l0_ref[...], l1_ref[...], l2_ref[...]]
        outs = [o0_ref, o1_ref, o2_ref]
        parts = []
        for h in range(ATT_HPG):
            lh = [l[:, h:h + 1] for l in lses]
            mx = jnp.maximum(jnp.maximum(lh[0], lh[1]), lh[2])
            e = [jnp.exp(l - mx) for l in lh]
            tot = e[0] + e[1] + e[2]
            acc = None
            for g in range(ATT_GROUPS):
                term = (e[g] / tot) * outs[g][:, h * ATT_HD:(h + 1) * ATT_HD].astype(F32)
                acc = term if acc is None else acc + term
            parts.append(acc)
        att = jnp.concatenate(parts, axis=1).astype(BF16)
    else:
        ret_ref, att_ref, gr_ref, ga_ref, x_ref, ple_ref = act_refs
        att = att_ref[...].astype(BF16)
    a = _dot(ret_ref[...].astype(BF16), wrb_ref[...])
    b = _dot(att, wab_ref[...])
    mixed = _sigmoid(gr_ref[...].astype(F32)) * a + _sigmoid(ga_ref[...].astype(F32)) * b
    x1 = x_ref[...] + _dot(mixed.astype(BF16), wo_ref[...])
    h2 = _rms(x1, ln2_ref[...]).astype(BF16)
    ff_chunk = D_MODEL
    acc = None
    for c in range(D_FF // ff_chunk):
        u = _dot(h2, wu_ref[:, c * ff_chunk:(c + 1) * ff_chunk])
        r = jnp.maximum(u, 0.0)
        t = _dot((r * r).astype(BF16), wd_ref[c * ff_chunk:(c + 1) * ff_chunk, :])
        acc = t if acc is None else acc + t
    x2 = x1 + acc
    gate = _sigmoid(_dot(x2.astype(BF16), wpg_ref[...]))
    x3 = x2 + gate * _dot(ple_ref[...].astype(BF16), wpl_ref[...])
    y_ref[...] = _rms(x3, lnf_ref[...])


def _tail_body(*refs, n_act, combine):
    _tail_math(refs[:n_act], refs[n_act:-1], refs[-1], combine)


def _tail(acts, weights, tm, combine):
    n = acts[0].shape[0]
    row = lambda a: pl.BlockSpec((tm, a.shape[1]), lambda i: (i, 0))
    return pl.pallas_call(
        functools.partial(_tail_body, n_act=len(acts), combine=combine),
        grid=(n // tm,),
        in_specs=[row(a) for a in acts] + [_resident(w.shape) for w in weights],
        out_specs=pl.BlockSpec((tm, D_MODEL), lambda i: (i, 0)),
        out_shape=jax.ShapeDtypeStruct((n, D_MODEL), F32),
        compiler_params=_params(1),
        name="tail" if combine else "tail_sample",
    )(*acts, *weights)


def _tail_shift_body(*refs, n_act, n_w, n_steps):
    nc = 2 * ATT_GROUPS
    acts = refs[:n_act]
    ws = refs[n_act:n_act + n_w]
    new_ref = refs[n_act + n_w]
    base = n_act + n_w + 1
    old = refs[base:base + nc]
    y_ref = refs[base + nc]
    out = refs[base + nc + 1:base + 2 * nc + 1]
    stage = refs[base + 2 * nc + 1:base + 3 * nc + 1]
    sem_in, sem_out, sem_row = refs[base + 3 * nc + 1:]
    s = pl.program_id(0)
    b = lax.shift_right_logical(s, 1)
    parity = lax.rem(s, 2)

    def copy_in(i, bi):
        width = old[i].shape[1]
        return pltpu.make_async_copy(old[i].at[bi, pl.ds(1, width - 1)], stage[i], sem_in.at[i])

    def copy_out(i, bi):
        width = old[i].shape[1]
        return pltpu.make_async_copy(stage[i], out[i].at[bi, pl.ds(0, width - 1)], sem_out.at[i])

    def copy_row(g, kind):
        i = 2 * g + kind
        return pltpu.make_async_copy(new_ref.at[b, kind + 1, g], out[i].at[b, old[i].shape[1] - 1], sem_row.at[g])

    for kind in range(2):
        mine = [2 * g + kind for g in range(ATT_GROUPS)]
        other = [2 * g + 1 - kind for g in range(ATT_GROUPS)]

        @pl.when(parity == kind)
        def _(kind=kind, mine=mine, other=other):
            if kind == 0:
                @pl.when(s == 0)
                def _():
                    for i in mine:
                        copy_in(i, 0).start()
            for i in mine:
                copy_in(i, b).wait()
            for i in mine:
                copy_out(i, b).start()
            if kind == 0:
                @pl.when(s > 0)
                def _():
                    for i in other:
                        copy_out(i, b - 1).wait()
                for i in other:
                    copy_in(i, b).start()
            else:
                for i in other:
                    copy_out(i, b).wait()

                @pl.when(s < n_steps - 1)
                def _():
                    for i in other:
                        copy_in(i, b + 1).start()
            for g in range(ATT_GROUPS):
                copy_row(g, kind).start()

    _tail_math(acts, ws, y_ref, True)

    for kind in range(2):
        @pl.when(parity == kind)
        def _(kind=kind):
            for g in range(ATT_GROUPS):
                copy_row(g, kind).wait()

    @pl.when(s == n_steps - 1)
    def _():
        for g in range(ATT_GROUPS):
            copy_out(2 * g + 1, b).wait()


def _tail_shift(acts, weights, tm, new_qkv, caches):
    n = acts[0].shape[0]
    n_steps = n // tm
    assert n_steps == 2 * new_qkv.shape[0]
    row = lambda a: pl.BlockSpec((tm, a.shape[1]), lambda i: (i, 0))
    anyspec = pl.BlockSpec(memory_space=pl.ANY)
    nc = len(caches)
    new_rows = new_qkv.reshape(new_qkv.shape[0], 3, ATT_GROUPS, ATT_HPG, ATT_HD)
    res = pl.pallas_call(
        functools.partial(_tail_shift_body, n_act=len(acts), n_w=len(weights), n_steps=n_steps),
        grid=(n_steps,),
        in_specs=[row(a) for a in acts] + [_resident(w.shape) for w in weights] + [anyspec] * (nc + 1),
        out_specs=[pl.BlockSpec((tm, D_MODEL), lambda i: (i, 0))] + [anyspec] * nc,
        out_shape=[jax.ShapeDtypeStruct((n, D_MODEL), F32)]
        + [jax.ShapeDtypeStruct(c.shape, c.dtype) for c in caches],
        scratch_shapes=[pltpu.VMEM((c.shape[1] - 1, ATT_HPG, ATT_HD), F32) for c in caches]
        + [pltpu.SemaphoreType.DMA((nc,)), pltpu.SemaphoreType.DMA((nc,)),
           pltpu.SemaphoreType.DMA((ATT_GROUPS,))],
        compiler_params=_params(1),
        name="tail",
    )(*acts, *weights, new_rows, *caches)
    return res[0], res[1:]


def _decode_body(q_ref, k_ref, v_ref, g_ref, gn_ref, st_ref, gam_ref,
                 qkv_ref, bias_ref,
                 ck0_ref, cv0_ref, ck1_ref, cv1_ref, ck2_ref, cv2_ref,
                 ret_ref, att_ref, nst_ref):
    sub = lax.broadcasted_iota(jnp.int32, (8, 2 * RET_DK), 0)
    lane = lax.broadcasted_iota(jnp.int32, (8, 2 * RET_DK), 1)
    row0 = sub == 0
    srow = lax.broadcasted_iota(jnp.int32, (2 * RET_DK, RET_DV), 0)
    for p in range(RET_PAIRS):
        pc = slice(p * 2 * RET_DK, (p + 1) * 2 * RET_DK)
        q2 = jnp.where(row0, jnp.broadcast_to(q_ref[:, pc], (8, 2 * RET_DK)), 0.0)
        k2 = jnp.where(row0, jnp.broadcast_to(k_ref[:, pc], (8, 2 * RET_DK)), 0.0)
        pst = st_ref[p]
        gam = gam_ref[p]
        outer = []
        for hh in range(2):
            h = 2 * p + hh
            hsel = (lane < RET_DK) if hh == 0 else (lane >= RET_DK)
            qm = jnp.where(hsel, q2, 0.0)
            km = jnp.where(hsel, k2, 0.0)
            vh = v_ref[:, h * RET_DV:(h + 1) * RET_DV]
            v8 = jnp.where(row0[:, :RET_DV], jnp.broadcast_to(vh, (8, RET_DV)), 0.0)
            cross = _dot(qm.astype(BF16), (pst * gam).astype(BF16))[0:1, :]
            qk = jnp.sum(qm[0:1, :] * km[0:1, :], axis=-1, keepdims=True)
            o = cross + qk * vh
            res = _gn_swish(o, g_ref[:, h * RET_DV:(h + 1) * RET_DV], gn_ref[:, h * RET_DV:(h + 1) * RET_DV])
            ret_ref[:, h * RET_DV:(h + 1) * RET_DV] = res
            outer.append(_dot_tn(k2.astype(BF16), v8.astype(BF16)))
        nst_ref[p] = pst * gam + jnp.where(srow < RET_DK, outer[0], outer[1])
    scale = ATT_HD ** -0.5
    caches = ((ck0_ref, cv0_ref), (ck1_ref, cv1_ref), (ck2_ref, cv2_ref))
    slot = lax.broadcasted_iota(jnp.int32, (ATT_NK, ATT_HD), 0)
    is_new = slot == 0
    for h in range(ATT_HPG):
        o_g, lse_g = [], []
        for g in range(ATT_GROUPS):
            ck_ref, cv_ref = caches[g]
            c0 = g * ATT_GW + h * ATT_HD
            qh = qkv_ref[:, c0:c0 + ATT_HD]
            kk = jnp.where(is_new, qkv_ref[:, ATT_W + c0:ATT_W + c0 + ATT_HD], ck_ref[:, h, :])
            vv = jnp.where(is_new, qkv_ref[:, 2 * ATT_W + c0:2 * ATT_W + c0 + ATT_HD], cv_ref[:, h, :])
            s = jnp.sum(kk * qh, axis=-1, keepdims=True) * scale + bias_ref[g * ATT_HPG + h]
            m = jnp.max(s, axis=0, keepdims=True)
            pr = jnp.exp(s - m)
            den = jnp.sum(pr, axis=0, keepdims=True)
            o_g.append(jnp.sum(pr * vv, axis=0, keepdims=True) / den)
            lse_g.append(m + jnp.log(den))
        mx = jnp.maximum(jnp.maximum(lse_g[0], lse_g[1]), lse_g[2])
        e = [jnp.exp(l - mx) for l in lse_g]
        tot = e[0] + e[1] + e[2]
        att_ref[:, h * ATT_HD:(h + 1) * ATT_HD] = ((e[0] / tot) * o_g[0] + (e[1] / tot) * o_g[1]
                                                   + (e[2] / tot) * o_g[2])


def _decode(q, k, v, g, gn, state, qkv, bias_cols, caches):
    batch = q.shape[0]
    lg = _ret_log_decay()
    gam = jnp.broadcast_to(jnp.repeat(jnp.exp(lg), RET_DK).reshape(RET_PAIRS, 2 * RET_DK, 1),
                           (RET_PAIRS, 2 * RET_DK, RET_DV))
    vec = lambda a: a.reshape(batch, 1, a.shape[1])
    vspec = lambda width: pl.BlockSpec((None, 1, width), lambda b: (b, 0, 0))
    const = lambda a: pl.BlockSpec(a.shape, lambda b: (0,) * a.ndim)
    st_spec = pl.BlockSpec((None, RET_PAIRS, 2 * RET_DK, RET_DV), lambda b: (b, 0, 0, 0))
    cache_in, cache_specs = [], []
    for gi in range(ATT_GROUPS):
        dil = ATT_DILATIONS[gi]
        for c in caches[2 * gi:2 * gi + 2]:
            cache_in.append(c.reshape(batch, ATT_NK, dil, ATT_HPG, ATT_HD))
            cache_specs.append(pl.BlockSpec((None, ATT_NK, None, ATT_HPG, ATT_HD), lambda b: (b, 0, 0, 0, 0)))
    st_pairs = state.reshape(batch, RET_PAIRS, 2 * RET_DK, RET_DV)
    ret, att, nst = pl.pallas_call(
        _decode_body,
        grid=(batch,),
        in_specs=[vspec(RET_QK_W), vspec(RET_QK_W), vspec(RET_V_W), vspec(RET_V_W), const(gn), st_spec, const(gam),
                  vspec(3 * ATT_W), const(bias_cols)] + cache_specs,
        out_specs=[vspec(RET_V_W), vspec(ATT_GW), st_spec],
        out_shape=[jax.ShapeDtypeStruct((batch, 1, RET_V_W), F32),
                   jax.ShapeDtypeStruct((batch, 1, ATT_GW), F32),
                   jax.ShapeDtypeStruct(st_pairs.shape, F32)],
        compiler_params=_params(1),
        name="decode",
    )(vec(q), vec(k), vec(v), vec(g), gn, st_pairs, gam, vec(qkv), bias_cols, *cache_in)
    return ret.reshape(batch, RET_V_W), att.reshape(batch, ATT_GW), nst.reshape(state.shape)


def kernel(x_prompt, x_sample, state_ret, cache_k_w128, cache_v_w128, cache_k_w512, cache_v_w512,
           cache_k_w2048, cache_v_w2048, p_prompt, p_sample, ln1_g, w_in, ret_gn_g, w_ret_br, w_att_br,
           w_out, ln2_g, w_up, w_down, w_ple, w_ple_gate, rel_bias, lnf_g):
    depth = w_in.shape[0]
    assert depth == 1
    batch, seq, _ = x_prompt.shape
    dec_batch, dec_seq, _ = x_sample.shape
    assert dec_seq == 1
    past_len = 16384
    l = 0
    ln1 = ln1_g[l][None, :]
    ln2 = ln2_g[l][None, :]
    lnf = lnf_g[None, :]
    gn = ret_gn_g[l][None, :]
    w_in_l = w_in[l]
    w_ret = jnp.concatenate([w_in_l[:, :COL_ATT], w_in_l[:, COL_GATE:]], axis=1).astype(BF16)
    w_att = w_in_l[:, COL_ATT:COL_GATE].astype(BF16)
    tail_w = (w_ret_br[l].astype(BF16), w_att_br[l].astype(BF16), w_out[l].astype(BF16), w_up[l].astype(BF16),
              w_down[l].astype(BF16), w_ple[l].astype(BF16), w_ple_gate[l].astype(BF16), ln2, lnf)
    biases = [_group_bias(rel_bias, g) for g in range(ATT_GROUPS)]

    caches = (cache_k_w128[l], cache_v_w128[l], cache_k_w512[l], cache_v_w512[l],
              cache_k_w2048[l], cache_v_w2048[l])

    xs = x_sample.reshape(dec_batch, D_MODEL)
    cos_s, sin_s = _rope_tables(jnp.full((dec_batch,), past_len, dtype=jnp.int32))
    sq, sk, sv, sg, sgr, sga = _inproj_ret(xs, ln1, w_ret, cos_s, sin_s, dec_batch, F32)
    s_qkv = _inproj_att_sample(xs, ln1, w_att)
    slot_off = jnp.concatenate([jnp.zeros((1,), jnp.int32), ATT_NK - jnp.arange(1, ATT_NK)])
    bias_cols = jnp.concatenate([b[:, slot_off] for b in biases], axis=0)[:, :, None]
    s_ret, s_attn, new_st = _decode(sq, sk, sv, sg, gn, state_ret[l], s_qkv, bias_cols, caches)
    ple_s = p_sample[l].reshape(dec_batch, D_PLE)
    y_s = _tail([s_ret, s_attn, sgr, sga, xs, ple_s], tail_w, dec_batch, False)
    y_sample = y_s.reshape(dec_batch, 1, D_MODEL)

    xp = x_prompt.reshape(batch * seq, D_MODEL)
    cos_p, sin_p = _rope_tables(jnp.arange(seq, dtype=jnp.int32))
    rq, rk, rv, rg, gr, ga = _inproj_ret(xp, ln1, w_ret, cos_p, sin_p, TM_INPROJ, BF16)
    att_o = _inproj_att_prompt(xp, ln1, w_att, TM_INPROJ, seq)
    aqs, aks, avs = att_o[0:3], att_o[3:6], att_o[6:9]
    kfull, vfull = att_o[9:12], att_o[12:15]
    ret_out, st_p = _retention_prompt(rq, rk, rv, rg, gn, batch, seq, TS_RETENTION)
    outs, lses = [], []
    for g in range(ATT_GROUPS):
        n_blocks = seq // ATT_DILATIONS[g] // ATT_NK
        tb = _band_tables(biases[g], n_blocks > 1)
        o, lse = _attention_prompt(aqs[g], aks[g], avs[g], tb, g)
        outs.append(o)
        lses.append(lse)
    ple_p = p_prompt[l].reshape(batch * seq, D_PLE)
    tm_tail = batch * seq // (2 * dec_batch)
    y_p, new_caches = _tail_shift([ret_out] + outs + lses + [gr, ga, xp, ple_p], tail_w, tm_tail, s_qkv, caches)
    y_prompt = y_p.reshape(batch, seq, D_MODEL)
    new_state_p = st_p[None]
    kv_p = []
    for g in range(ATT_GROUPS):
        shape = (1, batch, min(ATT_WINDOWS[g], seq), ATT_HPG, ATT_HD)
        kv_p.append(kfull[g].reshape(shape))
        kv_p.append(vfull[g].reshape(shape))
    kv_s = [c[None] for c in new_caches]

    return (y_prompt, y_sample, new_state_p, *kv_p, new_st[None], *kv_s)
```

```python
import functools
import math

import jax
import jax.numpy as jnp
import numpy as np
from jax import lax
from jax.experimental import pallas as pl
from jax.experimental.pallas import tpu as pltpu

F32 = jnp.float32
BF16 = jnp.bfloat16

D_MODEL = 1024
RET_HEADS = 8
RET_DK = 64
RET_DV = 128
RET_PAIRS = RET_HEADS // 2
RET_CHUNK = 128
ROPE_BASE = 10000.0
ATT_WINDOWS = (128, 512, 2048)
ATT_DILATIONS = (1, 4, 16)
ATT_GROUPS = 3
ATT_HPG = 4
ATT_HD = 128
ATT_NK = 128
ATT_GW = ATT_HPG * ATT_HD
REL_BUCKETS = 32
REL_MAX_DIST = 2048
D_FF = 4 * D_MODEL
D_PLE = 256
NORM_EPS = 1e-6
RET_QK_W = RET_HEADS * RET_DK
RET_V_W = RET_HEADS * RET_DV
ATT_W = ATT_GROUPS * ATT_GW
COL_RET = 0
COL_ATT = 2 * RET_QK_W + 2 * RET_V_W
COL_GATE = COL_ATT + 3 * ATT_W
N_IN = COL_GATE + 2 * D_MODEL

VMEM_LIMIT_V7X = 56 * 1024 * 1024
LANES = 128
TM_INPROJ = 512
TS_RETENTION = 512
ATTN_UNROLL = 4
PERM_ROWS = 256
SINGLE_BLOCK_GROUP = 4


def _dot(a, b):
    return jnp.dot(a, b, preferred_element_type=F32)


def _dot_nt(a, b):
    return lax.dot_general(a, b, (((1,), (1,)), ((), ())), preferred_element_type=F32)


def _dot_tn(a, b):
    return lax.dot_general(a, b, (((0,), (0,)), ((), ())), preferred_element_type=F32)


def _rms(x, g):
    return x * lax.rsqrt(jnp.mean(x * x, axis=-1, keepdims=True) + NORM_EPS) * g


def _sigmoid(x):
    return 1.0 / (1.0 + jnp.exp(-x))


def _resident(shape):
    return pl.BlockSpec(shape, lambda *_: (0,) * len(shape), pipeline_mode=pl.Buffered(1))


def _params(n_axes):
    return pltpu.CompilerParams(dimension_semantics=("arbitrary",) * n_axes,
                                vmem_limit_bytes=VMEM_LIMIT_V7X)


def _rope_inv_row():
    half = RET_DK // 2
    inv = ROPE_BASE ** (-jnp.arange(half, dtype=F32) / half)
    return jnp.tile(inv, LANES // half)[None, :]


def _ret_log_decay():
    return np.log1p(-np.exp2(-5.0 - np.arange(RET_HEADS, dtype=np.float32))).astype(np.float32)


def _ret_tables():
    c = RET_CHUNK
    lg = _ret_log_decay()
    i = np.arange(c, dtype=np.float32)
    diff = i[:, None] - i[None, :]
    dmask = np.where(diff[None] >= 0, np.exp(np.maximum(diff, 0.0)[None] * lg[:, None, None]), 0.0)
    q_decay = np.exp((i + 1.0)[:, None] * lg[None, :])
    k_decay = np.exp((c - 1.0 - i)[:, None] * lg[None, :])
    qdec = np.broadcast_to(q_decay.T[:, :, None], (RET_HEADS, c, RET_DV))
    kdec = np.repeat(k_decay, RET_DK, axis=1).reshape(c, RET_PAIRS, 2 * RET_DK).transpose(1, 0, 2)
    gc = np.repeat(np.exp(c * lg), RET_DV).reshape(RET_PAIRS, 1, 2 * RET_DV)
    return tuple(jnp.asarray(np.ascontiguousarray(t), F32) for t in (dmask, qdec, kdec, gc))


def _rel_buckets():
    max_exact = REL_BUCKETS // 2
    out = []
    for dil in ATT_DILATIONS:
        d = np.arange(ATT_NK, dtype=np.int32) * dil
        log_ratio = (np.log(np.maximum(d, 1).astype(np.float32) / np.float32(max_exact))
                     / np.float32(math.log(REL_MAX_DIST / max_exact)))
        large = max_exact + (log_ratio * np.float32(REL_BUCKETS - max_exact)).astype(np.int32)
        out.append(np.where(d < max_exact, d, np.minimum(large, REL_BUCKETS - 1)))
    return np.stack(out)


def _head_bias(rel_bias):
    buckets = np.repeat(_rel_buckets(), ATT_HPG, axis=0)
    head = np.arange(ATT_GROUPS * ATT_HPG)[:, None]
    return rel_bias[buckets, head].astype(F32)


def _band_table(bias):
    nk = ATT_NK
    nh = bias.shape[0]
    neg = lambda n: jnp.full((nh, n), -jnp.inf, F32)
    period = 3 * nk
    gvec = jnp.concatenate([neg(nk - 1), bias, neg(period - 2 * nk + 1)], axis=1)
    flat = jnp.tile(gvec, (1, nk + 1))[:, :nk * (period + 1)]
    hankel = flat.reshape(nh, nk, period + 1)[:, :, :2 * nk]
    return hankel[:, :, ::-1]


def _inproj_ret_body(x_ref, ln_ref, w_ref, inv_ref, q_ref, k_ref, v_ref, g_ref, gr_ref, ga_ref, cos_s, sin_s,
                     *, tm, tiles, fixed_pos):
    i = pl.program_id(0)
    lane = lax.broadcasted_iota(jnp.int32, (tm, LANES), 1)
    first_half = (lane % RET_DK) < (RET_DK // 2)

    def tables(pos):
        ang = pos.astype(F32) * inv_ref[...]
        sin = jnp.sin(ang)
        return jnp.cos(ang), jnp.where(first_half, -sin, sin)

    if fixed_pos is None:
        trow = pl.ds(pl.multiple_of(lax.rem(i, tiles) * tm, tm), tm)

        @pl.when(i < tiles)
        def _():
            cos_s[trow, :], sin_s[trow, :] = tables(i * tm + lax.broadcasted_iota(jnp.int32, (tm, LANES), 0))
        cos = cos_s[trow, :]
        sin = sin_s[trow, :]
    else:
        cos, sin = tables(jnp.full((tm, LANES), fixed_pos, jnp.int32))
    x = x_ref[...]
    h = _rms(x, ln_ref[...]).astype(BF16)
    qk = _dot(h, w_ref[:, 0:2 * RET_QK_W])
    n_q = RET_QK_W // LANES
    for c in range(2 * n_q):
        xc = qk[:, c * LANES:(c + 1) * LANES]
        swapped = jnp.where(first_half, pltpu.roll(xc, LANES - RET_DK // 2, 1), pltpu.roll(xc, RET_DK // 2, 1))
        r = xc * cos + swapped * sin
        if c < n_q:
            q_ref[:, c * LANES:(c + 1) * LANES] = r.astype(q_ref.dtype)
        else:
            k_ref[:, (c - n_q) * LANES:(c - n_q + 1) * LANES] = (r * (RET_DK ** -0.5)).astype(k_ref.dtype)
    o = 2 * RET_QK_W
    v_ref[...] = _dot(h, w_ref[:, o:o + RET_V_W]).astype(v_ref.dtype)
    o += RET_V_W
    g_ref[...] = _dot(h, w_ref[:, o:o + RET_V_W]).astype(g_ref.dtype)
    o += RET_V_W
    gr_ref[...] = _dot(h, w_ref[:, o:o + D_MODEL]).astype(gr_ref.dtype)
    o += D_MODEL
    ga_ref[...] = _dot(h, w_ref[:, o:o + D_MODEL]).astype(ga_ref.dtype)


def _inproj_ret(x2d, ln, w, inv_row, tm, out_dtype, seq=None, fixed_pos=None):
    n = x2d.shape[0]
    tiles = None if seq is None else seq // tm
    table_rows = tm if seq is None else seq
    wcols = w.shape[1]
    row = lambda width: pl.BlockSpec((tm, width), lambda i: (i, 0))
    widths = (RET_QK_W, RET_QK_W, RET_V_W, RET_V_W, D_MODEL, D_MODEL)
    return pl.pallas_call(
        functools.partial(_inproj_ret_body, tm=tm, tiles=tiles, fixed_pos=fixed_pos),
        grid=(n // tm,),
        in_specs=[row(D_MODEL), _resident((1, D_MODEL)), _resident((D_MODEL, wcols)), _resident((1, LANES))],
        out_specs=[row(wd) for wd in widths],
        out_shape=[jax.ShapeDtypeStruct((n, wd), out_dtype) for wd in widths],
        scratch_shapes=[pltpu.VMEM((table_rows, LANES), F32), pltpu.VMEM((table_rows, LANES), F32)],
        compiler_params=_params(1),
        name="inproj_ret",
    )(x2d, ln, w, inv_row)


def _inproj_att_body(x_ref, ln_ref, w_ref, *refs, tm, keeps, seq):
    lowp = refs[:3 * ATT_GROUPS]
    full = refs[3 * ATT_GROUPS:5 * ATT_GROUPS]
    res = refs[5 * ATT_GROUPS]
    x = x_ref[...]
    h = _rms(x, ln_ref[...]).astype(BF16)
    slot = 0
    for kind in range(3):
        for g in range(ATT_GROUPS):
            o = kind * ATT_W + g * ATT_GW
            dil = ATT_DILATIONS[g]
            dst = lowp[kind * ATT_GROUPS + g]
            r = _dot(h, w_ref[:, o:o + ATT_GW])
            if dil == 1:
                dst[0] = r.astype(dst.dtype)
            else:
                for hh in range(ATT_HPG):
                    res[slot, hh] = r[:, hh * ATT_HD:(hh + 1) * ATT_HD]
                for rr in range(dil):
                    for hh in range(ATT_HPG):
                        dst[rr, :, hh * ATT_HD:(hh + 1) * ATT_HD] = (
                            res[slot, hh, pl.ds(rr, tm // dil, stride=dil), :].astype(dst.dtype))
                slot += 1
            if kind > 0:
                cache = full[(kind - 1) * ATT_GROUPS + g]
                rows = tm if keeps[g] == seq else keeps[g]
                for hh in range(ATT_HPG):
                    cache[pl.ds(hh, rows, stride=ATT_HPG), :] = r[tm - rows:, hh * ATT_HD:(hh + 1) * ATT_HD]


def _inproj_att_prompt(x2d, ln, w, tm, seq):
    n = x2d.shape[0]
    batch = n // seq
    tiles = seq // tm
    keeps = tuple(min(wd, seq) for wd in ATT_WINDOWS)
    assert all(kp <= tm or kp == seq for kp in keeps) and seq % tm == 0
    out_specs, out_shape = [], []
    for _ in range(3):
        for g in range(ATT_GROUPS):
            dil = ATT_DILATIONS[g]
            out_specs.append(pl.BlockSpec((None, dil, tm // dil, ATT_GW), lambda i: (i // tiles, 0, i % tiles, 0)))
            out_shape.append(jax.ShapeDtypeStruct((batch, dil, seq // dil, ATT_GW), BF16))
    for _ in range(2):
        for g in range(ATT_GROUPS):
            if keeps[g] == seq:
                idx = lambda i: (i // tiles, i % tiles, 0)
                rows = tm
            else:
                idx = lambda i: (i // tiles, 0, 0)
                rows = keeps[g]
            out_specs.append(pl.BlockSpec((None, rows * ATT_HPG, ATT_HD), idx))
            out_shape.append(jax.ShapeDtypeStruct((batch, keeps[g] * ATT_HPG, ATT_HD), F32))
    return pl.pallas_call(
        functools.partial(_inproj_att_body, tm=tm, keeps=keeps, seq=seq),
        grid=(n // tm,),
        in_specs=[pl.BlockSpec((tm, D_MODEL), lambda i: (i, 0)), _resident((1, D_MODEL)),
                  _resident((D_MODEL, 3 * ATT_W))],
        out_specs=out_specs,
        out_shape=out_shape,
        scratch_shapes=[pltpu.VMEM((3 * sum(d > 1 for d in ATT_DILATIONS), ATT_HPG, tm, ATT_HD), F32)],
        compiler_params=_params(1),
        name="inproj_att",
    )(x2d, ln, w)


def _inproj_att_sample_body(x_ref, ln_ref, w_ref, o_ref):
    h = _rms(x_ref[...], ln_ref[...]).astype(BF16)
    for c in range(3 * ATT_GROUPS):
        o_ref[:, c * ATT_GW:(c + 1) * ATT_GW] = _dot(h, w_ref[:, c * ATT_GW:(c + 1) * ATT_GW])


def _inproj_att_sample(x2d, ln, w):
    n = x2d.shape[0]
    return pl.pallas_call(
        _inproj_att_sample_body,
        grid=(1,),
        in_specs=[pl.BlockSpec((n, D_MODEL), lambda i: (0, 0)), _resident((1, D_MODEL)),
                  _resident((D_MODEL, 3 * ATT_W))],
        out_specs=pl.BlockSpec((n, 3 * ATT_W), lambda i: (0, 0)),
        out_shape=jax.ShapeDtypeStruct((n, 3 * ATT_W), F32),
        compiler_params=_params(1),
        name="inproj_att_sample",
    )(x2d, ln, w)


def _gn_swish(o, gate, gn):
    mu = jnp.mean(o, axis=-1, keepdims=True)
    d = o - mu
    var = jnp.mean(d * d, axis=-1, keepdims=True)
    on = d * lax.rsqrt(var + NORM_EPS) * gn
    return gate * _sigmoid(gate) * on


def _retention_body(q_ref, k_ref, v_ref, g_ref, gn_ref, dm_ref, qdec_ref, kdec_ref, gc_ref,
                    out_ref, st_ref, state, *, n_chunks):
    c = RET_CHUNK

    @pl.when(pl.program_id(1) == 0)
    def _():
        state[...] = jnp.zeros_like(state)

    lane = lax.broadcasted_iota(jnp.int32, (c, 2 * RET_DK), 1)
    head0 = lane < RET_DK

    def chunk(ci, carry):
        rows = pl.ds(pl.multiple_of(ci * c, c), c)
        for p in range(RET_PAIRS):
            q2 = q_ref[rows, p * 2 * RET_DK:(p + 1) * 2 * RET_DK]
            k2 = k_ref[rows, p * 2 * RET_DK:(p + 1) * 2 * RET_DK]
            v2 = v_ref[rows, p * 2 * RET_DV:(p + 1) * 2 * RET_DV]
            pst = state[p]
            pst_lo = pst.astype(BF16)
            zero = jnp.zeros_like(q2)
            for hh in range(2):
                h = 2 * p + hh
                qm = jnp.where(head0 if hh == 0 else jnp.logical_not(head0), q2, zero)
                vh = v2[:, hh * RET_DV:(hh + 1) * RET_DV]
                s = _dot_nt(qm, k2) * dm_ref[h]
                o = _dot(s.astype(BF16), vh)
                o = o + _dot(qm, pst_lo[:, hh * RET_DV:(hh + 1) * RET_DV]) * qdec_ref[h]
                gate = g_ref[rows, h * RET_DV:(h + 1) * RET_DV].astype(F32)
                res = _gn_swish(o, gate, gn_ref[:, h * RET_DV:(h + 1) * RET_DV])
                out_ref[rows, h * RET_DV:(h + 1) * RET_DV] = res.astype(out_ref.dtype)
            kd = (k2.astype(F32) * kdec_ref[p]).astype(BF16)
            state[p] = pst * gc_ref[p] + _dot_tn(kd, v2)
        return carry

    lax.fori_loop(0, n_chunks, chunk, 0, unroll=True)
    for p in range(RET_PAIRS):
        pst = state[p]
        for hh in range(2):
            st_ref[2 * p + hh] = pst[hh * RET_DK:(hh + 1) * RET_DK, hh * RET_DV:(hh + 1) * RET_DV]


def _retention_prompt(q, k, v, g, gn, batch, seq, ts):
    n = q.shape[0]
    steps = seq // ts
    dmask, qdec, kdec, gc = _ret_tables()
    row = lambda width: pl.BlockSpec((ts, width), lambda b, s: (b * steps + s, 0))
    const = lambda a: pl.BlockSpec(a.shape, lambda b, s: (0,) * a.ndim)
    return pl.pallas_call(
        functools.partial(_retention_body, n_chunks=ts // RET_CHUNK),
        grid=(batch, steps),
        in_specs=[row(RET_QK_W), row(RET_QK_W), row(RET_V_W), row(RET_V_W), const(gn),
                  const(dmask), const(qdec), const(kdec), const(gc)],
        out_specs=[row(RET_V_W),
                   pl.BlockSpec((None, RET_HEADS, RET_DK, RET_DV), lambda b, s: (b, 0, 0, 0))],
        out_shape=[jax.ShapeDtypeStruct((n, RET_V_W), BF16),
                   jax.ShapeDtypeStruct((batch, RET_HEADS, RET_DK, RET_DV), F32)],
        scratch_shapes=[pltpu.VMEM((RET_PAIRS, 2 * RET_DK, 2 * RET_DV), F32)],
        compiler_params=_params(2),
        name="retention",
    )(q, k, v, g, gn, dmask, qdec, kdec, gc)


def _attn_body(q_ref, k_ref, v_ref, tb_ref, o_ref, lse_ref, *scratch, dil, n_blocks):
    nk = ATT_NK
    scale = ATT_HD ** -0.5
    lane = lax.broadcasted_iota(jnp.int32, (nk, LANES), 1)
    if n_blocks > 1:
        tabs = scratch[-1]
        col = lax.broadcasted_iota(jnp.int32, (nk, 2 * nk), 1)
        for h in range(ATT_HPG):
            tabs[0, h] = jnp.where(col < nk, -jnp.inf, tb_ref[h])
            tabs[1, h] = tb_ref[h]
    if dil > 1:
        o_s, l_s = scratch[:2]

    def chain(s_raw, v_ext, sel, h):
        s = s_raw * scale + (tabs[sel, h] if n_blocks > 1 else tb_ref[h, :, nk:])
        m = jnp.max(s, axis=-1, keepdims=True)
        p = jnp.exp(s - m)
        pv = _dot(p.astype(BF16), v_ext)
        den = pv[:, ATT_HD:]
        return (pv[:, :ATT_HD] / den).astype(BF16), m + jnp.log(den)

    def write(rs, rows, outs):
        lse_tile = jnp.zeros((nk, LANES), F32)
        for h, (o, lse) in enumerate(outs):
            hc = slice(h * ATT_HD, (h + 1) * ATT_HD)
            if dil == 1:
                o_ref[rows, hc] = o
            else:
                o_s[rs, rows, hc] = o
            lse_tile = jnp.where(lane == h, lse, lse_tile)
        if dil == 1:
            lse_ref[rows, :] = lse_tile
        else:
            l_s[rs, rows, :] = lse_tile

    heads = [slice(h * ATT_HD, (h + 1) * ATT_HD) for h in range(ATT_HPG)]

    def block(rs, n, n_prev, sel):
        rows = pl.ds(pl.multiple_of(n * nk, nk), nk)
        prev_rows = pl.ds(pl.multiple_of(n_prev * nk, nk), nk)
        q = q_ref[rs, rows, :]
        k_all = jnp.concatenate([k_ref[rs, prev_rows, :], k_ref[rs, rows, :]], axis=0)
        v_all = jnp.concatenate([v_ref[rs, prev_rows, :], v_ref[rs, rows, :]], axis=0)
        ones = jnp.ones((2 * nk, ATT_HD), BF16)
        outs = []
        for h, hc in enumerate(heads):
            v_ext = jnp.concatenate([v_all[:, hc], ones], axis=1)
            outs.append(chain(_dot_nt(q[:, hc], k_all[:, hc]), v_ext, sel, h))
        write(rs, rows, outs)

    def single_blocks(streams):
        ones = jnp.ones((nk, ATT_HD), BF16)
        scores = [[_dot_nt(q_ref[rs, :, hc], k_ref[rs, :, hc]) for hc in heads] for rs in streams]
        for i, rs in enumerate(streams):
            outs = [chain(scores[i][h], jnp.concatenate([v_ref[rs, :, hc], ones], axis=1), 0, h)
                    for h, hc in enumerate(heads)]
            write(rs, slice(None), outs)

    if n_blocks == 1:
        for r0 in range(0, dil, SINGLE_BLOCK_GROUP):
            single_blocks(range(r0, min(r0 + SINGLE_BLOCK_GROUP, dil)))
    else:
        for rs in range(dil):
            def loop(n, carry, rs=rs):
                block(rs, n, jnp.maximum(n - 1, 0), jnp.minimum(n, 1))
                return carry
            lax.fori_loop(0, n_blocks, loop, 0, unroll=ATTN_UNROLL)

    if dil > 1:
        piece = PERM_ROWS // dil
        row = lax.broadcasted_iota(jnp.int32, (PERM_ROWS, PERM_ROWS), 0)
        col = lax.broadcasted_iota(jnp.int32, (PERM_ROWS, PERM_ROWS), 1)
        perm = jnp.where((col % piece) * dil + col // piece == row, 1.0, 0.0).astype(BF16)
        for j in range(o_ref.shape[0] // PERM_ROWS):
            src = slice(j * piece, (j + 1) * piece)
            dst = slice(j * PERM_ROWS, (j + 1) * PERM_ROWS)
            o_ref[dst, :] = _dot(perm, jnp.concatenate([o_s[r, src, :] for r in range(dil)], axis=0)).astype(BF16)
            lse = jnp.concatenate([l_s[r, src, :] for r in range(dil)], axis=0)
            hi = lse.astype(BF16)
            rem = lse - hi.astype(F32)
            mid = rem.astype(BF16)
            lo = (rem - mid.astype(F32)).astype(BF16)
            lse_ref[dst, :] = _dot(perm, hi) + _dot(perm, mid) + _dot(perm, lo)


def _attention_prompt(aq, ak, av, tb, g):
    batch, dil, length, _ = aq.shape
    seq = dil * length
    n_blocks = length // ATT_NK
    blk = pl.BlockSpec((None, dil, length, ATT_GW), lambda b: (b, 0, 0, 0))
    scratch = [pltpu.VMEM((dil, length, ATT_GW), BF16), pltpu.VMEM((dil, length, LANES), F32)] if dil > 1 else []
    if n_blocks > 1:
        scratch.append(pltpu.VMEM((2, ATT_HPG, ATT_NK, 2 * ATT_NK), F32))
    o, lse = pl.pallas_call(
        functools.partial(_attn_body, dil=dil, n_blocks=n_blocks),
        grid=(batch,),
        in_specs=[blk, blk, blk, pl.BlockSpec((ATT_HPG, ATT_NK, 2 * ATT_NK), lambda b: (g, 0, 0))],
        out_specs=[pl.BlockSpec((None, seq, ATT_GW), lambda b: (b, 0, 0)),
                   pl.BlockSpec((None, seq, LANES), lambda b: (b, 0, 0))],
        out_shape=[jax.ShapeDtypeStruct((batch, seq, ATT_GW), BF16),
                   jax.ShapeDtypeStruct((batch, seq, LANES), F32)],
        scratch_shapes=scratch,
        compiler_params=_params(1),
        name="attention_g%d" % g,
    )(aq, ak, av, tb)
    return o.reshape(batch * seq, ATT_GW), lse.reshape(batch * seq, LANES)


def _tail_math(act_refs, w_refs, y_ref, combine):
    wrb_ref, wab_ref, wo_ref, wu_ref, wd_ref, wpl_ref, wpg_ref, ln2_ref, lnf_ref = w_refs
    if combine:
        ret_ref, o0_ref, o1_ref, o2_ref, l0_ref, l1_ref, l2_ref, gr_ref, ga_ref, x_ref, ple_ref = act_refs
        lses = [l0_ref[...], l1_ref[...], l2_ref[...]]
        outs = [o0_ref, o1_ref, o2_ref]
        parts = []
        for h in range(ATT_HPG):
            lh = [l[:, h:h + 1] for l in lses]
            mx = jnp.maximum(jnp.maximum(lh[0], lh[1]), lh[2])
            e = [jnp.exp(l - mx) for l in lh]
            tot = e[0] + e[1] + e[2]
            acc = None
            for g in range(ATT_GROUPS):
                term = (e[g] / tot) * outs[g][:, h * ATT_HD:(h + 1) * ATT_HD].astype(F32)
                acc = term if acc is None else acc + term
            parts.append(acc)
        att = jnp.concatenate(parts, axis=1).astype(BF16)
    else:
        ret_ref, att_ref, gr_ref, ga_ref, x_ref, ple_ref = act_refs
        att = att_ref[...].astype(BF16)
    a = _dot(ret_ref[...].astype(BF16), wrb_ref[...])
    b = _dot(att, wab_ref[...])
    mixed = _sigmoid(gr_ref[...].astype(F32)) * a + _sigmoid(ga_ref[...].astype(F32)) * b
    x1 = x_ref[...] + _dot(mixed.astype(BF16), wo_ref[...])
    h2 = _rms(x1, ln2_ref[...]).astype(BF16)
    ff_chunk = D_MODEL
    acc = None
    for c in range(D_FF // ff_chunk):
        u = _dot(h2, wu_ref[:, c * ff_chunk:(c + 1) * ff_chunk])
        r = jnp.maximum(u, 0.0)
        t = _dot((r * r).astype(BF16), wd_ref[c * ff_chunk:(c + 1) * ff_chunk, :])
        acc = t if acc is None else acc + t
    x2 = x1 + acc
    gate = _sigmoid(_dot(x2.astype(BF16), wpg_ref[...]))
    x3 = x2 + gate * _dot(ple_ref[...].astype(BF16), wpl_ref[...])
    y_ref[...] = _rms(x3, lnf_ref[...])


def _tail_body(*refs, n_act, combine):
    _tail_math(refs[:n_act], refs[n_act:-1], refs[-1], combine)


def _tail(acts, weights, tm, combine):
    n = acts[0].shape[0]
    row = lambda a: pl.BlockSpec((tm, a.shape[1]), lambda i: (i, 0))
    return pl.pallas_call(
        functools.partial(_tail_body, n_act=len(acts), combine=combine),
        grid=(n // tm,),
        in_specs=[row(a) for a in acts] + [_resident(w.shape) for w in weights],
        out_specs=pl.BlockSpec((tm, D_MODEL), lambda i: (i, 0)),
        out_shape=jax.ShapeDtypeStruct((n, D_MODEL), F32),
        compiler_params=_params(1),
        name="tail" if combine else "tail_sample",
    )(*acts, *weights)


def _tail_shift_body(*refs, n_act, n_w, n_steps):
    nc = 2 * ATT_GROUPS
    acts = refs[:n_act]
    ws = refs[n_act:n_act + n_w]
    new_ref = refs[n_act + n_w]
    base = n_act + n_w + 1
    old = refs[base:base + nc]
    y_ref = refs[base + nc]
    out = refs[base + nc + 1:base + 2 * nc + 1]
    stage = refs[base + 2 * nc + 1:base + 3 * nc + 1]
    sem_in, sem_out, sem_row = refs[base + 3 * nc + 1:]
    s = pl.program_id(0)
    b = lax.shift_right_logical(s, 1)
    parity = lax.rem(s, 2)

    def copy_in(i, bi):
        width = old[i].shape[1]
        return pltpu.make_async_copy(old[i].at[bi, pl.ds(1, width - 1)], stage[i], sem_in.at[i])

    def copy_out(i, bi):
        width = old[i].shape[1]
        return pltpu.make_async_copy(stage[i], out[i].at[bi, pl.ds(0, width - 1)], sem_out.at[i])

    def copy_row(g, kind):
        i = 2 * g + kind
        return pltpu.make_async_copy(new_ref.at[b, kind + 1, g], out[i].at[b, old[i].shape[1] - 1], sem_row.at[g])

    for kind in range(2):
        mine = [2 * g + kind for g in range(ATT_GROUPS)]
        other = [2 * g + 1 - kind for g in range(ATT_GROUPS)]

        @pl.when(parity == kind)
        def _(kind=kind, mine=mine, other=other):
            if kind == 0:
                @pl.when(s == 0)
                def _():
                    for i in mine:
                        copy_in(i, 0).start()
            for i in mine:
                copy_in(i, b).wait()
            for i in mine:
                copy_out(i, b).start()
            if kind == 0:
                @pl.when(s > 0)
                def _():
                    for i in other:
                        copy_out(i, b - 1).wait()
                for i in other:
                    copy_in(i, b).start()
            else:
                for i in other:
                    copy_out(i, b).wait()

                @pl.when(s < n_steps - 1)
                def _():
                    for i in other:
                        copy_in(i, b + 1).start()
            for g in range(ATT_GROUPS):
                copy_row(g, kind).start()

    _tail_math(acts, ws, y_ref, True)

    for kind in range(2):
        @pl.when(parity == kind)
        def _(kind=kind):
            for g in range(ATT_GROUPS):
                copy_row(g, kind).wait()

    @pl.when(s == n_steps - 1)
    def _():
        for g in range(ATT_GROUPS):
            copy_out(2 * g + 1, b).wait()


def _tail_shift(acts, weights, tm, new_qkv, caches):
    n = acts[0].shape[0]
    n_steps = n // tm
    assert n_steps == 2 * new_qkv.shape[0]
    row = lambda a: pl.BlockSpec((tm, a.shape[1]), lambda i: (i, 0))
    anyspec = pl.BlockSpec(memory_space=pl.ANY)
    nc = len(caches)
    new_rows = new_qkv.reshape(new_qkv.shape[0], 3, ATT_GROUPS, ATT_HPG, ATT_HD)
    res = pl.pallas_call(
        functools.partial(_tail_shift_body, n_act=len(acts), n_w=len(weights), n_steps=n_steps),
        grid=(n_steps,),
        in_specs=[row(a) for a in acts] + [_resident(w.shape) for w in weights] + [anyspec] * (nc + 1),
        out_specs=[pl.BlockSpec((tm, D_MODEL), lambda i: (i, 0))] + [anyspec] * nc,
        out_shape=[jax.ShapeDtypeStruct((n, D_MODEL), F32)]
        + [jax.ShapeDtypeStruct(c.shape, c.dtype) for c in caches],
        scratch_shapes=[pltpu.VMEM((c.shape[1] - 1, ATT_HPG, ATT_HD), F32) for c in caches]
        + [pltpu.SemaphoreType.DMA((nc,)), pltpu.SemaphoreType.DMA((nc,)),
           pltpu.SemaphoreType.DMA((ATT_GROUPS,))],
        compiler_params=_params(1),
        name="tail",
    )(*acts, *weights, new_rows, *caches)
    return res[0], res[1:]


def _decode_body(q_ref, k_ref, v_ref, g_ref, gn_ref, st_ref, gam_ref,
                 qkv_ref, bias_ref,
                 ck0_ref, cv0_ref, ck1_ref, cv1_ref, ck2_ref, cv2_ref,
                 ret_ref, att_ref, nst_ref):
    sub = lax.broadcasted_iota(jnp.int32, (8, 2 * RET_DK), 0)
    lane = lax.broadcasted_iota(jnp.int32, (8, 2 * RET_DK), 1)
    row0 = sub == 0
    srow = lax.broadcasted_iota(jnp.int32, (2 * RET_DK, RET_DV), 0)
    for p in range(RET_PAIRS):
        pc = slice(p * 2 * RET_DK, (p + 1) * 2 * RET_DK)
        q2 = jnp.where(row0, jnp.broadcast_to(q_ref[:, pc], (8, 2 * RET_DK)), 0.0)
        k2 = jnp.where(row0, jnp.broadcast_to(k_ref[:, pc], (8, 2 * RET_DK)), 0.0)
        pst = st_ref[p]
        gam = gam_ref[p]
        outer = []
        for hh in range(2):
            h = 2 * p + hh
            hsel = (lane < RET_DK) if hh == 0 else (lane >= RET_DK)
            qm = jnp.where(hsel, q2, 0.0)
            km = jnp.where(hsel, k2, 0.0)
            vh = v_ref[:, h * RET_DV:(h + 1) * RET_DV]
            v8 = jnp.where(row0[:, :RET_DV], jnp.broadcast_to(vh, (8, RET_DV)), 0.0)
            cross = _dot(qm.astype(BF16), (pst * gam).astype(BF16))[0:1, :]
            qk = jnp.sum(qm[0:1, :] * km[0:1, :], axis=-1, keepdims=True)
            o = cross + qk * vh
            res = _gn_swish(o, g_ref[:, h * RET_DV:(h + 1) * RET_DV], gn_ref[:, h * RET_DV:(h + 1) * RET_DV])
            ret_ref[:, h * RET_DV:(h + 1) * RET_DV] = res
            outer.append(_dot_tn(k2.astype(BF16), v8.astype(BF16)))
        nst_ref[p] = pst * gam + jnp.where(srow < RET_DK, outer[0], outer[1])
    scale = ATT_HD ** -0.5
    caches = ((ck0_ref, cv0_ref), (ck1_ref, cv1_ref), (ck2_ref, cv2_ref))
    slot = lax.broadcasted_iota(jnp.int32, (ATT_NK, ATT_HD), 0)
    is_new = slot == 0
    for h in range(ATT_HPG):
        o_g, lse_g = [], []
        for g in range(ATT_GROUPS):
            ck_ref, cv_ref = caches[g]
            c0 = g * ATT_GW + h * ATT_HD
            qh = qkv_ref[:, c0:c0 + ATT_HD]
            kk = jnp.where(is_new, qkv_ref[:, ATT_W + c0:ATT_W + c0 + ATT_HD], ck_ref[:, h, :])
            vv = jnp.where(is_new, qkv_ref[:, 2 * ATT_W + c0:2 * ATT_W + c0 + ATT_HD], cv_ref[:, h, :])
            s = jnp.sum(kk * qh, axis=-1, keepdims=True) * scale + bias_ref[g * ATT_HPG + h]
            m = jnp.max(s, axis=0, keepdims=True)
            pr = jnp.exp(s - m)
            den = jnp.sum(pr, axis=0, keepdims=True)
            o_g.append(jnp.sum(pr * vv, axis=0, keepdims=True) / den)
            lse_g.append(m + jnp.log(den))
        mx = jnp.maximum(jnp.maximum(lse_g[0], lse_g[1]), lse_g[2])
        e = [jnp.exp(l - mx) for l in lse_g]
        tot = e[0] + e[1] + e[2]
        att_ref[:, h * ATT_HD:(h + 1) * ATT_HD] = ((e[0] / tot) * o_g[0] + (e[1] / tot) * o_g[1]
                                                   + (e[2] / tot) * o_g[2])


def _decode(q, k, v, g, gn, state, qkv, bias_cols, caches):
    batch = q.shape[0]
    gam = jnp.asarray(np.ascontiguousarray(np.broadcast_to(
        np.repeat(np.exp(_ret_log_decay()), RET_DK).reshape(RET_PAIRS, 2 * RET_DK, 1),
        (RET_PAIRS, 2 * RET_DK, RET_DV))), F32)
    vec = lambda a: a.reshape(batch, 1, a.shape[1])
    vspec = lambda width: pl.BlockSpec((None, 1, width), lambda b: (b, 0, 0))
    const = lambda a: pl.BlockSpec(a.shape, lambda b: (0,) * a.ndim)
    st_spec = pl.BlockSpec((None, RET_PAIRS, 2 * RET_DK, RET_DV), lambda b: (b, 0, 0, 0))
    cache_in, cache_specs = [], []
    for gi in range(ATT_GROUPS):
        dil = ATT_DILATIONS[gi]
        for c in caches[2 * gi:2 * gi + 2]:
            cache_in.append(c.reshape(batch, ATT_NK, dil, ATT_HPG, ATT_HD))
            cache_specs.append(pl.BlockSpec((None, ATT_NK, None, ATT_HPG, ATT_HD), lambda b: (b, 0, 0, 0, 0)))
    st_pairs = state.reshape(batch, RET_PAIRS, 2 * RET_DK, RET_DV)
    ret, att, nst = pl.pallas_call(
        _decode_body,
        grid=(batch,),
        in_specs=[vspec(RET_QK_W), vspec(RET_QK_W), vspec(RET_V_W), vspec(RET_V_W), const(gn), st_spec, const(gam),
                  vspec(3 * ATT_W), const(bias_cols)] + cache_specs,
        out_specs=[vspec(RET_V_W), vspec(ATT_GW), st_spec],
        out_shape=[jax.ShapeDtypeStruct((batch, 1, RET_V_W), F32),
                   jax.ShapeDtypeStruct((batch, 1, ATT_GW), F32),
                   jax.ShapeDtypeStruct(st_pairs.shape, F32)],
        compiler_params=_params(1),
        name="decode",
    )(vec(q), vec(k), vec(v), vec(g), gn, st_pairs, gam, vec(qkv), bias_cols, *cache_in)
    return ret.reshape(batch, RET_V_W), att.reshape(batch, ATT_GW), nst.reshape(state.shape)


def kernel(x_prompt, x_sample, state_ret, cache_k_w128, cache_v_w128, cache_k_w512, cache_v_w512,
           cache_k_w2048, cache_v_w2048, p_prompt, p_sample, ln1_g, w_in, ret_gn_g, w_ret_br, w_att_br,
           w_out, ln2_g, w_up, w_down, w_ple, w_ple_gate, rel_bias, lnf_g):
    depth = w_in.shape[0]
    assert depth == 1
    batch, seq, _ = x_prompt.shape
    dec_batch, dec_seq, _ = x_sample.shape
    assert dec_seq == 1
    past_len = 16384
    l = 0
    ln1 = ln1_g[l][None, :]
    ln2 = ln2_g[l][None, :]
    lnf = lnf_g[None, :]
    gn = ret_gn_g[l][None, :]
    w_in_l = w_in[l]
    w_ret = jnp.concatenate([w_in_l[:, :COL_ATT], w_in_l[:, COL_GATE:]], axis=1).astype(BF16)
    w_att = w_in_l[:, COL_ATT:COL_GATE].astype(BF16)
    tail_w = (w_ret_br[l].astype(BF16), w_att_br[l].astype(BF16), w_out[l].astype(BF16), w_up[l].astype(BF16),
              w_down[l].astype(BF16), w_ple[l].astype(BF16), w_ple_gate[l].astype(BF16), ln2, lnf)
    head_bias = _head_bias(rel_bias)
    inv_row = _rope_inv_row()

    caches = (cache_k_w128[l], cache_v_w128[l], cache_k_w512[l], cache_v_w512[l],
              cache_k_w2048[l], cache_v_w2048[l])

    xs = x_sample.reshape(dec_batch, D_MODEL)
    sq, sk, sv, sg, sgr, sga = _inproj_ret(xs, ln1, w_ret, inv_row, dec_batch, F32, fixed_pos=past_len)
    s_qkv = _inproj_att_sample(xs, ln1, w_att)
    slot_off = np.concatenate([np.zeros((1,), np.int32), ATT_NK - np.arange(1, ATT_NK, dtype=np.int32)])
    bias_cols = head_bias[:, slot_off][:, :, None]
    s_ret, s_attn, new_st = _decode(sq, sk, sv, sg, gn, state_ret[l], s_qkv, bias_cols, caches)
    ple_s = p_sample[l].reshape(dec_batch, D_PLE)
    y_s = _tail([s_ret, s_attn, sgr, sga, xs, ple_s], tail_w, dec_batch, False)
    y_sample = y_s.reshape(dec_batch, 1, D_MODEL)

    xp = x_prompt.reshape(batch * seq, D_MODEL)
    rq, rk, rv, rg, gr, ga = _inproj_ret(xp, ln1, w_ret, inv_row, TM_INPROJ, BF16, seq=seq)
    att_o = _inproj_att_prompt(xp, ln1, w_att, TM_INPROJ, seq)
    aqs, aks, avs = att_o[0:3], att_o[3:6], att_o[6:9]
    kfull, vfull = att_o[9:12], att_o[12:15]
    ret_out, st_p = _retention_prompt(rq, rk, rv, rg, gn, batch, seq, TS_RETENTION)
    outs, lses = [], []
    tb = _band_table(head_bias)
    for g in range(ATT_GROUPS):
        o, lse = _attention_prompt(aqs[g], aks[g], avs[g], tb, g)
        outs.append(o)
        lses.append(lse)
    ple_p = p_prompt[l].reshape(batch * seq, D_PLE)
    tm_tail = batch * seq // (2 * dec_batch)
    y_p, new_caches = _tail_shift([ret_out] + outs + lses + [gr, ga, xp, ple_p], tail_w, tm_tail, s_qkv, caches)
    y_prompt = y_p.reshape(batch, seq, D_MODEL)
    new_state_p = st_p[None]
    kv_p = []
    for g in range(ATT_GROUPS):
        shape = (1, batch, min(ATT_WINDOWS[g], seq), ATT_HPG, ATT_HD)
        kv_p.append(kfull[g].reshape(shape))
        kv_p.append(vfull[g].reshape(shape))
    kv_s = [c[None] for c in new_caches]

    return (y_prompt, y_sample, new_state_p, *kv_p, new_st[None], *kv_s)
```

```python
import functools
import math

import jax
import jax.numpy as jnp
import numpy as np
from jax import lax
from jax.experimental import pallas as pl
from jax.experimental.pallas import tpu as pltpu

F32 = jnp.float32
BF16 = jnp.bfloat16

D_MODEL = 1024
RET_HEADS = 8
RET_DK = 64
RET_DV = 128
RET_PAIRS = RET_HEADS // 2
RET_CHUNK = 128
ROPE_BASE = 10000.0
ATT_WINDOWS = (128, 512, 2048)
ATT_DILATIONS = (1, 4, 16)
ATT_GROUPS = 3
ATT_HPG = 4
ATT_HD = 128
ATT_NK = 128
ATT_GW = ATT_HPG * ATT_HD
REL_BUCKETS = 32
REL_MAX_DIST = 2048
D_FF = 4 * D_MODEL
D_PLE = 256
NORM_EPS = 1e-6
RET_QK_W = RET_HEADS * RET_DK
RET_V_W = RET_HEADS * RET_DV
ATT_W = ATT_GROUPS * ATT_GW
COL_RET = 0
COL_ATT = 2 * RET_QK_W + 2 * RET_V_W
COL_GATE = COL_ATT + 3 * ATT_W
N_IN = COL_GATE + 2 * D_MODEL

VMEM_LIMIT_V7X = 56 * 1024 * 1024
LANES = 128
TM_INPROJ = 512
TS_RETENTION = 512
ATTN_UNROLL = 4
PERM_ROWS = 256
SINGLE_BLOCK_GROUP = 4


def _dot(a, b):
    return jnp.dot(a, b, preferred_element_type=F32)


def _dot_nt(a, b):
    return lax.dot_general(a, b, (((1,), (1,)), ((), ())), preferred_element_type=F32)


def _dot_tn(a, b):
    return lax.dot_general(a, b, (((0,), (0,)), ((), ())), preferred_element_type=F32)


def _rms(x, g):
    return x * lax.rsqrt(jnp.mean(x * x, axis=-1, keepdims=True) + NORM_EPS) * g


def _sigmoid(x):
    return 1.0 / (1.0 + jnp.exp(-x))


def _resident(shape):
    return pl.BlockSpec(shape, lambda *_: (0,) * len(shape), pipeline_mode=pl.Buffered(1))


def _params(n_axes):
    return pltpu.CompilerParams(dimension_semantics=("arbitrary",) * n_axes,
                                vmem_limit_bytes=VMEM_LIMIT_V7X)


def _rope_inv_row():
    half = RET_DK // 2
    inv = ROPE_BASE ** (-jnp.arange(half, dtype=F32) / half)
    return jnp.tile(inv, LANES // half)[None, :]


def _ret_log_decay():
    return np.log1p(-np.exp2(-5.0 - np.arange(RET_HEADS, dtype=np.float32))).astype(np.float32)


def _ret_tables():
    c = RET_CHUNK
    lg = _ret_log_decay()
    i = np.arange(c, dtype=np.float32)
    diff = i[:, None] - i[None, :]
    dmask = np.where(diff[None] >= 0, np.exp(np.maximum(diff, 0.0)[None] * lg[:, None, None]), 0.0)
    q_decay = np.exp((i + 1.0)[:, None] * lg[None, :])
    k_decay = np.exp((c - 1.0 - i)[:, None] * lg[None, :])
    qdec = np.broadcast_to(q_decay.T[:, :, None], (RET_HEADS, c, RET_DV))
    kdec = np.repeat(k_decay, RET_DK, axis=1).reshape(c, RET_PAIRS, 2 * RET_DK).transpose(1, 0, 2)
    gc = np.repeat(np.exp(c * lg), RET_DV).reshape(RET_PAIRS, 1, 2 * RET_DV)
    return tuple(jnp.asarray(np.ascontiguousarray(t), F32) for t in (dmask, qdec, kdec, gc))


def _rel_buckets():
    max_exact = REL_BUCKETS // 2
    out = []
    for dil in ATT_DILATIONS:
        d = np.arange(ATT_NK, dtype=np.int32) * dil
        log_ratio = (np.log(np.maximum(d, 1).astype(np.float32) / np.float32(max_exact))
                     / np.float32(math.log(REL_MAX_DIST / max_exact)))
        large = max_exact + (log_ratio * np.float32(REL_BUCKETS - max_exact)).astype(np.int32)
        out.append(np.where(d < max_exact, d, np.minimum(large, REL_BUCKETS - 1)))
    return np.stack(out)


def _band_select():
    nk = ATT_NK
    buckets = _rel_buckets()
    onehot = np.zeros((ATT_GROUPS, REL_BUCKETS, 3 * nk), np.float32)
    for g in range(ATT_GROUPS):
        for k in range(nk, 2 * nk):
            onehot[g, buckets[g, 2 * nk - 1 - k], k] = 1.0
    mask = np.full((1, 3 * nk), -np.inf, np.float32)
    mask[0, nk:2 * nk] = 0.0
    return jnp.asarray(onehot, BF16), jnp.asarray(mask, F32)


def _slot_select():
    buckets = _rel_buckets()
    onehot = np.zeros((ATT_GROUPS, ATT_NK, REL_BUCKETS), np.float32)
    for g in range(ATT_GROUPS):
        for slot in range(ATT_NK):
            onehot[g, slot, buckets[g, 0 if slot == 0 else ATT_NK - slot]] = 1.0
    return jnp.asarray(onehot, BF16)


def _split3(x):
    hi = x.astype(BF16)
    rem = x - hi.astype(F32)
    mid = rem.astype(BF16)
    lo = (rem - mid.astype(F32)).astype(BF16)
    return hi, mid, lo


def _inproj_ret_body(x_ref, ln_ref, w_ref, inv_ref, q_ref, k_ref, v_ref, g_ref, gr_ref, ga_ref, cos_s, sin_s,
                     *, tm, tiles, fixed_pos):
    i = pl.program_id(0)
    lane = lax.broadcasted_iota(jnp.int32, (tm, LANES), 1)
    first_half = (lane % RET_DK) < (RET_DK // 2)

    def tables(pos):
        ang = pos.astype(F32) * inv_ref[...]
        sin = jnp.sin(ang)
        return jnp.cos(ang), jnp.where(first_half, -sin, sin)

    if fixed_pos is None:
        trow = pl.ds(pl.multiple_of(lax.rem(i, tiles) * tm, tm), tm)

        @pl.when(i < tiles)
        def _():
            cos_s[trow, :], sin_s[trow, :] = tables(i * tm + lax.broadcasted_iota(jnp.int32, (tm, LANES), 0))
        cos = cos_s[trow, :]
        sin = sin_s[trow, :]
    else:
        cos, sin = tables(jnp.full((tm, LANES), fixed_pos, jnp.int32))
    x = x_ref[...]
    h = _rms(x, ln_ref[...]).astype(BF16)
    qk = _dot(h, w_ref[:, 0:2 * RET_QK_W])
    n_q = RET_QK_W // LANES
    for c in range(2 * n_q):
        xc = qk[:, c * LANES:(c + 1) * LANES]
        swapped = jnp.where(first_half, pltpu.roll(xc, LANES - RET_DK // 2, 1), pltpu.roll(xc, RET_DK // 2, 1))
        r = xc * cos + swapped * sin
        if c < n_q:
            q_ref[:, c * LANES:(c + 1) * LANES] = r.astype(q_ref.dtype)
        else:
            k_ref[:, (c - n_q) * LANES:(c - n_q + 1) * LANES] = (r * (RET_DK ** -0.5)).astype(k_ref.dtype)
    o = 2 * RET_QK_W
    v_ref[...] = _dot(h, w_ref[:, o:o + RET_V_W]).astype(v_ref.dtype)
    o += RET_V_W
    g_ref[...] = _dot(h, w_ref[:, o:o + RET_V_W]).astype(g_ref.dtype)
    o += RET_V_W
    gr_ref[...] = _dot(h, w_ref[:, o:o + D_MODEL]).astype(gr_ref.dtype)
    o += D_MODEL
    ga_ref[...] = _dot(h, w_ref[:, o:o + D_MODEL]).astype(ga_ref.dtype)


def _inproj_ret(x2d, ln, w, inv_row, tm, out_dtype, seq=None, fixed_pos=None):
    n = x2d.shape[0]
    tiles = None if seq is None else seq // tm
    table_rows = tm if seq is None else seq
    wcols = w.shape[1]
    row = lambda width: pl.BlockSpec((tm, width), lambda i: (i, 0))
    widths = (RET_QK_W, RET_QK_W, RET_V_W, RET_V_W, D_MODEL, D_MODEL)
    return pl.pallas_call(
        functools.partial(_inproj_ret_body, tm=tm, tiles=tiles, fixed_pos=fixed_pos),
        grid=(n // tm,),
        in_specs=[row(D_MODEL), _resident((1, D_MODEL)), _resident((D_MODEL, wcols)), _resident((1, LANES))],
        out_specs=[row(wd) for wd in widths],
        out_shape=[jax.ShapeDtypeStruct((n, wd), out_dtype) for wd in widths],
        scratch_shapes=[pltpu.VMEM((table_rows, LANES), F32), pltpu.VMEM((table_rows, LANES), F32)],
        compiler_params=_params(1),
        name="inproj_ret",
    )(x2d, ln, w, inv_row)


def _inproj_att_body(x_ref, ln_ref, w_ref, *refs, tm, keeps, seq):
    lowp = refs[:3 * ATT_GROUPS]
    full = refs[3 * ATT_GROUPS:5 * ATT_GROUPS]
    res = refs[5 * ATT_GROUPS]
    x = x_ref[...]
    h = _rms(x, ln_ref[...]).astype(BF16)
    slot = 0
    for kind in range(3):
        for g in range(ATT_GROUPS):
            o = kind * ATT_W + g * ATT_GW
            dil = ATT_DILATIONS[g]
            dst = lowp[kind * ATT_GROUPS + g]
            r = _dot(h, w_ref[:, o:o + ATT_GW])
            if dil == 1:
                dst[0] = r.astype(dst.dtype)
            else:
                for hh in range(ATT_HPG):
                    res[slot, hh] = r[:, hh * ATT_HD:(hh + 1) * ATT_HD]
                for rr in range(dil):
                    for hh in range(ATT_HPG):
                        dst[rr, :, hh * ATT_HD:(hh + 1) * ATT_HD] = (
                            res[slot, hh, pl.ds(rr, tm // dil, stride=dil), :].astype(dst.dtype))
                slot += 1
            if kind > 0:
                cache = full[(kind - 1) * ATT_GROUPS + g]
                rows = tm if keeps[g] == seq else keeps[g]
                for hh in range(ATT_HPG):
                    cache[pl.ds(hh, rows, stride=ATT_HPG), :] = r[tm - rows:, hh * ATT_HD:(hh + 1) * ATT_HD]


def _inproj_att_prompt(x2d, ln, w, tm, seq):
    n = x2d.shape[0]
    batch = n // seq
    tiles = seq // tm
    keeps = tuple(min(wd, seq) for wd in ATT_WINDOWS)
    assert all(kp <= tm or kp == seq for kp in keeps) and seq % tm == 0
    out_specs, out_shape = [], []
    for _ in range(3):
        for g in range(ATT_GROUPS):
            dil = ATT_DILATIONS[g]
            out_specs.append(pl.BlockSpec((None, dil, tm // dil, ATT_GW), lambda i: (i // tiles, 0, i % tiles, 0)))
            out_shape.append(jax.ShapeDtypeStruct((batch, dil, seq // dil, ATT_GW), BF16))
    for _ in range(2):
        for g in range(ATT_GROUPS):
            if keeps[g] == seq:
                idx = lambda i: (i // tiles, i % tiles, 0)
                rows = tm
            else:
                idx = lambda i: (i // tiles, 0, 0)
                rows = keeps[g]
            out_specs.append(pl.BlockSpec((None, rows * ATT_HPG, ATT_HD), idx))
            out_shape.append(jax.ShapeDtypeStruct((batch, keeps[g] * ATT_HPG, ATT_HD), F32))
    return pl.pallas_call(
        functools.partial(_inproj_att_body, tm=tm, keeps=keeps, seq=seq),
        grid=(n // tm,),
        in_specs=[pl.BlockSpec((tm, D_MODEL), lambda i: (i, 0)), _resident((1, D_MODEL)),
                  _resident((D_MODEL, 3 * ATT_W))],
        out_specs=out_specs,
        out_shape=out_shape,
        scratch_shapes=[pltpu.VMEM((3 * sum(d > 1 for d in ATT_DILATIONS), ATT_HPG, tm, ATT_HD), F32)],
        compiler_params=_params(1),
        name="inproj_att",
    )(x2d, ln, w)


def _inproj_att_sample_body(x_ref, ln_ref, w_ref, o_ref):
    h = _rms(x_ref[...], ln_ref[...]).astype(BF16)
    for c in range(3 * ATT_GROUPS):
        o_ref[:, c * ATT_GW:(c + 1) * ATT_GW] = _dot(h, w_ref[:, c * ATT_GW:(c + 1) * ATT_GW])


def _inproj_att_sample(x2d, ln, w):
    n = x2d.shape[0]
    return pl.pallas_call(
        _inproj_att_sample_body,
        grid=(1,),
        in_specs=[pl.BlockSpec((n, D_MODEL), lambda i: (0, 0)), _resident((1, D_MODEL)),
                  _resident((D_MODEL, 3 * ATT_W))],
        out_specs=pl.BlockSpec((n, 3 * ATT_W), lambda i: (0, 0)),
        out_shape=jax.ShapeDtypeStruct((n, 3 * ATT_W), F32),
        compiler_params=_params(1),
        name="inproj_att_sample",
    )(x2d, ln, w)


def _gn_swish(o, gate, gn):
    mu = jnp.mean(o, axis=-1, keepdims=True)
    d = o - mu
    var = jnp.mean(d * d, axis=-1, keepdims=True)
    on = d * lax.rsqrt(var + NORM_EPS) * gn
    return gate * _sigmoid(gate) * on


def _retention_body(q_ref, k_ref, v_ref, g_ref, gn_ref, dm_ref, qdec_ref, kdec_ref, gc_ref,
                    out_ref, st_ref, state, *, n_chunks):
    c = RET_CHUNK

    @pl.when(pl.program_id(1) == 0)
    def _():
        state[...] = jnp.zeros_like(state)

    lane = lax.broadcasted_iota(jnp.int32, (c, 2 * RET_DK), 1)
    head0 = lane < RET_DK

    def chunk(ci, carry):
        rows = pl.ds(pl.multiple_of(ci * c, c), c)
        for p in range(RET_PAIRS):
            q2 = q_ref[rows, p * 2 * RET_DK:(p + 1) * 2 * RET_DK]
            k2 = k_ref[rows, p * 2 * RET_DK:(p + 1) * 2 * RET_DK]
            v2 = v_ref[rows, p * 2 * RET_DV:(p + 1) * 2 * RET_DV]
            pst = state[p]
            pst_lo = pst.astype(BF16)
            zero = jnp.zeros_like(q2)
            for hh in range(2):
                h = 2 * p + hh
                qm = jnp.where(head0 if hh == 0 else jnp.logical_not(head0), q2, zero)
                vh = v2[:, hh * RET_DV:(hh + 1) * RET_DV]
                s = _dot_nt(qm, k2) * dm_ref[h]
                o = _dot(s.astype(BF16), vh)
                o = o + _dot(qm, pst_lo[:, hh * RET_DV:(hh + 1) * RET_DV]) * qdec_ref[h]
                gate = g_ref[rows, h * RET_DV:(h + 1) * RET_DV].astype(F32)
                res = _gn_swish(o, gate, gn_ref[:, h * RET_DV:(h + 1) * RET_DV])
                out_ref[rows, h * RET_DV:(h + 1) * RET_DV] = res.astype(out_ref.dtype)
            kd = (k2.astype(F32) * kdec_ref[p]).astype(BF16)
            state[p] = pst * gc_ref[p] + _dot_tn(kd, v2)
        return carry

    lax.fori_loop(0, n_chunks, chunk, 0, unroll=True)
    for p in range(RET_PAIRS):
        pst = state[p]
        for hh in range(2):
            st_ref[2 * p + hh] = pst[hh * RET_DK:(hh + 1) * RET_DK, hh * RET_DV:(hh + 1) * RET_DV]


def _retention_prompt(q, k, v, g, gn, batch, seq, ts):
    n = q.shape[0]
    steps = seq // ts
    dmask, qdec, kdec, gc = _ret_tables()
    row = lambda width: pl.BlockSpec((ts, width), lambda b, s: (b * steps + s, 0))
    const = lambda a: pl.BlockSpec(a.shape, lambda b, s: (0,) * a.ndim)
    return pl.pallas_call(
        functools.partial(_retention_body, n_chunks=ts // RET_CHUNK),
        grid=(batch, steps),
        in_specs=[row(RET_QK_W), row(RET_QK_W), row(RET_V_W), row(RET_V_W), const(gn),
                  const(dmask), const(qdec), const(kdec), const(gc)],
        out_specs=[row(RET_V_W),
                   pl.BlockSpec((None, RET_HEADS, RET_DK, RET_DV), lambda b, s: (b, 0, 0, 0))],
        out_shape=[jax.ShapeDtypeStruct((n, RET_V_W), BF16),
                   jax.ShapeDtypeStruct((batch, RET_HEADS, RET_DK, RET_DV), F32)],
        scratch_shapes=[pltpu.VMEM((RET_PAIRS, 2 * RET_DK, 2 * RET_DV), F32)],
        compiler_params=_params(2),
        name="retention",
    )(q, k, v, g, gn, dmask, qdec, kdec, gc)


def _attn_body(q_ref, k_ref, v_ref, rb_ref, sel_ref, neg_ref, o_ref, lse_ref, *scratch, group, dil, n_blocks):
    nk = ATT_NK
    scale = ATT_HD ** -0.5
    lane = lax.broadcasted_iota(jnp.int32, (nk, LANES), 1)
    tabs = scratch[-1]

    @pl.when(pl.program_id(0) == 0)
    def _():
        band = neg_ref[...]
        for piece in _split3(rb_ref[group * ATT_HPG:(group + 1) * ATT_HPG, :]):
            band = band + _dot(piece, sel_ref[...])
        col = lax.broadcasted_iota(jnp.int32, (1, 2 * nk), 1)
        for i in range(nk):
            window = band[:, nk - 1 - i:3 * nk - 1 - i]
            for h in range(ATT_HPG):
                tabs[1, h, i:i + 1, :] = window[h:h + 1, :]
                tabs[0, h, i:i + 1, :] = jnp.where(col < nk, -jnp.inf, window[h:h + 1, :])
    if dil > 1:
        o_s, l_s = scratch[:2]

    def chain(s_raw, v_ext, sel, h):
        s = s_raw * scale + (tabs[sel, h] if n_blocks > 1 else tabs[1, h, :, nk:])
        m = jnp.max(s, axis=-1, keepdims=True)
        p = jnp.exp(s - m)
        pv = _dot(p.astype(BF16), v_ext)
        den = pv[:, ATT_HD:]
        return (pv[:, :ATT_HD] / den).astype(BF16), m + jnp.log(den)

    def write(rs, rows, outs):
        lse_tile = jnp.zeros((nk, LANES), F32)
        for h, (o, lse) in enumerate(outs):
            hc = slice(h * ATT_HD, (h + 1) * ATT_HD)
            if dil == 1:
                o_ref[rows, hc] = o
            else:
                o_s[rs, rows, hc] = o
            lse_tile = jnp.where(lane == h, lse, lse_tile)
        if dil == 1:
            lse_ref[rows, :] = lse_tile
        else:
            l_s[rs, rows, :] = lse_tile

    heads = [slice(h * ATT_HD, (h + 1) * ATT_HD) for h in range(ATT_HPG)]

    def block(rs, n, n_prev, sel):
        rows = pl.ds(pl.multiple_of(n * nk, nk), nk)
        prev_rows = pl.ds(pl.multiple_of(n_prev * nk, nk), nk)
        q = q_ref[rs, rows, :]
        k_all = jnp.concatenate([k_ref[rs, prev_rows, :], k_ref[rs, rows, :]], axis=0)
        v_all = jnp.concatenate([v_ref[rs, prev_rows, :], v_ref[rs, rows, :]], axis=0)
        ones = jnp.ones((2 * nk, ATT_HD), BF16)
        outs = []
        for h, hc in enumerate(heads):
            v_ext = jnp.concatenate([v_all[:, hc], ones], axis=1)
            outs.append(chain(_dot_nt(q[:, hc], k_all[:, hc]), v_ext, sel, h))
        write(rs, rows, outs)

    def single_blocks(streams):
        ones = jnp.ones((nk, ATT_HD), BF16)
        scores = [[_dot_nt(q_ref[rs, :, hc], k_ref[rs, :, hc]) for hc in heads] for rs in streams]
        for i, rs in enumerate(streams):
            outs = [chain(scores[i][h], jnp.concatenate([v_ref[rs, :, hc], ones], axis=1), 0, h)
                    for h, hc in enumerate(heads)]
            write(rs, slice(None), outs)

    if n_blocks == 1:
        for r0 in range(0, dil, SINGLE_BLOCK_GROUP):
            single_blocks(range(r0, min(r0 + SINGLE_BLOCK_GROUP, dil)))
    else:
        for rs in range(dil):
            def loop(n, carry, rs=rs):
                block(rs, n, jnp.maximum(n - 1, 0), jnp.minimum(n, 1))
                return carry
            lax.fori_loop(0, n_blocks, loop, 0, unroll=ATTN_UNROLL)

    if dil > 1:
        piece = PERM_ROWS // dil
        row = lax.broadcasted_iota(jnp.int32, (PERM_ROWS, PERM_ROWS), 0)
        col = lax.broadcasted_iota(jnp.int32, (PERM_ROWS, PERM_ROWS), 1)
        perm = jnp.where((col % piece) * dil + col // piece == row, 1.0, 0.0).astype(BF16)
        for j in range(o_ref.shape[0] // PERM_ROWS):
            src = slice(j * piece, (j + 1) * piece)
            dst = slice(j * PERM_ROWS, (j + 1) * PERM_ROWS)
            o_ref[dst, :] = _dot(perm, jnp.concatenate([o_s[r, src, :] for r in range(dil)], axis=0)).astype(BF16)
            lse = jnp.concatenate([l_s[r, src, :] for r in range(dil)], axis=0)
            hi = lse.astype(BF16)
            rem = lse - hi.astype(F32)
            mid = rem.astype(BF16)
            lo = (rem - mid.astype(F32)).astype(BF16)
            lse_ref[dst, :] = _dot(perm, hi) + _dot(perm, mid) + _dot(perm, lo)


def _attention_prompt(aq, ak, av, rel_bias_t, onehot, neg, g):
    batch, dil, length, _ = aq.shape
    seq = dil * length
    n_blocks = length // ATT_NK
    blk = pl.BlockSpec((None, dil, length, ATT_GW), lambda b: (b, 0, 0, 0))
    scratch = [pltpu.VMEM((dil, length, ATT_GW), BF16), pltpu.VMEM((dil, length, LANES), F32)] if dil > 1 else []
    scratch.append(pltpu.VMEM((2, ATT_HPG, ATT_NK, 2 * ATT_NK), F32))
    o, lse = pl.pallas_call(
        functools.partial(_attn_body, group=g, dil=dil, n_blocks=n_blocks),
        grid=(batch,),
        in_specs=[blk, blk, blk, pl.BlockSpec(rel_bias_t.shape, lambda b: (0, 0)),
                  pl.BlockSpec((None, REL_BUCKETS, 3 * ATT_NK), lambda b: (g, 0, 0)),
                  pl.BlockSpec((1, 3 * ATT_NK), lambda b: (0, 0))],
        out_specs=[pl.BlockSpec((None, seq, ATT_GW), lambda b: (b, 0, 0)),
                   pl.BlockSpec((None, seq, LANES), lambda b: (b, 0, 0))],
        out_shape=[jax.ShapeDtypeStruct((batch, seq, ATT_GW), BF16),
                   jax.ShapeDtypeStruct((batch, seq, LANES), F32)],
        scratch_shapes=scratch,
        compiler_params=_params(1),
        name="attention_g%d" % g,
    )(aq, ak, av, rel_bias_t, onehot, neg)
    return o.reshape(batch * seq, ATT_GW), lse.reshape(batch * seq, LANES)


def _tail_math(act_refs, w_refs, y_ref, combine):
    wrb_ref, wab_ref, wo_ref, wu_ref, wd_ref, wpl_ref, wpg_ref, ln2_ref, lnf_ref = w_refs
    if combine:
        ret_ref, o0_ref, o1_ref, o2_ref, l0_ref, l1_ref, l2_ref, gr_ref, ga_ref, x_ref, ple_ref = act_refs
        lses = [l0_ref[...], l1_ref[...], l2_ref[...]]
        outs = [o0_ref, o1_ref, o2_ref]
        parts = []
        for h in range(ATT_HPG):
            lh = [l[:, h:h + 1] for l in lses]
            mx = jnp.maximum(jnp.maximum(lh[0], lh[1]), lh[2])
            e = [jnp.exp(l - mx) for l in lh]
            tot = e[0] + e[1] + e[2]
            acc = None
            for g in range(ATT_GROUPS):
                term = (e[g] / tot) * outs[g][:, h * ATT_HD:(h + 1) * ATT_HD].astype(F32)
                acc = term if acc is None else acc + term
            parts.append(acc)
        att = jnp.concatenate(parts, axis=1).astype(BF16)
    else:
        ret_ref, att_ref, gr_ref, ga_ref, x_ref, ple_ref = act_refs
        att = att_ref[...].astype(BF16)
    a = _dot(ret_ref[...].astype(BF16), wrb_ref[...])
    b = _dot(att, wab_ref[...])
    mixed = _sigmoid(gr_ref[...].astype(F32)) * a + _sigmoid(ga_ref[...].astype(F32)) * b
    x1 = x_ref[...] + _dot(mixed.astype(BF16), wo_ref[...])
    h2 = _rms(x1, ln2_ref[...]).astype(BF16)
    ff_chunk = D_MODEL
    acc = None
    for c in range(D_FF // ff_chunk):
        u = _dot(h2, wu_ref[:, c * ff_chunk:(c + 1) * ff_chunk])
        r = jnp.maximum(u, 0.0)
        t = _dot((r * r).astype(BF16), wd_ref[c * ff_chunk:(c + 1) * ff_chunk, :])
        acc = t if acc is None else acc + t
    x2 = x1 + acc
    gate = _sigmoid(_dot(x2.astype(BF16), wpg_ref[...]))
    x3 = x2 + gate * _dot(ple_ref[...].astype(BF16), wpl_ref[...])
    y_ref[...] = _rms(x3, lnf_ref[...])


def _tail_body(*refs, n_act, combine):
    _tail_math(refs[:n_act], refs[n_act:-1], refs[-1], combine)


def _tail(acts, weights, tm, combine):
    n = acts[0].shape[0]
    row = lambda a: pl.BlockSpec((tm, a.shape[1]), lambda i: (i, 0))
    return pl.pallas_call(
        functools.partial(_tail_body, n_act=len(acts), combine=combine),
        grid=(n // tm,),
        in_specs=[row(a) for a in acts] + [_resident(w.shape) for w in weights],
        out_specs=pl.BlockSpec((tm, D_MODEL), lambda i: (i, 0)),
        out_shape=jax.ShapeDtypeStruct((n, D_MODEL), F32),
        compiler_params=_params(1),
        name="tail" if combine else "tail_sample",
    )(*acts, *weights)


def _tail_shift_body(*refs, n_act, n_w, n_steps):
    nc = 2 * ATT_GROUPS
    acts = refs[:n_act]
    ws = refs[n_act:n_act + n_w]
    new_ref = refs[n_act + n_w]
    base = n_act + n_w + 1
    old = refs[base:base + nc]
    y_ref = refs[base + nc]
    out = refs[base + nc + 1:base + 2 * nc + 1]
    stage = refs[base + 2 * nc + 1:base + 3 * nc + 1]
    sem_in, sem_out, sem_row = refs[base + 3 * nc + 1:]
    s = pl.program_id(0)
    b = lax.shift_right_logical(s, 1)
    parity = lax.rem(s, 2)

    def copy_in(i, bi):
        width = old[i].shape[1]
        return pltpu.make_async_copy(old[i].at[bi, pl.ds(1, width - 1)], stage[i], sem_in.at[i])

    def copy_out(i, bi):
        width = old[i].shape[1]
        return pltpu.make_async_copy(stage[i], out[i].at[bi, pl.ds(0, width - 1)], sem_out.at[i])

    def copy_row(g, kind):
        i = 2 * g + kind
        return pltpu.make_async_copy(new_ref.at[b, kind + 1, g], out[i].at[b, old[i].shape[1] - 1], sem_row.at[g])

    for kind in range(2):
        mine = [2 * g + kind for g in range(ATT_GROUPS)]
        other = [2 * g + 1 - kind for g in range(ATT_GROUPS)]

        @pl.when(parity == kind)
        def _(kind=kind, mine=mine, other=other):
            if kind == 0:
                @pl.when(s == 0)
                def _():
                    for i in mine:
                        copy_in(i, 0).start()
            for i in mine:
                copy_in(i, b).wait()
            for i in mine:
                copy_out(i, b).start()
            if kind == 0:
                @pl.when(s > 0)
                def _():
                    for i in other:
                        copy_out(i, b - 1).wait()
                for i in other:
                    copy_in(i, b).start()
            else:
                for i in other:
                    copy_out(i, b).wait()

                @pl.when(s < n_steps - 1)
                def _():
                    for i in other:
                        copy_in(i, b + 1).start()
            for g in range(ATT_GROUPS):
                copy_row(g, kind).start()

    _tail_math(acts, ws, y_ref, True)

    for kind in range(2):
        @pl.when(parity == kind)
        def _(kind=kind):
            for g in range(ATT_GROUPS):
                copy_row(g, kind).wait()

    @pl.when(s == n_steps - 1)
    def _():
        for g in range(ATT_GROUPS):
            copy_out(2 * g + 1, b).wait()


def _tail_shift(acts, weights, tm, new_qkv, caches):
    n = acts[0].shape[0]
    n_steps = n // tm
    assert n_steps == 2 * new_qkv.shape[0]
    row = lambda a: pl.BlockSpec((tm, a.shape[1]), lambda i: (i, 0))
    anyspec = pl.BlockSpec(memory_space=pl.ANY)
    nc = len(caches)
    new_rows = new_qkv.reshape(new_qkv.shape[0], 3, ATT_GROUPS, ATT_HPG, ATT_HD)
    res = pl.pallas_call(
        functools.partial(_tail_shift_body, n_act=len(acts), n_w=len(weights), n_steps=n_steps),
        grid=(n_steps,),
        in_specs=[row(a) for a in acts] + [_resident(w.shape) for w in weights] + [anyspec] * (nc + 1),
        out_specs=[pl.BlockSpec((tm, D_MODEL), lambda i: (i, 0))] + [anyspec] * nc,
        out_shape=[jax.ShapeDtypeStruct((n, D_MODEL), F32)]
        + [jax.ShapeDtypeStruct(c.shape, c.dtype) for c in caches],
        scratch_shapes=[pltpu.VMEM((c.shape[1] - 1, ATT_HPG, ATT_HD), F32) for c in caches]
        + [pltpu.SemaphoreType.DMA((nc,)), pltpu.SemaphoreType.DMA((nc,)),
           pltpu.SemaphoreType.DMA((ATT_GROUPS,))],
        compiler_params=_params(1),
        name="tail",
    )(*acts, *weights, new_rows, *caches)
    return res[0], res[1:]


def _decode_body(q_ref, k_ref, v_ref, g_ref, gn_ref, st_ref, gam_ref,
                 qkv_ref, rb_ref, slot_ref,
                 ck0_ref, cv0_ref, ck1_ref, cv1_ref, ck2_ref, cv2_ref,
                 ret_ref, att_ref, nst_ref):
    sub = lax.broadcasted_iota(jnp.int32, (8, 2 * RET_DK), 0)
    lane = lax.broadcasted_iota(jnp.int32, (8, 2 * RET_DK), 1)
    row0 = sub == 0
    srow = lax.broadcasted_iota(jnp.int32, (2 * RET_DK, RET_DV), 0)
    for p in range(RET_PAIRS):
        pc = slice(p * 2 * RET_DK, (p + 1) * 2 * RET_DK)
        q2 = jnp.where(row0, jnp.broadcast_to(q_ref[:, pc], (8, 2 * RET_DK)), 0.0)
        k2 = jnp.where(row0, jnp.broadcast_to(k_ref[:, pc], (8, 2 * RET_DK)), 0.0)
        pst = st_ref[p]
        gam = gam_ref[p]
        outer = []
        for hh in range(2):
            h = 2 * p + hh
            hsel = (lane < RET_DK) if hh == 0 else (lane >= RET_DK)
            qm = jnp.where(hsel, q2, 0.0)
            km = jnp.where(hsel, k2, 0.0)
            vh = v_ref[:, h * RET_DV:(h + 1) * RET_DV]
            v8 = jnp.where(row0[:, :RET_DV], jnp.broadcast_to(vh, (8, RET_DV)), 0.0)
            cross = _dot(qm.astype(BF16), (pst * gam).astype(BF16))[0:1, :]
            qk = jnp.sum(qm[0:1, :] * km[0:1, :], axis=-1, keepdims=True)
            o = cross + qk * vh
            res = _gn_swish(o, g_ref[:, h * RET_DV:(h + 1) * RET_DV], gn_ref[:, h * RET_DV:(h + 1) * RET_DV])
            ret_ref[:, h * RET_DV:(h + 1) * RET_DV] = res
            outer.append(_dot_tn(k2.astype(BF16), v8.astype(BF16)))
        nst_ref[p] = pst * gam + jnp.where(srow < RET_DK, outer[0], outer[1])
    scale = ATT_HD ** -0.5
    caches = ((ck0_ref, cv0_ref), (ck1_ref, cv1_ref), (ck2_ref, cv2_ref))
    slot = lax.broadcasted_iota(jnp.int32, (ATT_NK, ATT_HD), 0)
    is_new = slot == 0
    pieces = _split3(rb_ref[...])
    slot_bias = []
    for g in range(ATT_GROUPS):
        slot_bias.append(_dot(slot_ref[g], pieces[0]) + _dot(slot_ref[g], pieces[1]) + _dot(slot_ref[g], pieces[2]))
    for h in range(ATT_HPG):
        o_g, lse_g = [], []
        for g in range(ATT_GROUPS):
            ck_ref, cv_ref = caches[g]
            c0 = g * ATT_GW + h * ATT_HD
            qh = qkv_ref[:, c0:c0 + ATT_HD]
            kk = jnp.where(is_new, qkv_ref[:, ATT_W + c0:ATT_W + c0 + ATT_HD], ck_ref[:, h, :])
            vv = jnp.where(is_new, qkv_ref[:, 2 * ATT_W + c0:2 * ATT_W + c0 + ATT_HD], cv_ref[:, h, :])
            col = g * ATT_HPG + h
            s = jnp.sum(kk * qh, axis=-1, keepdims=True) * scale + slot_bias[g][:, col:col + 1]
            m = jnp.max(s, axis=0, keepdims=True)
            pr = jnp.exp(s - m)
            den = jnp.sum(pr, axis=0, keepdims=True)
            o_g.append(jnp.sum(pr * vv, axis=0, keepdims=True) / den)
            lse_g.append(m + jnp.log(den))
        mx = jnp.maximum(jnp.maximum(lse_g[0], lse_g[1]), lse_g[2])
        e = [jnp.exp(l - mx) for l in lse_g]
        tot = e[0] + e[1] + e[2]
        att_ref[:, h * ATT_HD:(h + 1) * ATT_HD] = ((e[0] / tot) * o_g[0] + (e[1] / tot) * o_g[1]
                                                   + (e[2] / tot) * o_g[2])


def _decode(q, k, v, g, gn, state, qkv, rel_bias, caches):
    batch = q.shape[0]
    gam = jnp.asarray(np.ascontiguousarray(np.broadcast_to(
        np.repeat(np.exp(_ret_log_decay()), RET_DK).reshape(RET_PAIRS, 2 * RET_DK, 1),
        (RET_PAIRS, 2 * RET_DK, RET_DV))), F32)
    vec = lambda a: a.reshape(batch, 1, a.shape[1])
    vspec = lambda width: pl.BlockSpec((None, 1, width), lambda b: (b, 0, 0))
    const = lambda a: pl.BlockSpec(a.shape, lambda b: (0,) * a.ndim)
    st_spec = pl.BlockSpec((None, RET_PAIRS, 2 * RET_DK, RET_DV), lambda b: (b, 0, 0, 0))
    cache_in, cache_specs = [], []
    for gi in range(ATT_GROUPS):
        dil = ATT_DILATIONS[gi]
        for c in caches[2 * gi:2 * gi + 2]:
            cache_in.append(c.reshape(batch, ATT_NK, dil, ATT_HPG, ATT_HD))
            cache_specs.append(pl.BlockSpec((None, ATT_NK, None, ATT_HPG, ATT_HD), lambda b: (b, 0, 0, 0, 0)))
    st_pairs = state.reshape(batch, RET_PAIRS, 2 * RET_DK, RET_DV)
    slot_sel = _slot_select()
    ret, att, nst = pl.pallas_call(
        _decode_body,
        grid=(batch,),
        in_specs=[vspec(RET_QK_W), vspec(RET_QK_W), vspec(RET_V_W), vspec(RET_V_W), const(gn), st_spec, const(gam),
                  vspec(3 * ATT_W), const(rel_bias), const(slot_sel)] + cache_specs,
        out_specs=[vspec(RET_V_W), vspec(ATT_GW), st_spec],
        out_shape=[jax.ShapeDtypeStruct((batch, 1, RET_V_W), F32),
                   jax.ShapeDtypeStruct((batch, 1, ATT_GW), F32),
                   jax.ShapeDtypeStruct(st_pairs.shape, F32)],
        compiler_params=_params(1),
        name="decode",
    )(vec(q), vec(k), vec(v), vec(g), gn, st_pairs, gam, vec(qkv), rel_bias, slot_sel, *cache_in)
    return ret.reshape(batch, RET_V_W), att.reshape(batch, ATT_GW), nst.reshape(state.shape)


def kernel(x_prompt, x_sample, state_ret, cache_k_w128, cache_v_w128, cache_k_w512, cache_v_w512,
           cache_k_w2048, cache_v_w2048, p_prompt, p_sample, ln1_g, w_in, ret_gn_g, w_ret_br, w_att_br,
           w_out, ln2_g, w_up, w_down, w_ple, w_ple_gate, rel_bias, lnf_g):
    depth = w_in.shape[0]
    assert depth == 1
    batch, seq, _ = x_prompt.shape
    dec_batch, dec_seq, _ = x_sample.shape
    assert dec_seq == 1
    past_len = 16384
    l = 0
    ln1 = ln1_g[l][None, :]
    ln2 = ln2_g[l][None, :]
    lnf = lnf_g[None, :]
    gn = ret_gn_g[l][None, :]
    w_in_l = w_in[l]
    w_ret = jnp.concatenate([w_in_l[:, :COL_ATT], w_in_l[:, COL_GATE:]], axis=1).astype(BF16)
    w_att = w_in_l[:, COL_ATT:COL_GATE].astype(BF16)
    tail_w = (w_ret_br[l].astype(BF16), w_att_br[l].astype(BF16), w_out[l].astype(BF16), w_up[l].astype(BF16),
              w_down[l].astype(BF16), w_ple[l].astype(BF16), w_ple_gate[l].astype(BF16), ln2, lnf)
    inv_row = _rope_inv_row()

    caches = (cache_k_w128[l], cache_v_w128[l], cache_k_w512[l], cache_v_w512[l],
              cache_k_w2048[l], cache_v_w2048[l])

    xs = x_sample.reshape(dec_batch, D_MODEL)
    sq, sk, sv, sg, sgr, sga = _inproj_ret(xs, ln1, w_ret, inv_row, dec_batch, F32, fixed_pos=past_len)
    s_qkv = _inproj_att_sample(xs, ln1, w_att)
    s_ret, s_attn, new_st = _decode(sq, sk, sv, sg, gn, state_ret[l], s_qkv, rel_bias, caches)
    ple_s = p_sample[l].reshape(dec_batch, D_PLE)
    y_s = _tail([s_ret, s_attn, sgr, sga, xs, ple_s], tail_w, dec_batch, False)
    y_sample = y_s.reshape(dec_batch, 1, D_MODEL)

    xp = x_prompt.reshape(batch * seq, D_MODEL)
    rq, rk, rv, rg, gr, ga = _inproj_ret(xp, ln1, w_ret, inv_row, TM_INPROJ, BF16, seq=seq)
    att_o = _inproj_att_prompt(xp, ln1, w_att, TM_INPROJ, seq)
    aqs, aks, avs = att_o[0:3], att_o[3:6], att_o[6:9]
    kfull, vfull = att_o[9:12], att_o[12:15]
    ret_out, st_p = _retention_prompt(rq, rk, rv, rg, gn, batch, seq, TS_RETENTION)
    outs, lses = [], []
    band_onehot, band_neg = _band_select()
    rel_bias_t = rel_bias.T
    for g in range(ATT_GROUPS):
        o, lse = _attention_prompt(aqs[g], aks[g], avs[g], rel_bias_t, band_onehot, band_neg, g)
        outs.append(o)
        lses.append(lse)
    ple_p = p_prompt[l].reshape(batch * seq, D_PLE)
    tm_tail = batch * seq // (2 * dec_batch)
    y_p, new_caches = _tail_shift([ret_out] + outs + lses + [gr, ga, xp, ple_p], tail_w, tm_tail, s_qkv, caches)
    y_prompt = y_p.reshape(batch, seq, D_MODEL)
    new_state_p = st_p[None]
    kv_p = []
    for g in range(ATT_GROUPS):
        shape = (1, batch, min(ATT_WINDOWS[g], seq), ATT_HPG, ATT_HD)
        kv_p.append(kfull[g].reshape(shape))
        kv_p.append(vfull[g].reshape(shape))
    kv_s = [c[None] for c in new_caches]

    return (y_prompt, y_sample, new_state_p, *kv_p, new_st[None], *kv_s)
```

```python
import functools
import math

import jax
import jax.numpy as jnp
import numpy as np
from jax import lax
from jax.experimental import pallas as pl
from jax.experimental.pallas import tpu as pltpu

F32 = jnp.float32
BF16 = jnp.bfloat16

D_MODEL = 1024
RET_HEADS = 8
RET_DK = 64
RET_DV = 128
RET_PAIRS = RET_HEADS // 2
RET_CHUNK = 128
ROPE_BASE = 10000.0
ATT_WINDOWS = (128, 512, 2048)
ATT_DILATIONS = (1, 4, 16)
ATT_GROUPS = 3
ATT_HPG = 4
ATT_HD = 128
ATT_NK = 128
ATT_GW = ATT_HPG * ATT_HD
REL_BUCKETS = 32
REL_MAX_DIST = 2048
D_FF = 4 * D_MODEL
D_PLE = 256
NORM_EPS = 1e-6
RET_QK_W = RET_HEADS * RET_DK
RET_V_W = RET_HEADS * RET_DV
ATT_W = ATT_GROUPS * ATT_GW
COL_RET = 0
COL_ATT = 2 * RET_QK_W + 2 * RET_V_W
COL_GATE = COL_ATT + 3 * ATT_W
N_IN = COL_GATE + 2 * D_MODEL

VMEM_LIMIT_V7X = 56 * 1024 * 1024
LANES = 128
TM_INPROJ = 512
TS_RETENTION = 512
ATTN_UNROLL = 4
PERM_ROWS = 256
SINGLE_BLOCK_GROUP = 4


def _dot(a, b):
    return jnp.dot(a, b, preferred_element_type=F32)


def _dot_nt(a, b):
    return lax.dot_general(a, b, (((1,), (1,)), ((), ())), preferred_element_type=F32)


def _dot_tn(a, b):
    return lax.dot_general(a, b, (((0,), (0,)), ((), ())), preferred_element_type=F32)


def _rms(x, g):
    return x * lax.rsqrt(jnp.mean(x * x, axis=-1, keepdims=True) + NORM_EPS) * g


def _sigmoid(x):
    return 1.0 / (1.0 + jnp.exp(-x))


def _resident(shape):
    return pl.BlockSpec(shape, lambda *_: (0,) * len(shape), pipeline_mode=pl.Buffered(1))


def _params(n_axes):
    return pltpu.CompilerParams(dimension_semantics=("arbitrary",) * n_axes,
                                vmem_limit_bytes=VMEM_LIMIT_V7X)


def _rope_inv_row():
    half = RET_DK // 2
    inv = ROPE_BASE ** (-jnp.arange(half, dtype=F32) / half)
    return jnp.tile(inv, LANES // half)[None, :]


def _ret_log_decay():
    return np.log1p(-np.exp2(-5.0 - np.arange(RET_HEADS, dtype=np.float32))).astype(np.float32)


def _ret_tables():
    c = RET_CHUNK
    lg = _ret_log_decay()
    i = np.arange(c, dtype=np.float32)
    diff = i[:, None] - i[None, :]
    dmask = np.where(diff[None] >= 0, np.exp(np.maximum(diff, 0.0)[None] * lg[:, None, None]), 0.0)
    q_decay = np.exp((i + 1.0)[:, None] * lg[None, :])
    k_decay = np.exp((c - 1.0 - i)[:, None] * lg[None, :])
    qdec = np.broadcast_to(q_decay.T[:, :, None], (RET_HEADS, c, RET_DV))
    kdec = np.repeat(k_decay, RET_DK, axis=1).reshape(c, RET_PAIRS, 2 * RET_DK).transpose(1, 0, 2)
    gc = np.repeat(np.exp(c * lg), RET_DV).reshape(RET_PAIRS, 1, 2 * RET_DV)
    return tuple(jnp.asarray(np.ascontiguousarray(t), F32) for t in (dmask, qdec, kdec, gc))


def _rel_buckets():
    max_exact = REL_BUCKETS // 2
    out = []
    for dil in ATT_DILATIONS:
        d = np.arange(ATT_NK, dtype=np.int32) * dil
        log_ratio = (np.log(np.maximum(d, 1).astype(np.float32) / np.float32(max_exact))
                     / np.float32(math.log(REL_MAX_DIST / max_exact)))
        large = max_exact + (log_ratio * np.float32(REL_BUCKETS - max_exact)).astype(np.int32)
        out.append(np.where(d < max_exact, d, np.minimum(large, REL_BUCKETS - 1)))
    return np.stack(out)


def _band_select():
    nk = ATT_NK
    buckets = _rel_buckets()
    onehot = np.zeros((ATT_GROUPS, REL_BUCKETS, 3 * nk), np.float32)
    for g in range(ATT_GROUPS):
        for k in range(nk, 2 * nk):
            onehot[g, buckets[g, 2 * nk - 1 - k], k] = 1.0
    mask = np.full((1, 3 * nk), -np.inf, np.float32)
    mask[0, nk:2 * nk] = 0.0
    return jnp.asarray(onehot, BF16), jnp.asarray(mask, F32)


def _slot_select():
    buckets = _rel_buckets()
    onehot = np.zeros((ATT_GROUPS, ATT_NK, REL_BUCKETS), np.float32)
    for g in range(ATT_GROUPS):
        for slot in range(ATT_NK):
            onehot[g, slot, buckets[g, 0 if slot == 0 else ATT_NK - slot]] = 1.0
    return jnp.asarray(onehot, BF16)


def _split3(x):
    hi = x.astype(BF16)
    rem = x - hi.astype(F32)
    mid = rem.astype(BF16)
    lo = (rem - mid.astype(F32)).astype(BF16)
    return hi, mid, lo


def _inproj_ret_body(x_ref, ln_ref, w_ref, wg_ref, inv_ref, q_ref, k_ref, v_ref, g_ref, gr_ref, ga_ref, cos_s, sin_s,
                     *, tm, tiles, fixed_pos):
    i = pl.program_id(0)
    lane = lax.broadcasted_iota(jnp.int32, (tm, LANES), 1)
    first_half = (lane % RET_DK) < (RET_DK // 2)

    def tables(pos):
        ang = pos.astype(F32) * inv_ref[...]
        sin = jnp.sin(ang)
        return jnp.cos(ang), jnp.where(first_half, -sin, sin)

    if fixed_pos is None:
        trow = pl.ds(pl.multiple_of(lax.rem(i, tiles) * tm, tm), tm)

        @pl.when(i < tiles)
        def _():
            cos_s[trow, :], sin_s[trow, :] = tables(i * tm + lax.broadcasted_iota(jnp.int32, (tm, LANES), 0))
        cos = cos_s[trow, :]
        sin = sin_s[trow, :]
    else:
        cos, sin = tables(jnp.full((tm, LANES), fixed_pos, jnp.int32))
    x = x_ref[...]
    h = _rms(x, ln_ref[...]).astype(BF16)
    qk = _dot(h, w_ref[:, 0:2 * RET_QK_W])
    n_q = RET_QK_W // LANES
    for c in range(2 * n_q):
        xc = qk[:, c * LANES:(c + 1) * LANES]
        swapped = jnp.where(first_half, pltpu.roll(xc, LANES - RET_DK // 2, 1), pltpu.roll(xc, RET_DK // 2, 1))
        r = xc * cos + swapped * sin
        if c < n_q:
            q_ref[:, c * LANES:(c + 1) * LANES] = r.astype(q_ref.dtype)
        else:
            k_ref[:, (c - n_q) * LANES:(c - n_q + 1) * LANES] = (r * (RET_DK ** -0.5)).astype(k_ref.dtype)
    o = 2 * RET_QK_W
    v_ref[...] = _dot(h, w_ref[:, o:o + RET_V_W]).astype(v_ref.dtype)
    o += RET_V_W
    g_ref[...] = _dot(h, w_ref[:, o:o + RET_V_W]).astype(g_ref.dtype)
    gr_ref[...] = _dot(h, wg_ref[:, 0:D_MODEL]).astype(gr_ref.dtype)
    ga_ref[...] = _dot(h, wg_ref[:, D_MODEL:2 * D_MODEL]).astype(ga_ref.dtype)


def _inproj_ret(x2d, ln, w, w_gate, inv_row, tm, out_dtype, seq=None, fixed_pos=None):
    n = x2d.shape[0]
    tiles = None if seq is None else seq // tm
    table_rows = tm if seq is None else seq
    row = lambda width: pl.BlockSpec((tm, width), lambda i: (i, 0))
    widths = (RET_QK_W, RET_QK_W, RET_V_W, RET_V_W, D_MODEL, D_MODEL)
    return pl.pallas_call(
        functools.partial(_inproj_ret_body, tm=tm, tiles=tiles, fixed_pos=fixed_pos),
        grid=(n // tm,),
        in_specs=[row(D_MODEL), _resident((1, D_MODEL)), _resident(w.shape), _resident(w_gate.shape),
                  _resident((1, LANES))],
        out_specs=[row(wd) for wd in widths],
        out_shape=[jax.ShapeDtypeStruct((n, wd), out_dtype) for wd in widths],
        scratch_shapes=[pltpu.VMEM((table_rows, LANES), F32), pltpu.VMEM((table_rows, LANES), F32)],
        compiler_params=_params(1),
        name="inproj_ret",
    )(x2d, ln, w, w_gate, inv_row)


def _inproj_att_body(x_ref, ln_ref, w_ref, *refs, tm, keeps, seq):
    lowp = refs[:3 * ATT_GROUPS]
    full = refs[3 * ATT_GROUPS:5 * ATT_GROUPS]
    res = refs[5 * ATT_GROUPS]
    x = x_ref[...]
    h = _rms(x, ln_ref[...]).astype(BF16)
    slot = 0
    for g in reversed(range(ATT_GROUPS)):
        for kind in reversed(range(3)):
            o = kind * ATT_W + g * ATT_GW
            dil = ATT_DILATIONS[g]
            dst = lowp[kind * ATT_GROUPS + g]
            r = _dot(h, w_ref[:, o:o + ATT_GW])
            if dil == 1:
                dst[0] = r.astype(dst.dtype)
            else:
                for hh in range(ATT_HPG):
                    res[slot, hh] = r[:, hh * ATT_HD:(hh + 1) * ATT_HD]
                for rr in range(dil):
                    for hh in range(ATT_HPG):
                        dst[rr, :, hh * ATT_HD:(hh + 1) * ATT_HD] = (
                            res[slot, hh, pl.ds(rr, tm // dil, stride=dil), :].astype(dst.dtype))
                slot += 1
            if kind > 0:
                cache = full[(kind - 1) * ATT_GROUPS + g]
                rows = tm if keeps[g] == seq else keeps[g]
                for hh in range(ATT_HPG):
                    cache[pl.ds(hh, rows, stride=ATT_HPG), :] = r[tm - rows:, hh * ATT_HD:(hh + 1) * ATT_HD]


def _inproj_att_prompt(x2d, ln, w, tm, seq):
    n = x2d.shape[0]
    batch = n // seq
    tiles = seq // tm
    keeps = tuple(min(wd, seq) for wd in ATT_WINDOWS)
    assert all(kp <= tm or kp == seq for kp in keeps) and seq % tm == 0
    out_specs, out_shape = [], []
    for _ in range(3):
        for g in range(ATT_GROUPS):
            dil = ATT_DILATIONS[g]
            out_specs.append(pl.BlockSpec((None, dil, tm // dil, ATT_GW), lambda i: (i // tiles, 0, i % tiles, 0)))
            out_shape.append(jax.ShapeDtypeStruct((batch, dil, seq // dil, ATT_GW), BF16))
    for _ in range(2):
        for g in range(ATT_GROUPS):
            if keeps[g] == seq:
                idx = lambda i: (i // tiles, i % tiles, 0)
                rows = tm
            else:
                idx = lambda i: (i // tiles, 0, 0)
                rows = keeps[g]
            out_specs.append(pl.BlockSpec((None, rows * ATT_HPG, ATT_HD), idx))
            out_shape.append(jax.ShapeDtypeStruct((batch, keeps[g] * ATT_HPG, ATT_HD), F32))
    return pl.pallas_call(
        functools.partial(_inproj_att_body, tm=tm, keeps=keeps, seq=seq),
        grid=(n // tm,),
        in_specs=[pl.BlockSpec((tm, D_MODEL), lambda i: (i, 0)), _resident((1, D_MODEL)),
                  _resident((D_MODEL, 3 * ATT_W))],
        out_specs=out_specs,
        out_shape=out_shape,
        scratch_shapes=[pltpu.VMEM((3 * sum(d > 1 for d in ATT_DILATIONS), ATT_HPG, tm, ATT_HD), F32)],
        compiler_params=_params(1),
        name="inproj_att",
    )(x2d, ln, w)


def _inproj_att_sample_body(x_ref, ln_ref, w_ref, o_ref):
    h = _rms(x_ref[...], ln_ref[...]).astype(BF16)
    for c in range(3 * ATT_GROUPS):
        o_ref[:, c * ATT_GW:(c + 1) * ATT_GW] = _dot(h, w_ref[:, c * ATT_GW:(c + 1) * ATT_GW])


def _inproj_att_sample(x2d, ln, w):
    n = x2d.shape[0]
    return pl.pallas_call(
        _inproj_att_sample_body,
        grid=(1,),
        in_specs=[pl.BlockSpec((n, D_MODEL), lambda i: (0, 0)), _resident((1, D_MODEL)),
                  _resident((D_MODEL, 3 * ATT_W))],
        out_specs=pl.BlockSpec((n, 3 * ATT_W), lambda i: (0, 0)),
        out_shape=jax.ShapeDtypeStruct((n, 3 * ATT_W), F32),
        compiler_params=_params(1),
        name="inproj_att_sample",
    )(x2d, ln, w)


def _gn_swish(o, gate, gn):
    mu = jnp.mean(o, axis=-1, keepdims=True)
    d = o - mu
    var = jnp.mean(d * d, axis=-1, keepdims=True)
    on = d * lax.rsqrt(var + NORM_EPS) * gn
    return gate * _sigmoid(gate) * on


def _retention_body(q_ref, k_ref, v_ref, g_ref, gn_ref, dm_ref, qdec_ref, kdec_ref, gc_ref,
                    out_ref, st_ref, state, *, n_chunks):
    c = RET_CHUNK

    @pl.when(pl.program_id(1) == 0)
    def _():
        state[...] = jnp.zeros_like(state)

    lane = lax.broadcasted_iota(jnp.int32, (c, 2 * RET_DK), 1)
    head0 = lane < RET_DK

    def chunk(ci, carry):
        rows = pl.ds(pl.multiple_of(ci * c, c), c)
        for p in range(RET_PAIRS):
            q2 = q_ref[rows, p * 2 * RET_DK:(p + 1) * 2 * RET_DK]
            k2 = k_ref[rows, p * 2 * RET_DK:(p + 1) * 2 * RET_DK]
            v2 = v_ref[rows, p * 2 * RET_DV:(p + 1) * 2 * RET_DV]
            pst = state[p]
            pst_lo = pst.astype(BF16)
            zero = jnp.zeros_like(q2)
            for hh in range(2):
                h = 2 * p + hh
                qm = jnp.where(head0 if hh == 0 else jnp.logical_not(head0), q2, zero)
                vh = v2[:, hh * RET_DV:(hh + 1) * RET_DV]
                s = _dot_nt(qm, k2) * dm_ref[h]
                o = _dot(s.astype(BF16), vh)
                o = o + _dot(qm, pst_lo[:, hh * RET_DV:(hh + 1) * RET_DV]) * qdec_ref[h]
                gate = g_ref[rows, h * RET_DV:(h + 1) * RET_DV].astype(F32)
                res = _gn_swish(o, gate, gn_ref[:, h * RET_DV:(h + 1) * RET_DV])
                out_ref[rows, h * RET_DV:(h + 1) * RET_DV] = res.astype(out_ref.dtype)
            kd = (k2.astype(F32) * kdec_ref[p]).astype(BF16)
            state[p] = pst * gc_ref[p] + _dot_tn(kd, v2)
        return carry

    lax.fori_loop(0, n_chunks, chunk, 0, unroll=True)
    for p in range(RET_PAIRS):
        pst = state[p]
        for hh in range(2):
            st_ref[2 * p + hh] = pst[hh * RET_DK:(hh + 1) * RET_DK, hh * RET_DV:(hh + 1) * RET_DV]


def _retention_prompt(q, k, v, g, gn, batch, seq, ts):
    n = q.shape[0]
    steps = seq // ts
    dmask, qdec, kdec, gc = _ret_tables()
    row = lambda width: pl.BlockSpec((ts, width), lambda b, s: (b * steps + s, 0))
    const = lambda a: pl.BlockSpec(a.shape, lambda b, s: (0,) * a.ndim)
    return pl.pallas_call(
        functools.partial(_retention_body, n_chunks=ts // RET_CHUNK),
        grid=(batch, steps),
        in_specs=[row(RET_QK_W), row(RET_QK_W), row(RET_V_W), row(RET_V_W), const(gn),
                  const(dmask), const(qdec), const(kdec), const(gc)],
        out_specs=[row(RET_V_W),
                   pl.BlockSpec((None, RET_HEADS, RET_DK, RET_DV), lambda b, s: (b, 0, 0, 0))],
        out_shape=[jax.ShapeDtypeStruct((n, RET_V_W), BF16),
                   jax.ShapeDtypeStruct((batch, RET_HEADS, RET_DK, RET_DV), F32)],
        scratch_shapes=[pltpu.VMEM((RET_PAIRS, 2 * RET_DK, 2 * RET_DV), F32)],
        compiler_params=_params(2),
        name="retention",
    )(q, k, v, g, gn, dmask, qdec, kdec, gc)


def _attn_body(q_ref, k_ref, v_ref, rb_ref, sel_ref, neg_ref, o_ref, lse_ref, *scratch, group, dil, n_blocks):
    nk = ATT_NK
    scale = ATT_HD ** -0.5
    lane = lax.broadcasted_iota(jnp.int32, (nk, LANES), 1)
    tabs = scratch[-1]

    @pl.when(pl.program_id(0) == 0)
    def _():
        band = neg_ref[...]
        for piece in _split3(rb_ref[group * ATT_HPG:(group + 1) * ATT_HPG, :]):
            band = band + _dot(piece, sel_ref[...])
        col = lax.broadcasted_iota(jnp.int32, (1, 2 * nk), 1)
        for i in range(nk):
            window = band[:, nk - 1 - i:3 * nk - 1 - i]
            for h in range(ATT_HPG):
                tabs[1, h, i:i + 1, :] = window[h:h + 1, :]
                tabs[0, h, i:i + 1, :] = jnp.where(col < nk, -jnp.inf, window[h:h + 1, :])
    if dil > 1:
        o_s, l_s = scratch[:2]

    def chain(s_raw, v_ext, sel, h):
        s = s_raw * scale + (tabs[sel, h] if n_blocks > 1 else tabs[1, h, :, nk:])
        m = jnp.max(s, axis=-1, keepdims=True)
        p = jnp.exp(s - m)
        pv = _dot(p.astype(BF16), v_ext)
        den = pv[:, ATT_HD:]
        return (pv[:, :ATT_HD] / den).astype(BF16), m + jnp.log(den)

    def write(rs, rows, outs):
        lse_tile = jnp.zeros((nk, LANES), F32)
        for h, (o, lse) in enumerate(outs):
            hc = slice(h * ATT_HD, (h + 1) * ATT_HD)
            if dil == 1:
                o_ref[rows, hc] = o
            else:
                o_s[rs, rows, hc] = o
            lse_tile = jnp.where(lane == h, lse, lse_tile)
        if dil == 1:
            lse_ref[rows, :] = lse_tile
        else:
            l_s[rs, rows, :] = lse_tile

    heads = [slice(h * ATT_HD, (h + 1) * ATT_HD) for h in range(ATT_HPG)]

    def block(rs, n, n_prev, sel):
        rows = pl.ds(pl.multiple_of(n * nk, nk), nk)
        prev_rows = pl.ds(pl.multiple_of(n_prev * nk, nk), nk)
        q = q_ref[rs, rows, :]
        k_all = jnp.concatenate([k_ref[rs, prev_rows, :], k_ref[rs, rows, :]], axis=0)
        v_all = jnp.concatenate([v_ref[rs, prev_rows, :], v_ref[rs, rows, :]], axis=0)
        ones = jnp.ones((2 * nk, ATT_HD), BF16)
        outs = []
        for h, hc in enumerate(heads):
            v_ext = jnp.concatenate([v_all[:, hc], ones], axis=1)
            outs.append(chain(_dot_nt(q[:, hc], k_all[:, hc]), v_ext, sel, h))
        write(rs, rows, outs)

    def single_blocks(streams):
        ones = jnp.ones((nk, ATT_HD), BF16)
        scores = [[_dot_nt(q_ref[rs, :, hc], k_ref[rs, :, hc]) for hc in heads] for rs in streams]
        for i, rs in enumerate(streams):
            outs = [chain(scores[i][h], jnp.concatenate([v_ref[rs, :, hc], ones], axis=1), 0, h)
                    for h, hc in enumerate(heads)]
            write(rs, slice(None), outs)

    if n_blocks == 1:
        for r0 in range(0, dil, SINGLE_BLOCK_GROUP):
            single_blocks(range(r0, min(r0 + SINGLE_BLOCK_GROUP, dil)))
    else:
        for rs in range(dil):
            def loop(n, carry, rs=rs):
                block(rs, n, jnp.maximum(n - 1, 0), jnp.minimum(n, 1))
                return carry
            lax.fori_loop(0, n_blocks, loop, 0, unroll=ATTN_UNROLL)

    if dil > 1:
        piece = PERM_ROWS // dil
        row = lax.broadcasted_iota(jnp.int32, (PERM_ROWS, PERM_ROWS), 0)
        col = lax.broadcasted_iota(jnp.int32, (PERM_ROWS, PERM_ROWS), 1)
        perm = jnp.where((col % piece) * dil + col // piece == row, 1.0, 0.0).astype(BF16)
        for j in range(o_ref.shape[0] // PERM_ROWS):
            src = slice(j * piece, (j + 1) * piece)
            dst = slice(j * PERM_ROWS, (j + 1) * PERM_ROWS)
            o_ref[dst, :] = _dot(perm, jnp.concatenate([o_s[r, src, :] for r in range(dil)], axis=0)).astype(BF16)
            lse = jnp.concatenate([l_s[r, src, :] for r in range(dil)], axis=0)
            hi = lse.astype(BF16)
            rem = lse - hi.astype(F32)
            mid = rem.astype(BF16)
            lo = (rem - mid.astype(F32)).astype(BF16)
            lse_ref[dst, :] = _dot(perm, hi) + _dot(perm, mid) + _dot(perm, lo)


def _attention_prompt(aq, ak, av, rel_bias_t, onehot, neg, g):
    batch, dil, length, _ = aq.shape
    seq = dil * length
    n_blocks = length // ATT_NK
    blk = pl.BlockSpec((None, dil, length, ATT_GW), lambda b: (b, 0, 0, 0))
    scratch = [pltpu.VMEM((dil, length, ATT_GW), BF16), pltpu.VMEM((dil, length, LANES), F32)] if dil > 1 else []
    scratch.append(pltpu.VMEM((2, ATT_HPG, ATT_NK, 2 * ATT_NK), F32))
    o, lse = pl.pallas_call(
        functools.partial(_attn_body, group=g, dil=dil, n_blocks=n_blocks),
        grid=(batch,),
        in_specs=[blk, blk, blk, pl.BlockSpec(rel_bias_t.shape, lambda b: (0, 0)),
                  pl.BlockSpec((None, REL_BUCKETS, 3 * ATT_NK), lambda b: (g, 0, 0)),
                  pl.BlockSpec((1, 3 * ATT_NK), lambda b: (0, 0))],
        out_specs=[pl.BlockSpec((None, seq, ATT_GW), lambda b: (b, 0, 0)),
                   pl.BlockSpec((None, seq, LANES), lambda b: (b, 0, 0))],
        out_shape=[jax.ShapeDtypeStruct((batch, seq, ATT_GW), BF16),
                   jax.ShapeDtypeStruct((batch, seq, LANES), F32)],
        scratch_shapes=scratch,
        compiler_params=_params(1),
        name="attention_g%d" % g,
    )(aq, ak, av, rel_bias_t, onehot, neg)
    return o.reshape(batch * seq, ATT_GW), lse.reshape(batch * seq, LANES)


def _tail_math(act_refs, w_refs, y_ref, combine):
    wrb_ref, wab_ref, wo_ref, wu_ref, wd_ref, wpl_ref, wpg_ref, ln2_ref, lnf_ref = w_refs
    if combine:
        ret_ref, o0_ref, o1_ref, o2_ref, l0_ref, l1_ref, l2_ref, gr_ref, ga_ref, x_ref, ple_ref = act_refs
        lses = [l0_ref[...], l1_ref[...], l2_ref[...]]
        outs = [o0_ref, o1_ref, o2_ref]
        parts = []
        for h in range(ATT_HPG):
            lh = [l[:, h:h + 1] for l in lses]
            mx = jnp.maximum(jnp.maximum(lh[0], lh[1]), lh[2])
            e = [jnp.exp(l - mx) for l in lh]
            tot = e[0] + e[1] + e[2]
            acc = None
            for g in range(ATT_GROUPS):
                term = (e[g] / tot) * outs[g][:, h * ATT_HD:(h + 1) * ATT_HD].astype(F32)
                acc = term if acc is None else acc + term
            parts.append(acc)
        att = jnp.concatenate(parts, axis=1).astype(BF16)
    else:
        ret_ref, att_ref, gr_ref, ga_ref, x_ref, ple_ref = act_refs
        att = att_ref[...].astype(BF16)
    a = _dot(ret_ref[...].astype(BF16), wrb_ref[...])
    b = _dot(att, wab_ref[...])
    mixed = _sigmoid(gr_ref[...].astype(F32)) * a + _sigmoid(ga_ref[...].astype(F32)) * b
    x1 = x_ref[...] + _dot(mixed.astype(BF16), wo_ref[...])
    h2 = _rms(x1, ln2_ref[...]).astype(BF16)
    ff_chunk = D_MODEL
    acc = None
    for c in range(D_FF // ff_chunk):
        u = _dot(h2, wu_ref[:, c * ff_chunk:(c + 1) * ff_chunk])
        r = jnp.maximum(u, 0.0)
        t = _dot((r * r).astype(BF16), wd_ref[c * ff_chunk:(c + 1) * ff_chunk, :])
        acc = t if acc is None else acc + t
    x2 = x1 + acc
    gate = _sigmoid(_dot(x2.astype(BF16), wpg_ref[...]))
    x3 = x2 + gate * _dot(ple_ref[...].astype(BF16), wpl_ref[...])
    y_ref[...] = _rms(x3, lnf_ref[...])


def _tail_body(*refs, n_act, combine):
    _tail_math(refs[:n_act], refs[n_act:-1], refs[-1], combine)


def _tail(acts, weights, tm, combine):
    n = acts[0].shape[0]
    row = lambda a: pl.BlockSpec((tm, a.shape[1]), lambda i: (i, 0))
    return pl.pallas_call(
        functools.partial(_tail_body, n_act=len(acts), combine=combine),
        grid=(n // tm,),
        in_specs=[row(a) for a in acts] + [_resident(w.shape) for w in weights],
        out_specs=pl.BlockSpec((tm, D_MODEL), lambda i: (i, 0)),
        out_shape=jax.ShapeDtypeStruct((n, D_MODEL), F32),
        compiler_params=_params(1),
        name="tail" if combine else "tail_sample",
    )(*acts, *weights)


def _tail_shift_body(*refs, n_act, n_w, n_steps):
    nc = 2 * ATT_GROUPS
    acts = refs[:n_act]
    ws = refs[n_act:n_act + n_w]
    new_ref = refs[n_act + n_w]
    base = n_act + n_w + 1
    old = refs[base:base + nc]
    y_ref = refs[base + nc]
    out = refs[base + nc + 1:base + 2 * nc + 1]
    stage = refs[base + 2 * nc + 1:base + 3 * nc + 1]
    sem_in, sem_out, sem_row = refs[base + 3 * nc + 1:]
    s = pl.program_id(0)
    b = lax.shift_right_logical(s, 1)
    parity = lax.rem(s, 2)

    def copy_in(i, bi):
        width = old[i].shape[1]
        return pltpu.make_async_copy(old[i].at[bi, pl.ds(1, width - 1)], stage[i], sem_in.at[i])

    def copy_out(i, bi):
        width = old[i].shape[1]
        return pltpu.make_async_copy(stage[i], out[i].at[bi, pl.ds(0, width - 1)], sem_out.at[i])

    def copy_row(g, kind):
        i = 2 * g + kind
        return pltpu.make_async_copy(new_ref.at[b, kind + 1, g], out[i].at[b, old[i].shape[1] - 1], sem_row.at[g])

    for kind in range(2):
        mine = [2 * g + kind for g in range(ATT_GROUPS)]
        other = [2 * g + 1 - kind for g in range(ATT_GROUPS)]

        @pl.when(parity == kind)
        def _(kind=kind, mine=mine, other=other):
            if kind == 0:
                @pl.when(s == 0)
                def _():
                    for i in mine:
                        copy_in(i, 0).start()
            for i in mine:
                copy_in(i, b).wait()
            for i in mine:
                copy_out(i, b).start()
            if kind == 0:
                @pl.when(s > 0)
                def _():
                    for i in other:
                        copy_out(i, b - 1).wait()
                for i in other:
                    copy_in(i, b).start()
            else:
                for i in other:
                    copy_out(i, b).wait()

                @pl.when(s < n_steps - 1)
                def _():
                    for i in other:
                        copy_in(i, b + 1).start()
            for g in range(ATT_GROUPS):
                copy_row(g, kind).start()

    _tail_math(acts, ws, y_ref, True)

    for kind in range(2):
        @pl.when(parity == kind)
        def _(kind=kind):
            for g in range(ATT_GROUPS):
                copy_row(g, kind).wait()

    @pl.when(s == n_steps - 1)
    def _():
        for g in range(ATT_GROUPS):
            copy_out(2 * g + 1, b).wait()


def _tail_shift(acts, weights, tm, new_qkv, caches):
    n = acts[0].shape[0]
    n_steps = n // tm
    assert n_steps == 2 * new_qkv.shape[0]
    row = lambda a: pl.BlockSpec((tm, a.shape[1]), lambda i: (i, 0))
    anyspec = pl.BlockSpec(memory_space=pl.ANY)
    nc = len(caches)
    new_rows = new_qkv.reshape(new_qkv.shape[0], 3, ATT_GROUPS, ATT_HPG, ATT_HD)
    res = pl.pallas_call(
        functools.partial(_tail_shift_body, n_act=len(acts), n_w=len(weights), n_steps=n_steps),
        grid=(n_steps,),
        in_specs=[row(a) for a in acts] + [_resident(w.shape) for w in weights] + [anyspec] * (nc + 1),
        out_specs=[pl.BlockSpec((tm, D_MODEL), lambda i: (i, 0))] + [anyspec] * nc,
        out_shape=[jax.ShapeDtypeStruct((n, D_MODEL), F32)]
        + [jax.ShapeDtypeStruct(c.shape, c.dtype) for c in caches],
        scratch_shapes=[pltpu.VMEM((c.shape[1] - 1, ATT_HPG, ATT_HD), F32) for c in caches]
        + [pltpu.SemaphoreType.DMA((nc,)), pltpu.SemaphoreType.DMA((nc,)),
           pltpu.SemaphoreType.DMA((ATT_GROUPS,))],
        compiler_params=_params(1),
        name="tail",
    )(*acts, *weights, new_rows, *caches)
    return res[0], res[1:]


def _decode_body(q_ref, k_ref, v_ref, g_ref, gn_ref, st_ref, gam_ref,
                 qkv_ref, rb_ref, slot_ref,
                 ck0_ref, cv0_ref, ck1_ref, cv1_ref, ck2_ref, cv2_ref,
                 ret_ref, att_ref, nst_ref, slot_bias):
    sub = lax.broadcasted_iota(jnp.int32, (8, 2 * RET_DK), 0)
    lane = lax.broadcasted_iota(jnp.int32, (8, 2 * RET_DK), 1)
    row0 = sub == 0
    srow = lax.broadcasted_iota(jnp.int32, (2 * RET_DK, RET_DV), 0)
    for p in range(RET_PAIRS):
        pc = slice(p * 2 * RET_DK, (p + 1) * 2 * RET_DK)
        q2 = jnp.where(row0, jnp.broadcast_to(q_ref[:, pc], (8, 2 * RET_DK)), 0.0)
        k2 = jnp.where(row0, jnp.broadcast_to(k_ref[:, pc], (8, 2 * RET_DK)), 0.0)
        pst = st_ref[p]
        gam = gam_ref[p]
        outer = []
        for hh in range(2):
            h = 2 * p + hh
            hsel = (lane < RET_DK) if hh == 0 else (lane >= RET_DK)
            qm = jnp.where(hsel, q2, 0.0)
            km = jnp.where(hsel, k2, 0.0)
            vh = v_ref[:, h * RET_DV:(h + 1) * RET_DV]
            v8 = jnp.where(row0[:, :RET_DV], jnp.broadcast_to(vh, (8, RET_DV)), 0.0)
            cross = _dot(qm.astype(BF16), (pst * gam).astype(BF16))[0:1, :]
            qk = jnp.sum(qm[0:1, :] * km[0:1, :], axis=-1, keepdims=True)
            o = cross + qk * vh
            res = _gn_swish(o, g_ref[:, h * RET_DV:(h + 1) * RET_DV], gn_ref[:, h * RET_DV:(h + 1) * RET_DV])
            ret_ref[:, h * RET_DV:(h + 1) * RET_DV] = res
            outer.append(_dot_tn(k2.astype(BF16), v8.astype(BF16)))
        nst_ref[p] = pst * gam + jnp.where(srow < RET_DK, outer[0], outer[1])
    scale = ATT_HD ** -0.5
    caches = ((ck0_ref, cv0_ref), (ck1_ref, cv1_ref), (ck2_ref, cv2_ref))
    slot = lax.broadcasted_iota(jnp.int32, (ATT_NK, ATT_HD), 0)
    is_new = slot == 0

    @pl.when(pl.program_id(0) == 0)
    def _():
        pieces = _split3(rb_ref[...])
        for g in range(ATT_GROUPS):
            slot_bias[g] = (_dot(slot_ref[g], pieces[0]) + _dot(slot_ref[g], pieces[1])
                            + _dot(slot_ref[g], pieces[2]))
    for h in range(ATT_HPG):
        o_g, lse_g = [], []
        for g in range(ATT_GROUPS):
            ck_ref, cv_ref = caches[g]
            c0 = g * ATT_GW + h * ATT_HD
            qh = qkv_ref[:, c0:c0 + ATT_HD]
            kk = jnp.where(is_new, qkv_ref[:, ATT_W + c0:ATT_W + c0 + ATT_HD], ck_ref[:, h, :])
            vv = jnp.where(is_new, qkv_ref[:, 2 * ATT_W + c0:2 * ATT_W + c0 + ATT_HD], cv_ref[:, h, :])
            col = g * ATT_HPG + h
            s = jnp.sum(kk * qh, axis=-1, keepdims=True) * scale + slot_bias[g, :, col:col + 1]
            m = jnp.max(s, axis=0, keepdims=True)
            pr = jnp.exp(s - m)
            den = jnp.sum(pr, axis=0, keepdims=True)
            o_g.append(jnp.sum(pr * vv, axis=0, keepdims=True) / den)
            lse_g.append(m + jnp.log(den))
        mx = jnp.maximum(jnp.maximum(lse_g[0], lse_g[1]), lse_g[2])
        e = [jnp.exp(l - mx) for l in lse_g]
        tot = e[0] + e[1] + e[2]
        att_ref[:, h * ATT_HD:(h + 1) * ATT_HD] = ((e[0] / tot) * o_g[0] + (e[1] / tot) * o_g[1]
                                                   + (e[2] / tot) * o_g[2])


def _decode(q, k, v, g, gn, state, qkv, rel_bias, caches):
    batch = q.shape[0]
    gam = jnp.asarray(np.ascontiguousarray(np.broadcast_to(
        np.repeat(np.exp(_ret_log_decay()), RET_DK).reshape(RET_PAIRS, 2 * RET_DK, 1),
        (RET_PAIRS, 2 * RET_DK, RET_DV))), F32)
    vec = lambda a: a.reshape(batch, 1, a.shape[1])
    vspec = lambda width: pl.BlockSpec((None, 1, width), lambda b: (b, 0, 0))
    const = lambda a: pl.BlockSpec(a.shape, lambda b: (0,) * a.ndim)
    st_spec = pl.BlockSpec((None, RET_PAIRS, 2 * RET_DK, RET_DV), lambda b: (b, 0, 0, 0))
    cache_in, cache_specs = [], []
    for gi in range(ATT_GROUPS):
        dil = ATT_DILATIONS[gi]
        for c in caches[2 * gi:2 * gi + 2]:
            cache_in.append(c.reshape(batch, ATT_NK, dil, ATT_HPG, ATT_HD))
            cache_specs.append(pl.BlockSpec((None, ATT_NK, None, ATT_HPG, ATT_HD), lambda b: (b, 0, 0, 0, 0)))
    st_pairs = state.reshape(batch, RET_PAIRS, 2 * RET_DK, RET_DV)
    slot_sel = _slot_select()
    ret, att, nst = pl.pallas_call(
        _decode_body,
        grid=(batch,),
        in_specs=[vspec(RET_QK_W), vspec(RET_QK_W), vspec(RET_V_W), vspec(RET_V_W), const(gn), st_spec, const(gam),
                  vspec(3 * ATT_W), const(rel_bias), const(slot_sel)] + cache_specs,
        out_specs=[vspec(RET_V_W), vspec(ATT_GW), st_spec],
        out_shape=[jax.ShapeDtypeStruct((batch, 1, RET_V_W), F32),
                   jax.ShapeDtypeStruct((batch, 1, ATT_GW), F32),
                   jax.ShapeDtypeStruct(st_pairs.shape, F32)],
        scratch_shapes=[pltpu.VMEM((ATT_GROUPS, ATT_NK, rel_bias.shape[1]), F32)],
        compiler_params=_params(1),
        name="decode",
    )(vec(q), vec(k), vec(v), vec(g), gn, st_pairs, gam, vec(qkv), rel_bias, slot_sel, *cache_in)
    return ret.reshape(batch, RET_V_W), att.reshape(batch, ATT_GW), nst.reshape(state.shape)


def kernel(x_prompt, x_sample, state_ret, cache_k_w128, cache_v_w128, cache_k_w512, cache_v_w512,
           cache_k_w2048, cache_v_w2048, p_prompt, p_sample, ln1_g, w_in, ret_gn_g, w_ret_br, w_att_br,
           w_out, ln2_g, w_up, w_down, w_ple, w_ple_gate, rel_bias, lnf_g):
    depth = w_in.shape[0]
    assert depth == 1
    batch, seq, _ = x_prompt.shape
    dec_batch, dec_seq, _ = x_sample.shape
    assert dec_seq == 1
    past_len = 16384
    l = 0
    ln1 = ln1_g[l][None, :]
    ln2 = ln2_g[l][None, :]
    lnf = lnf_g[None, :]
    gn = ret_gn_g[l][None, :]
    w_in_l = w_in[l]
    w_ret = w_in_l[:, :COL_ATT].astype(BF16)
    w_gate = w_in_l[:, COL_GATE:].astype(BF16)
    w_att = w_in_l[:, COL_ATT:COL_GATE].astype(BF16)
    tail_w = (w_ret_br[l].astype(BF16), w_att_br[l].astype(BF16), w_out[l].astype(BF16), w_up[l].astype(BF16),
              w_down[l].astype(BF16), w_ple[l].astype(BF16), w_ple_gate[l].astype(BF16), ln2, lnf)
    inv_row = _rope_inv_row()

    caches = (cache_k_w128[l], cache_v_w128[l], cache_k_w512[l], cache_v_w512[l],
              cache_k_w2048[l], cache_v_w2048[l])

    xs = x_sample.reshape(dec_batch, D_MODEL)
    sq, sk, sv, sg, sgr, sga = _inproj_ret(xs, ln1, w_ret, w_gate, inv_row, dec_batch, F32, fixed_pos=past_len)
    s_qkv = _inproj_att_sample(xs, ln1, w_att)
    s_ret, s_attn, new_st = _decode(sq, sk, sv, sg, gn, state_ret[l], s_qkv, rel_bias, caches)
    ple_s = p_sample[l].reshape(dec_batch, D_PLE)
    y_s = _tail([s_ret, s_attn, sgr, sga, xs, ple_s], tail_w, dec_batch, False)
    y_sample = y_s.reshape(dec_batch, 1, D_MODEL)

    xp = x_prompt.reshape(batch * seq, D_MODEL)
    rq, rk, rv, rg, gr, ga = _inproj_ret(xp, ln1, w_ret, w_gate, inv_row, TM_INPROJ, BF16, seq=seq)
    att_o = _inproj_att_prompt(xp, ln1, w_att, TM_INPROJ, seq)
    aqs, aks, avs = att_o[0:3], att_o[3:6], att_o[6:9]
    kfull, vfull = att_o[9:12], att_o[12:15]
    ret_out, st_p = _retention_prompt(rq, rk, rv, rg, gn, batch, seq, TS_RETENTION)
    outs, lses = [], []
    band_onehot, band_neg = _band_select()
    rel_bias_t = rel_bias.T
    for g in range(ATT_GROUPS):
        o, lse = _attention_prompt(aqs[g], aks[g], avs[g], rel_bias_t, band_onehot, band_neg, g)
        outs.append(o)
        lses.append(lse)
    ple_p = p_prompt[l].reshape(batch * seq, D_PLE)
    tm_tail = batch * seq // (2 * dec_batch)
    y_p, new_caches = _tail_shift([ret_out] + outs + lses + [gr, ga, xp, ple_p], tail_w, tm_tail, s_qkv, caches)
    y_prompt = y_p.reshape(batch, seq, D_MODEL)
    new_state_p = st_p[None]
    kv_p = []
    for g in range(ATT_GROUPS):
        shape = (1, batch, min(ATT_WINDOWS[g], seq), ATT_HPG, ATT_HD)
        kv_p.append(kfull[g].reshape(shape))
        kv_p.append(vfull[g].reshape(shape))
    kv_s = [c[None] for c in new_caches]

    return (y_prompt, y_sample, new_state_p, *kv_p, new_st[None], *kv_s)
```

```python
import functools
import math

import jax
import jax.numpy as jnp
import numpy as np
from jax import lax
from jax.experimental import pallas as pl
from jax.experimental.pallas import tpu as pltpu

F32 = jnp.float32
BF16 = jnp.bfloat16

D_MODEL = 1024
RET_HEADS = 8
RET_DK = 64
RET_DV = 128
RET_PAIRS = RET_HEADS // 2
RET_CHUNK = 128
ROPE_BASE = 10000.0
ATT_WINDOWS = (128, 512, 2048)
ATT_DILATIONS = (1, 4, 16)
ATT_GROUPS = 3
ATT_HPG = 4
ATT_HD = 128
ATT_NK = 128
ATT_GW = ATT_HPG * ATT_HD
REL_BUCKETS = 32
REL_MAX_DIST = 2048
D_FF = 4 * D_MODEL
D_PLE = 256
NORM_EPS = 1e-6
RET_QK_W = RET_HEADS * RET_DK
RET_V_W = RET_HEADS * RET_DV
ATT_W = ATT_GROUPS * ATT_GW
COL_RET = 0
COL_ATT = 2 * RET_QK_W + 2 * RET_V_W
COL_GATE = COL_ATT + 3 * ATT_W
N_IN = COL_GATE + 2 * D_MODEL

VMEM_LIMIT_V7X = 56 * 1024 * 1024
LANES = 128
TM_INPROJ = 512
TS_RETENTION = 512
ATTN_UNROLL = 4
PERM_ROWS = 256
SINGLE_BLOCK_GROUP = 4


def _dot(a, b):
    return jnp.dot(a, b, preferred_element_type=F32)


def _dot_nt(a, b):
    return lax.dot_general(a, b, (((1,), (1,)), ((), ())), preferred_element_type=F32)


def _dot_tn(a, b):
    return lax.dot_general(a, b, (((0,), (0,)), ((), ())), preferred_element_type=F32)


def _rms(x, g):
    return x * lax.rsqrt(jnp.mean(x * x, axis=-1, keepdims=True) + NORM_EPS) * g


def _sigmoid(x):
    return 1.0 / (1.0 + jnp.exp(-x))


def _resident(shape):
    return pl.BlockSpec(shape, lambda *_: (0,) * len(shape), pipeline_mode=pl.Buffered(1))


def _col_window(rows, width, block_index):
    return pl.BlockSpec((rows, width), lambda *_: (0, block_index), pipeline_mode=pl.Buffered(1))


def _params(n_axes):
    return pltpu.CompilerParams(dimension_semantics=("arbitrary",) * n_axes,
                                vmem_limit_bytes=VMEM_LIMIT_V7X)


def _rope_inv_row():
    half = RET_DK // 2
    inv = ROPE_BASE ** (-jnp.arange(half, dtype=F32) / half)
    return jnp.tile(inv, LANES // half)[None, :]


def _ret_log_decay():
    return np.log1p(-np.exp2(-5.0 - np.arange(RET_HEADS, dtype=np.float32))).astype(np.float32)


def _ret_tables():
    c = RET_CHUNK
    lg = _ret_log_decay()
    i = np.arange(c, dtype=np.float32)
    diff = i[:, None] - i[None, :]
    dmask = np.where(diff[None] >= 0, np.exp(np.maximum(diff, 0.0)[None] * lg[:, None, None]), 0.0)
    q_decay = np.exp((i + 1.0)[:, None] * lg[None, :])
    k_decay = np.exp((c - 1.0 - i)[:, None] * lg[None, :])
    qdec = np.broadcast_to(q_decay.T[:, :, None], (RET_HEADS, c, RET_DV))
    kdec = np.repeat(k_decay, RET_DK, axis=1).reshape(c, RET_PAIRS, 2 * RET_DK).transpose(1, 0, 2)
    gc = np.repeat(np.exp(c * lg), RET_DV).reshape(RET_PAIRS, 1, 2 * RET_DV)
    return tuple(jnp.asarray(np.ascontiguousarray(t), F32) for t in (dmask, qdec, kdec, gc))


def _rel_buckets():
    max_exact = REL_BUCKETS // 2
    out = []
    for dil in ATT_DILATIONS:
        d = np.arange(ATT_NK, dtype=np.int32) * dil
        log_ratio = (np.log(np.maximum(d, 1).astype(np.float32) / np.float32(max_exact))
                     / np.float32(math.log(REL_MAX_DIST / max_exact)))
        large = max_exact + (log_ratio * np.float32(REL_BUCKETS - max_exact)).astype(np.int32)
        out.append(np.where(d < max_exact, d, np.minimum(large, REL_BUCKETS - 1)))
    return np.stack(out)


def _band_select():
    nk = ATT_NK
    buckets = _rel_buckets()
    onehot = np.zeros((ATT_GROUPS, REL_BUCKETS, 3 * nk), np.float32)
    for g in range(ATT_GROUPS):
        for k in range(nk, 2 * nk):
            onehot[g, buckets[g, 2 * nk - 1 - k], k] = 1.0
    mask = np.full((1, 3 * nk), -np.inf, np.float32)
    mask[0, nk:2 * nk] = 0.0
    return jnp.asarray(onehot, BF16), jnp.asarray(mask, F32)


def _slot_select():
    buckets = _rel_buckets()
    onehot = np.zeros((ATT_GROUPS, REL_BUCKETS, ATT_NK), np.float32)
    for g in range(ATT_GROUPS):
        for slot in range(ATT_NK):
            onehot[g, buckets[g, 0 if slot == 0 else ATT_NK - slot], slot] = 1.0
    return jnp.asarray(onehot, BF16)


def _split3(x):
    hi = x.astype(BF16)
    rem = x - hi.astype(F32)
    mid = rem.astype(BF16)
    lo = (rem - mid.astype(F32)).astype(BF16)
    return hi, mid, lo


def _inproj_ret_body(x_ref, ln_ref, w_ref, wg0_ref, wg1_ref, wg2_ref, wg3_ref, inv_ref,
                     q_ref, k_ref, v_ref, g_ref, gr_ref, ga_ref, cos_s, sin_s,
                     *, tm, tiles, fixed_pos):
    i = pl.program_id(0)
    lane = lax.broadcasted_iota(jnp.int32, (tm, LANES), 1)
    first_half = (lane % RET_DK) < (RET_DK // 2)

    def tables(pos):
        ang = pos.astype(F32) * inv_ref[...]
        sin = jnp.sin(ang)
        return jnp.cos(ang), jnp.where(first_half, -sin, sin)

    if fixed_pos is None:
        trow = pl.ds(pl.multiple_of(lax.rem(i, tiles) * tm, tm), tm)

        @pl.when(i < tiles)
        def _():
            cos_s[trow, :], sin_s[trow, :] = tables(i * tm + lax.broadcasted_iota(jnp.int32, (tm, LANES), 0))
        cos = cos_s[trow, :]
        sin = sin_s[trow, :]
    else:
        cos, sin = tables(jnp.full((tm, LANES), fixed_pos, jnp.int32))
    x = x_ref[...]
    h = _rms(x, ln_ref[...]).astype(BF16)
    qk = _dot(h, w_ref[:, 0:2 * RET_QK_W])
    n_q = RET_QK_W // LANES
    for c in range(2 * n_q):
        xc = qk[:, c * LANES:(c + 1) * LANES]
        swapped = jnp.where(first_half, pltpu.roll(xc, LANES - RET_DK // 2, 1), pltpu.roll(xc, RET_DK // 2, 1))
        r = xc * cos + swapped * sin
        if c < n_q:
            q_ref[:, c * LANES:(c + 1) * LANES] = r.astype(q_ref.dtype)
        else:
            k_ref[:, (c - n_q) * LANES:(c - n_q + 1) * LANES] = (r * (RET_DK ** -0.5)).astype(k_ref.dtype)
    o = 2 * RET_QK_W
    v_ref[...] = _dot(h, w_ref[:, o:o + RET_V_W]).astype(v_ref.dtype)
    o += RET_V_W
    g_ref[...] = _dot(h, w_ref[:, o:o + RET_V_W]).astype(g_ref.dtype)
    half = D_MODEL // 2
    gr_ref[:, :half] = _dot(h, wg0_ref[...]).astype(gr_ref.dtype)
    gr_ref[:, half:] = _dot(h, wg1_ref[...]).astype(gr_ref.dtype)
    ga_ref[:, :half] = _dot(h, wg2_ref[...]).astype(ga_ref.dtype)
    ga_ref[:, half:] = _dot(h, wg3_ref[...]).astype(ga_ref.dtype)


def _inproj_ret(x2d, ln, w_in, inv_row, tm, out_dtype, seq=None, fixed_pos=None):
    n = x2d.shape[0]
    tiles = None if seq is None else seq // tm
    table_rows = tm if seq is None else seq
    row = lambda width: pl.BlockSpec((tm, width), lambda i: (i, 0))
    widths = (RET_QK_W, RET_QK_W, RET_V_W, RET_V_W, D_MODEL, D_MODEL)
    half = D_MODEL // 2
    assert COL_GATE % half == 0
    return pl.pallas_call(
        functools.partial(_inproj_ret_body, tm=tm, tiles=tiles, fixed_pos=fixed_pos),
        grid=(n // tm,),
        in_specs=[row(D_MODEL), _resident((1, D_MODEL)), _col_window(D_MODEL, COL_ATT, 0)]
        + [_col_window(D_MODEL, half, COL_GATE // half + j) for j in range(4)] + [_resident((1, LANES))],
        out_specs=[row(wd) for wd in widths],
        out_shape=[jax.ShapeDtypeStruct((n, wd), out_dtype) for wd in widths],
        scratch_shapes=[pltpu.VMEM((table_rows, LANES), F32), pltpu.VMEM((table_rows, LANES), F32)],
        compiler_params=_params(1),
        name="inproj_ret",
    )(x2d, ln, w_in, w_in, w_in, w_in, w_in, inv_row)


def _inproj_att_body(x_ref, ln_ref, wq_ref, wk_ref, wv_ref, *refs, tm, keeps, seq):
    lowp = refs[:3 * ATT_GROUPS]
    full = refs[3 * ATT_GROUPS:5 * ATT_GROUPS]
    res = refs[5 * ATT_GROUPS]
    x = x_ref[...]
    h = _rms(x, ln_ref[...]).astype(BF16)
    slot = 0
    for g in reversed(range(ATT_GROUPS)):
        for kind in reversed(range(3)):
            dil = ATT_DILATIONS[g]
            dst = lowp[kind * ATT_GROUPS + g]
            r = _dot(h, (wq_ref, wk_ref, wv_ref)[kind][:, g * ATT_GW:(g + 1) * ATT_GW])
            if dil == 1:
                dst[0] = r.astype(dst.dtype)
            else:
                for hh in range(ATT_HPG):
                    res[slot, hh] = r[:, hh * ATT_HD:(hh + 1) * ATT_HD]
                for rr in range(dil):
                    for hh in range(ATT_HPG):
                        dst[rr, :, hh * ATT_HD:(hh + 1) * ATT_HD] = (
                            res[slot, hh, pl.ds(rr, tm // dil, stride=dil), :].astype(dst.dtype))
                slot += 1
            if kind > 0:
                cache = full[(kind - 1) * ATT_GROUPS + g]
                rows = tm if keeps[g] == seq else keeps[g]
                for hh in range(ATT_HPG):
                    cache[pl.ds(hh, rows, stride=ATT_HPG), :] = r[tm - rows:, hh * ATT_HD:(hh + 1) * ATT_HD]


def _inproj_att_prompt(x2d, ln, w_in, tm, seq):
    n = x2d.shape[0]
    batch = n // seq
    tiles = seq // tm
    keeps = tuple(min(wd, seq) for wd in ATT_WINDOWS)
    assert all(kp <= tm or kp == seq for kp in keeps) and seq % tm == 0
    out_specs, out_shape = [], []
    for _ in range(3):
        for g in range(ATT_GROUPS):
            dil = ATT_DILATIONS[g]
            out_specs.append(pl.BlockSpec((None, dil, tm // dil, ATT_GW), lambda i: (i // tiles, 0, i % tiles, 0)))
            out_shape.append(jax.ShapeDtypeStruct((batch, dil, seq // dil, ATT_GW), BF16))
    for _ in range(2):
        for g in range(ATT_GROUPS):
            if keeps[g] == seq:
                idx = lambda i: (i // tiles, i % tiles, 0)
                rows = tm
            else:
                idx = lambda i: (i // tiles, 0, 0)
                rows = keeps[g]
            out_specs.append(pl.BlockSpec((None, rows * ATT_HPG, ATT_HD), idx))
            out_shape.append(jax.ShapeDtypeStruct((batch, keeps[g] * ATT_HPG, ATT_HD), F32))
    return pl.pallas_call(
        functools.partial(_inproj_att_body, tm=tm, keeps=keeps, seq=seq),
        grid=(n // tm,),
        in_specs=[pl.BlockSpec((tm, D_MODEL), lambda i: (i, 0)), _resident((1, D_MODEL))]
        + [_col_window(D_MODEL, ATT_W, COL_ATT // ATT_W + kind) for kind in range(3)],
        out_specs=out_specs,
        out_shape=out_shape,
        scratch_shapes=[pltpu.VMEM((3 * sum(d > 1 for d in ATT_DILATIONS), ATT_HPG, tm, ATT_HD), F32)],
        compiler_params=_params(1),
        name="inproj_att",
    )(x2d, ln, w_in, w_in, w_in)


def _inproj_att_sample_body(x_ref, ln_ref, wq_ref, wk_ref, wv_ref, o_ref):
    h = _rms(x_ref[...], ln_ref[...]).astype(BF16)
    for kind, w_ref in enumerate((wq_ref, wk_ref, wv_ref)):
        for g in range(ATT_GROUPS):
            c = kind * ATT_GROUPS + g
            o_ref[:, c * ATT_GW:(c + 1) * ATT_GW] = _dot(h, w_ref[:, g * ATT_GW:(g + 1) * ATT_GW])


def _inproj_att_sample(x2d, ln, w_in):
    n = x2d.shape[0]
    return pl.pallas_call(
        _inproj_att_sample_body,
        grid=(1,),
        in_specs=[pl.BlockSpec((n, D_MODEL), lambda i: (0, 0)), _resident((1, D_MODEL))]
        + [_col_window(D_MODEL, ATT_W, COL_ATT // ATT_W + kind) for kind in range(3)],
        out_specs=pl.BlockSpec((n, 3 * ATT_W), lambda i: (0, 0)),
        out_shape=jax.ShapeDtypeStruct((n, 3 * ATT_W), F32),
        compiler_params=_params(1),
        name="inproj_att_sample",
    )(x2d, ln, w_in, w_in, w_in)


def _gn_swish(o, gate, gn):
    mu = jnp.mean(o, axis=-1, keepdims=True)
    d = o - mu
    var = jnp.mean(d * d, axis=-1, keepdims=True)
    on = d * lax.rsqrt(var + NORM_EPS) * gn
    return gate * _sigmoid(gate) * on


def _retention_body(q_ref, k_ref, v_ref, g_ref, gn_ref, dm_ref, qdec_ref, kdec_ref, gc_ref,
                    out_ref, st_ref, state, *, n_chunks):
    c = RET_CHUNK

    @pl.when(pl.program_id(1) == 0)
    def _():
        state[...] = jnp.zeros_like(state)

    lane = lax.broadcasted_iota(jnp.int32, (c, 2 * RET_DK), 1)
    head0 = lane < RET_DK

    def chunk(ci, carry):
        rows = pl.ds(pl.multiple_of(ci * c, c), c)
        for p in range(RET_PAIRS):
            q2 = q_ref[rows, p * 2 * RET_DK:(p + 1) * 2 * RET_DK]
            k2 = k_ref[rows, p * 2 * RET_DK:(p + 1) * 2 * RET_DK]
            v2 = v_ref[rows, p * 2 * RET_DV:(p + 1) * 2 * RET_DV]
            pst = state[p]
            pst_lo = pst.astype(BF16)
            zero = jnp.zeros_like(q2)
            for hh in range(2):
                h = 2 * p + hh
                qm = jnp.where(head0 if hh == 0 else jnp.logical_not(head0), q2, zero)
                vh = v2[:, hh * RET_DV:(hh + 1) * RET_DV]
                s = _dot_nt(qm, k2) * dm_ref[h]
                o = _dot(s.astype(BF16), vh)
                o = o + _dot(qm, pst_lo[:, hh * RET_DV:(hh + 1) * RET_DV]) * qdec_ref[h]
                gate = g_ref[rows, h * RET_DV:(h + 1) * RET_DV].astype(F32)
                res = _gn_swish(o, gate, gn_ref[:, h * RET_DV:(h + 1) * RET_DV])
                out_ref[rows, h * RET_DV:(h + 1) * RET_DV] = res.astype(out_ref.dtype)
            kd = (k2.astype(F32) * kdec_ref[p]).astype(BF16)
            state[p] = pst * gc_ref[p] + _dot_tn(kd, v2)
        return carry

    lax.fori_loop(0, n_chunks, chunk, 0, unroll=True)
    for p in range(RET_PAIRS):
        pst = state[p]
        for hh in range(2):
            st_ref[2 * p + hh] = pst[hh * RET_DK:(hh + 1) * RET_DK, hh * RET_DV:(hh + 1) * RET_DV]


def _retention_prompt(q, k, v, g, gn, batch, seq, ts):
    n = q.shape[0]
    steps = seq // ts
    dmask, qdec, kdec, gc = _ret_tables()
    row = lambda width: pl.BlockSpec((ts, width), lambda b, s: (b * steps + s, 0))
    const = lambda a: pl.BlockSpec(a.shape, lambda b, s: (0,) * a.ndim)
    return pl.pallas_call(
        functools.partial(_retention_body, n_chunks=ts // RET_CHUNK),
        grid=(batch, steps),
        in_specs=[row(RET_QK_W), row(RET_QK_W), row(RET_V_W), row(RET_V_W), const(gn),
                  const(dmask), const(qdec), const(kdec), const(gc)],
        out_specs=[row(RET_V_W),
                   pl.BlockSpec((None, RET_HEADS, RET_DK, RET_DV), lambda b, s: (b, 0, 0, 0))],
        out_shape=[jax.ShapeDtypeStruct((n, RET_V_W), BF16),
                   jax.ShapeDtypeStruct((batch, RET_HEADS, RET_DK, RET_DV), F32)],
        scratch_shapes=[pltpu.VMEM((RET_PAIRS, 2 * RET_DK, 2 * RET_DV), F32)],
        compiler_params=_params(2),
        name="retention",
    )(q, k, v, g, gn, dmask, qdec, kdec, gc)


def _attn_body(q_ref, k_ref, v_ref, rb_ref, sel_ref, neg_ref, o_ref, lse_ref, *scratch, group, dil, n_blocks):
    nk = ATT_NK
    scale = ATT_HD ** -0.5
    lane = lax.broadcasted_iota(jnp.int32, (nk, LANES), 1)
    tabs = scratch[-1]

    @pl.when(pl.program_id(0) == 0)
    def _():
        band = neg_ref[...]
        for piece in _split3(rb_ref[group * ATT_HPG:(group + 1) * ATT_HPG, :]):
            band = band + _dot(piece, sel_ref[...])
        col = lax.broadcasted_iota(jnp.int32, (1, 2 * nk), 1)
        for i in range(nk):
            window = band[:, nk - 1 - i:3 * nk - 1 - i]
            for h in range(ATT_HPG):
                tabs[1, h, i:i + 1, :] = window[h:h + 1, :]
                tabs[0, h, i:i + 1, :] = jnp.where(col < nk, -jnp.inf, window[h:h + 1, :])
    if dil > 1:
        o_s, l_s = scratch[:2]

    def chain(s_raw, v_ext, sel, h):
        s = s_raw * scale + (tabs[sel, h] if n_blocks > 1 else tabs[1, h, :, nk:])
        m = jnp.max(s, axis=-1, keepdims=True)
        p = jnp.exp(s - m)
        pv = _dot(p.astype(BF16), v_ext)
        den = pv[:, ATT_HD:]
        return (pv[:, :ATT_HD] / den).astype(BF16), m + jnp.log(den)

    def write(rs, rows, outs):
        lse_tile = jnp.zeros((nk, LANES), F32)
        for h, (o, lse) in enumerate(outs):
            hc = slice(h * ATT_HD, (h + 1) * ATT_HD)
            if dil == 1:
                o_ref[rows, hc] = o
            else:
                o_s[rs, rows, hc] = o
            lse_tile = jnp.where(lane == h, lse, lse_tile)
        if dil == 1:
            lse_ref[rows, :] = lse_tile
        else:
            l_s[rs, rows, :] = lse_tile

    heads = [slice(h * ATT_HD, (h + 1) * ATT_HD) for h in range(ATT_HPG)]

    def block(rs, n, n_prev, sel):
        rows = pl.ds(pl.multiple_of(n * nk, nk), nk)
        prev_rows = pl.ds(pl.multiple_of(n_prev * nk, nk), nk)
        q = q_ref[rs, rows, :]
        k_all = jnp.concatenate([k_ref[rs, prev_rows, :], k_ref[rs, rows, :]], axis=0)
        v_all = jnp.concatenate([v_ref[rs, prev_rows, :], v_ref[rs, rows, :]], axis=0)
        ones = jnp.ones((2 * nk, ATT_HD), BF16)
        outs = []
        for h, hc in enumerate(heads):
            v_ext = jnp.concatenate([v_all[:, hc], ones], axis=1)
            outs.append(chain(_dot_nt(q[:, hc], k_all[:, hc]), v_ext, sel, h))
        write(rs, rows, outs)

    def single_blocks(streams):
        ones = jnp.ones((nk, ATT_HD), BF16)
        scores = [[_dot_nt(q_ref[rs, :, hc], k_ref[rs, :, hc]) for hc in heads] for rs in streams]
        for i, rs in enumerate(streams):
            outs = [chain(scores[i][h], jnp.concatenate([v_ref[rs, :, hc], ones], axis=1), 0, h)
                    for h, hc in enumerate(heads)]
            write(rs, slice(None), outs)

    if n_blocks == 1:
        for r0 in range(0, dil, SINGLE_BLOCK_GROUP):
            single_blocks(range(r0, min(r0 + SINGLE_BLOCK_GROUP, dil)))
    else:
        for rs in range(dil):
            def loop(n, carry, rs=rs):
                block(rs, n, jnp.maximum(n - 1, 0), jnp.minimum(n, 1))
                return carry
            lax.fori_loop(0, n_blocks, loop, 0, unroll=ATTN_UNROLL)

    if dil > 1:
        piece = PERM_ROWS // dil
        row = lax.broadcasted_iota(jnp.int32, (PERM_ROWS, PERM_ROWS), 0)
        col = lax.broadcasted_iota(jnp.int32, (PERM_ROWS, PERM_ROWS), 1)
        perm = jnp.where((col % piece) * dil + col // piece == row, 1.0, 0.0).astype(BF16)
        for j in range(o_ref.shape[0] // PERM_ROWS):
            src = slice(j * piece, (j + 1) * piece)
            dst = slice(j * PERM_ROWS, (j + 1) * PERM_ROWS)
            o_ref[dst, :] = _dot(perm, jnp.concatenate([o_s[r, src, :] for r in range(dil)], axis=0)).astype(BF16)
            lse = jnp.concatenate([l_s[r, src, :] for r in range(dil)], axis=0)
            hi = lse.astype(BF16)
            rem = lse - hi.astype(F32)
            mid = rem.astype(BF16)
            lo = (rem - mid.astype(F32)).astype(BF16)
            lse_ref[dst, :] = _dot(perm, hi) + _dot(perm, mid) + _dot(perm, lo)


def _attention_prompt(aq, ak, av, rel_bias_t, onehot, neg, g):
    batch, dil, length, _ = aq.shape
    seq = dil * length
    n_blocks = length // ATT_NK
    blk = pl.BlockSpec((None, dil, length, ATT_GW), lambda b: (b, 0, 0, 0))
    scratch = [pltpu.VMEM((dil, length, ATT_GW), BF16), pltpu.VMEM((dil, length, LANES), F32)] if dil > 1 else []
    scratch.append(pltpu.VMEM((2, ATT_HPG, ATT_NK, 2 * ATT_NK), F32))
    o, lse = pl.pallas_call(
        functools.partial(_attn_body, group=g, dil=dil, n_blocks=n_blocks),
        grid=(batch,),
        in_specs=[blk, blk, blk, pl.BlockSpec(rel_bias_t.shape, lambda b: (0, 0)),
                  pl.BlockSpec((None, REL_BUCKETS, 3 * ATT_NK), lambda b: (g, 0, 0)),
                  pl.BlockSpec((1, 3 * ATT_NK), lambda b: (0, 0))],
        out_specs=[pl.BlockSpec((None, seq, ATT_GW), lambda b: (b, 0, 0)),
                   pl.BlockSpec((None, seq, LANES), lambda b: (b, 0, 0))],
        out_shape=[jax.ShapeDtypeStruct((batch, seq, ATT_GW), BF16),
                   jax.ShapeDtypeStruct((batch, seq, LANES), F32)],
        scratch_shapes=scratch,
        compiler_params=_params(1),
        name="attention_g%d" % g,
    )(aq, ak, av, rel_bias_t, onehot, neg)
    return o.reshape(batch * seq, ATT_GW), lse.reshape(batch * seq, LANES)


def _tail_math(act_refs, w_refs, y_ref, combine):
    wrb_ref, wab_ref, wo_ref, wu_ref, wd_ref, wpl_ref, wpg_ref, ln2_ref, lnf_ref = w_refs
    if combine:
        ret_ref, o0_ref, o1_ref, o2_ref, l0_ref, l1_ref, l2_ref, gr_ref, ga_ref, x_ref, ple_ref = act_refs
        lses = [l0_ref[...], l1_ref[...], l2_ref[...]]
        outs = [o0_ref, o1_ref, o2_ref]
        parts = []
        for h in range(ATT_HPG):
            lh = [l[:, h:h + 1] for l in lses]
            mx = jnp.maximum(jnp.maximum(lh[0], lh[1]), lh[2])
            e = [jnp.exp(l - mx) for l in lh]
            tot = e[0] + e[1] + e[2]
            acc = None
            for g in range(ATT_GROUPS):
                term = (e[g] / tot) * outs[g][:, h * ATT_HD:(h + 1) * ATT_HD].astype(F32)
                acc = term if acc is None else acc + term
            parts.append(acc)
        att = jnp.concatenate(parts, axis=1).astype(BF16)
    else:
        ret_ref, att_ref, gr_ref, ga_ref, x_ref, ple_ref = act_refs
        att = att_ref[...].astype(BF16)
    a = _dot(ret_ref[...].astype(BF16), wrb_ref[...])
    b = _dot(att, wab_ref[...])
    mixed = _sigmoid(gr_ref[...].astype(F32)) * a + _sigmoid(ga_ref[...].astype(F32)) * b
    x1 = x_ref[...] + _dot(mixed.astype(BF16), wo_ref[...])
    h2 = _rms(x1, ln2_ref[...]).astype(BF16)
    ff_chunk = D_MODEL
    acc = None
    for c in range(D_FF // ff_chunk):
        u = _dot(h2, wu_ref[:, c * ff_chunk:(c + 1) * ff_chunk])
        r = jnp.maximum(u, 0.0)
        t = _dot((r * r).astype(BF16), wd_ref[c * ff_chunk:(c + 1) * ff_chunk, :])
        acc = t if acc is None else acc + t
    x2 = x1 + acc
    gate = _sigmoid(_dot(x2.astype(BF16), wpg_ref[...]))
    x3 = x2 + gate * _dot(ple_ref[...].astype(BF16), wpl_ref[...])
    y_ref[...] = _rms(x3, lnf_ref[...])


def _tail_body(*refs, n_act, combine):
    _tail_math(refs[:n_act], refs[n_act:-1], refs[-1], combine)


def _tail(acts, weights, tm, combine):
    n = acts[0].shape[0]
    row = lambda a: pl.BlockSpec((tm, a.shape[1]), lambda i: (i, 0))
    return pl.pallas_call(
        functools.partial(_tail_body, n_act=len(acts), combine=combine),
        grid=(n // tm,),
        in_specs=[row(a) for a in acts] + [_resident(w.shape) for w in weights],
        out_specs=pl.BlockSpec((tm, D_MODEL), lambda i: (i, 0)),
        out_shape=jax.ShapeDtypeStruct((n, D_MODEL), F32),
        compiler_params=_params(1),
        name="tail" if combine else "tail_sample",
    )(*acts, *weights)


def _tail_shift_body(*refs, n_act, n_w, n_steps):
    nc = 2 * ATT_GROUPS
    acts = refs[:n_act]
    ws = refs[n_act:n_act + n_w]
    new_ref = refs[n_act + n_w]
    base = n_act + n_w + 1
    old = refs[base:base + nc]
    y_ref = refs[base + nc]
    out = refs[base + nc + 1:base + 2 * nc + 1]
    stage = refs[base + 2 * nc + 1:base + 3 * nc + 1]
    sem_in, sem_out, sem_row = refs[base + 3 * nc + 1:]
    s = pl.program_id(0)
    b = lax.shift_right_logical(s, 1)
    parity = lax.rem(s, 2)

    def copy_in(i, bi):
        width = old[i].shape[1]
        return pltpu.make_async_copy(old[i].at[bi, pl.ds(1, width - 1)], stage[i], sem_in.at[i])

    def copy_out(i, bi):
        width = old[i].shape[1]
        return pltpu.make_async_copy(stage[i], out[i].at[bi, pl.ds(0, width - 1)], sem_out.at[i])

    def copy_row(g, kind):
        i = 2 * g + kind
        return pltpu.make_async_copy(new_ref.at[b, kind + 1, g], out[i].at[b, old[i].shape[1] - 1], sem_row.at[g])

    for kind in range(2):
        mine = [2 * g + kind for g in range(ATT_GROUPS)]
        other = [2 * g + 1 - kind for g in range(ATT_GROUPS)]

        @pl.when(parity == kind)
        def _(kind=kind, mine=mine, other=other):
            if kind == 0:
                @pl.when(s == 0)
                def _():
                    for i in mine:
                        copy_in(i, 0).start()
            for i in mine:
                copy_in(i, b).wait()
            for i in mine:
                copy_out(i, b).start()
            if kind == 0:
                @pl.when(s > 0)
                def _():
                    for i in other:
                        copy_out(i, b - 1).wait()
                for i in other:
                    copy_in(i, b).start()
            else:
                for i in other:
                    copy_out(i, b).wait()

                @pl.when(s < n_steps - 1)
                def _():
                    for i in other:
                        copy_in(i, b + 1).start()
            for g in range(ATT_GROUPS):
                copy_row(g, kind).start()

    _tail_math(acts, ws, y_ref, True)

    for kind in range(2):
        @pl.when(parity == kind)
        def _(kind=kind):
            for g in range(ATT_GROUPS):
                copy_row(g, kind).wait()

    @pl.when(s == n_steps - 1)
    def _():
        for g in range(ATT_GROUPS):
            copy_out(2 * g + 1, b).wait()


def _tail_shift(acts, weights, tm, new_qkv, caches):
    n = acts[0].shape[0]
    n_steps = n // tm
    assert n_steps == 2 * new_qkv.shape[0]
    row = lambda a: pl.BlockSpec((tm, a.shape[1]), lambda i: (i, 0))
    anyspec = pl.BlockSpec(memory_space=pl.ANY)
    nc = len(caches)
    new_rows = new_qkv.reshape(new_qkv.shape[0], 3, ATT_GROUPS, ATT_HPG, ATT_HD)
    res = pl.pallas_call(
        functools.partial(_tail_shift_body, n_act=len(acts), n_w=len(weights), n_steps=n_steps),
        grid=(n_steps,),
        in_specs=[row(a) for a in acts] + [_resident(w.shape) for w in weights] + [anyspec] * (nc + 1),
        out_specs=[pl.BlockSpec((tm, D_MODEL), lambda i: (i, 0))] + [anyspec] * nc,
        out_shape=[jax.ShapeDtypeStruct((n, D_MODEL), F32)]
        + [jax.ShapeDtypeStruct(c.shape, c.dtype) for c in caches],
        scratch_shapes=[pltpu.VMEM((c.shape[1] - 1, ATT_HPG, ATT_HD), F32) for c in caches]
        + [pltpu.SemaphoreType.DMA((nc,)), pltpu.SemaphoreType.DMA((nc,)),
           pltpu.SemaphoreType.DMA((ATT_GROUPS,))],
        compiler_params=_params(1),
        name="tail",
    )(*acts, *weights, new_rows, *caches)
    return res[0], res[1:]


def _decode_body(q_ref, k_ref, v_ref, g_ref, gn_ref, st_ref, gam_ref,
                 qkv_ref, rb_ref, slot_ref,
                 ck0_ref, cv0_ref, ck1_ref, cv1_ref, ck2_ref, cv2_ref,
                 ret_ref, att_ref, nst_ref, slot_bias):
    sub = lax.broadcasted_iota(jnp.int32, (8, 2 * RET_DK), 0)
    lane = lax.broadcasted_iota(jnp.int32, (8, 2 * RET_DK), 1)
    row0 = sub == 0
    srow = lax.broadcasted_iota(jnp.int32, (2 * RET_DK, RET_DV), 0)
    for p in range(RET_PAIRS):
        pc = slice(p * 2 * RET_DK, (p + 1) * 2 * RET_DK)
        q2 = jnp.where(row0, jnp.broadcast_to(q_ref[:, pc], (8, 2 * RET_DK)), 0.0)
        k2 = jnp.where(row0, jnp.broadcast_to(k_ref[:, pc], (8, 2 * RET_DK)), 0.0)
        pst = st_ref[p]
        gam = gam_ref[p]
        outer = []
        for hh in range(2):
            h = 2 * p + hh
            hsel = (lane < RET_DK) if hh == 0 else (lane >= RET_DK)
            qm = jnp.where(hsel, q2, 0.0)
            km = jnp.where(hsel, k2, 0.0)
            vh = v_ref[:, h * RET_DV:(h + 1) * RET_DV]
            v8 = jnp.where(row0[:, :RET_DV], jnp.broadcast_to(vh, (8, RET_DV)), 0.0)
            cross = _dot(qm.astype(BF16), (pst * gam).astype(BF16))[0:1, :]
            qk = jnp.sum(qm[0:1, :] * km[0:1, :], axis=-1, keepdims=True)
            o = cross + qk * vh
            res = _gn_swish(o, g_ref[:, h * RET_DV:(h + 1) * RET_DV], gn_ref[:, h * RET_DV:(h + 1) * RET_DV])
            ret_ref[:, h * RET_DV:(h + 1) * RET_DV] = res
            outer.append(_dot_tn(k2.astype(BF16), v8.astype(BF16)))
        nst_ref[p] = pst * gam + jnp.where(srow < RET_DK, outer[0], outer[1])
    scale = ATT_HD ** -0.5
    caches = ((ck0_ref, cv0_ref), (ck1_ref, cv1_ref), (ck2_ref, cv2_ref))
    slot = lax.broadcasted_iota(jnp.int32, (ATT_NK, ATT_HD), 0)
    is_new = slot == 0
    first = row0[:, :ATT_HD]

    @pl.when(pl.program_id(0) == 0)
    def _():
        pieces = _split3(rb_ref[...])
        for g in range(ATT_GROUPS):
            rows = slice(g * ATT_HPG, (g + 1) * ATT_HPG)
            slot_bias[rows, :] = (_dot(pieces[0][rows], slot_ref[g]) + _dot(pieces[1][rows], slot_ref[g])
                                  + _dot(pieces[2][rows], slot_ref[g]))

    def one_row(x):
        return jnp.where(first, jnp.broadcast_to(x, (8, ATT_HD)), 0.0).astype(BF16)

    scores = {}
    for h in range(ATT_HPG):
        for g in range(ATT_GROUPS):
            c0 = g * ATT_GW + h * ATT_HD
            kk = jnp.where(is_new, qkv_ref[:, ATT_W + c0:ATT_W + c0 + ATT_HD], caches[g][0][:, h, :])
            scores[g, h] = _dot_nt(one_row(qkv_ref[:, c0:c0 + ATT_HD]), kk.astype(BF16))[0:1, :]
    for h in range(ATT_HPG):
        o_g, lse_g = [], []
        for g in range(ATT_GROUPS):
            c0 = g * ATT_GW + h * ATT_HD
            vv = jnp.where(is_new, qkv_ref[:, 2 * ATT_W + c0:2 * ATT_W + c0 + ATT_HD], caches[g][1][:, h, :])
            gh = g * ATT_HPG + h
            s = scores[g, h] * scale + slot_bias[gh:gh + 1, :]
            m = jnp.max(s, axis=-1, keepdims=True)
            pr = jnp.exp(s - m)
            den = jnp.sum(pr, axis=-1, keepdims=True)
            o_g.append(_dot(one_row(pr), vv.astype(BF16))[0:1, :] / den)
            lse_g.append(m + jnp.log(den))
        mx = jnp.maximum(jnp.maximum(lse_g[0], lse_g[1]), lse_g[2])
        e = [jnp.exp(l - mx) for l in lse_g]
        tot = e[0] + e[1] + e[2]
        att_ref[:, h * ATT_HD:(h + 1) * ATT_HD] = ((e[0] / tot) * o_g[0] + (e[1] / tot) * o_g[1]
                                                   + (e[2] / tot) * o_g[2])


def _decode(q, k, v, g, gn, state, qkv, rel_bias_t, caches):
    batch = q.shape[0]
    gam = jnp.asarray(np.ascontiguousarray(np.broadcast_to(
        np.repeat(np.exp(_ret_log_decay()), RET_DK).reshape(RET_PAIRS, 2 * RET_DK, 1),
        (RET_PAIRS, 2 * RET_DK, RET_DV))), F32)
    vec = lambda a: a.reshape(batch, 1, a.shape[1])
    vspec = lambda width: pl.BlockSpec((None, 1, width), lambda b: (b, 0, 0))
    const = lambda a: pl.BlockSpec(a.shape, lambda b: (0,) * a.ndim)
    st_spec = pl.BlockSpec((None, RET_PAIRS, 2 * RET_DK, RET_DV), lambda b: (b, 0, 0, 0))
    cache_in, cache_specs = [], []
    for gi in range(ATT_GROUPS):
        dil = ATT_DILATIONS[gi]
        for c in caches[2 * gi:2 * gi + 2]:
            cache_in.append(c.reshape(batch, ATT_NK, dil, ATT_HPG, ATT_HD))
            cache_specs.append(pl.BlockSpec((None, ATT_NK, None, ATT_HPG, ATT_HD), lambda b: (b, 0, 0, 0, 0)))
    st_pairs = state.reshape(batch, RET_PAIRS, 2 * RET_DK, RET_DV)
    slot_sel = _slot_select()
    ret, att, nst = pl.pallas_call(
        _decode_body,
        grid=(batch,),
        in_specs=[vspec(RET_QK_W), vspec(RET_QK_W), vspec(RET_V_W), vspec(RET_V_W), const(gn), st_spec, const(gam),
                  vspec(3 * ATT_W), const(rel_bias_t), const(slot_sel)] + cache_specs,
        out_specs=[vspec(RET_V_W), vspec(ATT_GW), st_spec],
        out_shape=[jax.ShapeDtypeStruct((batch, 1, RET_V_W), F32),
                   jax.ShapeDtypeStruct((batch, 1, ATT_GW), F32),
                   jax.ShapeDtypeStruct(st_pairs.shape, F32)],
        scratch_shapes=[pltpu.VMEM((rel_bias_t.shape[0], ATT_NK), F32)],
        compiler_params=_params(1),
        name="decode",
    )(vec(q), vec(k), vec(v), vec(g), gn, st_pairs, gam, vec(qkv), rel_bias_t, slot_sel, *cache_in)
    return ret.reshape(batch, RET_V_W), att.reshape(batch, ATT_GW), nst.reshape(state.shape)


def kernel(x_prompt, x_sample, state_ret, cache_k_w128, cache_v_w128, cache_k_w512, cache_v_w512,
           cache_k_w2048, cache_v_w2048, p_prompt, p_sample, ln1_g, w_in, ret_gn_g, w_ret_br, w_att_br,
           w_out, ln2_g, w_up, w_down, w_ple, w_ple_gate, rel_bias, lnf_g):
    depth = w_in.shape[0]
    assert depth == 1
    batch, seq, _ = x_prompt.shape
    dec_batch, dec_seq, _ = x_sample.shape
    assert dec_seq == 1
    past_len = 16384
    l = 0
    ln1 = ln1_g[l][None, :]
    ln2 = ln2_g[l][None, :]
    lnf = lnf_g[None, :]
    gn = ret_gn_g[l][None, :]
    w_in_lo = w_in[l].astype(BF16)
    tail_w = (w_ret_br[l].astype(BF16), w_att_br[l].astype(BF16), w_out[l].astype(BF16), w_up[l].astype(BF16),
              w_down[l].astype(BF16), w_ple[l].astype(BF16), w_ple_gate[l].astype(BF16), ln2, lnf)
    inv_row = _rope_inv_row()

    caches = (cache_k_w128[l], cache_v_w128[l], cache_k_w512[l], cache_v_w512[l],
              cache_k_w2048[l], cache_v_w2048[l])

    xs = x_sample.reshape(dec_batch, D_MODEL)
    sq, sk, sv, sg, sgr, sga = _inproj_ret(xs, ln1, w_in_lo, inv_row, dec_batch, F32, fixed_pos=past_len)
    s_qkv = _inproj_att_sample(xs, ln1, w_in_lo)
    rel_bias_t = rel_bias.T
    s_ret, s_attn, new_st = _decode(sq, sk, sv, sg, gn, state_ret[l], s_qkv, rel_bias_t, caches)
    ple_s = p_sample[l].reshape(dec_batch, D_PLE)
    y_s = _tail([s_ret, s_attn, sgr, sga, xs, ple_s], tail_w, dec_batch, False)
    y_sample = y_s.reshape(dec_batch, 1, D_MODEL)

    xp = x_prompt.reshape(batch * seq, D_MODEL)
    rq, rk, rv, rg, gr, ga = _inproj_ret(xp, ln1, w_in_lo, inv_row, TM_INPROJ, BF16, seq=seq)
    att_o = _inproj_att_prompt(xp, ln1, w_in_lo, TM_INPROJ, seq)
    aqs, aks, avs = att_o[0:3], att_o[3:6], att_o[6:9]
    kfull, vfull = att_o[9:12], att_o[12:15]
    ret_out, st_p = _retention_prompt(rq, rk, rv, rg, gn, batch, seq, TS_RETENTION)
    outs, lses = [], []
    band_onehot, band_neg = _band_select()
    for g in range(ATT_GROUPS):
        o, lse = _attention_prompt(aqs[g], aks[g], avs[g], rel_bias_t, band_onehot, band_neg, g)
        outs.append(o)
        lses.append(lse)
    ple_p = p_prompt[l].reshape(batch * seq, D_PLE)
    tm_tail = batch * seq // (2 * dec_batch)
    y_p, new_caches = _tail_shift([ret_out] + outs + lses + [gr, ga, xp, ple_p], tail_w, tm_tail, s_qkv, caches)
    y_prompt = y_p.reshape(batch, seq, D_MODEL)
    new_state_p = st_p[None]
    kv_p = []
    for g in range(ATT_GROUPS):
        shape = (1, batch, min(ATT_WINDOWS[g], seq), ATT_HPG, ATT_HD)
        kv_p.append(kfull[g].reshape(shape))
        kv_p.append(vfull[g].reshape(shape))
    kv_s = [c[None] for c in new_caches]

    return (y_prompt, y_sample, new_state_p, *kv_p, new_st[None], *kv_s)
```

```python
import functools
import math

import jax
import jax.numpy as jnp
import numpy as np
from jax import lax
from jax.experimental import pallas as pl
from jax.experimental.pallas import tpu as pltpu

F32 = jnp.float32
BF16 = jnp.bfloat16

D_MODEL = 1024
RET_HEADS = 8
RET_DK = 64
RET_DV = 128
RET_PAIRS = RET_HEADS // 2
RET_CHUNK = 128
ROPE_BASE = 10000.0
ATT_WINDOWS = (128, 512, 2048)
ATT_DILATIONS = (1, 4, 16)
ATT_GROUPS = 3
ATT_HPG = 4
ATT_HD = 128
ATT_NK = 128
ATT_GW = ATT_HPG * ATT_HD
REL_BUCKETS = 32
REL_MAX_DIST = 2048
D_FF = 4 * D_MODEL
D_PLE = 256
NORM_EPS = 1e-6
RET_QK_W = RET_HEADS * RET_DK
RET_V_W = RET_HEADS * RET_DV
ATT_W = ATT_GROUPS * ATT_GW
COL_RET = 0
COL_ATT = 2 * RET_QK_W + 2 * RET_V_W
COL_GATE = COL_ATT + 3 * ATT_W
N_IN = COL_GATE + 2 * D_MODEL

VMEM_LIMIT_V7X = 56 * 1024 * 1024
LANES = 128
TM_INPROJ = 512
TS_RETENTION = 512
ATTN_UNROLL = 4
PERM_ROWS = 256
SINGLE_BLOCK_GROUP = 4


def _dot(a, b):
    return jnp.dot(a, b, preferred_element_type=F32)


def _dot_nt(a, b):
    return lax.dot_general(a, b, (((1,), (1,)), ((), ())), preferred_element_type=F32)


def _dot_tn(a, b):
    return lax.dot_general(a, b, (((0,), (0,)), ((), ())), preferred_element_type=F32)


def _rms(x, g):
    return x * lax.rsqrt(jnp.mean(x * x, axis=-1, keepdims=True) + NORM_EPS) * g


def _sigmoid(x):
    return 1.0 / (1.0 + jnp.exp(-x))


def _resident(shape):
    return pl.BlockSpec(shape, lambda *_: (0,) * len(shape), pipeline_mode=pl.Buffered(1))


def _col_window(rows, width, block_index):
    return pl.BlockSpec((rows, width), lambda *_: (0, block_index), pipeline_mode=pl.Buffered(1))


def _params(n_axes):
    return pltpu.CompilerParams(dimension_semantics=("arbitrary",) * n_axes,
                                vmem_limit_bytes=VMEM_LIMIT_V7X)


def _rope_inv_row():
    half = RET_DK // 2
    inv = ROPE_BASE ** (-jnp.arange(half, dtype=F32) / half)
    return jnp.tile(inv, LANES // half)[None, :]


def _ret_log_decay():
    return np.log1p(-np.exp2(-5.0 - np.arange(RET_HEADS, dtype=np.float32))).astype(np.float32)


def _ret_tables():
    c = RET_CHUNK
    lg = _ret_log_decay()
    i = np.arange(c, dtype=np.float32)
    diff = i[:, None] - i[None, :]
    dmask = np.where(diff[None] >= 0, np.exp(np.maximum(diff, 0.0)[None] * lg[:, None, None]), 0.0)
    q_decay = np.exp((i + 1.0)[:, None] * lg[None, :])
    k_decay = np.exp((c - 1.0 - i)[:, None] * lg[None, :])
    qdec = np.broadcast_to(q_decay.T[:, :, None], (RET_HEADS, c, RET_DV))
    kdec = np.repeat(k_decay, RET_DK, axis=1).reshape(c, RET_PAIRS, 2 * RET_DK).transpose(1, 0, 2)
    gc = np.repeat(np.exp(c * lg), RET_DV).reshape(RET_PAIRS, 1, 2 * RET_DV)
    return tuple(jnp.asarray(np.ascontiguousarray(t), F32) for t in (dmask, qdec, kdec, gc))


def _rel_buckets():
    max_exact = REL_BUCKETS // 2
    out = []
    for dil in ATT_DILATIONS:
        d = np.arange(ATT_NK, dtype=np.int32) * dil
        log_ratio = (np.log(np.maximum(d, 1).astype(np.float32) / np.float32(max_exact))
                     / np.float32(math.log(REL_MAX_DIST / max_exact)))
        large = max_exact + (log_ratio * np.float32(REL_BUCKETS - max_exact)).astype(np.int32)
        out.append(np.where(d < max_exact, d, np.minimum(large, REL_BUCKETS - 1)))
    return np.stack(out)


def _band_select():
    nk = ATT_NK
    buckets = _rel_buckets()
    onehot = np.zeros((ATT_GROUPS, REL_BUCKETS, 3 * nk), np.float32)
    for g in range(ATT_GROUPS):
        for k in range(nk, 2 * nk):
            onehot[g, buckets[g, 2 * nk - 1 - k], k] = 1.0
    mask = np.full((1, 3 * nk), -np.inf, np.float32)
    mask[0, nk:2 * nk] = 0.0
    return jnp.asarray(onehot, BF16), jnp.asarray(mask, F32)


def _slot_select():
    buckets = _rel_buckets()
    onehot = np.zeros((ATT_GROUPS, REL_BUCKETS, ATT_NK), np.float32)
    for g in range(ATT_GROUPS):
        for slot in range(ATT_NK):
            onehot[g, buckets[g, 0 if slot == 0 else ATT_NK - slot], slot] = 1.0
    return jnp.asarray(onehot, BF16)


def _split3(x):
    hi = x.astype(BF16)
    rem = x - hi.astype(F32)
    mid = rem.astype(BF16)
    lo = (rem - mid.astype(F32)).astype(BF16)
    return hi, mid, lo


def _inproj_ret_body(x_ref, ln_ref, w_ref, wg0_ref, wg1_ref, wg2_ref, wg3_ref, inv_ref,
                     q_ref, k_ref, v_ref, g_ref, gr_ref, ga_ref, cos_s, sin_s,
                     *, tm, tiles, fixed_pos):
    i = pl.program_id(0)
    lane = lax.broadcasted_iota(jnp.int32, (tm, LANES), 1)
    first_half = (lane % RET_DK) < (RET_DK // 2)

    def tables(pos):
        ang = pos.astype(F32) * inv_ref[...]
        sin = jnp.sin(ang)
        return jnp.cos(ang), jnp.where(first_half, -sin, sin)

    if fixed_pos is None:
        trow = pl.ds(pl.multiple_of(lax.rem(i, tiles) * tm, tm), tm)

        @pl.when(i < tiles)
        def _():
            cos_s[trow, :], sin_s[trow, :] = tables(i * tm + lax.broadcasted_iota(jnp.int32, (tm, LANES), 0))
        cos = cos_s[trow, :]
        sin = sin_s[trow, :]
    else:
        cos, sin = tables(jnp.full((tm, LANES), fixed_pos, jnp.int32))
    x = x_ref[...]
    h = _rms(x, ln_ref[...]).astype(BF16)
    qk = _dot(h, w_ref[:, 0:2 * RET_QK_W])
    n_q = RET_QK_W // LANES
    for c in range(2 * n_q):
        xc = qk[:, c * LANES:(c + 1) * LANES]
        swapped = jnp.where(first_half, pltpu.roll(xc, LANES - RET_DK // 2, 1), pltpu.roll(xc, RET_DK // 2, 1))
        r = xc * cos + swapped * sin
        if c < n_q:
            q_ref[:, c * LANES:(c + 1) * LANES] = r.astype(q_ref.dtype)
        else:
            k_ref[:, (c - n_q) * LANES:(c - n_q + 1) * LANES] = (r * (RET_DK ** -0.5)).astype(k_ref.dtype)
    o = 2 * RET_QK_W
    v_ref[...] = _dot(h, w_ref[:, o:o + RET_V_W]).astype(v_ref.dtype)
    o += RET_V_W
    g_ref[...] = _dot(h, w_ref[:, o:o + RET_V_W]).astype(g_ref.dtype)
    half = D_MODEL // 2
    gr_ref[:, :half] = _dot(h, wg0_ref[...]).astype(gr_ref.dtype)
    gr_ref[:, half:] = _dot(h, wg1_ref[...]).astype(gr_ref.dtype)
    ga_ref[:, :half] = _dot(h, wg2_ref[...]).astype(ga_ref.dtype)
    ga_ref[:, half:] = _dot(h, wg3_ref[...]).astype(ga_ref.dtype)


def _inproj_ret(x2d, ln, w_in, inv_row, tm, out_dtype, seq=None, fixed_pos=None):
    n = x2d.shape[0]
    tiles = None if seq is None else seq // tm
    table_rows = tm if seq is None else seq
    row = lambda width: pl.BlockSpec((tm, width), lambda i: (i, 0))
    widths = (RET_QK_W, RET_QK_W, RET_V_W, RET_V_W, D_MODEL, D_MODEL)
    half = D_MODEL // 2
    assert COL_GATE % half == 0
    return pl.pallas_call(
        functools.partial(_inproj_ret_body, tm=tm, tiles=tiles, fixed_pos=fixed_pos),
        grid=(n // tm,),
        in_specs=[row(D_MODEL), _resident((1, D_MODEL)), _col_window(D_MODEL, COL_ATT, 0)]
        + [_col_window(D_MODEL, half, COL_GATE // half + j) for j in range(4)] + [_resident((1, LANES))],
        out_specs=[row(wd) for wd in widths],
        out_shape=[jax.ShapeDtypeStruct((n, wd), out_dtype) for wd in widths],
        scratch_shapes=[pltpu.VMEM((table_rows, LANES), F32), pltpu.VMEM((table_rows, LANES), F32)],
        compiler_params=_params(1),
        name="inproj_ret",
    )(x2d, ln, w_in, w_in, w_in, w_in, w_in, inv_row)


def _inproj_att_body(x_ref, ln_ref, wq_ref, wk_ref, wv_ref, *refs, tm, keeps, seq):
    lowp = refs[:3 * ATT_GROUPS]
    full = refs[3 * ATT_GROUPS:5 * ATT_GROUPS]
    res = refs[5 * ATT_GROUPS]
    x = x_ref[...]
    h = _rms(x, ln_ref[...]).astype(BF16)
    slot = 0
    for g in reversed(range(ATT_GROUPS)):
        for kind in reversed(range(3)):
            dil = ATT_DILATIONS[g]
            dst = lowp[kind * ATT_GROUPS + g]
            r = _dot(h, (wq_ref, wk_ref, wv_ref)[kind][:, g * ATT_GW:(g + 1) * ATT_GW])
            if dil == 1:
                dst[0] = r.astype(dst.dtype)
            else:
                for hh in range(ATT_HPG):
                    res[slot, hh] = r[:, hh * ATT_HD:(hh + 1) * ATT_HD]
                for rr in range(dil):
                    for hh in range(ATT_HPG):
                        dst[rr, :, hh * ATT_HD:(hh + 1) * ATT_HD] = (
                            res[slot, hh, pl.ds(rr, tm // dil, stride=dil), :].astype(dst.dtype))
                slot += 1
            if kind > 0:
                cache = full[(kind - 1) * ATT_GROUPS + g]
                rows = tm if keeps[g] == seq else keeps[g]
                for hh in range(ATT_HPG):
                    cache[pl.ds(hh, rows, stride=ATT_HPG), :] = r[tm - rows:, hh * ATT_HD:(hh + 1) * ATT_HD]


def _inproj_att_prompt(x2d, ln, w_in, tm, seq):
    n = x2d.shape[0]
    batch = n // seq
    tiles = seq // tm
    keeps = tuple(min(wd, seq) for wd in ATT_WINDOWS)
    assert all(kp <= tm or kp == seq for kp in keeps) and seq % tm == 0
    out_specs, out_shape = [], []
    for _ in range(3):
        for g in range(ATT_GROUPS):
            dil = ATT_DILATIONS[g]
            out_specs.append(pl.BlockSpec((None, dil, tm // dil, ATT_GW), lambda i: (i // tiles, 0, i % tiles, 0)))
            out_shape.append(jax.ShapeDtypeStruct((batch, dil, seq // dil, ATT_GW), BF16))
    for _ in range(2):
        for g in range(ATT_GROUPS):
            if keeps[g] == seq:
                idx = lambda i: (i // tiles, i % tiles, 0)
                rows = tm
            else:
                idx = lambda i: (i // tiles, 0, 0)
                rows = keeps[g]
            out_specs.append(pl.BlockSpec((None, rows * ATT_HPG, ATT_HD), idx))
            out_shape.append(jax.ShapeDtypeStruct((batch, keeps[g] * ATT_HPG, ATT_HD), F32))
    return pl.pallas_call(
        functools.partial(_inproj_att_body, tm=tm, keeps=keeps, seq=seq),
        grid=(n // tm,),
        in_specs=[pl.BlockSpec((tm, D_MODEL), lambda i: (i, 0)), _resident((1, D_MODEL))]
        + [_col_window(D_MODEL, ATT_W, COL_ATT // ATT_W + kind) for kind in range(3)],
        out_specs=out_specs,
        out_shape=out_shape,
        scratch_shapes=[pltpu.VMEM((3 * sum(d > 1 for d in ATT_DILATIONS), ATT_HPG, tm, ATT_HD), F32)],
        compiler_params=_params(1),
        name="inproj_att",
    )(x2d, ln, w_in, w_in, w_in)


def _inproj_att_sample_body(x_ref, ln_ref, wq_ref, wk_ref, wv_ref, o_ref):
    h = _rms(x_ref[...], ln_ref[...]).astype(BF16)
    for kind, w_ref in enumerate((wq_ref, wk_ref, wv_ref)):
        for g in range(ATT_GROUPS):
            c = kind * ATT_GROUPS + g
            o_ref[:, c * ATT_GW:(c + 1) * ATT_GW] = _dot(h, w_ref[:, g * ATT_GW:(g + 1) * ATT_GW])


def _inproj_att_sample(x2d, ln, w_in):
    n = x2d.shape[0]
    return pl.pallas_call(
        _inproj_att_sample_body,
        grid=(1,),
        in_specs=[pl.BlockSpec((n, D_MODEL), lambda i: (0, 0)), _resident((1, D_MODEL))]
        + [_col_window(D_MODEL, ATT_W, COL_ATT // ATT_W + kind) for kind in range(3)],
        out_specs=pl.BlockSpec((n, 3 * ATT_W), lambda i: (0, 0)),
        out_shape=jax.ShapeDtypeStruct((n, 3 * ATT_W), F32),
        compiler_params=_params(1),
        name="inproj_att_sample",
    )(x2d, ln, w_in, w_in, w_in)


def _gn_swish(o, gate, gn):
    mu = jnp.mean(o, axis=-1, keepdims=True)
    d = o - mu
    var = jnp.mean(d * d, axis=-1, keepdims=True)
    on = d * lax.rsqrt(var + NORM_EPS) * gn
    return gate * _sigmoid(gate) * on


def _retention_body(q_ref, k_ref, v_ref, g_ref, gn_ref, dm_ref, qdec_ref, kdec_ref, gc_ref,
                    out_ref, st_ref, state, *, n_chunks):
    c = RET_CHUNK

    @pl.when(pl.program_id(1) == 0)
    def _():
        state[...] = jnp.zeros_like(state)

    lane = lax.broadcasted_iota(jnp.int32, (c, 2 * RET_DK), 1)
    head0 = lane < RET_DK

    def chunk(ci, carry):
        rows = pl.ds(pl.multiple_of(ci * c, c), c)
        scores, cross, values = {}, {}, {}
        for p in range(RET_PAIRS):
            q2 = q_ref[rows, p * 2 * RET_DK:(p + 1) * 2 * RET_DK]
            k2 = k_ref[rows, p * 2 * RET_DK:(p + 1) * 2 * RET_DK]
            v2 = v_ref[rows, p * 2 * RET_DV:(p + 1) * 2 * RET_DV]
            pst = state[p]
            pst_lo = pst.astype(BF16)
            zero = jnp.zeros_like(q2)
            for hh in range(2):
                h = 2 * p + hh
                qm = jnp.where(head0 if hh == 0 else jnp.logical_not(head0), q2, zero)
                values[h] = v2[:, hh * RET_DV:(hh + 1) * RET_DV]
                scores[h] = _dot_nt(qm, k2)
                cross[h] = _dot(qm, pst_lo[:, hh * RET_DV:(hh + 1) * RET_DV])
            kd = (k2.astype(F32) * kdec_ref[p]).astype(BF16)
            state[p] = pst * gc_ref[p] + _dot_tn(kd, v2)
        for h in range(RET_HEADS):
            o = _dot((scores[h] * dm_ref[h]).astype(BF16), values[h]) + cross[h] * qdec_ref[h]
            gate = g_ref[rows, h * RET_DV:(h + 1) * RET_DV].astype(F32)
            res = _gn_swish(o, gate, gn_ref[:, h * RET_DV:(h + 1) * RET_DV])
            out_ref[rows, h * RET_DV:(h + 1) * RET_DV] = res.astype(out_ref.dtype)
        return carry

    lax.fori_loop(0, n_chunks, chunk, 0, unroll=True)
    for p in range(RET_PAIRS):
        pst = state[p]
        for hh in range(2):
            st_ref[2 * p + hh] = pst[hh * RET_DK:(hh + 1) * RET_DK, hh * RET_DV:(hh + 1) * RET_DV]


def _retention_prompt(q, k, v, g, gn, batch, seq, ts):
    n = q.shape[0]
    steps = seq // ts
    dmask, qdec, kdec, gc = _ret_tables()
    row = lambda width: pl.BlockSpec((ts, width), lambda b, s: (b * steps + s, 0))
    const = lambda a: pl.BlockSpec(a.shape, lambda b, s: (0,) * a.ndim)
    return pl.pallas_call(
        functools.partial(_retention_body, n_chunks=ts // RET_CHUNK),
        grid=(batch, steps),
        in_specs=[row(RET_QK_W), row(RET_QK_W), row(RET_V_W), row(RET_V_W), const(gn),
                  const(dmask), const(qdec), const(kdec), const(gc)],
        out_specs=[row(RET_V_W),
                   pl.BlockSpec((None, RET_HEADS, RET_DK, RET_DV), lambda b, s: (b, 0, 0, 0))],
        out_shape=[jax.ShapeDtypeStruct((n, RET_V_W), BF16),
                   jax.ShapeDtypeStruct((batch, RET_HEADS, RET_DK, RET_DV), F32)],
        scratch_shapes=[pltpu.VMEM((RET_PAIRS, 2 * RET_DK, 2 * RET_DV), F32)],
        compiler_params=_params(2),
        name="retention",
    )(q, k, v, g, gn, dmask, qdec, kdec, gc)


def _attn_body(q_ref, k_ref, v_ref, rb_ref, sel_ref, neg_ref, o_ref, lse_ref, *scratch, group, dil, n_blocks):
    nk = ATT_NK
    scale = ATT_HD ** -0.5
    lane = lax.broadcasted_iota(jnp.int32, (nk, LANES), 1)
    tabs = scratch[-1]

    @pl.when(pl.program_id(0) == 0)
    def _():
        band = neg_ref[...]
        for piece in _split3(rb_ref[group * ATT_HPG:(group + 1) * ATT_HPG, :]):
            band = band + _dot(piece, sel_ref[...])
        col = lax.broadcasted_iota(jnp.int32, (1, 2 * nk), 1)
        for i in range(nk):
            window = band[:, nk - 1 - i:3 * nk - 1 - i]
            for h in range(ATT_HPG):
                tabs[1, h, i:i + 1, :] = window[h:h + 1, :]
                tabs[0, h, i:i + 1, :] = jnp.where(col < nk, -jnp.inf, window[h:h + 1, :])
    if dil > 1:
        o_s, l_s = scratch[:2]

    def chain(s_raw, v_ext, sel, h):
        s = s_raw * scale + (tabs[sel, h] if n_blocks > 1 else tabs[1, h, :, nk:])
        m = jnp.max(s, axis=-1, keepdims=True)
        p = jnp.exp(s - m)
        pv = _dot(p.astype(BF16), v_ext)
        den = pv[:, ATT_HD:]
        return (pv[:, :ATT_HD] / den).astype(BF16), m + jnp.log(den)

    def write(rs, rows, outs):
        lse_tile = jnp.zeros((nk, LANES), F32)
        for h, (o, lse) in enumerate(outs):
            hc = slice(h * ATT_HD, (h + 1) * ATT_HD)
            if dil == 1:
                o_ref[rows, hc] = o
            else:
                o_s[rs, rows, hc] = o
            lse_tile = jnp.where(lane == h, lse, lse_tile)
        if dil == 1:
            lse_ref[rows, :] = lse_tile
        else:
            l_s[rs, rows, :] = lse_tile

    heads = [slice(h * ATT_HD, (h + 1) * ATT_HD) for h in range(ATT_HPG)]

    def block(rs, n, n_prev, sel):
        rows = pl.ds(pl.multiple_of(n * nk, nk), nk)
        prev_rows = pl.ds(pl.multiple_of(n_prev * nk, nk), nk)
        q = q_ref[rs, rows, :]
        k_all = jnp.concatenate([k_ref[rs, prev_rows, :], k_ref[rs, rows, :]], axis=0)
        v_all = jnp.concatenate([v_ref[rs, prev_rows, :], v_ref[rs, rows, :]], axis=0)
        ones = jnp.ones((2 * nk, ATT_HD), BF16)
        outs = []
        for h, hc in enumerate(heads):
            v_ext = jnp.concatenate([v_all[:, hc], ones], axis=1)
            outs.append(chain(_dot_nt(q[:, hc], k_all[:, hc]), v_ext, sel, h))
        write(rs, rows, outs)

    def single_blocks(streams):
        ones = jnp.ones((nk, ATT_HD), BF16)
        scores = [[_dot_nt(q_ref[rs, :, hc], k_ref[rs, :, hc]) for hc in heads] for rs in streams]
        for i, rs in enumerate(streams):
            outs = [chain(scores[i][h], jnp.concatenate([v_ref[rs, :, hc], ones], axis=1), 0, h)
                    for h, hc in enumerate(heads)]
            write(rs, slice(None), outs)

    if n_blocks == 1:
        for r0 in range(0, dil, SINGLE_BLOCK_GROUP):
            single_blocks(range(r0, min(r0 + SINGLE_BLOCK_GROUP, dil)))
    else:
        for rs in range(dil):
            def loop(n, carry, rs=rs):
                block(rs, n, jnp.maximum(n - 1, 0), jnp.minimum(n, 1))
                return carry
            lax.fori_loop(0, n_blocks, loop, 0, unroll=ATTN_UNROLL)

    if dil > 1:
        piece = PERM_ROWS // dil
        row = lax.broadcasted_iota(jnp.int32, (PERM_ROWS, PERM_ROWS), 0)
        col = lax.broadcasted_iota(jnp.int32, (PERM_ROWS, PERM_ROWS), 1)
        perm = jnp.where((col % piece) * dil + col // piece == row, 1.0, 0.0).astype(BF16)
        for j in range(o_ref.shape[0] // PERM_ROWS):
            src = slice(j * piece, (j + 1) * piece)
            dst = slice(j * PERM_ROWS, (j + 1) * PERM_ROWS)
            o_ref[dst, :] = _dot(perm, jnp.concatenate([o_s[r, src, :] for r in range(dil)], axis=0)).astype(BF16)
            lse = jnp.concatenate([l_s[r, src, :] for r in range(dil)], axis=0)
            hi = lse.astype(BF16)
            rem = lse - hi.astype(F32)
            mid = rem.astype(BF16)
            lo = (rem - mid.astype(F32)).astype(BF16)
            lse_ref[dst, :] = _dot(perm, hi) + _dot(perm, mid) + _dot(perm, lo)


def _attention_prompt(aq, ak, av, rel_bias_t, onehot, neg, g):
    batch, dil, length, _ = aq.shape
    seq = dil * length
    n_blocks = length // ATT_NK
    blk = pl.BlockSpec((None, dil, length, ATT_GW), lambda b: (b, 0, 0, 0))
    scratch = [pltpu.VMEM((dil, length, ATT_GW), BF16), pltpu.VMEM((dil, length, LANES), F32)] if dil > 1 else []
    scratch.append(pltpu.VMEM((2, ATT_HPG, ATT_NK, 2 * ATT_NK), F32))
    o, lse = pl.pallas_call(
        functools.partial(_attn_body, group=g, dil=dil, n_blocks=n_blocks),
        grid=(batch,),
        in_specs=[blk, blk, blk, pl.BlockSpec(rel_bias_t.shape, lambda b: (0, 0)),
                  pl.BlockSpec((None, REL_BUCKETS, 3 * ATT_NK), lambda b: (g, 0, 0)),
                  pl.BlockSpec((1, 3 * ATT_NK), lambda b: (0, 0))],
        out_specs=[pl.BlockSpec((None, seq, ATT_GW), lambda b: (b, 0, 0)),
                   pl.BlockSpec((None, seq, LANES), lambda b: (b, 0, 0))],
        out_shape=[jax.ShapeDtypeStruct((batch, seq, ATT_GW), BF16),
                   jax.ShapeDtypeStruct((batch, seq, LANES), F32)],
        scratch_shapes=scratch,
        compiler_params=_params(1),
        name="attention_g%d" % g,
    )(aq, ak, av, rel_bias_t, onehot, neg)
    return o.reshape(batch * seq, ATT_GW), lse.reshape(batch * seq, LANES)


def _tail_math(act_refs, w_refs, y_ref, combine):
    wrb_ref, wab_ref, wo_ref, wu_ref, wd_ref, wpl_ref, wpg_ref, ln2_ref, lnf_ref = w_refs
    if combine:
        ret_ref, o0_ref, o1_ref, o2_ref, l0_ref, l1_ref, l2_ref, gr_ref, ga_ref, x_ref, ple_ref = act_refs
        lses = [l0_ref[...], l1_ref[...], l2_ref[...]]
        outs = [o0_ref, o1_ref, o2_ref]
        parts = []
        for h in range(ATT_HPG):
            lh = [l[:, h:h + 1] for l in lses]
            mx = jnp.maximum(jnp.maximum(lh[0], lh[1]), lh[2])
            e = [jnp.exp(l - mx) for l in lh]
            tot = e[0] + e[1] + e[2]
            acc = None
            for g in range(ATT_GROUPS):
                term = (e[g] / tot) * outs[g][:, h * ATT_HD:(h + 1) * ATT_HD].astype(F32)
                acc = term if acc is None else acc + term
            parts.append(acc)
        att = jnp.concatenate(parts, axis=1).astype(BF16)
    else:
        ret_ref, att_ref, gr_ref, ga_ref, x_ref, ple_ref = act_refs
        att = att_ref[...].astype(BF16)
    a = _dot(ret_ref[...].astype(BF16), wrb_ref[...])
    emb = _dot(ple_ref[...].astype(BF16), wpl_ref[...])
    b = _dot(att, wab_ref[...])
    mixed = _sigmoid(gr_ref[...].astype(F32)) * a + _sigmoid(ga_ref[...].astype(F32)) * b
    x1 = x_ref[...] + _dot(mixed.astype(BF16), wo_ref[...])
    h2 = _rms(x1, ln2_ref[...]).astype(BF16)
    ff_chunk = D_MODEL
    n_ff = D_FF // ff_chunk
    up = lambda c: _dot(h2, wu_ref[:, c * ff_chunk:(c + 1) * ff_chunk])
    acc = None
    u = up(0)
    for c in range(n_ff):
        u_next = up(c + 1) if c + 1 < n_ff else None
        r = jnp.maximum(u, 0.0)
        t = _dot((r * r).astype(BF16), wd_ref[c * ff_chunk:(c + 1) * ff_chunk, :])
        acc = t if acc is None else acc + t
        u = u_next
    x2 = x1 + acc
    gate = _sigmoid(_dot(x2.astype(BF16), wpg_ref[...]))
    x3 = x2 + gate * emb
    y_ref[...] = _rms(x3, lnf_ref[...])


def _tail_body(*refs, n_act, combine):
    _tail_math(refs[:n_act], refs[n_act:-1], refs[-1], combine)


def _tail(acts, weights, tm, combine):
    n = acts[0].shape[0]
    row = lambda a: pl.BlockSpec((tm, a.shape[1]), lambda i: (i, 0))
    return pl.pallas_call(
        functools.partial(_tail_body, n_act=len(acts), combine=combine),
        grid=(n // tm,),
        in_specs=[row(a) for a in acts] + [_resident(w.shape) for w in weights],
        out_specs=pl.BlockSpec((tm, D_MODEL), lambda i: (i, 0)),
        out_shape=jax.ShapeDtypeStruct((n, D_MODEL), F32),
        compiler_params=_params(1),
        name="tail" if combine else "tail_sample",
    )(*acts, *weights)


def _tail_shift_body(*refs, n_act, n_w, n_steps):
    nc = 2 * ATT_GROUPS
    acts = refs[:n_act]
    ws = refs[n_act:n_act + n_w]
    new_ref = refs[n_act + n_w]
    base = n_act + n_w + 1
    old = refs[base:base + nc]
    y_ref = refs[base + nc]
    out = refs[base + nc + 1:base + 2 * nc + 1]
    stage = refs[base + 2 * nc + 1:base + 3 * nc + 1]
    sem_in, sem_out, sem_row = refs[base + 3 * nc + 1:]
    s = pl.program_id(0)
    b = lax.shift_right_logical(s, 1)
    parity = lax.rem(s, 2)

    def copy_in(i, bi):
        width = old[i].shape[1]
        return pltpu.make_async_copy(old[i].at[bi, pl.ds(1, width - 1)], stage[i], sem_in.at[i])

    def copy_out(i, bi):
        width = old[i].shape[1]
        return pltpu.make_async_copy(stage[i], out[i].at[bi, pl.ds(0, width - 1)], sem_out.at[i])

    def copy_row(g, kind):
        i = 2 * g + kind
        return pltpu.make_async_copy(new_ref.at[b, kind + 1, g], out[i].at[b, old[i].shape[1] - 1], sem_row.at[g])

    for kind in range(2):
        mine = [2 * g + kind for g in range(ATT_GROUPS)]
        other = [2 * g + 1 - kind for g in range(ATT_GROUPS)]

        @pl.when(parity == kind)
        def _(kind=kind, mine=mine, other=other):
            if kind == 0:
                @pl.when(s == 0)
                def _():
                    for i in mine:
                        copy_in(i, 0).start()
            for i in mine:
                copy_in(i, b).wait()
            for i in mine:
                copy_out(i, b).start()
            if kind == 0:
                @pl.when(s > 0)
                def _():
                    for i in other:
                        copy_out(i, b - 1).wait()
                for i in other:
                    copy_in(i, b).start()
            else:
                for i in other:
                    copy_out(i, b).wait()

                @pl.when(s < n_steps - 1)
                def _():
                    for i in other:
                        copy_in(i, b + 1).start()
            for g in range(ATT_GROUPS):
                copy_row(g, kind).start()

    _tail_math(acts, ws, y_ref, True)

    for kind in range(2):
        @pl.when(parity == kind)
        def _(kind=kind):
            for g in range(ATT_GROUPS):
                copy_row(g, kind).wait()

    @pl.when(s == n_steps - 1)
    def _():
        for g in range(ATT_GROUPS):
            copy_out(2 * g + 1, b).wait()


def _tail_shift(acts, weights, tm, new_qkv, caches):
    n = acts[0].shape[0]
    n_steps = n // tm
    assert n_steps == 2 * new_qkv.shape[0]
    row = lambda a: pl.BlockSpec((tm, a.shape[1]), lambda i: (i, 0))
    anyspec = pl.BlockSpec(memory_space=pl.ANY)
    nc = len(caches)
    new_rows = new_qkv.reshape(new_qkv.shape[0], 3, ATT_GROUPS, ATT_HPG, ATT_HD)
    res = pl.pallas_call(
        functools.partial(_tail_shift_body, n_act=len(acts), n_w=len(weights), n_steps=n_steps),
        grid=(n_steps,),
        in_specs=[row(a) for a in acts] + [_resident(w.shape) for w in weights] + [anyspec] * (nc + 1),
        out_specs=[pl.BlockSpec((tm, D_MODEL), lambda i: (i, 0))] + [anyspec] * nc,
        out_shape=[jax.ShapeDtypeStruct((n, D_MODEL), F32)]
        + [jax.ShapeDtypeStruct(c.shape, c.dtype) for c in caches],
        scratch_shapes=[pltpu.VMEM((c.shape[1] - 1, ATT_HPG, ATT_HD), F32) for c in caches]
        + [pltpu.SemaphoreType.DMA((nc,)), pltpu.SemaphoreType.DMA((nc,)),
           pltpu.SemaphoreType.DMA((ATT_GROUPS,))],
        compiler_params=_params(1),
        name="tail",
    )(*acts, *weights, new_rows, *caches)
    return res[0], res[1:]


def _decode_body(q_ref, k_ref, v_ref, g_ref, gn_ref, st_ref, gam_ref,
                 qkv_ref, rb_ref, slot_ref,
                 ck0_ref, cv0_ref, ck1_ref, cv1_ref, ck2_ref, cv2_ref,
                 ret_ref, att_ref, nst_ref, slot_bias):
    sub = lax.broadcasted_iota(jnp.int32, (8, 2 * RET_DK), 0)
    lane = lax.broadcasted_iota(jnp.int32, (8, 2 * RET_DK), 1)
    row0 = sub == 0
    srow = lax.broadcasted_iota(jnp.int32, (2 * RET_DK, RET_DV), 0)
    for p in range(RET_PAIRS):
        pc = slice(p * 2 * RET_DK, (p + 1) * 2 * RET_DK)
        q2 = jnp.where(row0, jnp.broadcast_to(q_ref[:, pc], (8, 2 * RET_DK)), 0.0)
        k2 = jnp.where(row0, jnp.broadcast_to(k_ref[:, pc], (8, 2 * RET_DK)), 0.0)
        pst = st_ref[p]
        gam = gam_ref[p]
        outer = []
        for hh in range(2):
            h = 2 * p + hh
            hsel = (lane < RET_DK) if hh == 0 else (lane >= RET_DK)
            qm = jnp.where(hsel, q2, 0.0)
            km = jnp.where(hsel, k2, 0.0)
            vh = v_ref[:, h * RET_DV:(h + 1) * RET_DV]
            v8 = jnp.where(row0[:, :RET_DV], jnp.broadcast_to(vh, (8, RET_DV)), 0.0)
            cross = _dot(qm.astype(BF16), (pst * gam).astype(BF16))[0:1, :]
            qk = jnp.sum(qm[0:1, :] * km[0:1, :], axis=-1, keepdims=True)
            o = cross + qk * vh
            res = _gn_swish(o, g_ref[:, h * RET_DV:(h + 1) * RET_DV], gn_ref[:, h * RET_DV:(h + 1) * RET_DV])
            ret_ref[:, h * RET_DV:(h + 1) * RET_DV] = res
            outer.append(_dot_tn(k2.astype(BF16), v8.astype(BF16)))
        nst_ref[p] = pst * gam + jnp.where(srow < RET_DK, outer[0], outer[1])
    scale = ATT_HD ** -0.5
    caches = ((ck0_ref, cv0_ref), (ck1_ref, cv1_ref), (ck2_ref, cv2_ref))
    slot = lax.broadcasted_iota(jnp.int32, (ATT_NK, ATT_HD), 0)
    is_new = slot == 0
    first = row0[:, :ATT_HD]

    @pl.when(pl.program_id(0) == 0)
    def _():
        pieces = _split3(rb_ref[...])
        for g in range(ATT_GROUPS):
            rows = slice(g * ATT_HPG, (g + 1) * ATT_HPG)
            slot_bias[rows, :] = (_dot(pieces[0][rows], slot_ref[g]) + _dot(pieces[1][rows], slot_ref[g])
                                  + _dot(pieces[2][rows], slot_ref[g]))

    def one_row(x):
        return jnp.where(first, jnp.broadcast_to(x, (8, ATT_HD)), 0.0).astype(BF16)

    scores = {}
    for h in range(ATT_HPG):
        for g in range(ATT_GROUPS):
            c0 = g * ATT_GW + h * ATT_HD
            kk = jnp.where(is_new, qkv_ref[:, ATT_W + c0:ATT_W + c0 + ATT_HD], caches[g][0][:, h, :])
            scores[g, h] = _dot_nt(one_row(qkv_ref[:, c0:c0 + ATT_HD]), kk.astype(BF16))[0:1, :]
    for h in range(ATT_HPG):
        o_g, lse_g = [], []
        for g in range(ATT_GROUPS):
            c0 = g * ATT_GW + h * ATT_HD
            vv = jnp.where(is_new, qkv_ref[:, 2 * ATT_W + c0:2 * ATT_W + c0 + ATT_HD], caches[g][1][:, h, :])
            gh = g * ATT_HPG + h
            s = scores[g, h] * scale + slot_bias[gh:gh + 1, :]
            m = jnp.max(s, axis=-1, keepdims=True)
            pr = jnp.exp(s - m)
            den = jnp.sum(pr, axis=-1, keepdims=True)
            o_g.append(_dot(one_row(pr), vv.astype(BF16))[0:1, :] / den)
            lse_g.append(m + jnp.log(den))
        mx = jnp.maximum(jnp.maximum(lse_g[0], lse_g[1]), lse_g[2])
        e = [jnp.exp(l - mx) for l in lse_g]
        tot = e[0] + e[1] + e[2]
        att_ref[:, h * ATT_HD:(h + 1) * ATT_HD] = ((e[0] / tot) * o_g[0] + (e[1] / tot) * o_g[1]
                                                   + (e[2] / tot) * o_g[2])


def _decode(q, k, v, g, gn, state, qkv, rel_bias_t, caches):
    batch = q.shape[0]
    gam = jnp.asarray(np.ascontiguousarray(np.broadcast_to(
        np.repeat(np.exp(_ret_log_decay()), RET_DK).reshape(RET_PAIRS, 2 * RET_DK, 1),
        (RET_PAIRS, 2 * RET_DK, RET_DV))), F32)
    vec = lambda a: a.reshape(batch, 1, a.shape[1])
    vspec = lambda width: pl.BlockSpec((None, 1, width), lambda b: (b, 0, 0))
    const = lambda a: pl.BlockSpec(a.shape, lambda b: (0,) * a.ndim)
    st_spec = pl.BlockSpec((None, RET_PAIRS, 2 * RET_DK, RET_DV), lambda b: (b, 0, 0, 0))
    cache_in, cache_specs = [], []
    for gi in range(ATT_GROUPS):
        dil = ATT_DILATIONS[gi]
        for c in caches[2 * gi:2 * gi + 2]:
            cache_in.append(c.reshape(batch, ATT_NK, dil, ATT_HPG, ATT_HD))
            cache_specs.append(pl.BlockSpec((None, ATT_NK, None, ATT_HPG, ATT_HD), lambda b: (b, 0, 0, 0, 0)))
    st_pairs = state.reshape(batch, RET_PAIRS, 2 * RET_DK, RET_DV)
    slot_sel = _slot_select()
    ret, att, nst = pl.pallas_call(
        _decode_body,
        grid=(batch,),
        in_specs=[vspec(RET_QK_W), vspec(RET_QK_W), vspec(RET_V_W), vspec(RET_V_W), const(gn), st_spec, const(gam),
                  vspec(3 * ATT_W), const(rel_bias_t), const(slot_sel)] + cache_specs,
        out_specs=[vspec(RET_V_W), vspec(ATT_GW), st_spec],
        out_shape=[jax.ShapeDtypeStruct((batch, 1, RET_V_W), F32),
                   jax.ShapeDtypeStruct((batch, 1, ATT_GW), F32),
                   jax.ShapeDtypeStruct(st_pairs.shape, F32)],
        scratch_shapes=[pltpu.VMEM((rel_bias_t.shape[0], ATT_NK), F32)],
        compiler_params=_params(1),
        name="decode",
    )(vec(q), vec(k), vec(v), vec(g), gn, st_pairs, gam, vec(qkv), rel_bias_t, slot_sel, *cache_in)
    return ret.reshape(batch, RET_V_W), att.reshape(batch, ATT_GW), nst.reshape(state.shape)


def kernel(x_prompt, x_sample, state_ret, cache_k_w128, cache_v_w128, cache_k_w512, cache_v_w512,
           cache_k_w2048, cache_v_w2048, p_prompt, p_sample, ln1_g, w_in, ret_gn_g, w_ret_br, w_att_br,
           w_out, ln2_g, w_up, w_down, w_ple, w_ple_gate, rel_bias, lnf_g):
    depth = w_in.shape[0]
    assert depth == 1
    batch, seq, _ = x_prompt.shape
    dec_batch, dec_seq, _ = x_sample.shape
    assert dec_seq == 1
    past_len = 16384
    l = 0
    ln1 = ln1_g[l][None, :]
    ln2 = ln2_g[l][None, :]
    lnf = lnf_g[None, :]
    gn = ret_gn_g[l][None, :]
    w_in_lo = w_in[l].astype(BF16)
    tail_w = (w_ret_br[l].astype(BF16), w_att_br[l].astype(BF16), w_out[l].astype(BF16), w_up[l].astype(BF16),
              w_down[l].astype(BF16), w_ple[l].astype(BF16), w_ple_gate[l].astype(BF16), ln2, lnf)
    inv_row = _rope_inv_row()

    caches = (cache_k_w128[l], cache_v_w128[l], cache_k_w512[l], cache_v_w512[l],
              cache_k_w2048[l], cache_v_w2048[l])

    xs = x_sample.reshape(dec_batch, D_MODEL)
    sq, sk, sv, sg, sgr, sga = _inproj_ret(xs, ln1, w_in_lo, inv_row, dec_batch, F32, fixed_pos=past_len)
    s_qkv = _inproj_att_sample(xs, ln1, w_in_lo)
    rel_bias_t = rel_bias.T
    s_ret, s_attn, new_st = _decode(sq, sk, sv, sg, gn, state_ret[l], s_qkv, rel_bias_t, caches)
    ple_s = p_sample[l].reshape(dec_batch, D_PLE)
    y_s = _tail([s_ret, s_attn, sgr, sga, xs, ple_s], tail_w, dec_batch, False)
    y_sample = y_s.reshape(dec_batch, 1, D_MODEL)

    xp = x_prompt.reshape(batch * seq, D_MODEL)
    rq, rk, rv, rg, gr, ga = _inproj_ret(xp, ln1, w_in_lo, inv_row, TM_INPROJ, BF16, seq=seq)
    att_o = _inproj_att_prompt(xp, ln1, w_in_lo, TM_INPROJ, seq)
    aqs, aks, avs = att_o[0:3], att_o[3:6], att_o[6:9]
    kfull, vfull = att_o[9:12], att_o[12:15]
    ret_out, st_p = _retention_prompt(rq, rk, rv, rg, gn, batch, seq, TS_RETENTION)
    outs, lses = [], []
    band_onehot, band_neg = _band_select()
    for g in range(ATT_GROUPS):
        o, lse = _attention_prompt(aqs[g], aks[g], avs[g], rel_bias_t, band_onehot, band_neg, g)
        outs.append(o)
        lses.append(lse)
    ple_p = p_prompt[l].reshape(batch * seq, D_PLE)
    tm_tail = batch * seq // (2 * dec_batch)
    y_p, new_caches = _tail_shift([ret_out] + outs + lses + [gr, ga, xp, ple_p], tail_w, tm_tail, s_qkv, caches)
    y_prompt = y_p.reshape(batch, seq, D_MODEL)
    new_state_p = st_p[None]
    kv_p = []
    for g in range(ATT_GROUPS):
        shape = (1, batch, min(ATT_WINDOWS[g], seq), ATT_HPG, ATT_HD)
        kv_p.append(kfull[g].reshape(shape))
        kv_p.append(vfull[g].reshape(shape))
    kv_s = [c[None] for c in new_caches]

    return (y_prompt, y_sample, new_state_p, *kv_p, new_st[None], *kv_s)
```

```python
import functools
import math

import jax
import jax.numpy as jnp
import numpy as np
from jax import lax
from jax.experimental import pallas as pl
from jax.experimental.pallas import tpu as pltpu

F32 = jnp.float32
BF16 = jnp.bfloat16

D_MODEL = 1024
RET_HEADS = 8
RET_DK = 64
RET_DV = 128
RET_PAIRS = RET_HEADS // 2
RET_CHUNK = 128
ROPE_BASE = 10000.0
ATT_WINDOWS = (128, 512, 2048)
ATT_DILATIONS = (1, 4, 16)
ATT_GROUPS = 3
ATT_HPG = 4
ATT_HD = 128
ATT_NK = 128
ATT_GW = ATT_HPG * ATT_HD
REL_BUCKETS = 32
REL_MAX_DIST = 2048
D_FF = 4 * D_MODEL
D_PLE = 256
NORM_EPS = 1e-6
RET_QK_W = RET_HEADS * RET_DK
RET_V_W = RET_HEADS * RET_DV
ATT_W = ATT_GROUPS * ATT_GW
COL_RET = 0
COL_ATT = 2 * RET_QK_W + 2 * RET_V_W
COL_GATE = COL_ATT + 3 * ATT_W
N_IN = COL_GATE + 2 * D_MODEL

VMEM_LIMIT_V7X = 56 * 1024 * 1024
LANES = 128
TM_INPROJ = 512
TS_RETENTION = 512
ATTN_UNROLL = 8
PERM_ROWS = 256
SINGLE_BLOCK_GROUP = 4
DECODE_SEQS = 4


def _dot(a, b):
    return jnp.dot(a, b, preferred_element_type=F32)


def _dot_nt(a, b):
    return lax.dot_general(a, b, (((1,), (1,)), ((), ())), preferred_element_type=F32)


def _dot_tn(a, b):
    return lax.dot_general(a, b, (((0,), (0,)), ((), ())), preferred_element_type=F32)


def _rms(x, g):
    return x * lax.rsqrt(jnp.mean(x * x, axis=-1, keepdims=True) + NORM_EPS) * g


def _sigmoid(x):
    return 1.0 / (1.0 + jnp.exp(-x))


def _resident(shape):
    return pl.BlockSpec(shape, lambda *_: (0,) * len(shape), pipeline_mode=pl.Buffered(1))


def _col_window(rows, width, block_index):
    return pl.BlockSpec((rows, width), lambda *_: (0, block_index), pipeline_mode=pl.Buffered(1))


def _params(n_axes):
    return pltpu.CompilerParams(dimension_semantics=("arbitrary",) * n_axes,
                                vmem_limit_bytes=VMEM_LIMIT_V7X)


def _rope_inv_row():
    half = RET_DK // 2
    inv = ROPE_BASE ** (-jnp.arange(half, dtype=F32) / half)
    return jnp.tile(inv, LANES // half)[None, :]


def _ret_log_decay():
    return np.log1p(-np.exp2(-5.0 - np.arange(RET_HEADS, dtype=np.float32))).astype(np.float32)


def _ret_tables():
    c = RET_CHUNK
    lg = _ret_log_decay()
    i = np.arange(c, dtype=np.float32)
    diff = i[:, None] - i[None, :]
    dmask = np.where(diff[None] >= 0, np.exp(np.maximum(diff, 0.0)[None] * lg[:, None, None]), 0.0)
    q_decay = np.exp((i + 1.0)[:, None] * lg[None, :])
    k_decay = np.exp((c - 1.0 - i)[:, None] * lg[None, :])
    qdec = np.broadcast_to(q_decay.T[:, :, None], (RET_HEADS, c, RET_DV))
    kdec = np.repeat(k_decay, RET_DK, axis=1).reshape(c, RET_PAIRS, 2 * RET_DK).transpose(1, 0, 2)
    gc = np.repeat(np.exp(c * lg), RET_DV).reshape(RET_PAIRS, 1, 2 * RET_DV)
    return tuple(jnp.asarray(np.ascontiguousarray(t), F32) for t in (dmask, qdec, kdec, gc))


def _rel_buckets():
    max_exact = REL_BUCKETS // 2
    out = []
    for dil in ATT_DILATIONS:
        d = np.arange(ATT_NK, dtype=np.int32) * dil
        log_ratio = (np.log(np.maximum(d, 1).astype(np.float32) / np.float32(max_exact))
                     / np.float32(math.log(REL_MAX_DIST / max_exact)))
        large = max_exact + (log_ratio * np.float32(REL_BUCKETS - max_exact)).astype(np.int32)
        out.append(np.where(d < max_exact, d, np.minimum(large, REL_BUCKETS - 1)))
    return np.stack(out)


def _band_select():
    nk = ATT_NK
    buckets = _rel_buckets()
    onehot = np.zeros((ATT_GROUPS, REL_BUCKETS, 3 * nk), np.float32)
    for g in range(ATT_GROUPS):
        for k in range(nk, 2 * nk):
            onehot[g, buckets[g, 2 * nk - 1 - k], k] = 1.0
    mask = np.full((1, 3 * nk), -np.inf, np.float32)
    mask[0, nk:2 * nk] = 0.0
    return jnp.asarray(onehot, BF16), jnp.asarray(mask, F32)


def _slot_select():
    buckets = _rel_buckets()
    onehot = np.zeros((ATT_GROUPS, REL_BUCKETS, ATT_NK), np.float32)
    for g in range(ATT_GROUPS):
        for slot in range(ATT_NK):
            onehot[g, buckets[g, 0 if slot == 0 else ATT_NK - slot], slot] = 1.0
    return jnp.asarray(onehot, BF16)


def _split3(x):
    hi = x.astype(BF16)
    rem = x - hi.astype(F32)
    mid = rem.astype(BF16)
    lo = (rem - mid.astype(F32)).astype(BF16)
    return hi, mid, lo


def _inproj_ret_body(x_ref, ln_ref, w_ref, wg0_ref, wg1_ref, wg2_ref, wg3_ref, inv_ref,
                     q_ref, k_ref, v_ref, g_ref, gr_ref, ga_ref, cos_s, sin_s,
                     *, tm, tiles, fixed_pos):
    i = pl.program_id(0)
    lane = lax.broadcasted_iota(jnp.int32, (tm, LANES), 1)
    first_half = (lane % RET_DK) < (RET_DK // 2)

    def tables(pos):
        ang = pos.astype(F32) * inv_ref[...]
        sin = jnp.sin(ang)
        return jnp.cos(ang), jnp.where(first_half, -sin, sin)

    if fixed_pos is None:
        trow = pl.ds(pl.multiple_of(lax.rem(i, tiles) * tm, tm), tm)

        @pl.when(i < tiles)
        def _():
            cos_s[trow, :], sin_s[trow, :] = tables(i * tm + lax.broadcasted_iota(jnp.int32, (tm, LANES), 0))
        cos = cos_s[trow, :]
        sin = sin_s[trow, :]
    else:
        cos, sin = tables(jnp.full((tm, LANES), fixed_pos, jnp.int32))
    x = x_ref[...]
    h = _rms(x, ln_ref[...]).astype(BF16)
    qk = _dot(h, w_ref[:, 0:2 * RET_QK_W])
    n_q = RET_QK_W // LANES
    for c in range(2 * n_q):
        xc = qk[:, c * LANES:(c + 1) * LANES]
        swapped = jnp.where(first_half, pltpu.roll(xc, LANES - RET_DK // 2, 1), pltpu.roll(xc, RET_DK // 2, 1))
        r = xc * cos + swapped * sin
        if c < n_q:
            q_ref[:, c * LANES:(c + 1) * LANES] = r.astype(q_ref.dtype)
        else:
            k_ref[:, (c - n_q) * LANES:(c - n_q + 1) * LANES] = (r * (RET_DK ** -0.5)).astype(k_ref.dtype)
    o = 2 * RET_QK_W
    v_ref[...] = _dot(h, w_ref[:, o:o + RET_V_W]).astype(v_ref.dtype)
    o += RET_V_W
    g_ref[...] = _dot(h, w_ref[:, o:o + RET_V_W]).astype(g_ref.dtype)
    half = D_MODEL // 2
    gr_ref[:, :half] = _dot(h, wg0_ref[...]).astype(gr_ref.dtype)
    gr_ref[:, half:] = _dot(h, wg1_ref[...]).astype(gr_ref.dtype)
    ga_ref[:, :half] = _dot(h, wg2_ref[...]).astype(ga_ref.dtype)
    ga_ref[:, half:] = _dot(h, wg3_ref[...]).astype(ga_ref.dtype)


def _inproj_ret(x2d, ln, w_in, inv_row, tm, out_dtype, seq=None, fixed_pos=None):
    n = x2d.shape[0]
    tiles = None if seq is None else seq // tm
    table_rows = tm if seq is None else seq
    row = lambda width: pl.BlockSpec((tm, width), lambda i: (i, 0))
    widths = (RET_QK_W, RET_QK_W, RET_V_W, RET_V_W, D_MODEL, D_MODEL)
    half = D_MODEL // 2
    assert COL_GATE % half == 0
    return pl.pallas_call(
        functools.partial(_inproj_ret_body, tm=tm, tiles=tiles, fixed_pos=fixed_pos),
        grid=(n // tm,),
        in_specs=[row(D_MODEL), _resident((1, D_MODEL)), _col_window(D_MODEL, COL_ATT, 0)]
        + [_col_window(D_MODEL, half, COL_GATE // half + j) for j in range(4)] + [_resident((1, LANES))],
        out_specs=[row(wd) for wd in widths],
        out_shape=[jax.ShapeDtypeStruct((n, wd), out_dtype) for wd in widths],
        scratch_shapes=[pltpu.VMEM((table_rows, LANES), F32), pltpu.VMEM((table_rows, LANES), F32)],
        compiler_params=_params(1),
        name="inproj_ret",
    )(x2d, ln, w_in, w_in, w_in, w_in, w_in, inv_row)


def _inproj_att_body(x_ref, ln_ref, wq_ref, wk_ref, wv_ref, *refs, tm, keeps, seq):
    lowp = refs[:3 * ATT_GROUPS]
    full = refs[3 * ATT_GROUPS:5 * ATT_GROUPS]
    res = refs[5 * ATT_GROUPS]
    h = _rms(x_ref[...], ln_ref[...]).astype(BF16)
    slot = 0
    for g in reversed(range(ATT_GROUPS)):
        for kind in reversed(range(3)):
            dil = ATT_DILATIONS[g]
            dst = lowp[kind * ATT_GROUPS + g]
            r = _dot(h, (wq_ref, wk_ref, wv_ref)[kind][:, g * ATT_GW:(g + 1) * ATT_GW])
            if dil == 1:
                dst[0] = r.astype(dst.dtype)
            else:
                for hh in range(ATT_HPG):
                    res[slot, hh] = r[:, hh * ATT_HD:(hh + 1) * ATT_HD]
                for rr in range(dil):
                    for hh in range(ATT_HPG):
                        dst[rr, :, hh * ATT_HD:(hh + 1) * ATT_HD] = (
                            res[slot, hh, pl.ds(rr, tm // dil, stride=dil), :].astype(dst.dtype))
                slot += 1
            if kind > 0:
                cache = full[(kind - 1) * ATT_GROUPS + g]
                rows = tm if keeps[g] == seq else keeps[g]
                for hh in range(ATT_HPG):
                    cache[pl.ds(hh, rows, stride=ATT_HPG), :] = r[tm - rows:, hh * ATT_HD:(hh + 1) * ATT_HD]


def _inproj_att_prompt(x2d, ln, w_in, tm, seq):
    n = x2d.shape[0]
    batch = n // seq
    tiles = seq // tm
    keeps = tuple(min(wd, seq) for wd in ATT_WINDOWS)
    assert all(kp <= tm or kp == seq for kp in keeps) and seq % tm == 0
    out_specs, out_shape = [], []
    for _ in range(3):
        for g in range(ATT_GROUPS):
            dil = ATT_DILATIONS[g]
            out_specs.append(pl.BlockSpec((None, dil, tm // dil, ATT_GW), lambda i: (i // tiles, 0, i % tiles, 0)))
            out_shape.append(jax.ShapeDtypeStruct((batch, dil, seq // dil, ATT_GW), BF16))
    for _ in range(2):
        for g in range(ATT_GROUPS):
            if keeps[g] == seq:
                idx = lambda i: (i // tiles, i % tiles, 0)
                rows = tm
            else:
                idx = lambda i: (i // tiles, 0, 0)
                rows = keeps[g]
            out_specs.append(pl.BlockSpec((None, rows * ATT_HPG, ATT_HD), idx))
            out_shape.append(jax.ShapeDtypeStruct((batch, keeps[g] * ATT_HPG, ATT_HD), F32))
    return pl.pallas_call(
        functools.partial(_inproj_att_body, tm=tm, keeps=keeps, seq=seq),
        grid=(n // tm,),
        in_specs=[pl.BlockSpec((tm, D_MODEL), lambda i: (i, 0)), _resident((1, D_MODEL))]
        + [_col_window(D_MODEL, ATT_W, COL_ATT // ATT_W + kind) for kind in range(3)],
        out_specs=out_specs,
        out_shape=out_shape,
        scratch_shapes=[pltpu.VMEM((3 * sum(d > 1 for d in ATT_DILATIONS), ATT_HPG, tm, ATT_HD), F32)],
        compiler_params=_params(1),
        name="inproj_att",
    )(x2d, ln, w_in, w_in, w_in)


def _inproj_att_sample_body(x_ref, ln_ref, wq_ref, wk_ref, wv_ref, o_ref):
    h = _rms(x_ref[...], ln_ref[...]).astype(BF16)
    for kind, w_ref in enumerate((wq_ref, wk_ref, wv_ref)):
        for g in range(ATT_GROUPS):
            c = kind * ATT_GROUPS + g
            o_ref[:, c * ATT_GW:(c + 1) * ATT_GW] = _dot(h, w_ref[:, g * ATT_GW:(g + 1) * ATT_GW])


def _inproj_att_sample(x2d, ln, w_in):
    n = x2d.shape[0]
    return pl.pallas_call(
        _inproj_att_sample_body,
        grid=(1,),
        in_specs=[pl.BlockSpec((n, D_MODEL), lambda i: (0, 0)), _resident((1, D_MODEL))]
        + [_col_window(D_MODEL, ATT_W, COL_ATT // ATT_W + kind) for kind in range(3)],
        out_specs=pl.BlockSpec((n, 3 * ATT_W), lambda i: (0, 0)),
        out_shape=jax.ShapeDtypeStruct((n, 3 * ATT_W), F32),
        compiler_params=_params(1),
        name="inproj_att_sample",
    )(x2d, ln, w_in, w_in, w_in)


def _gn_swish(o, gate, gn):
    mu = jnp.mean(o, axis=-1, keepdims=True)
    d = o - mu
    var = jnp.mean(d * d, axis=-1, keepdims=True)
    on = d * lax.rsqrt(var + NORM_EPS) * gn
    return gate * _sigmoid(gate) * on


def _retention_body(q_ref, k_ref, v_ref, g_ref, gn_ref, dm_ref, qdec_ref, kdec_ref, gc_ref,
                    out_ref, st_ref, state, *, n_chunks):
    c = RET_CHUNK

    @pl.when(pl.program_id(1) == 0)
    def _():
        state[...] = jnp.zeros_like(state)

    lane = lax.broadcasted_iota(jnp.int32, (c, 2 * RET_DK), 1)
    head0 = lane < RET_DK

    def chunk(ci, carry):
        rows = pl.ds(pl.multiple_of(ci * c, c), c)
        scores, cross, values = {}, {}, {}
        for p in range(RET_PAIRS):
            q2 = q_ref[rows, p * 2 * RET_DK:(p + 1) * 2 * RET_DK]
            k2 = k_ref[rows, p * 2 * RET_DK:(p + 1) * 2 * RET_DK]
            v2 = v_ref[rows, p * 2 * RET_DV:(p + 1) * 2 * RET_DV]
            pst = state[p]
            pst_lo = pst.astype(BF16)
            zero = jnp.zeros_like(q2)
            for hh in range(2):
                h = 2 * p + hh
                qm = jnp.where(head0 if hh == 0 else jnp.logical_not(head0), q2, zero)
                values[h] = v2[:, hh * RET_DV:(hh + 1) * RET_DV]
                scores[h] = _dot_nt(qm, k2)
                cross[h] = _dot(qm, pst_lo[:, hh * RET_DV:(hh + 1) * RET_DV])
            kd = (k2.astype(F32) * kdec_ref[p]).astype(BF16)
            state[p] = pst * gc_ref[p] + _dot_tn(kd, v2)
        for h in range(RET_HEADS):
            o = _dot((scores[h] * dm_ref[h]).astype(BF16), values[h]) + cross[h] * qdec_ref[h]
            gate = g_ref[rows, h * RET_DV:(h + 1) * RET_DV].astype(F32)
            res = _gn_swish(o, gate, gn_ref[:, h * RET_DV:(h + 1) * RET_DV])
            out_ref[rows, h * RET_DV:(h + 1) * RET_DV] = res.astype(out_ref.dtype)
        return carry

    lax.fori_loop(0, n_chunks, chunk, 0, unroll=True)
    for p in range(RET_PAIRS):
        pst = state[p]
        for hh in range(2):
            st_ref[2 * p + hh] = pst[hh * RET_DK:(hh + 1) * RET_DK, hh * RET_DV:(hh + 1) * RET_DV]


def _retention_prompt(q, k, v, g, gn, batch, seq, ts):
    n = q.shape[0]
    steps = seq // ts
    dmask, qdec, kdec, gc = _ret_tables()
    row = lambda width: pl.BlockSpec((ts, width), lambda b, s: (b * steps + s, 0))
    const = lambda a: pl.BlockSpec(a.shape, lambda b, s: (0,) * a.ndim)
    return pl.pallas_call(
        functools.partial(_retention_body, n_chunks=ts // RET_CHUNK),
        grid=(batch, steps),
        in_specs=[row(RET_QK_W), row(RET_QK_W), row(RET_V_W), row(RET_V_W), const(gn),
                  const(dmask), const(qdec), const(kdec), const(gc)],
        out_specs=[row(RET_V_W),
                   pl.BlockSpec((None, RET_HEADS, RET_DK, RET_DV), lambda b, s: (b, 0, 0, 0))],
        out_shape=[jax.ShapeDtypeStruct((n, RET_V_W), BF16),
                   jax.ShapeDtypeStruct((batch, RET_HEADS, RET_DK, RET_DV), F32)],
        scratch_shapes=[pltpu.VMEM((RET_PAIRS, 2 * RET_DK, 2 * RET_DV), F32)],
        compiler_params=_params(2),
        name="retention",
    )(q, k, v, g, gn, dmask, qdec, kdec, gc)


def _attn_body(q_ref, k_ref, v_ref, rb_ref, sel_ref, neg_ref, o_ref, lse_ref, *scratch, group, dil, n_blocks):
    nk = ATT_NK
    scale = ATT_HD ** -0.5
    lane = lax.broadcasted_iota(jnp.int32, (nk, LANES), 1)
    tabs = scratch[-1]

    @pl.when(pl.program_id(0) == 0)
    def _():
        band = neg_ref[...]
        for piece in _split3(rb_ref[group * ATT_HPG:(group + 1) * ATT_HPG, :]):
            band = band + _dot(piece, sel_ref[...])
        col = lax.broadcasted_iota(jnp.int32, (1, 2 * nk), 1)
        for i in range(nk):
            window = band[:, nk - 1 - i:3 * nk - 1 - i]
            for h in range(ATT_HPG):
                tabs[1, h, i:i + 1, :] = window[h:h + 1, :]
                tabs[0, h, i:i + 1, :] = jnp.where(col < nk, -jnp.inf, window[h:h + 1, :])
    if dil > 1:
        o_s, l_s = scratch[:2]

    def chain(s_raw, v_ext, sel, h):
        s = s_raw * scale + (tabs[sel, h] if n_blocks > 1 else tabs[1, h, :, nk:])
        m = jnp.max(s, axis=-1, keepdims=True)
        p = jnp.exp(s - m)
        pv = _dot(p.astype(BF16), v_ext)
        den = pv[:, ATT_HD:]
        return (pv[:, :ATT_HD] / den).astype(BF16), m + jnp.log(den)

    def write(rs, rows, outs):
        lse_tile = jnp.zeros((nk, LANES), F32)
        for h, (o, lse) in enumerate(outs):
            hc = slice(h * ATT_HD, (h + 1) * ATT_HD)
            if dil == 1:
                o_ref[rows, hc] = o
            else:
                o_s[rs, rows, hc] = o
            lse_tile = jnp.where(lane == h, lse, lse_tile)
        if dil == 1:
            lse_ref[rows, :] = lse_tile
        else:
            l_s[rs, rows, :] = lse_tile

    heads = [slice(h * ATT_HD, (h + 1) * ATT_HD) for h in range(ATT_HPG)]

    def block(rs, n, n_prev, sel):
        rows = pl.ds(pl.multiple_of(n * nk, nk), nk)
        prev_rows = pl.ds(pl.multiple_of(n_prev * nk, nk), nk)
        q = q_ref[rs, rows, :]
        k_all = jnp.concatenate([k_ref[rs, prev_rows, :], k_ref[rs, rows, :]], axis=0)
        v_all = jnp.concatenate([v_ref[rs, prev_rows, :], v_ref[rs, rows, :]], axis=0)
        ones = jnp.ones((2 * nk, ATT_HD), BF16)
        outs = []
        for h, hc in enumerate(heads):
            v_ext = jnp.concatenate([v_all[:, hc], ones], axis=1)
            outs.append(chain(_dot_nt(q[:, hc], k_all[:, hc]), v_ext, sel, h))
        write(rs, rows, outs)

    def single_blocks(streams):
        ones = jnp.ones((nk, ATT_HD), BF16)
        scores = [[_dot_nt(q_ref[rs, :, hc], k_ref[rs, :, hc]) for hc in heads] for rs in streams]
        for i, rs in enumerate(streams):
            outs = [chain(scores[i][h], jnp.concatenate([v_ref[rs, :, hc], ones], axis=1), 0, h)
                    for h, hc in enumerate(heads)]
            write(rs, slice(None), outs)

    if n_blocks == 1:
        for r0 in range(0, dil, SINGLE_BLOCK_GROUP):
            single_blocks(range(r0, min(r0 + SINGLE_BLOCK_GROUP, dil)))
    else:
        for rs in range(dil):
            def loop(n, carry, rs=rs):
                block(rs, n, jnp.maximum(n - 1, 0), jnp.minimum(n, 1))
                return carry
            lax.fori_loop(0, n_blocks, loop, 0, unroll=ATTN_UNROLL)

    if dil > 1:
        piece = PERM_ROWS // dil
        row = lax.broadcasted_iota(jnp.int32, (PERM_ROWS, PERM_ROWS), 0)
        col = lax.broadcasted_iota(jnp.int32, (PERM_ROWS, PERM_ROWS), 1)
        perm = jnp.where((col % piece) * dil + col // piece == row, 1.0, 0.0).astype(BF16)
        for j in range(o_ref.shape[0] // PERM_ROWS):
            src = slice(j * piece, (j + 1) * piece)
            dst = slice(j * PERM_ROWS, (j + 1) * PERM_ROWS)
            o_ref[dst, :] = _dot(perm, jnp.concatenate([o_s[r, src, :] for r in range(dil)], axis=0)).astype(BF16)
            lse = jnp.concatenate([l_s[r, src, :] for r in range(dil)], axis=0)
            hi = lse.astype(BF16)
            rem = lse - hi.astype(F32)
            mid = rem.astype(BF16)
            lo = (rem - mid.astype(F32)).astype(BF16)
            lse_ref[dst, :] = _dot(perm, hi) + _dot(perm, mid) + _dot(perm, lo)


def _attention_prompt(aq, ak, av, rel_bias_t, onehot, neg, g):
    batch, dil, length, _ = aq.shape
    seq = dil * length
    n_blocks = length // ATT_NK
    blk = pl.BlockSpec((None, dil, length, ATT_GW), lambda b: (b, 0, 0, 0))
    scratch = [pltpu.VMEM((dil, length, ATT_GW), BF16), pltpu.VMEM((dil, length, LANES), F32)] if dil > 1 else []
    scratch.append(pltpu.VMEM((2, ATT_HPG, ATT_NK, 2 * ATT_NK), F32))
    o, lse = pl.pallas_call(
        functools.partial(_attn_body, group=g, dil=dil, n_blocks=n_blocks),
        grid=(batch,),
        in_specs=[blk, blk, blk, pl.BlockSpec(rel_bias_t.shape, lambda b: (0, 0)),
                  pl.BlockSpec((None, REL_BUCKETS, 3 * ATT_NK), lambda b: (g, 0, 0)),
                  pl.BlockSpec((1, 3 * ATT_NK), lambda b: (0, 0))],
        out_specs=[pl.BlockSpec((None, seq, ATT_GW), lambda b: (b, 0, 0)),
                   pl.BlockSpec((None, seq, LANES), lambda b: (b, 0, 0))],
        out_shape=[jax.ShapeDtypeStruct((batch, seq, ATT_GW), BF16),
                   jax.ShapeDtypeStruct((batch, seq, LANES), F32)],
        scratch_shapes=scratch,
        compiler_params=_params(1),
        name="attention_g%d" % g,
    )(aq, ak, av, rel_bias_t, onehot, neg)
    return o.reshape(batch * seq, ATT_GW), lse.reshape(batch * seq, LANES)


def _tail_math(act_refs, w_refs, y_ref, combine):
    wrb_ref, wab_ref, wo_ref, wu_ref, wd_ref, wpl_ref, wpg_ref, ln2_ref, lnf_ref = w_refs
    if combine:
        ret_ref, o0_ref, o1_ref, o2_ref, l0_ref, l1_ref, l2_ref, gr_ref, ga_ref, x_ref, ple_ref = act_refs
        lses = [l0_ref[...], l1_ref[...], l2_ref[...]]
        outs = [o0_ref, o1_ref, o2_ref]
        parts = []
        for h in range(ATT_HPG):
            lh = [l[:, h:h + 1] for l in lses]
            mx = jnp.maximum(jnp.maximum(lh[0], lh[1]), lh[2])
            e = [jnp.exp(l - mx) for l in lh]
            tot = e[0] + e[1] + e[2]
            acc = None
            for g in range(ATT_GROUPS):
                term = (e[g] / tot) * outs[g][:, h * ATT_HD:(h + 1) * ATT_HD].astype(F32)
                acc = term if acc is None else acc + term
            parts.append(acc)
        att = jnp.concatenate(parts, axis=1).astype(BF16)
    else:
        ret_ref, att_ref, gr_ref, ga_ref, x_ref, ple_ref = act_refs
        att = att_ref[...].astype(BF16)
    a = _dot(ret_ref[...].astype(BF16), wrb_ref[...])
    b = _dot(att, wab_ref[...])
    mixed = _sigmoid(gr_ref[...].astype(F32)) * a + _sigmoid(ga_ref[...].astype(F32)) * b
    x1 = x_ref[...] + _dot(mixed.astype(BF16), wo_ref[...])
    h2 = _rms(x1, ln2_ref[...]).astype(BF16)
    ff_chunk = D_MODEL
    acc = None
    for c in range(D_FF // ff_chunk):
        u = _dot(h2, wu_ref[:, c * ff_chunk:(c + 1) * ff_chunk])
        r = jnp.maximum(u, 0.0)
        t = _dot((r * r).astype(BF16), wd_ref[c * ff_chunk:(c + 1) * ff_chunk, :])
        acc = t if acc is None else acc + t
    x2 = x1 + acc
    gate = _sigmoid(_dot(x2.astype(BF16), wpg_ref[...]))
    x3 = x2 + gate * _dot(ple_ref[...].astype(BF16), wpl_ref[...])
    y_ref[...] = _rms(x3, lnf_ref[...])


def _tail_body(*refs, n_act, combine):
    _tail_math(refs[:n_act], refs[n_act:-1], refs[-1], combine)


def _tail(acts, weights, tm, combine):
    n = acts[0].shape[0]
    row = lambda a: pl.BlockSpec((tm, a.shape[1]), lambda i: (i, 0))
    return pl.pallas_call(
        functools.partial(_tail_body, n_act=len(acts), combine=combine),
        grid=(n // tm,),
        in_specs=[row(a) for a in acts] + [_resident(w.shape) for w in weights],
        out_specs=pl.BlockSpec((tm, D_MODEL), lambda i: (i, 0)),
        out_shape=jax.ShapeDtypeStruct((n, D_MODEL), F32),
        compiler_params=_params(1),
        name="tail" if combine else "tail_sample",
    )(*acts, *weights)


def _tail_shift_body(*refs, n_act, n_w, n_steps):
    nc = 2 * ATT_GROUPS
    acts = refs[:n_act]
    ws = refs[n_act:n_act + n_w]
    new_ref = refs[n_act + n_w]
    base = n_act + n_w + 1
    old = refs[base:base + nc]
    y_ref = refs[base + nc]
    out = refs[base + nc + 1:base + 2 * nc + 1]
    stage = refs[base + 2 * nc + 1:base + 3 * nc + 1]
    sem_in, sem_out, sem_row = refs[base + 3 * nc + 1:]
    s = pl.program_id(0)
    b = lax.shift_right_logical(s, 1)
    parity = lax.rem(s, 2)

    def copy_in(i, bi):
        width = old[i].shape[1]
        return pltpu.make_async_copy(old[i].at[bi, pl.ds(1, width - 1)], stage[i], sem_in.at[i])

    def copy_out(i, bi):
        width = old[i].shape[1]
        return pltpu.make_async_copy(stage[i], out[i].at[bi, pl.ds(0, width - 1)], sem_out.at[i])

    def copy_row(g, kind):
        i = 2 * g + kind
        return pltpu.make_async_copy(new_ref.at[b, kind + 1, g], out[i].at[b, old[i].shape[1] - 1], sem_row.at[g])

    for kind in range(2):
        mine = [2 * g + kind for g in range(ATT_GROUPS)]
        other = [2 * g + 1 - kind for g in range(ATT_GROUPS)]

        @pl.when(parity == kind)
        def _(kind=kind, mine=mine, other=other):
            if kind == 0:
                @pl.when(s == 0)
                def _():
                    for i in mine:
                        copy_in(i, 0).start()
            for i in mine:
                copy_in(i, b).wait()
            for i in mine:
                copy_out(i, b).start()
            if kind == 0:
                @pl.when(s > 0)
                def _():
                    for i in other:
                        copy_out(i, b - 1).wait()
                for i in other:
                    copy_in(i, b).start()
            else:
                for i in other:
                    copy_out(i, b).wait()

                @pl.when(s < n_steps - 1)
                def _():
                    for i in other:
                        copy_in(i, b + 1).start()
            for g in range(ATT_GROUPS):
                copy_row(g, kind).start()

    _tail_math(acts, ws, y_ref, True)

    for kind in range(2):
        @pl.when(parity == kind)
        def _(kind=kind):
            for g in range(ATT_GROUPS):
                copy_row(g, kind).wait()

    @pl.when(s == n_steps - 1)
    def _():
        for g in range(ATT_GROUPS):
            copy_out(2 * g + 1, b).wait()


def _tail_shift(acts, weights, tm, new_qkv, caches):
    n = acts[0].shape[0]
    n_steps = n // tm
    assert n_steps == 2 * new_qkv.shape[0]
    row = lambda a: pl.BlockSpec((tm, a.shape[1]), lambda i: (i, 0))
    anyspec = pl.BlockSpec(memory_space=pl.ANY)
    nc = len(caches)
    new_rows = new_qkv.reshape(new_qkv.shape[0], 3, ATT_GROUPS, ATT_HPG, ATT_HD)
    res = pl.pallas_call(
        functools.partial(_tail_shift_body, n_act=len(acts), n_w=len(weights), n_steps=n_steps),
        grid=(n_steps,),
        in_specs=[row(a) for a in acts] + [_resident(w.shape) for w in weights] + [anyspec] * (nc + 1),
        out_specs=[pl.BlockSpec((tm, D_MODEL), lambda i: (i, 0))] + [anyspec] * nc,
        out_shape=[jax.ShapeDtypeStruct((n, D_MODEL), F32)]
        + [jax.ShapeDtypeStruct(c.shape, c.dtype) for c in caches],
        scratch_shapes=[pltpu.VMEM((c.shape[1] - 1, ATT_HPG, ATT_HD), F32) for c in caches]
        + [pltpu.SemaphoreType.DMA((nc,)), pltpu.SemaphoreType.DMA((nc,)),
           pltpu.SemaphoreType.DMA((ATT_GROUPS,))],
        compiler_params=_params(1),
        name="tail",
    )(*acts, *weights, new_rows, *caches)
    return res[0], res[1:]


def _decode_body(q_ref, k_ref, v_ref, g_ref, gn_ref, st_ref, gam_ref,
                 qkv_ref, rb_ref, slot_ref,
                 ck0_ref, cv0_ref, ck1_ref, cv1_ref, ck2_ref, cv2_ref,
                 ret_ref, att_ref, nst_ref, slot_bias, *, seqs):
    sub = lax.broadcasted_iota(jnp.int32, (8, 2 * RET_DK), 0)
    lane = lax.broadcasted_iota(jnp.int32, (8, 2 * RET_DK), 1)
    row0 = sub == 0
    srow = lax.broadcasted_iota(jnp.int32, (2 * RET_DK, RET_DV), 0)
    for i in range(seqs):
        for p in range(RET_PAIRS):
            pc = slice(p * 2 * RET_DK, (p + 1) * 2 * RET_DK)
            q2 = jnp.where(row0, jnp.broadcast_to(q_ref[i, :, pc], (8, 2 * RET_DK)), 0.0)
            k2 = jnp.where(row0, jnp.broadcast_to(k_ref[i, :, pc], (8, 2 * RET_DK)), 0.0)
            pst = st_ref[i, p]
            gam = gam_ref[p]
            outer = []
            for hh in range(2):
                h = 2 * p + hh
                hc = slice(h * RET_DV, (h + 1) * RET_DV)
                hsel = (lane < RET_DK) if hh == 0 else (lane >= RET_DK)
                qm = jnp.where(hsel, q2, 0.0)
                km = jnp.where(hsel, k2, 0.0)
                vh = v_ref[i, :, hc]
                v8 = jnp.where(row0[:, :RET_DV], jnp.broadcast_to(vh, (8, RET_DV)), 0.0)
                cross = _dot(qm.astype(BF16), (pst * gam).astype(BF16))[0:1, :]
                qk = jnp.sum(qm[0:1, :] * km[0:1, :], axis=-1, keepdims=True)
                o = cross + qk * vh
                ret_ref[i, :, hc] = _gn_swish(o, g_ref[i, :, hc], gn_ref[:, hc])
                outer.append(_dot_tn(k2.astype(BF16), v8.astype(BF16)))
            nst_ref[i, p] = pst * gam + jnp.where(srow < RET_DK, outer[0], outer[1])
    scale = ATT_HD ** -0.5
    caches = ((ck0_ref, cv0_ref), (ck1_ref, cv1_ref), (ck2_ref, cv2_ref))
    slot = lax.broadcasted_iota(jnp.int32, (ATT_NK, ATT_HD), 0)
    is_new = slot == 0
    first = row0[:, :ATT_HD]

    @pl.when(pl.program_id(0) == 0)
    def _():
        pieces = _split3(rb_ref[...])
        for g in range(ATT_GROUPS):
            rows = slice(g * ATT_HPG, (g + 1) * ATT_HPG)
            slot_bias[rows, :] = (_dot(pieces[0][rows], slot_ref[g]) + _dot(pieces[1][rows], slot_ref[g])
                                  + _dot(pieces[2][rows], slot_ref[g]))

    def one_row(x):
        return jnp.where(first, jnp.broadcast_to(x, (8, ATT_HD)), 0.0).astype(BF16)

    scores = {}
    for i in range(seqs):
        for h in range(ATT_HPG):
            for g in range(ATT_GROUPS):
                c0 = g * ATT_GW + h * ATT_HD
                kk = jnp.where(is_new, qkv_ref[i, :, ATT_W + c0:ATT_W + c0 + ATT_HD], caches[g][0][i, :, h, :])
                scores[i, g, h] = _dot_nt(one_row(qkv_ref[i, :, c0:c0 + ATT_HD]), kk.astype(BF16))[0:1, :]
    for i in range(seqs):
        for h in range(ATT_HPG):
            o_g, lse_g = [], []
            for g in range(ATT_GROUPS):
                c0 = g * ATT_GW + h * ATT_HD
                vv = jnp.where(is_new, qkv_ref[i, :, 2 * ATT_W + c0:2 * ATT_W + c0 + ATT_HD],
                               caches[g][1][i, :, h, :])
                gh = g * ATT_HPG + h
                s = scores[i, g, h] * scale + slot_bias[gh:gh + 1, :]
                m = jnp.max(s, axis=-1, keepdims=True)
                pr = jnp.exp(s - m)
                den = jnp.sum(pr, axis=-1, keepdims=True)
                o_g.append(_dot(one_row(pr), vv.astype(BF16))[0:1, :] / den)
                lse_g.append(m + jnp.log(den))
            mx = jnp.maximum(jnp.maximum(lse_g[0], lse_g[1]), lse_g[2])
            e = [jnp.exp(l - mx) for l in lse_g]
            tot = e[0] + e[1] + e[2]
            att_ref[i, :, h * ATT_HD:(h + 1) * ATT_HD] = ((e[0] / tot) * o_g[0] + (e[1] / tot) * o_g[1]
                                                          + (e[2] / tot) * o_g[2])


def _decode(q, k, v, g, gn, state, qkv, rel_bias_t, caches):
    batch = q.shape[0]
    gam = jnp.asarray(np.ascontiguousarray(np.broadcast_to(
        np.repeat(np.exp(_ret_log_decay()), RET_DK).reshape(RET_PAIRS, 2 * RET_DK, 1),
        (RET_PAIRS, 2 * RET_DK, RET_DV))), F32)
    vec = lambda a: a.reshape(batch, 1, a.shape[1])
    seqs = DECODE_SEQS
    assert batch % seqs == 0
    vspec = lambda width: pl.BlockSpec((seqs, 1, width), lambda b: (b, 0, 0))
    const = lambda a: pl.BlockSpec(a.shape, lambda b: (0,) * a.ndim)
    st_spec = pl.BlockSpec((seqs, RET_PAIRS, 2 * RET_DK, RET_DV), lambda b: (b, 0, 0, 0))
    cache_in, cache_specs = [], []
    for gi in range(ATT_GROUPS):
        dil = ATT_DILATIONS[gi]
        for c in caches[2 * gi:2 * gi + 2]:
            cache_in.append(c.reshape(batch, ATT_NK, dil, ATT_HPG, ATT_HD))
            cache_specs.append(pl.BlockSpec((seqs, ATT_NK, None, ATT_HPG, ATT_HD), lambda b: (b, 0, 0, 0, 0)))
    st_pairs = state.reshape(batch, RET_PAIRS, 2 * RET_DK, RET_DV)
    slot_sel = _slot_select()
    ret, att, nst = pl.pallas_call(
        functools.partial(_decode_body, seqs=seqs),
        grid=(batch // seqs,),
        in_specs=[vspec(RET_QK_W), vspec(RET_QK_W), vspec(RET_V_W), vspec(RET_V_W), const(gn), st_spec, const(gam),
                  vspec(3 * ATT_W), const(rel_bias_t), const(slot_sel)] + cache_specs,
        out_specs=[vspec(RET_V_W), vspec(ATT_GW), st_spec],
        out_shape=[jax.ShapeDtypeStruct((batch, 1, RET_V_W), F32),
                   jax.ShapeDtypeStruct((batch, 1, ATT_GW), F32),
                   jax.ShapeDtypeStruct(st_pairs.shape, F32)],
        scratch_shapes=[pltpu.VMEM((rel_bias_t.shape[0], ATT_NK), F32)],
        compiler_params=_params(1),
        name="decode",
    )(vec(q), vec(k), vec(v), vec(g), gn, st_pairs, gam, vec(qkv), rel_bias_t, slot_sel, *cache_in)
    return ret.reshape(batch, RET_V_W), att.reshape(batch, ATT_GW), nst.reshape(state.shape)


def kernel(x_prompt, x_sample, state_ret, cache_k_w128, cache_v_w128, cache_k_w512, cache_v_w512,
           cache_k_w2048, cache_v_w2048, p_prompt, p_sample, ln1_g, w_in, ret_gn_g, w_ret_br, w_att_br,
           w_out, ln2_g, w_up, w_down, w_ple, w_ple_gate, rel_bias, lnf_g):
    depth = w_in.shape[0]
    assert depth == 1
    batch, seq, _ = x_prompt.shape
    dec_batch, dec_seq, _ = x_sample.shape
    assert dec_seq == 1
    past_len = 16384
    l = 0
    ln1 = ln1_g[l][None, :]
    ln2 = ln2_g[l][None, :]
    lnf = lnf_g[None, :]
    gn = ret_gn_g[l][None, :]
    w_in_lo = w_in[l].astype(BF16)
    tail_w = (w_ret_br[l].astype(BF16), w_att_br[l].astype(BF16), w_out[l].astype(BF16), w_up[l].astype(BF16),
              w_down[l].astype(BF16), w_ple[l].astype(BF16), w_ple_gate[l].astype(BF16), ln2, lnf)
    inv_row = _rope_inv_row()

    caches = (cache_k_w128[l], cache_v_w128[l], cache_k_w512[l], cache_v_w512[l],
              cache_k_w2048[l], cache_v_w2048[l])

    xs = x_sample.reshape(dec_batch, D_MODEL)
    sq, sk, sv, sg, sgr, sga = _inproj_ret(xs, ln1, w_in_lo, inv_row, dec_batch, F32, fixed_pos=past_len)
    s_qkv = _inproj_att_sample(xs, ln1, w_in_lo)
    rel_bias_t = rel_bias.T
    s_ret, s_attn, new_st = _decode(sq, sk, sv, sg, gn, state_ret[l], s_qkv, rel_bias_t, caches)
    ple_s = p_sample[l].reshape(dec_batch, D_PLE)
    y_s = _tail([s_ret, s_attn, sgr, sga, xs, ple_s], tail_w, dec_batch, False)
    y_sample = y_s.reshape(dec_batch, 1, D_MODEL)

    xp = x_prompt.reshape(batch * seq, D_MODEL)
    rq, rk, rv, rg, gr, ga = _inproj_ret(xp, ln1, w_in_lo, inv_row, TM_INPROJ, BF16, seq=seq)
    att_o = _inproj_att_prompt(xp, ln1, w_in_lo, TM_INPROJ, seq)
    aqs, aks, avs = att_o[0:3], att_o[3:6], att_o[6:9]
    kfull, vfull = att_o[9:12], att_o[12:15]
    ret_out, st_p = _retention_prompt(rq, rk, rv, rg, gn, batch, seq, TS_RETENTION)
    outs, lses = [], []
    band_onehot, band_neg = _band_select()
    for g in range(ATT_GROUPS):
        o, lse = _attention_prompt(aqs[g], aks[g], avs[g], rel_bias_t, band_onehot, band_neg, g)
        outs.append(o)
        lses.append(lse)
    ple_p = p_prompt[l].reshape(batch * seq, D_PLE)
    tm_tail = batch * seq // (2 * dec_batch)
    y_p, new_caches = _tail_shift([ret_out] + outs + lses + [gr, ga, xp, ple_p], tail_w, tm_tail, s_qkv, caches)
    y_prompt = y_p.reshape(batch, seq, D_MODEL)
    new_state_p = st_p[None]
    kv_p = []
    for g in range(ATT_GROUPS):
        shape = (1, batch, min(ATT_WINDOWS[g], seq), ATT_HPG, ATT_HD)
        kv_p.append(kfull[g].reshape(shape))
        kv_p.append(vfull[g].reshape(shape))
    kv_s = [c[None] for c in new_caches]

    return (y_prompt, y_sample, new_state_p, *kv_p, new_st[None], *kv_s)
```

```python
import functools
import math

import jax
import jax.numpy as jnp
import numpy as np
from jax import lax
from jax.experimental import pallas as pl
from jax.experimental.pallas import tpu as pltpu

F32 = jnp.float32
BF16 = jnp.bfloat16

D_MODEL = 1024
RET_HEADS = 8
RET_DK = 64
RET_DV = 128
RET_PAIRS = RET_HEADS // 2
RET_CHUNK = 128
ROPE_BASE = 10000.0
ATT_WINDOWS = (128, 512, 2048)
ATT_DILATIONS = (1, 4, 16)
ATT_GROUPS = 3
ATT_HPG = 4
ATT_HD = 128
ATT_NK = 128
ATT_GW = ATT_HPG * ATT_HD
REL_BUCKETS = 32
REL_MAX_DIST = 2048
D_FF = 4 * D_MODEL
D_PLE = 256
NORM_EPS = 1e-6
RET_QK_W = RET_HEADS * RET_DK
RET_V_W = RET_HEADS * RET_DV
ATT_W = ATT_GROUPS * ATT_GW
COL_RET = 0
COL_ATT = 2 * RET_QK_W + 2 * RET_V_W
COL_GATE = COL_ATT + 3 * ATT_W
N_IN = COL_GATE + 2 * D_MODEL

VMEM_LIMIT_V7X = 56 * 1024 * 1024
VMEM_LIMIT_TAIL_V7X = 62 * 1024 * 1024
LANES = 128
TM_INPROJ = 512
TS_RETENTION = 512
ATTN_UNROLL = 8
PERM_ROWS = 256
SINGLE_BLOCK_GROUP = 4
DECODE_SEQS = 4


def _dot(a, b):
    return jnp.dot(a, b, preferred_element_type=F32)


def _dot_nt(a, b):
    return lax.dot_general(a, b, (((1,), (1,)), ((), ())), preferred_element_type=F32)


def _dot_tn(a, b):
    return lax.dot_general(a, b, (((0,), (0,)), ((), ())), preferred_element_type=F32)


def _rms(x, g):
    return x * lax.rsqrt(jnp.mean(x * x, axis=-1, keepdims=True) + NORM_EPS) * g


def _sigmoid(x):
    return 1.0 / (1.0 + jnp.exp(-x))


def _resident(shape):
    return pl.BlockSpec(shape, lambda *_: (0,) * len(shape), pipeline_mode=pl.Buffered(1))


def _col_window(rows, width, block_index):
    return pl.BlockSpec((rows, width), lambda *_: (0, block_index), pipeline_mode=pl.Buffered(1))


def _params(n_axes):
    return pltpu.CompilerParams(dimension_semantics=("arbitrary",) * n_axes,
                                vmem_limit_bytes=VMEM_LIMIT_V7X)


def _rope_inv_row():
    half = RET_DK // 2
    inv = ROPE_BASE ** (-jnp.arange(half, dtype=F32) / half)
    return jnp.tile(inv, LANES // half)[None, :]


def _ret_log_decay():
    return np.log1p(-np.exp2(-5.0 - np.arange(RET_HEADS, dtype=np.float32))).astype(np.float32)


def _ret_tables():
    c = RET_CHUNK
    lg = _ret_log_decay()
    i = np.arange(c, dtype=np.float32)
    diff = i[:, None] - i[None, :]
    dmask = np.where(diff[None] >= 0, np.exp(np.maximum(diff, 0.0)[None] * lg[:, None, None]), 0.0)
    q_decay = np.exp((i + 1.0)[:, None] * lg[None, :])
    k_decay = np.exp((c - 1.0 - i)[:, None] * lg[None, :])
    qdec = np.broadcast_to(q_decay.T[:, :, None], (RET_HEADS, c, RET_DV))
    kdec = np.repeat(k_decay, RET_DK, axis=1).reshape(c, RET_PAIRS, 2 * RET_DK).transpose(1, 0, 2)
    gc = np.repeat(np.exp(c * lg), RET_DV).reshape(RET_PAIRS, 1, 2 * RET_DV)
    return tuple(jnp.asarray(np.ascontiguousarray(t), F32) for t in (dmask, qdec, kdec, gc))


def _rel_buckets():
    max_exact = REL_BUCKETS // 2
    out = []
    for dil in ATT_DILATIONS:
        d = np.arange(ATT_NK, dtype=np.int32) * dil
        log_ratio = (np.log(np.maximum(d, 1).astype(np.float32) / np.float32(max_exact))
                     / np.float32(math.log(REL_MAX_DIST / max_exact)))
        large = max_exact + (log_ratio * np.float32(REL_BUCKETS - max_exact)).astype(np.int32)
        out.append(np.where(d < max_exact, d, np.minimum(large, REL_BUCKETS - 1)))
    return np.stack(out)


def _band_select():
    nk = ATT_NK
    buckets = _rel_buckets()
    onehot = np.zeros((ATT_GROUPS, REL_BUCKETS, 3 * nk), np.float32)
    for g in range(ATT_GROUPS):
        for k in range(nk, 2 * nk):
            onehot[g, buckets[g, 2 * nk - 1 - k], k] = 1.0
    mask = np.full((1, 3 * nk), -np.inf, np.float32)
    mask[0, nk:2 * nk] = 0.0
    return jnp.asarray(onehot, BF16), jnp.asarray(mask, F32)


def _slot_select():
    buckets = _rel_buckets()
    onehot = np.zeros((ATT_GROUPS, REL_BUCKETS, ATT_NK), np.float32)
    for g in range(ATT_GROUPS):
        for slot in range(ATT_NK):
            onehot[g, buckets[g, 0 if slot == 0 else ATT_NK - slot], slot] = 1.0
    return jnp.asarray(onehot, BF16)


def _split3(x):
    hi = x.astype(BF16)
    rem = x - hi.astype(F32)
    mid = rem.astype(BF16)
    lo = (rem - mid.astype(F32)).astype(BF16)
    return hi, mid, lo


def _inproj_ret_body(x_ref, ln_ref, w_ref, wg0_ref, wg1_ref, wg2_ref, wg3_ref, inv_ref,
                     q_ref, k_ref, v_ref, g_ref, gr_ref, ga_ref, cos_s, sin_s,
                     *, tm, tiles, fixed_pos):
    i = pl.program_id(0)
    lane = lax.broadcasted_iota(jnp.int32, (tm, LANES), 1)
    first_half = (lane % RET_DK) < (RET_DK // 2)

    def tables(pos):
        ang = pos.astype(F32) * inv_ref[...]
        sin = jnp.sin(ang)
        return jnp.cos(ang), jnp.where(first_half, -sin, sin)

    if fixed_pos is None:
        trow = pl.ds(pl.multiple_of(lax.rem(i, tiles) * tm, tm), tm)

        @pl.when(i < tiles)
        def _():
            cos_s[trow, :], sin_s[trow, :] = tables(i * tm + lax.broadcasted_iota(jnp.int32, (tm, LANES), 0))
        cos = cos_s[trow, :]
        sin = sin_s[trow, :]
    else:
        cos, sin = tables(jnp.full((tm, LANES), fixed_pos, jnp.int32))
    x = x_ref[...]
    h = _rms(x, ln_ref[...]).astype(BF16)
    qk = _dot(h, w_ref[:, 0:2 * RET_QK_W])
    n_q = RET_QK_W // LANES
    for c in range(2 * n_q):
        xc = qk[:, c * LANES:(c + 1) * LANES]
        swapped = jnp.where(first_half, pltpu.roll(xc, LANES - RET_DK // 2, 1), pltpu.roll(xc, RET_DK // 2, 1))
        r = xc * cos + swapped * sin
        if c < n_q:
            q_ref[:, c * LANES:(c + 1) * LANES] = r.astype(q_ref.dtype)
        else:
            k_ref[:, (c - n_q) * LANES:(c - n_q + 1) * LANES] = (r * (RET_DK ** -0.5)).astype(k_ref.dtype)
    o = 2 * RET_QK_W
    v_ref[...] = _dot(h, w_ref[:, o:o + RET_V_W]).astype(v_ref.dtype)
    o += RET_V_W
    g_ref[...] = _dot(h, w_ref[:, o:o + RET_V_W]).astype(g_ref.dtype)
    half = D_MODEL // 2
    gr_ref[:, :half] = _dot(h, wg0_ref[...]).astype(gr_ref.dtype)
    gr_ref[:, half:] = _dot(h, wg1_ref[...]).astype(gr_ref.dtype)
    ga_ref[:, :half] = _dot(h, wg2_ref[...]).astype(ga_ref.dtype)
    ga_ref[:, half:] = _dot(h, wg3_ref[...]).astype(ga_ref.dtype)


def _inproj_ret(x2d, ln, w_in, inv_row, tm, out_dtype, seq=None, fixed_pos=None):
    n = x2d.shape[0]
    tiles = None if seq is None else seq // tm
    table_rows = tm if seq is None else seq
    row = lambda width: pl.BlockSpec((tm, width), lambda i: (i, 0))
    widths = (RET_QK_W, RET_QK_W, RET_V_W, RET_V_W, D_MODEL, D_MODEL)
    half = D_MODEL // 2
    assert COL_GATE % half == 0
    return pl.pallas_call(
        functools.partial(_inproj_ret_body, tm=tm, tiles=tiles, fixed_pos=fixed_pos),
        grid=(n // tm,),
        in_specs=[row(D_MODEL), _resident((1, D_MODEL)), _col_window(D_MODEL, COL_ATT, 0)]
        + [_col_window(D_MODEL, half, COL_GATE // half + j) for j in range(4)] + [_resident((1, LANES))],
        out_specs=[row(wd) for wd in widths],
        out_shape=[jax.ShapeDtypeStruct((n, wd), out_dtype) for wd in widths],
        scratch_shapes=[pltpu.VMEM((table_rows, LANES), F32), pltpu.VMEM((table_rows, LANES), F32)],
        compiler_params=_params(1),
        name="inproj_ret",
    )(x2d, ln, w_in, w_in, w_in, w_in, w_in, inv_row)


def _inproj_att_body(x_ref, ln_ref, wq_ref, wk_ref, wv_ref, *refs, tm, keeps, seq):
    lowp = refs[:3 * ATT_GROUPS]
    full = refs[3 * ATT_GROUPS:5 * ATT_GROUPS]
    res = refs[5 * ATT_GROUPS]
    h = _rms(x_ref[...], ln_ref[...]).astype(BF16)
    slot = 0
    for g in reversed(range(ATT_GROUPS)):
        for kind in reversed(range(3)):
            dil = ATT_DILATIONS[g]
            dst = lowp[kind * ATT_GROUPS + g]
            r = _dot(h, (wq_ref, wk_ref, wv_ref)[kind][:, g * ATT_GW:(g + 1) * ATT_GW])
            if dil == 1:
                dst[0] = r.astype(dst.dtype)
            else:
                for hh in range(ATT_HPG):
                    res[slot, hh] = r[:, hh * ATT_HD:(hh + 1) * ATT_HD]
                for rr in range(dil):
                    for hh in range(ATT_HPG):
                        dst[rr, :, hh * ATT_HD:(hh + 1) * ATT_HD] = (
                            res[slot, hh, pl.ds(rr, tm // dil, stride=dil), :].astype(dst.dtype))
                slot += 1
            if kind > 0:
                cache = full[(kind - 1) * ATT_GROUPS + g]
                rows = tm if keeps[g] == seq else keeps[g]
                for hh in range(ATT_HPG):
                    cache[pl.ds(hh, rows, stride=ATT_HPG), :] = r[tm - rows:, hh * ATT_HD:(hh + 1) * ATT_HD]


def _inproj_att_prompt(x2d, ln, w_in, tm, seq):
    n = x2d.shape[0]
    batch = n // seq
    tiles = seq // tm
    keeps = tuple(min(wd, seq) for wd in ATT_WINDOWS)
    assert all(kp <= tm or kp == seq for kp in keeps) and seq % tm == 0
    out_specs, out_shape = [], []
    for _ in range(3):
        for g in range(ATT_GROUPS):
            dil = ATT_DILATIONS[g]
            out_specs.append(pl.BlockSpec((None, dil, tm // dil, ATT_GW), lambda i: (i // tiles, 0, i % tiles, 0)))
            out_shape.append(jax.ShapeDtypeStruct((batch, dil, seq // dil, ATT_GW), BF16))
    for _ in range(2):
        for g in range(ATT_GROUPS):
            if keeps[g] == seq:
                idx = lambda i: (i // tiles, i % tiles, 0)
                rows = tm
            else:
                idx = lambda i: (i // tiles, 0, 0)
                rows = keeps[g]
            out_specs.append(pl.BlockSpec((None, rows * ATT_HPG, ATT_HD), idx))
            out_shape.append(jax.ShapeDtypeStruct((batch, keeps[g] * ATT_HPG, ATT_HD), F32))
    return pl.pallas_call(
        functools.partial(_inproj_att_body, tm=tm, keeps=keeps, seq=seq),
        grid=(n // tm,),
        in_specs=[pl.BlockSpec((tm, D_MODEL), lambda i: (i, 0)), _resident((1, D_MODEL))]
        + [_col_window(D_MODEL, ATT_W, COL_ATT // ATT_W + kind) for kind in range(3)],
        out_specs=out_specs,
        out_shape=out_shape,
        scratch_shapes=[pltpu.VMEM((3 * sum(d > 1 for d in ATT_DILATIONS), ATT_HPG, tm, ATT_HD), F32)],
        compiler_params=_params(1),
        name="inproj_att",
    )(x2d, ln, w_in, w_in, w_in)


def _inproj_att_sample_body(x_ref, ln_ref, wq_ref, wk_ref, wv_ref, o_ref):
    h = _rms(x_ref[...], ln_ref[...]).astype(BF16)
    for kind, w_ref in enumerate((wq_ref, wk_ref, wv_ref)):
        for g in range(ATT_GROUPS):
            c = kind * ATT_GROUPS + g
            o_ref[:, c * ATT_GW:(c + 1) * ATT_GW] = _dot(h, w_ref[:, g * ATT_GW:(g + 1) * ATT_GW])


def _inproj_att_sample(x2d, ln, w_in):
    n = x2d.shape[0]
    return pl.pallas_call(
        _inproj_att_sample_body,
        grid=(1,),
        in_specs=[pl.BlockSpec((n, D_MODEL), lambda i: (0, 0)), _resident((1, D_MODEL))]
        + [_col_window(D_MODEL, ATT_W, COL_ATT // ATT_W + kind) for kind in range(3)],
        out_specs=pl.BlockSpec((n, 3 * ATT_W), lambda i: (0, 0)),
        out_shape=jax.ShapeDtypeStruct((n, 3 * ATT_W), F32),
        compiler_params=_params(1),
        name="inproj_att_sample",
    )(x2d, ln, w_in, w_in, w_in)


def _gn_swish(o, gate, gn):
    mu = jnp.mean(o, axis=-1, keepdims=True)
    d = o - mu
    var = jnp.mean(d * d, axis=-1, keepdims=True)
    on = d * lax.rsqrt(var + NORM_EPS) * gn
    return gate * _sigmoid(gate) * on


def _retention_body(q_ref, k_ref, v_ref, g_ref, gn_ref, dm_ref, qdec_ref, kdec_ref, gc_ref,
                    out_ref, st_ref, state, *, n_chunks):
    c = RET_CHUNK

    @pl.when(pl.program_id(1) == 0)
    def _():
        state[...] = jnp.zeros_like(state)

    lane = lax.broadcasted_iota(jnp.int32, (c, 2 * RET_DK), 1)
    head0 = lane < RET_DK

    def chunk(ci, carry):
        rows = pl.ds(pl.multiple_of(ci * c, c), c)
        scores, cross, values = {}, {}, {}
        for p in range(RET_PAIRS):
            q2 = q_ref[rows, p * 2 * RET_DK:(p + 1) * 2 * RET_DK]
            k2 = k_ref[rows, p * 2 * RET_DK:(p + 1) * 2 * RET_DK]
            v2 = v_ref[rows, p * 2 * RET_DV:(p + 1) * 2 * RET_DV]
            pst = state[p]
            pst_lo = pst.astype(BF16)
            zero = jnp.zeros_like(q2)
            for hh in range(2):
                h = 2 * p + hh
                qm = jnp.where(head0 if hh == 0 else jnp.logical_not(head0), q2, zero)
                values[h] = v2[:, hh * RET_DV:(hh + 1) * RET_DV]
                scores[h] = _dot_nt(qm, k2)
                cross[h] = _dot(qm, pst_lo[:, hh * RET_DV:(hh + 1) * RET_DV])
            kd = (k2.astype(F32) * kdec_ref[p]).astype(BF16)
            state[p] = pst * gc_ref[p] + _dot_tn(kd, v2)
        for h in range(RET_HEADS):
            o = _dot((scores[h] * dm_ref[h]).astype(BF16), values[h]) + cross[h] * qdec_ref[h]
            gate = g_ref[rows, h * RET_DV:(h + 1) * RET_DV].astype(F32)
            res = _gn_swish(o, gate, gn_ref[:, h * RET_DV:(h + 1) * RET_DV])
            out_ref[rows, h * RET_DV:(h + 1) * RET_DV] = res.astype(out_ref.dtype)
        return carry

    lax.fori_loop(0, n_chunks, chunk, 0, unroll=True)
    for p in range(RET_PAIRS):
        pst = state[p]
        for hh in range(2):
            st_ref[2 * p + hh] = pst[hh * RET_DK:(hh + 1) * RET_DK, hh * RET_DV:(hh + 1) * RET_DV]


def _retention_prompt(q, k, v, g, gn, batch, seq, ts):
    n = q.shape[0]
    steps = seq // ts
    dmask, qdec, kdec, gc = _ret_tables()
    row = lambda width: pl.BlockSpec((ts, width), lambda b, s: (b * steps + s, 0))
    const = lambda a: pl.BlockSpec(a.shape, lambda b, s: (0,) * a.ndim)
    return pl.pallas_call(
        functools.partial(_retention_body, n_chunks=ts // RET_CHUNK),
        grid=(batch, steps),
        in_specs=[row(RET_QK_W), row(RET_QK_W), row(RET_V_W), row(RET_V_W), const(gn),
                  const(dmask), const(qdec), const(kdec), const(gc)],
        out_specs=[row(RET_V_W),
                   pl.BlockSpec((None, RET_HEADS, RET_DK, RET_DV), lambda b, s: (b, 0, 0, 0))],
        out_shape=[jax.ShapeDtypeStruct((n, RET_V_W), BF16),
                   jax.ShapeDtypeStruct((batch, RET_HEADS, RET_DK, RET_DV), F32)],
        scratch_shapes=[pltpu.VMEM((RET_PAIRS, 2 * RET_DK, 2 * RET_DV), F32)],
        compiler_params=_params(2),
        name="retention",
    )(q, k, v, g, gn, dmask, qdec, kdec, gc)


def _attn_body(q_ref, k_ref, v_ref, rb_ref, sel_ref, neg_ref, o_ref, lse_ref, *scratch, group, dil, n_blocks):
    nk = ATT_NK
    scale = ATT_HD ** -0.5
    lane = lax.broadcasted_iota(jnp.int32, (nk, LANES), 1)
    tabs = scratch[-1]

    @pl.when(pl.program_id(0) == 0)
    def _():
        band = neg_ref[...]
        for piece in _split3(rb_ref[group * ATT_HPG:(group + 1) * ATT_HPG, :]):
            band = band + _dot(piece, sel_ref[...])
        col = lax.broadcasted_iota(jnp.int32, (1, 2 * nk), 1)
        for i in range(nk):
            window = band[:, nk - 1 - i:3 * nk - 1 - i]
            for h in range(ATT_HPG):
                tabs[1, h, i:i + 1, :] = window[h:h + 1, :]
                tabs[0, h, i:i + 1, :] = jnp.where(col < nk, -jnp.inf, window[h:h + 1, :])
    if dil > 1:
        o_s, l_s = scratch[:2]

    def chain(s_raw, v_ext, sel, h):
        s = s_raw * scale + (tabs[sel, h] if n_blocks > 1 else tabs[1, h, :, nk:])
        m = jnp.max(s, axis=-1, keepdims=True)
        p = jnp.exp(s - m)
        pv = _dot(p.astype(BF16), v_ext)
        den = pv[:, ATT_HD:]
        return (pv[:, :ATT_HD] / den).astype(BF16), m + jnp.log(den)

    def write(rs, rows, outs):
        lse_tile = jnp.zeros((nk, LANES), F32)
        for h, (o, lse) in enumerate(outs):
            hc = slice(h * ATT_HD, (h + 1) * ATT_HD)
            if dil == 1:
                o_ref[rows, hc] = o
            else:
                o_s[rs, rows, hc] = o
            lse_tile = jnp.where(lane == h, lse, lse_tile)
        if dil == 1:
            lse_ref[rows, :] = lse_tile
        else:
            l_s[rs, rows, :] = lse_tile

    heads = [slice(h * ATT_HD, (h + 1) * ATT_HD) for h in range(ATT_HPG)]

    def block(rs, n, n_prev, sel):
        rows = pl.ds(pl.multiple_of(n * nk, nk), nk)
        prev_rows = pl.ds(pl.multiple_of(n_prev * nk, nk), nk)
        q = q_ref[rs, rows, :]
        k_all = jnp.concatenate([k_ref[rs, prev_rows, :], k_ref[rs, rows, :]], axis=0)
        v_all = jnp.concatenate([v_ref[rs, prev_rows, :], v_ref[rs, rows, :]], axis=0)
        ones = jnp.ones((2 * nk, ATT_HD), BF16)
        outs = []
        for h, hc in enumerate(heads):
            v_ext = jnp.concatenate([v_all[:, hc], ones], axis=1)
            outs.append(chain(_dot_nt(q[:, hc], k_all[:, hc]), v_ext, sel, h))
        write(rs, rows, outs)

    def single_blocks(streams):
        ones = jnp.ones((nk, ATT_HD), BF16)
        scores = [[_dot_nt(q_ref[rs, :, hc], k_ref[rs, :, hc]) for hc in heads] for rs in streams]
        for i, rs in enumerate(streams):
            outs = [chain(scores[i][h], jnp.concatenate([v_ref[rs, :, hc], ones], axis=1), 0, h)
                    for h, hc in enumerate(heads)]
            write(rs, slice(None), outs)

    if n_blocks == 1:
        for r0 in range(0, dil, SINGLE_BLOCK_GROUP):
            single_blocks(range(r0, min(r0 + SINGLE_BLOCK_GROUP, dil)))
    else:
        for rs in range(dil):
            def loop(n, carry, rs=rs):
                block(rs, n, jnp.maximum(n - 1, 0), jnp.minimum(n, 1))
                return carry
            lax.fori_loop(0, n_blocks, loop, 0, unroll=ATTN_UNROLL)

    if dil > 1:
        piece = PERM_ROWS // dil
        row = lax.broadcasted_iota(jnp.int32, (PERM_ROWS, PERM_ROWS), 0)
        col = lax.broadcasted_iota(jnp.int32, (PERM_ROWS, PERM_ROWS), 1)
        perm = jnp.where((col % piece) * dil + col // piece == row, 1.0, 0.0).astype(BF16)
        for j in range(o_ref.shape[0] // PERM_ROWS):
            src = slice(j * piece, (j + 1) * piece)
            dst = slice(j * PERM_ROWS, (j + 1) * PERM_ROWS)
            o_ref[dst, :] = _dot(perm, jnp.concatenate([o_s[r, src, :] for r in range(dil)], axis=0)).astype(BF16)
            lse = jnp.concatenate([l_s[r, src, :] for r in range(dil)], axis=0)
            hi = lse.astype(BF16)
            rem = lse - hi.astype(F32)
            mid = rem.astype(BF16)
            lo = (rem - mid.astype(F32)).astype(BF16)
            lse_ref[dst, :] = _dot(perm, hi) + _dot(perm, mid) + _dot(perm, lo)


def _attention_prompt(aq, ak, av, rel_bias_t, onehot, neg, g):
    batch, dil, length, _ = aq.shape
    seq = dil * length
    n_blocks = length // ATT_NK
    blk = pl.BlockSpec((None, dil, length, ATT_GW), lambda b: (b, 0, 0, 0))
    scratch = [pltpu.VMEM((dil, length, ATT_GW), BF16), pltpu.VMEM((dil, length, LANES), F32)] if dil > 1 else []
    scratch.append(pltpu.VMEM((2, ATT_HPG, ATT_NK, 2 * ATT_NK), F32))
    o, lse = pl.pallas_call(
        functools.partial(_attn_body, group=g, dil=dil, n_blocks=n_blocks),
        grid=(batch,),
        in_specs=[blk, blk, blk, pl.BlockSpec(rel_bias_t.shape, lambda b: (0, 0)),
                  pl.BlockSpec((None, REL_BUCKETS, 3 * ATT_NK), lambda b: (g, 0, 0)),
                  pl.BlockSpec((1, 3 * ATT_NK), lambda b: (0, 0))],
        out_specs=[pl.BlockSpec((None, seq, ATT_GW), lambda b: (b, 0, 0)),
                   pl.BlockSpec((None, seq, LANES), lambda b: (b, 0, 0))],
        out_shape=[jax.ShapeDtypeStruct((batch, seq, ATT_GW), BF16),
                   jax.ShapeDtypeStruct((batch, seq, LANES), F32)],
        scratch_shapes=scratch,
        compiler_params=_params(1),
        name="attention_g%d" % g,
    )(aq, ak, av, rel_bias_t, onehot, neg)
    return o.reshape(batch * seq, ATT_GW), lse.reshape(batch * seq, LANES)


def _tail_math(act_refs, w_refs, y_ref, combine, middle=None):
    wrb_ref, wab_ref, wo_ref, wu_ref, wd_ref, wpl_ref, wpg_ref, ln2_ref, lnf_ref = w_refs
    if combine:
        ret_ref, o0_ref, o1_ref, o2_ref, l0_ref, l1_ref, l2_ref, gr_ref, ga_ref, x_ref, ple_ref = act_refs
        lses = [l0_ref[...], l1_ref[...], l2_ref[...]]
        outs = [o0_ref, o1_ref, o2_ref]
        parts = []
        for h in range(ATT_HPG):
            lh = [l[:, h:h + 1] for l in lses]
            mx = jnp.maximum(jnp.maximum(lh[0], lh[1]), lh[2])
            e = [jnp.exp(l - mx) for l in lh]
            tot = e[0] + e[1] + e[2]
            acc = None
            for g in range(ATT_GROUPS):
                term = (e[g] / tot) * outs[g][:, h * ATT_HD:(h + 1) * ATT_HD].astype(F32)
                acc = term if acc is None else acc + term
            parts.append(acc)
        att = jnp.concatenate(parts, axis=1).astype(BF16)
    else:
        ret_ref, att_ref, gr_ref, ga_ref, x_ref, ple_ref = act_refs
        att = att_ref[...].astype(BF16)
    a = _dot(ret_ref[...].astype(BF16), wrb_ref[...])
    b = _dot(att, wab_ref[...])
    mixed = _sigmoid(gr_ref[...].astype(F32)) * a + _sigmoid(ga_ref[...].astype(F32)) * b
    x1 = x_ref[...] + _dot(mixed.astype(BF16), wo_ref[...])
    h2 = _rms(x1, ln2_ref[...]).astype(BF16)
    ff_chunk = D_MODEL
    acc = None
    for c in range(D_FF // ff_chunk):
        u = _dot(h2, wu_ref[:, c * ff_chunk:(c + 1) * ff_chunk])
        r = jnp.maximum(u, 0.0)
        t = _dot((r * r).astype(BF16), wd_ref[c * ff_chunk:(c + 1) * ff_chunk, :])
        acc = t if acc is None else acc + t
        if middle is not None and c == D_FF // ff_chunk // 2 - 1:
            middle()
    x2 = x1 + acc
    gate = _sigmoid(_dot(x2.astype(BF16), wpg_ref[...]))
    x3 = x2 + gate * _dot(ple_ref[...].astype(BF16), wpl_ref[...])
    y_ref[...] = _rms(x3, lnf_ref[...])


def _tail_body(*refs, n_act, combine):
    _tail_math(refs[:n_act], refs[n_act:-1], refs[-1], combine)


def _tail(acts, weights, tm, combine):
    n = acts[0].shape[0]
    row = lambda a: pl.BlockSpec((tm, a.shape[1]), lambda i: (i, 0))
    return pl.pallas_call(
        functools.partial(_tail_body, n_act=len(acts), combine=combine),
        grid=(n // tm,),
        in_specs=[row(a) for a in acts] + [_resident(w.shape) for w in weights],
        out_specs=pl.BlockSpec((tm, D_MODEL), lambda i: (i, 0)),
        out_shape=jax.ShapeDtypeStruct((n, D_MODEL), F32),
        compiler_params=_params(1),
        name="tail" if combine else "tail_sample",
    )(*acts, *weights)


def _tail_shift_body(*refs, n_act, n_w, n_steps):
    nc = 2 * ATT_GROUPS
    acts = refs[:n_act]
    ws = refs[n_act:n_act + n_w]
    new_ref = refs[n_act + n_w]
    base = n_act + n_w + 1
    old = refs[base:base + nc]
    y_ref = refs[base + nc]
    out = refs[base + nc + 1:base + 2 * nc + 1]
    stage = refs[base + 2 * nc + 1:base + 3 * nc + 1]
    sem_in, sem_out, sem_row = refs[base + 3 * nc + 1:]
    s = pl.program_id(0)
    keys = [2 * g for g in range(ATT_GROUPS)]
    values = [2 * g + 1 for g in range(ATT_GROUPS)]

    def copy_in(i, seq):
        width = old[i].shape[1]
        return pltpu.make_async_copy(old[i].at[seq, pl.ds(1, width - 1)], stage[i], sem_in.at[i])

    def copy_out(i, seq):
        width = old[i].shape[1]
        return pltpu.make_async_copy(stage[i], out[i].at[seq, pl.ds(0, width - 1)], sem_out.at[i])

    def copy_row(i):
        g, kind = divmod(i, 2)
        return pltpu.make_async_copy(new_ref.at[s, kind + 1, g], out[i].at[s, old[i].shape[1] - 1], sem_row.at[i])

    @pl.when(s == 0)
    def _():
        for i in keys:
            copy_in(i, 0).start()
    for i in keys:
        copy_in(i, s).wait()
    for i in keys:
        copy_out(i, s).start()

    @pl.when(s > 0)
    def _():
        for i in values:
            copy_out(i, s - 1).wait()
    for i in values:
        copy_in(i, s).start()
    for i in range(nc):
        copy_row(i).start()

    def middle():
        for i in values:
            copy_in(i, s).wait()
        for i in values:
            copy_out(i, s).start()
        for i in keys:
            copy_out(i, s).wait()

        @pl.when(s < n_steps - 1)
        def _():
            for i in keys:
                copy_in(i, s + 1).start()

    _tail_math(acts, ws, y_ref, True, middle)

    for i in range(nc):
        copy_row(i).wait()

    @pl.when(s == n_steps - 1)
    def _():
        for i in values:
            copy_out(i, s).wait()


def _tail_shift(acts, weights, tm, new_qkv, caches):
    n = acts[0].shape[0]
    n_steps = n // tm
    assert n_steps == new_qkv.shape[0]
    row = lambda a: pl.BlockSpec((tm, a.shape[1]), lambda i: (i, 0))
    anyspec = pl.BlockSpec(memory_space=pl.ANY)
    nc = len(caches)
    new_rows = new_qkv.reshape(new_qkv.shape[0], 3, ATT_GROUPS, ATT_HPG, ATT_HD)
    res = pl.pallas_call(
        functools.partial(_tail_shift_body, n_act=len(acts), n_w=len(weights), n_steps=n_steps),
        grid=(n_steps,),
        in_specs=[row(a) for a in acts] + [_resident(w.shape) for w in weights] + [anyspec] * (nc + 1),
        out_specs=[pl.BlockSpec((tm, D_MODEL), lambda i: (i, 0))] + [anyspec] * nc,
        out_shape=[jax.ShapeDtypeStruct((n, D_MODEL), F32)]
        + [jax.ShapeDtypeStruct(c.shape, c.dtype) for c in caches],
        scratch_shapes=[pltpu.VMEM((c.shape[1] - 1, ATT_HPG, ATT_HD), F32) for c in caches]
        + [pltpu.SemaphoreType.DMA((nc,)), pltpu.SemaphoreType.DMA((nc,)), pltpu.SemaphoreType.DMA((nc,))],
        compiler_params=pltpu.CompilerParams(dimension_semantics=("arbitrary",),
                                             vmem_limit_bytes=VMEM_LIMIT_TAIL_V7X),
        name="tail",
    )(*acts, *weights, new_rows, *caches)
    return res[0], res[1:]


def _decode_body(q_ref, k_ref, v_ref, g_ref, gn_ref, st_ref, gam_ref,
                 qkv_ref, rb_ref, slot_ref,
                 ck0_ref, cv0_ref, ck1_ref, cv1_ref, ck2_ref, cv2_ref,
                 ret_ref, att_ref, nst_ref, slot_bias, *, seqs):
    sub = lax.broadcasted_iota(jnp.int32, (8, 2 * RET_DK), 0)
    lane = lax.broadcasted_iota(jnp.int32, (8, 2 * RET_DK), 1)
    row0 = sub == 0
    srow = lax.broadcasted_iota(jnp.int32, (2 * RET_DK, RET_DV), 0)
    for i in range(seqs):
        for p in range(RET_PAIRS):
            pc = slice(p * 2 * RET_DK, (p + 1) * 2 * RET_DK)
            q2 = jnp.where(row0, jnp.broadcast_to(q_ref[i, :, pc], (8, 2 * RET_DK)), 0.0)
            k2 = jnp.where(row0, jnp.broadcast_to(k_ref[i, :, pc], (8, 2 * RET_DK)), 0.0)
            pst = st_ref[i, p]
            gam = gam_ref[p]
            outer = []
            for hh in range(2):
                h = 2 * p + hh
                hc = slice(h * RET_DV, (h + 1) * RET_DV)
                hsel = (lane < RET_DK) if hh == 0 else (lane >= RET_DK)
                qm = jnp.where(hsel, q2, 0.0)
                km = jnp.where(hsel, k2, 0.0)
                vh = v_ref[i, :, hc]
                v8 = jnp.where(row0[:, :RET_DV], jnp.broadcast_to(vh, (8, RET_DV)), 0.0)
                cross = _dot(qm.astype(BF16), (pst * gam).astype(BF16))[0:1, :]
                qk = jnp.sum(qm[0:1, :] * km[0:1, :], axis=-1, keepdims=True)
                o = cross + qk * vh
                ret_ref[i, :, hc] = _gn_swish(o, g_ref[i, :, hc], gn_ref[:, hc])
                outer.append(_dot_tn(k2.astype(BF16), v8.astype(BF16)))
            nst_ref[i, p] = pst * gam + jnp.where(srow < RET_DK, outer[0], outer[1])
    scale = ATT_HD ** -0.5
    caches = ((ck0_ref, cv0_ref), (ck1_ref, cv1_ref), (ck2_ref, cv2_ref))
    slot = lax.broadcasted_iota(jnp.int32, (ATT_NK, ATT_HD), 0)
    is_new = slot == 0
    first = row0[:, :ATT_HD]

    @pl.when(pl.program_id(0) == 0)
    def _():
        pieces = _split3(rb_ref[...])
        for g in range(ATT_GROUPS):
            rows = slice(g * ATT_HPG, (g + 1) * ATT_HPG)
            slot_bias[rows, :] = (_dot(pieces[0][rows], slot_ref[g]) + _dot(pieces[1][rows], slot_ref[g])
                                  + _dot(pieces[2][rows], slot_ref[g]))

    def one_row(x):
        return jnp.where(first, jnp.broadcast_to(x, (8, ATT_HD)), 0.0).astype(BF16)

    scores = {}
    for i in range(seqs):
        for h in range(ATT_HPG):
            for g in range(ATT_GROUPS):
                c0 = g * ATT_GW + h * ATT_HD
                kk = jnp.where(is_new, qkv_ref[i, :, ATT_W + c0:ATT_W + c0 + ATT_HD], caches[g][0][i, :, h, :])
                scores[i, g, h] = _dot_nt(one_row(qkv_ref[i, :, c0:c0 + ATT_HD]), kk.astype(BF16))[0:1, :]
    for i in range(seqs):
        for h in range(ATT_HPG):
            o_g, lse_g = [], []
            for g in range(ATT_GROUPS):
                c0 = g * ATT_GW + h * ATT_HD
                vv = jnp.where(is_new, qkv_ref[i, :, 2 * ATT_W + c0:2 * ATT_W + c0 + ATT_HD],
                               caches[g][1][i, :, h, :])
                gh = g * ATT_HPG + h
                s = scores[i, g, h] * scale + slot_bias[gh:gh + 1, :]
                m = jnp.max(s, axis=-1, keepdims=True)
                pr = jnp.exp(s - m)
                den = jnp.sum(pr, axis=-1, keepdims=True)
                o_g.append(_dot(one_row(pr), vv.astype(BF16))[0:1, :] / den)
                lse_g.append(m + jnp.log(den))
            mx = jnp.maximum(jnp.maximum(lse_g[0], lse_g[1]), lse_g[2])
            e = [jnp.exp(l - mx) for l in lse_g]
            tot = e[0] + e[1] + e[2]
            att_ref[i, :, h * ATT_HD:(h + 1) * ATT_HD] = ((e[0] / tot) * o_g[0] + (e[1] / tot) * o_g[1]
                                                          + (e[2] / tot) * o_g[2])


def _decode(q, k, v, g, gn, state, qkv, rel_bias_t, caches):
    batch = q.shape[0]
    gam = jnp.asarray(np.ascontiguousarray(np.broadcast_to(
        np.repeat(np.exp(_ret_log_decay()), RET_DK).reshape(RET_PAIRS, 2 * RET_DK, 1),
        (RET_PAIRS, 2 * RET_DK, RET_DV))), F32)
    vec = lambda a: a.reshape(batch, 1, a.shape[1])
    seqs = DECODE_SEQS
    assert batch % seqs == 0
    vspec = lambda width: pl.BlockSpec((seqs, 1, width), lambda b: (b, 0, 0))
    const = lambda a: pl.BlockSpec(a.shape, lambda b: (0,) * a.ndim)
    st_spec = pl.BlockSpec((seqs, RET_PAIRS, 2 * RET_DK, RET_DV), lambda b: (b, 0, 0, 0))
    cache_in, cache_specs = [], []
    for gi in range(ATT_GROUPS):
        dil = ATT_DILATIONS[gi]
        for c in caches[2 * gi:2 * gi + 2]:
            cache_in.append(c.reshape(batch, ATT_NK, dil, ATT_HPG, ATT_HD))
            cache_specs.append(pl.BlockSpec((seqs, ATT_NK, None, ATT_HPG, ATT_HD), lambda b: (b, 0, 0, 0, 0)))
    st_pairs = state.reshape(batch, RET_PAIRS, 2 * RET_DK, RET_DV)
    slot_sel = _slot_select()
    ret, att, nst = pl.pallas_call(
        functools.partial(_decode_body, seqs=seqs),
        grid=(batch // seqs,),
        in_specs=[vspec(RET_QK_W), vspec(RET_QK_W), vspec(RET_V_W), vspec(RET_V_W), const(gn), st_spec, const(gam),
                  vspec(3 * ATT_W), const(rel_bias_t), const(slot_sel)] + cache_specs,
        out_specs=[vspec(RET_V_W), vspec(ATT_GW), st_spec],
        out_shape=[jax.ShapeDtypeStruct((batch, 1, RET_V_W), F32),
                   jax.ShapeDtypeStruct((batch, 1, ATT_GW), F32),
                   jax.ShapeDtypeStruct(st_pairs.shape, F32)],
        scratch_shapes=[pltpu.VMEM((rel_bias_t.shape[0], ATT_NK), F32)],
        compiler_params=_params(1),
        name="decode",
    )(vec(q), vec(k), vec(v), vec(g), gn, st_pairs, gam, vec(qkv), rel_bias_t, slot_sel, *cache_in)
    return ret.reshape(batch, RET_V_W), att.reshape(batch, ATT_GW), nst.reshape(state.shape)


def kernel(x_prompt, x_sample, state_ret, cache_k_w128, cache_v_w128, cache_k_w512, cache_v_w512,
           cache_k_w2048, cache_v_w2048, p_prompt, p_sample, ln1_g, w_in, ret_gn_g, w_ret_br, w_att_br,
           w_out, ln2_g, w_up, w_down, w_ple, w_ple_gate, rel_bias, lnf_g):
    depth = w_in.shape[0]
    assert depth == 1
    batch, seq, _ = x_prompt.shape
    dec_batch, dec_seq, _ = x_sample.shape
    assert dec_seq == 1
    past_len = 16384
    l = 0
    ln1 = ln1_g[l][None, :]
    ln2 = ln2_g[l][None, :]
    lnf = lnf_g[None, :]
    gn = ret_gn_g[l][None, :]
    w_in_lo = w_in[l].astype(BF16)
    tail_w = (w_ret_br[l].astype(BF16), w_att_br[l].astype(BF16), w_out[l].astype(BF16), w_up[l].astype(BF16),
              w_down[l].astype(BF16), w_ple[l].astype(BF16), w_ple_gate[l].astype(BF16), ln2, lnf)
    inv_row = _rope_inv_row()

    caches = (cache_k_w128[l], cache_v_w128[l], cache_k_w512[l], cache_v_w512[l],
              cache_k_w2048[l], cache_v_w2048[l])

    xs = x_sample.reshape(dec_batch, D_MODEL)
    sq, sk, sv, sg, sgr, sga = _inproj_ret(xs, ln1, w_in_lo, inv_row, dec_batch, F32, fixed_pos=past_len)
    s_qkv = _inproj_att_sample(xs, ln1, w_in_lo)
    rel_bias_t = rel_bias.T
    s_ret, s_attn, new_st = _decode(sq, sk, sv, sg, gn, state_ret[l], s_qkv, rel_bias_t, caches)
    ple_s = p_sample[l].reshape(dec_batch, D_PLE)
    y_s = _tail([s_ret, s_attn, sgr, sga, xs, ple_s], tail_w, dec_batch, False)
    y_sample = y_s.reshape(dec_batch, 1, D_MODEL)

    xp = x_prompt.reshape(batch * seq, D_MODEL)
    rq, rk, rv, rg, gr, ga = _inproj_ret(xp, ln1, w_in_lo, inv_row, TM_INPROJ, BF16, seq=seq)
    att_o = _inproj_att_prompt(xp, ln1, w_in_lo, TM_INPROJ, seq)
    aqs, aks, avs = att_o[0:3], att_o[3:6], att_o[6:9]
    kfull, vfull = att_o[9:12], att_o[12:15]
    ret_out, st_p = _retention_prompt(rq, rk, rv, rg, gn, batch, seq, TS_RETENTION)
    outs, lses = [], []
    band_onehot, band_neg = _band_select()
    for g in range(ATT_GROUPS):
        o, lse = _attention_prompt(aqs[g], aks[g], avs[g], rel_bias_t, band_onehot, band_neg, g)
        outs.append(o)
        lses.append(lse)
    ple_p = p_prompt[l].reshape(batch * seq, D_PLE)
    tm_tail = batch * seq // dec_batch
    y_p, new_caches = _tail_shift([ret_out] + outs + lses + [gr, ga, xp, ple_p], tail_w, tm_tail, s_qkv, caches)
    y_prompt = y_p.reshape(batch, seq, D_MODEL)
    new_state_p = st_p[None]
    kv_p = []
    for g in range(ATT_GROUPS):
        shape = (1, batch, min(ATT_WINDOWS[g], seq), ATT_HPG, ATT_HD)
        kv_p.append(kfull[g].reshape(shape))
        kv_p.append(vfull[g].reshape(shape))
    kv_s = [c[None] for c in new_caches]

    return (y_prompt, y_sample, new_state_p, *kv_p, new_st[None], *kv_s)
```

```python
import functools
import math

import jax
import jax.numpy as jnp
import numpy as np
from jax import lax
from jax.experimental import pallas as pl
from jax.experimental.pallas import tpu as pltpu

F32 = jnp.float32
BF16 = jnp.bfloat16

D_MODEL = 1024
RET_HEADS = 8
RET_DK = 64
RET_DV = 128
RET_PAIRS = RET_HEADS // 2
RET_CHUNK = 128
ROPE_BASE = 10000.0
ATT_WINDOWS = (128, 512, 2048)
ATT_DILATIONS = (1, 4, 16)
ATT_GROUPS = 3
ATT_HPG = 4
ATT_HD = 128
ATT_NK = 128
ATT_GW = ATT_HPG * ATT_HD
REL_BUCKETS = 32
REL_MAX_DIST = 2048
D_FF = 4 * D_MODEL
D_PLE = 256
NORM_EPS = 1e-6
RET_QK_W = RET_HEADS * RET_DK
RET_V_W = RET_HEADS * RET_DV
ATT_W = ATT_GROUPS * ATT_GW
COL_RET = 0
COL_ATT = 2 * RET_QK_W + 2 * RET_V_W
COL_GATE = COL_ATT + 3 * ATT_W
N_IN = COL_GATE + 2 * D_MODEL

VMEM_LIMIT_V7X = 56 * 1024 * 1024
VMEM_LIMIT_TAIL_V7X = 62 * 1024 * 1024
LANES = 128
TM_INPROJ = 512
TS_RETENTION = 512
ATTN_UNROLL = 8
PERM_ROWS = 256
SINGLE_BLOCK_GROUP = 4
DECODE_SEQS = 4


def _dot(a, b):
    return jnp.dot(a, b, preferred_element_type=F32)


def _dot_nt(a, b):
    return lax.dot_general(a, b, (((1,), (1,)), ((), ())), preferred_element_type=F32)


def _dot_tn(a, b):
    return lax.dot_general(a, b, (((0,), (0,)), ((), ())), preferred_element_type=F32)


def _rms(x, g):
    return x * lax.rsqrt(jnp.mean(x * x, axis=-1, keepdims=True) + NORM_EPS) * g


def _sigmoid(x):
    return 1.0 / (1.0 + jnp.exp(-x))


def _resident(shape):
    return pl.BlockSpec(shape, lambda *_: (0,) * len(shape), pipeline_mode=pl.Buffered(1))


def _col_window(rows, width, block_index):
    return pl.BlockSpec((rows, width), lambda *_: (0, block_index), pipeline_mode=pl.Buffered(1))


def _params(n_axes):
    return pltpu.CompilerParams(dimension_semantics=("arbitrary",) * n_axes,
                                vmem_limit_bytes=VMEM_LIMIT_V7X)


def _rope_inv_row():
    half = RET_DK // 2
    inv = ROPE_BASE ** (-jnp.arange(half, dtype=F32) / half)
    return jnp.tile(inv, LANES // half)[None, :]


def _ret_log_decay():
    return np.log1p(-np.exp2(-5.0 - np.arange(RET_HEADS, dtype=np.float32))).astype(np.float32)


def _ret_tables():
    c = RET_CHUNK
    lg = _ret_log_decay()
    i = np.arange(c, dtype=np.float32)
    diff = i[:, None] - i[None, :]
    dmask = np.where(diff[None] >= 0, np.exp(np.maximum(diff, 0.0)[None] * lg[:, None, None]), 0.0)
    q_decay = np.exp((i + 1.0)[:, None] * lg[None, :])
    k_decay = np.exp((c - 1.0 - i)[:, None] * lg[None, :])
    qdec = np.broadcast_to(q_decay.T[:, :, None], (RET_HEADS, c, RET_DV))
    kdec = np.repeat(k_decay, RET_DK, axis=1).reshape(c, RET_PAIRS, 2 * RET_DK).transpose(1, 0, 2)
    gc = np.repeat(np.exp(c * lg), RET_DV).reshape(RET_PAIRS, 1, 2 * RET_DV)
    return tuple(jnp.asarray(np.ascontiguousarray(t), F32) for t in (dmask, qdec, kdec, gc))


def _rel_buckets():
    max_exact = REL_BUCKETS // 2
    out = []
    for dil in ATT_DILATIONS:
        d = np.arange(ATT_NK, dtype=np.int32) * dil
        log_ratio = (np.log(np.maximum(d, 1).astype(np.float32) / np.float32(max_exact))
                     / np.float32(math.log(REL_MAX_DIST / max_exact)))
        large = max_exact + (log_ratio * np.float32(REL_BUCKETS - max_exact)).astype(np.int32)
        out.append(np.where(d < max_exact, d, np.minimum(large, REL_BUCKETS - 1)))
    return np.stack(out)


def _band_select():
    nk = ATT_NK
    buckets = _rel_buckets()
    onehot = np.zeros((ATT_GROUPS, REL_BUCKETS, 3 * nk), np.float32)
    for g in range(ATT_GROUPS):
        for k in range(nk, 2 * nk):
            onehot[g, buckets[g, 2 * nk - 1 - k], k] = 1.0
    mask = np.full((1, 3 * nk), -np.inf, np.float32)
    mask[0, nk:2 * nk] = 0.0
    return jnp.asarray(onehot, BF16), jnp.asarray(mask, F32)


def _slot_select():
    buckets = _rel_buckets()
    onehot = np.zeros((ATT_GROUPS, REL_BUCKETS, ATT_NK), np.float32)
    for g in range(ATT_GROUPS):
        for slot in range(ATT_NK):
            onehot[g, buckets[g, 0 if slot == 0 else ATT_NK - slot], slot] = 1.0
    return jnp.asarray(onehot, BF16)


def _split3(x):
    hi = x.astype(BF16)
    rem = x - hi.astype(F32)
    mid = rem.astype(BF16)
    lo = (rem - mid.astype(F32)).astype(BF16)
    return hi, mid, lo


def _inproj_ret_body(x_ref, ln_ref, w_ref, wg0_ref, wg1_ref, wg2_ref, wg3_ref, inv_ref,
                     q_ref, k_ref, v_ref, g_ref, gr_ref, ga_ref, cos_s, sin_s,
                     *, tm, tiles, fixed_pos):
    i = pl.program_id(0)
    lane = lax.broadcasted_iota(jnp.int32, (tm, LANES), 1)
    first_half = (lane % RET_DK) < (RET_DK // 2)

    def tables(pos):
        ang = pos.astype(F32) * inv_ref[...]
        sin = jnp.sin(ang)
        return jnp.cos(ang), jnp.where(first_half, -sin, sin)

    if fixed_pos is None:
        trow = pl.ds(pl.multiple_of(lax.rem(i, tiles) * tm, tm), tm)

        @pl.when(i < tiles)
        def _():
            cos_s[trow, :], sin_s[trow, :] = tables(i * tm + lax.broadcasted_iota(jnp.int32, (tm, LANES), 0))
        cos = cos_s[trow, :]
        sin = sin_s[trow, :]
    else:
        cos, sin = tables(jnp.full((tm, LANES), fixed_pos, jnp.int32))
    x = x_ref[...]
    h = _rms(x, ln_ref[...]).astype(BF16)
    qk = _dot(h, w_ref[:, 0:2 * RET_QK_W])
    n_q = RET_QK_W // LANES
    for c in range(2 * n_q):
        xc = qk[:, c * LANES:(c + 1) * LANES]
        swapped = jnp.where(first_half, pltpu.roll(xc, LANES - RET_DK // 2, 1), pltpu.roll(xc, RET_DK // 2, 1))
        r = xc * cos + swapped * sin
        if c < n_q:
            q_ref[:, c * LANES:(c + 1) * LANES] = r.astype(q_ref.dtype)
        else:
            k_ref[:, (c - n_q) * LANES:(c - n_q + 1) * LANES] = (r * (RET_DK ** -0.5)).astype(k_ref.dtype)
    o = 2 * RET_QK_W
    v_ref[...] = _dot(h, w_ref[:, o:o + RET_V_W]).astype(v_ref.dtype)
    o += RET_V_W
    g_ref[...] = _dot(h, w_ref[:, o:o + RET_V_W]).astype(g_ref.dtype)
    half = D_MODEL // 2
    gr_ref[:, :half] = _dot(h, wg0_ref[...]).astype(gr_ref.dtype)
    gr_ref[:, half:] = _dot(h, wg1_ref[...]).astype(gr_ref.dtype)
    ga_ref[:, :half] = _dot(h, wg2_ref[...]).astype(ga_ref.dtype)
    ga_ref[:, half:] = _dot(h, wg3_ref[...]).astype(ga_ref.dtype)


def _inproj_ret(x2d, ln, w_in, inv_row, tm, out_dtype, seq=None, fixed_pos=None):
    n = x2d.shape[0]
    tiles = None if seq is None else seq // tm
    table_rows = tm if seq is None else seq
    row = lambda width: pl.BlockSpec((tm, width), lambda i: (i, 0))
    widths = (RET_QK_W, RET_QK_W, RET_V_W, RET_V_W, D_MODEL, D_MODEL)
    half = D_MODEL // 2
    assert COL_GATE % half == 0
    return pl.pallas_call(
        functools.partial(_inproj_ret_body, tm=tm, tiles=tiles, fixed_pos=fixed_pos),
        grid=(n // tm,),
        in_specs=[row(D_MODEL), _resident((1, D_MODEL)), _col_window(D_MODEL, COL_ATT, 0)]
        + [_col_window(D_MODEL, half, COL_GATE // half + j) for j in range(4)] + [_resident((1, LANES))],
        out_specs=[row(wd) for wd in widths],
        out_shape=[jax.ShapeDtypeStruct((n, wd), out_dtype) for wd in widths],
        scratch_shapes=[pltpu.VMEM((table_rows, LANES), F32), pltpu.VMEM((table_rows, LANES), F32)],
        compiler_params=_params(1),
        name="inproj_ret",
    )(x2d, ln, w_in, w_in, w_in, w_in, w_in, inv_row)


def _inproj_att_body(x_ref, ln_ref, wq_ref, wk_ref, wv_ref, *refs, tm, keeps, seq):
    lowp = refs[:3 * ATT_GROUPS]
    full = refs[3 * ATT_GROUPS:5 * ATT_GROUPS]
    h = _rms(x_ref[...], ln_ref[...]).astype(BF16)
    for g in reversed(range(ATT_GROUPS)):
        for kind in reversed(range(3)):
            dil = ATT_DILATIONS[g]
            dst = lowp[kind * ATT_GROUPS + g]
            r = _dot(h, (wq_ref, wk_ref, wv_ref)[kind][:, g * ATT_GW:(g + 1) * ATT_GW])
            if dil == 1:
                dst[0] = r.astype(dst.dtype)
            else:
                dst[...] = jnp.swapaxes(r.reshape(tm // dil, dil, ATT_GW), 0, 1).astype(dst.dtype)
            if kind > 0:
                cache = full[(kind - 1) * ATT_GROUPS + g]
                rows = tm if keeps[g] == seq else keeps[g]
                for hh in range(ATT_HPG):
                    cache[pl.ds(hh, rows, stride=ATT_HPG), :] = r[tm - rows:, hh * ATT_HD:(hh + 1) * ATT_HD]


def _inproj_att_prompt(x2d, ln, w_in, tm, seq):
    n = x2d.shape[0]
    batch = n // seq
    tiles = seq // tm
    keeps = tuple(min(wd, seq) for wd in ATT_WINDOWS)
    assert all(kp <= tm or kp == seq for kp in keeps) and seq % tm == 0
    out_specs, out_shape = [], []
    for _ in range(3):
        for g in range(ATT_GROUPS):
            dil = ATT_DILATIONS[g]
            out_specs.append(pl.BlockSpec((None, dil, tm // dil, ATT_GW), lambda i: (i // tiles, 0, i % tiles, 0)))
            out_shape.append(jax.ShapeDtypeStruct((batch, dil, seq // dil, ATT_GW), BF16))
    for _ in range(2):
        for g in range(ATT_GROUPS):
            if keeps[g] == seq:
                idx = lambda i: (i // tiles, i % tiles, 0)
                rows = tm
            else:
                idx = lambda i: (i // tiles, 0, 0)
                rows = keeps[g]
            out_specs.append(pl.BlockSpec((None, rows * ATT_HPG, ATT_HD), idx))
            out_shape.append(jax.ShapeDtypeStruct((batch, keeps[g] * ATT_HPG, ATT_HD), F32))
    return pl.pallas_call(
        functools.partial(_inproj_att_body, tm=tm, keeps=keeps, seq=seq),
        grid=(n // tm,),
        in_specs=[pl.BlockSpec((tm, D_MODEL), lambda i: (i, 0)), _resident((1, D_MODEL))]
        + [_col_window(D_MODEL, ATT_W, COL_ATT // ATT_W + kind) for kind in range(3)],
        out_specs=out_specs,
        out_shape=out_shape,
        compiler_params=_params(1),
        name="inproj_att",
    )(x2d, ln, w_in, w_in, w_in)


def _inproj_att_sample_body(x_ref, ln_ref, wq_ref, wk_ref, wv_ref, o_ref):
    h = _rms(x_ref[...], ln_ref[...]).astype(BF16)
    for kind, w_ref in enumerate((wq_ref, wk_ref, wv_ref)):
        for g in range(ATT_GROUPS):
            c = kind * ATT_GROUPS + g
            o_ref[:, c * ATT_GW:(c + 1) * ATT_GW] = _dot(h, w_ref[:, g * ATT_GW:(g + 1) * ATT_GW])


def _inproj_att_sample(x2d, ln, w_in):
    n = x2d.shape[0]
    return pl.pallas_call(
        _inproj_att_sample_body,
        grid=(1,),
        in_specs=[pl.BlockSpec((n, D_MODEL), lambda i: (0, 0)), _resident((1, D_MODEL))]
        + [_col_window(D_MODEL, ATT_W, COL_ATT // ATT_W + kind) for kind in range(3)],
        out_specs=pl.BlockSpec((n, 3 * ATT_W), lambda i: (0, 0)),
        out_shape=jax.ShapeDtypeStruct((n, 3 * ATT_W), F32),
        compiler_params=_params(1),
        name="inproj_att_sample",
    )(x2d, ln, w_in, w_in, w_in)


def _gn_swish(o, gate, gn):
    mu = jnp.mean(o, axis=-1, keepdims=True)
    d = o - mu
    var = jnp.mean(d * d, axis=-1, keepdims=True)
    on = d * lax.rsqrt(var + NORM_EPS) * gn
    return gate * _sigmoid(gate) * on


def _retention_body(q_ref, k_ref, v_ref, g_ref, gn_ref, dm_ref, qdec_ref, kdec_ref, gc_ref,
                    out_ref, st_ref, state, *, n_chunks):
    c = RET_CHUNK

    @pl.when(pl.program_id(1) == 0)
    def _():
        state[...] = jnp.zeros_like(state)

    lane = lax.broadcasted_iota(jnp.int32, (c, 2 * RET_DK), 1)
    head0 = lane < RET_DK

    def chunk(ci, carry):
        rows = pl.ds(pl.multiple_of(ci * c, c), c)
        scores, cross, values = {}, {}, {}
        for p in range(RET_PAIRS):
            q2 = q_ref[rows, p * 2 * RET_DK:(p + 1) * 2 * RET_DK]
            k2 = k_ref[rows, p * 2 * RET_DK:(p + 1) * 2 * RET_DK]
            v2 = v_ref[rows, p * 2 * RET_DV:(p + 1) * 2 * RET_DV]
            pst = state[p]
            pst_lo = pst.astype(BF16)
            zero = jnp.zeros_like(q2)
            for hh in range(2):
                h = 2 * p + hh
                qm = jnp.where(head0 if hh == 0 else jnp.logical_not(head0), q2, zero)
                values[h] = v2[:, hh * RET_DV:(hh + 1) * RET_DV]
                scores[h] = _dot_nt(qm, k2)
                cross[h] = _dot(qm, pst_lo[:, hh * RET_DV:(hh + 1) * RET_DV])
            kd = (k2.astype(F32) * kdec_ref[p]).astype(BF16)
            state[p] = pst * gc_ref[p] + _dot_tn(kd, v2)
        for h in range(RET_HEADS):
            o = _dot((scores[h] * dm_ref[h]).astype(BF16), values[h]) + cross[h] * qdec_ref[h]
            gate = g_ref[rows, h * RET_DV:(h + 1) * RET_DV].astype(F32)
            res = _gn_swish(o, gate, gn_ref[:, h * RET_DV:(h + 1) * RET_DV])
            out_ref[rows, h * RET_DV:(h + 1) * RET_DV] = res.astype(out_ref.dtype)
        return carry

    lax.fori_loop(0, n_chunks, chunk, 0, unroll=True)
    for p in range(RET_PAIRS):
        pst = state[p]
        for hh in range(2):
            st_ref[2 * p + hh] = pst[hh * RET_DK:(hh + 1) * RET_DK, hh * RET_DV:(hh + 1) * RET_DV]


def _retention_prompt(q, k, v, g, gn, batch, seq, ts):
    n = q.shape[0]
    steps = seq // ts
    dmask, qdec, kdec, gc = _ret_tables()
    row = lambda width: pl.BlockSpec((ts, width), lambda b, s: (b * steps + s, 0))
    const = lambda a: pl.BlockSpec(a.shape, lambda b, s: (0,) * a.ndim)
    return pl.pallas_call(
        functools.partial(_retention_body, n_chunks=ts // RET_CHUNK),
        grid=(batch, steps),
        in_specs=[row(RET_QK_W), row(RET_QK_W), row(RET_V_W), row(RET_V_W), const(gn),
                  const(dmask), const(qdec), const(kdec), const(gc)],
        out_specs=[row(RET_V_W),
                   pl.BlockSpec((None, RET_HEADS, RET_DK, RET_DV), lambda b, s: (b, 0, 0, 0))],
        out_shape=[jax.ShapeDtypeStruct((n, RET_V_W), BF16),
                   jax.ShapeDtypeStruct((batch, RET_HEADS, RET_DK, RET_DV), F32)],
        scratch_shapes=[pltpu.VMEM((RET_PAIRS, 2 * RET_DK, 2 * RET_DV), F32)],
        compiler_params=_params(2),
        name="retention",
    )(q, k, v, g, gn, dmask, qdec, kdec, gc)


def _attn_body(q_ref, k_ref, v_ref, rb_ref, sel_ref, neg_ref, o_ref, lse_ref, *scratch, group, dil, n_blocks):
    nk = ATT_NK
    scale = ATT_HD ** -0.5
    lane = lax.broadcasted_iota(jnp.int32, (nk, LANES), 1)
    tabs = scratch[-1]

    @pl.when(pl.program_id(0) == 0)
    def _():
        band = neg_ref[...]
        for piece in _split3(rb_ref[group * ATT_HPG:(group + 1) * ATT_HPG, :]):
            band = band + _dot(piece, sel_ref[...])
        col = lax.broadcasted_iota(jnp.int32, (1, 2 * nk), 1)
        for i in range(nk):
            window = band[:, nk - 1 - i:3 * nk - 1 - i]
            for h in range(ATT_HPG):
                tabs[1, h, i:i + 1, :] = window[h:h + 1, :]
                tabs[0, h, i:i + 1, :] = jnp.where(col < nk, -jnp.inf, window[h:h + 1, :])
    if dil > 1:
        o_s, l_s = scratch[:2]

    def chain(s_raw, v_ext, sel, h):
        s = s_raw * scale + (tabs[sel, h] if n_blocks > 1 else tabs[1, h, :, nk:])
        m = jnp.max(s, axis=-1, keepdims=True)
        p = jnp.exp(s - m)
        pv = _dot(p.astype(BF16), v_ext)
        den = pv[:, ATT_HD:]
        return (pv[:, :ATT_HD] / den).astype(BF16), m + jnp.log(den)

    def write(rs, rows, outs):
        lse_tile = jnp.zeros((nk, LANES), F32)
        for h, (o, lse) in enumerate(outs):
            hc = slice(h * ATT_HD, (h + 1) * ATT_HD)
            if dil == 1:
                o_ref[rows, hc] = o
            else:
                o_s[rs, rows, hc] = o
            lse_tile = jnp.where(lane == h, lse, lse_tile)
        if dil == 1:
            lse_ref[rows, :] = lse_tile
        else:
            l_s[rs, rows, :] = lse_tile

    heads = [slice(h * ATT_HD, (h + 1) * ATT_HD) for h in range(ATT_HPG)]

    def block(rs, n, n_prev, sel):
        rows = pl.ds(pl.multiple_of(n * nk, nk), nk)
        prev_rows = pl.ds(pl.multiple_of(n_prev * nk, nk), nk)
        q = q_ref[rs, rows, :]
        k_all = jnp.concatenate([k_ref[rs, prev_rows, :], k_ref[rs, rows, :]], axis=0)
        v_all = jnp.concatenate([v_ref[rs, prev_rows, :], v_ref[rs, rows, :]], axis=0)
        ones = jnp.ones((2 * nk, ATT_HD), BF16)
        outs = []
        for h, hc in enumerate(heads):
            v_ext = jnp.concatenate([v_all[:, hc], ones], axis=1)
            outs.append(chain(_dot_nt(q[:, hc], k_all[:, hc]), v_ext, sel, h))
        write(rs, rows, outs)

    def single_blocks(streams):
        ones = jnp.ones((nk, ATT_HD), BF16)
        scores = [[_dot_nt(q_ref[rs, :, hc], k_ref[rs, :, hc]) for hc in heads] for rs in streams]
        for i, rs in enumerate(streams):
            outs = [chain(scores[i][h], jnp.concatenate([v_ref[rs, :, hc], ones], axis=1), 0, h)
                    for h, hc in enumerate(heads)]
            write(rs, slice(None), outs)

    if n_blocks == 1:
        for r0 in range(0, dil, SINGLE_BLOCK_GROUP):
            single_blocks(range(r0, min(r0 + SINGLE_BLOCK_GROUP, dil)))
    else:
        for rs in range(dil):
            def loop(n, carry, rs=rs):
                block(rs, n, jnp.maximum(n - 1, 0), jnp.minimum(n, 1))
                return carry
            lax.fori_loop(0, n_blocks, loop, 0, unroll=ATTN_UNROLL)

    if dil > 1:
        rows = o_ref.shape[0]
        for j in range(rows // PERM_ROWS):
            src = slice(j * (PERM_ROWS // dil), (j + 1) * (PERM_ROWS // dil))
            dst = slice(j * PERM_ROWS, (j + 1) * PERM_ROWS)
            o_ref[dst, :] = jnp.swapaxes(o_s[:, src, :], 0, 1).reshape(PERM_ROWS, ATT_GW).astype(BF16)
            lse_ref[dst, :] = jnp.swapaxes(l_s[:, src, :], 0, 1).reshape(PERM_ROWS, LANES)


def _attention_prompt(aq, ak, av, rel_bias_t, onehot, neg, g):
    batch, dil, length, _ = aq.shape
    seq = dil * length
    n_blocks = length // ATT_NK
    blk = pl.BlockSpec((None, dil, length, ATT_GW), lambda b: (b, 0, 0, 0))
    scratch = [pltpu.VMEM((dil, length, ATT_GW), BF16), pltpu.VMEM((dil, length, LANES), F32)] if dil > 1 else []
    scratch.append(pltpu.VMEM((2, ATT_HPG, ATT_NK, 2 * ATT_NK), F32))
    o, lse = pl.pallas_call(
        functools.partial(_attn_body, group=g, dil=dil, n_blocks=n_blocks),
        grid=(batch,),
        in_specs=[blk, blk, blk, pl.BlockSpec(rel_bias_t.shape, lambda b: (0, 0)),
                  pl.BlockSpec((None, REL_BUCKETS, 3 * ATT_NK), lambda b: (g, 0, 0)),
                  pl.BlockSpec((1, 3 * ATT_NK), lambda b: (0, 0))],
        out_specs=[pl.BlockSpec((None, seq, ATT_GW), lambda b: (b, 0, 0)),
                   pl.BlockSpec((None, seq, LANES), lambda b: (b, 0, 0))],
        out_shape=[jax.ShapeDtypeStruct((batch, seq, ATT_GW), BF16),
                   jax.ShapeDtypeStruct((batch, seq, LANES), F32)],
        scratch_shapes=scratch,
        compiler_params=_params(1),
        name="attention_g%d" % g,
    )(aq, ak, av, rel_bias_t, onehot, neg)
    return o.reshape(batch * seq, ATT_GW), lse.reshape(batch * seq, LANES)


def _tail_math(act_refs, w_refs, y_ref, combine, middle=None):
    wrb_ref, wab_ref, wo_ref, wu_ref, wd_ref, wpl_ref, wpg_ref, ln2_ref, lnf_ref = w_refs
    if combine:
        ret_ref, o0_ref, o1_ref, o2_ref, l0_ref, l1_ref, l2_ref, gr_ref, ga_ref, x_ref, ple_ref = act_refs
        lses = [l0_ref[...], l1_ref[...], l2_ref[...]]
        outs = [o0_ref, o1_ref, o2_ref]
        parts = []
        for h in range(ATT_HPG):
            lh = [l[:, h:h + 1] for l in lses]
            mx = jnp.maximum(jnp.maximum(lh[0], lh[1]), lh[2])
            e = [jnp.exp(l - mx) for l in lh]
            tot = e[0] + e[1] + e[2]
            acc = None
            for g in range(ATT_GROUPS):
                term = (e[g] / tot) * outs[g][:, h * ATT_HD:(h + 1) * ATT_HD].astype(F32)
                acc = term if acc is None else acc + term
            parts.append(acc)
        att = jnp.concatenate(parts, axis=1).astype(BF16)
    else:
        ret_ref, att_ref, gr_ref, ga_ref, x_ref, ple_ref = act_refs
        att = att_ref[...].astype(BF16)
    a = _dot(ret_ref[...].astype(BF16), wrb_ref[...])
    b = _dot(att, wab_ref[...])
    mixed = _sigmoid(gr_ref[...].astype(F32)) * a + _sigmoid(ga_ref[...].astype(F32)) * b
    x1 = x_ref[...] + _dot(mixed.astype(BF16), wo_ref[...])
    h2 = _rms(x1, ln2_ref[...]).astype(BF16)
    ff_chunk = D_MODEL
    acc = None
    for c in range(D_FF // ff_chunk):
        u = _dot(h2, wu_ref[:, c * ff_chunk:(c + 1) * ff_chunk])
        r = jnp.maximum(u, 0.0)
        t = _dot((r * r).astype(BF16), wd_ref[c * ff_chunk:(c + 1) * ff_chunk, :])
        acc = t if acc is None else acc + t
        if middle is not None and c == D_FF // ff_chunk // 2 - 1:
            middle()
    x2 = x1 + acc
    gate = _sigmoid(_dot(x2.astype(BF16), wpg_ref[...]))
    x3 = x2 + gate * _dot(ple_ref[...].astype(BF16), wpl_ref[...])
    y_ref[...] = _rms(x3, lnf_ref[...])


def _tail_body(*refs, n_act, combine):
    _tail_math(refs[:n_act], refs[n_act:-1], refs[-1], combine)


def _tail(acts, weights, tm, combine):
    n = acts[0].shape[0]
    row = lambda a: pl.BlockSpec((tm, a.shape[1]), lambda i: (i, 0))
    return pl.pallas_call(
        functools.partial(_tail_body, n_act=len(acts), combine=combine),
        grid=(n // tm,),
        in_specs=[row(a) for a in acts] + [_resident(w.shape) for w in weights],
        out_specs=pl.BlockSpec((tm, D_MODEL), lambda i: (i, 0)),
        out_shape=jax.ShapeDtypeStruct((n, D_MODEL), F32),
        compiler_params=_params(1),
        name="tail" if combine else "tail_sample",
    )(*acts, *weights)


def _tail_shift_body(*refs, n_act, n_w, n_steps):
    nc = 2 * ATT_GROUPS
    acts = refs[:n_act]
    ws = refs[n_act:n_act + n_w]
    new_ref = refs[n_act + n_w]
    base = n_act + n_w + 1
    old = refs[base:base + nc]
    y_ref = refs[base + nc]
    out = refs[base + nc + 1:base + 2 * nc + 1]
    stage = refs[base + 2 * nc + 1:base + 3 * nc + 1]
    sem_in, sem_out, sem_row = refs[base + 3 * nc + 1:]
    s = pl.program_id(0)
    keys = [2 * g for g in range(ATT_GROUPS)]
    values = [2 * g + 1 for g in range(ATT_GROUPS)]

    def copy_in(i, seq):
        width = old[i].shape[1]
        return pltpu.make_async_copy(old[i].at[seq, pl.ds(1, width - 1)], stage[i], sem_in.at[i])

    def copy_out(i, seq):
        width = old[i].shape[1]
        return pltpu.make_async_copy(stage[i], out[i].at[seq, pl.ds(0, width - 1)], sem_out.at[i])

    def copy_row(i):
        g, kind = divmod(i, 2)
        return pltpu.make_async_copy(new_ref.at[s, kind + 1, g], out[i].at[s, old[i].shape[1] - 1], sem_row.at[i])

    @pl.when(s == 0)
    def _():
        for i in keys:
            copy_in(i, 0).start()
    for i in keys:
        copy_in(i, s).wait()
    for i in keys:
        copy_out(i, s).start()

    @pl.when(s > 0)
    def _():
        for i in values:
            copy_out(i, s - 1).wait()
    for i in values:
        copy_in(i, s).start()
    for i in range(nc):
        copy_row(i).start()

    def middle():
        for i in values:
            copy_in(i, s).wait()
        for i in values:
            copy_out(i, s).start()
        for i in keys:
            copy_out(i, s).wait()

        @pl.when(s < n_steps - 1)
        def _():
            for i in keys:
                copy_in(i, s + 1).start()

    _tail_math(acts, ws, y_ref, True, middle)

    for i in range(nc):
        copy_row(i).wait()

    @pl.when(s == n_steps - 1)
    def _():
        for i in values:
            copy_out(i, s).wait()


def _tail_shift(acts, weights, tm, new_qkv, caches):
    n = acts[0].shape[0]
    n_steps = n // tm
    assert n_steps == new_qkv.shape[0]
    row = lambda a: pl.BlockSpec((tm, a.shape[1]), lambda i: (i, 0))
    anyspec = pl.BlockSpec(memory_space=pl.ANY)
    nc = len(caches)
    new_rows = new_qkv.reshape(new_qkv.shape[0], 3, ATT_GROUPS, ATT_HPG, ATT_HD)
    res = pl.pallas_call(
        functools.partial(_tail_shift_body, n_act=len(acts), n_w=len(weights), n_steps=n_steps),
        grid=(n_steps,),
        in_specs=[row(a) for a in acts] + [_resident(w.shape) for w in weights] + [anyspec] * (nc + 1),
        out_specs=[pl.BlockSpec((tm, D_MODEL), lambda i: (i, 0))] + [anyspec] * nc,
        out_shape=[jax.ShapeDtypeStruct((n, D_MODEL), F32)]
        + [jax.ShapeDtypeStruct(c.shape, c.dtype) for c in caches],
        scratch_shapes=[pltpu.VMEM((c.shape[1] - 1, ATT_HPG, ATT_HD), F32) for c in caches]
        + [pltpu.SemaphoreType.DMA((nc,)), pltpu.SemaphoreType.DMA((nc,)), pltpu.SemaphoreType.DMA((nc,))],
        compiler_params=pltpu.CompilerParams(dimension_semantics=("arbitrary",),
                                             vmem_limit_bytes=VMEM_LIMIT_TAIL_V7X),
        name="tail",
    )(*acts, *weights, new_rows, *caches)
    return res[0], res[1:]


def _decode_body(q_ref, k_ref, v_ref, g_ref, gn_ref, st_ref, gam_ref,
                 qkv_ref, rb_ref, slot_ref,
                 ck0_ref, cv0_ref, ck1_ref, cv1_ref, ck2_ref, cv2_ref,
                 ret_ref, att_ref, nst_ref, slot_bias, *, seqs):
    sub = lax.broadcasted_iota(jnp.int32, (8, 2 * RET_DK), 0)
    lane = lax.broadcasted_iota(jnp.int32, (8, 2 * RET_DK), 1)
    row0 = sub == 0
    srow = lax.broadcasted_iota(jnp.int32, (2 * RET_DK, RET_DV), 0)
    for i in range(seqs):
        for p in range(RET_PAIRS):
            pc = slice(p * 2 * RET_DK, (p + 1) * 2 * RET_DK)
            q2 = jnp.where(row0, jnp.broadcast_to(q_ref[i, :, pc], (8, 2 * RET_DK)), 0.0)
            k2 = jnp.where(row0, jnp.broadcast_to(k_ref[i, :, pc], (8, 2 * RET_DK)), 0.0)
            pst = st_ref[i, p]
            gam = gam_ref[p]
            outer = []
            for hh in range(2):
                h = 2 * p + hh
                hc = slice(h * RET_DV, (h + 1) * RET_DV)
                hsel = (lane < RET_DK) if hh == 0 else (lane >= RET_DK)
                qm = jnp.where(hsel, q2, 0.0)
                km = jnp.where(hsel, k2, 0.0)
                vh = v_ref[i, :, hc]
                v8 = jnp.where(row0[:, :RET_DV], jnp.broadcast_to(vh, (8, RET_DV)), 0.0)
                cross = _dot(qm.astype(BF16), (pst * gam).astype(BF16))[0:1, :]
                qk = jnp.sum(qm[0:1, :] * km[0:1, :], axis=-1, keepdims=True)
                o = cross + qk * vh
                ret_ref[i, :, hc] = _gn_swish(o, g_ref[i, :, hc], gn_ref[:, hc])
                outer.append(_dot_tn(k2.astype(BF16), v8.astype(BF16)))
            nst_ref[i, p] = pst * gam + jnp.where(srow < RET_DK, outer[0], outer[1])
    scale = ATT_HD ** -0.5
    caches = ((ck0_ref, cv0_ref), (ck1_ref, cv1_ref), (ck2_ref, cv2_ref))
    slot = lax.broadcasted_iota(jnp.int32, (ATT_NK, ATT_HD), 0)
    is_new = slot == 0
    first = row0[:, :ATT_HD]

    @pl.when(pl.program_id(0) == 0)
    def _():
        pieces = _split3(rb_ref[...])
        for g in range(ATT_GROUPS):
            rows = slice(g * ATT_HPG, (g + 1) * ATT_HPG)
            slot_bias[rows, :] = (_dot(pieces[0][rows], slot_ref[g]) + _dot(pieces[1][rows], slot_ref[g])
                                  + _dot(pieces[2][rows], slot_ref[g]))

    def one_row(x):
        return jnp.where(first, jnp.broadcast_to(x, (8, ATT_HD)), 0.0).astype(BF16)

    scores = {}
    for i in range(seqs):
        for h in range(ATT_HPG):
            for g in range(ATT_GROUPS):
                c0 = g * ATT_GW + h * ATT_HD
                kk = jnp.where(is_new, qkv_ref[i, :, ATT_W + c0:ATT_W + c0 + ATT_HD], caches[g][0][i, :, h, :])
                scores[i, g, h] = _dot_nt(one_row(qkv_ref[i, :, c0:c0 + ATT_HD]), kk.astype(BF16))[0:1, :]
    for i in range(seqs):
        for h in range(ATT_HPG):
            o_g, lse_g = [], []
            for g in range(ATT_GROUPS):
                c0 = g * ATT_GW + h * ATT_HD
                vv = jnp.where(is_new, qkv_ref[i, :, 2 * ATT_W + c0:2 * ATT_W + c0 + ATT_HD],
                               caches[g][1][i, :, h, :])
                gh = g * ATT_HPG + h
                s = scores[i, g, h] * scale + slot_bias[gh:gh + 1, :]
                m = jnp.max(s, axis=-1, keepdims=True)
                pr = jnp.exp(s - m)
                den = jnp.sum(pr, axis=-1, keepdims=True)
                o_g.append(_dot(one_row(pr), vv.astype(BF16))[0:1, :] / den)
                lse_g.append(m + jnp.log(den))
            mx = jnp.maximum(jnp.maximum(lse_g[0], lse_g[1]), lse_g[2])
            e = [jnp.exp(l - mx) for l in lse_g]
            tot = e[0] + e[1] + e[2]
            att_ref[i, :, h * ATT_HD:(h + 1) * ATT_HD] = ((e[0] / tot) * o_g[0] + (e[1] / tot) * o_g[1]
                                                          + (e[2] / tot) * o_g[2])


def _decode(q, k, v, g, gn, state, qkv, rel_bias_t, caches):
    batch = q.shape[0]
    gam = jnp.asarray(np.ascontiguousarray(np.broadcast_to(
        np.repeat(np.exp(_ret_log_decay()), RET_DK).reshape(RET_PAIRS, 2 * RET_DK, 1),
        (RET_PAIRS, 2 * RET_DK, RET_DV))), F32)
    vec = lambda a: a.reshape(batch, 1, a.shape[1])
    seqs = DECODE_SEQS
    assert batch % seqs == 0
    vspec = lambda width: pl.BlockSpec((seqs, 1, width), lambda b: (b, 0, 0))
    const = lambda a: pl.BlockSpec(a.shape, lambda b: (0,) * a.ndim)
    st_spec = pl.BlockSpec((seqs, RET_PAIRS, 2 * RET_DK, RET_DV), lambda b: (b, 0, 0, 0))
    cache_in, cache_specs = [], []
    for gi in range(ATT_GROUPS):
        dil = ATT_DILATIONS[gi]
        for c in caches[2 * gi:2 * gi + 2]:
            cache_in.append(c.reshape(batch, ATT_NK, dil, ATT_HPG, ATT_HD))
            cache_specs.append(pl.BlockSpec((seqs, ATT_NK, None, ATT_HPG, ATT_HD), lambda b: (b, 0, 0, 0, 0)))
    st_pairs = state.reshape(batch, RET_PAIRS, 2 * RET_DK, RET_DV)
    slot_sel = _slot_select()
    ret, att, nst = pl.pallas_call(
        functools.partial(_decode_body, seqs=seqs),
        grid=(batch // seqs,),
        in_specs=[vspec(RET_QK_W), vspec(RET_QK_W), vspec(RET_V_W), vspec(RET_V_W), const(gn), st_spec, const(gam),
                  vspec(3 * ATT_W), const(rel_bias_t), const(slot_sel)] + cache_specs,
        out_specs=[vspec(RET_V_W), vspec(ATT_GW), st_spec],
        out_shape=[jax.ShapeDtypeStruct((batch, 1, RET_V_W), F32),
                   jax.ShapeDtypeStruct((batch, 1, ATT_GW), F32),
                   jax.ShapeDtypeStruct(st_pairs.shape, F32)],
        scratch_shapes=[pltpu.VMEM((rel_bias_t.shape[0], ATT_NK), F32)],
        compiler_params=_params(1),
        name="decode",
    )(vec(q), vec(k), vec(v), vec(g), gn, st_pairs, gam, vec(qkv), rel_bias_t, slot_sel, *cache_in)
    return ret.reshape(batch, RET_V_W), att.reshape(batch, ATT_GW), nst.reshape(state.shape)


def kernel(x_prompt, x_sample, state_ret, cache_k_w128, cache_v_w128, cache_k_w512, cache_v_w512,
           cache_k_w2048, cache_v_w2048, p_prompt, p_sample, ln1_g, w_in, ret_gn_g, w_ret_br, w_att_br,
           w_out, ln2_g, w_up, w_down, w_ple, w_ple_gate, rel_bias, lnf_g):
    depth = w_in.shape[0]
    assert depth == 1
    batch, seq, _ = x_prompt.shape
    dec_batch, dec_seq, _ = x_sample.shape
    assert dec_seq == 1
    past_len = 16384
    l = 0
    ln1 = ln1_g[l][None, :]
    ln2 = ln2_g[l][None, :]
    lnf = lnf_g[None, :]
    gn = ret_gn_g[l][None, :]
    w_in_lo = w_in[l].astype(BF16)
    tail_w = (w_ret_br[l].astype(BF16), w_att_br[l].astype(BF16), w_out[l].astype(BF16), w_up[l].astype(BF16),
              w_down[l].astype(BF16), w_ple[l].astype(BF16), w_ple_gate[l].astype(BF16), ln2, lnf)
    inv_row = _rope_inv_row()

    caches = (cache_k_w128[l], cache_v_w128[l], cache_k_w512[l], cache_v_w512[l],
              cache_k_w2048[l], cache_v_w2048[l])

    xs = x_sample.reshape(dec_batch, D_MODEL)
    sq, sk, sv, sg, sgr, sga = _inproj_ret(xs, ln1, w_in_lo, inv_row, dec_batch, F32, fixed_pos=past_len)
    s_qkv = _inproj_att_sample(xs, ln1, w_in_lo)
    rel_bias_t = rel_bias.T
    s_ret, s_attn, new_st = _decode(sq, sk, sv, sg, gn, state_ret[l], s_qkv, rel_bias_t, caches)
    ple_s = p_sample[l].reshape(dec_batch, D_PLE)
    y_s = _tail([s_ret, s_attn, sgr, sga, xs, ple_s], tail_w, dec_batch, False)
    y_sample = y_s.reshape(dec_batch, 1, D_MODEL)

    xp = x_prompt.reshape(batch * seq, D_MODEL)
    rq, rk, rv, rg, gr, ga = _inproj_ret(xp, ln1, w_in_lo, inv_row, TM_INPROJ, BF16, seq=seq)
    att_o = _inproj_att_prompt(xp, ln1, w_in_lo, TM_INPROJ, seq)
    aqs, aks, avs = att_o[0:3], att_o[3:6], att_o[6:9]
    kfull, vfull = att_o[9:12], att_o[12:15]
    ret_out, st_p = _retention_prompt(rq, rk, rv, rg, gn, batch, seq, TS_RETENTION)
    outs, lses = [], []
    band_onehot, band_neg = _band_select()
    for g in range(ATT_GROUPS):
        o, lse = _attention_prompt(aqs[g], aks[g], avs[g], rel_bias_t, band_onehot, band_neg, g)
        outs.append(o)
        lses.append(lse)
    ple_p = p_prompt[l].reshape(batch * seq, D_PLE)
    tm_tail = batch * seq // dec_batch
    y_p, new_caches = _tail_shift([ret_out] + outs + lses + [gr, ga, xp, ple_p], tail_w, tm_tail, s_qkv, caches)
    y_prompt = y_p.reshape(batch, seq, D_MODEL)
    new_state_p = st_p[None]
    kv_p = []
    for g in range(ATT_GROUPS):
        shape = (1, batch, min(ATT_WINDOWS[g], seq), ATT_HPG, ATT_HD)
        kv_p.append(kfull[g].reshape(shape))
        kv_p.append(vfull[g].reshape(shape))
    kv_s = [c[None] for c in new_caches]

    return (y_prompt, y_sample, new_state_p, *kv_p, new_st[None], *kv_s)
```

```python
import functools
import math

import jax
import jax.numpy as jnp
import numpy as np
from jax import lax
from jax.experimental import pallas as pl
from jax.experimental.pallas import tpu as pltpu

F32 = jnp.float32
BF16 = jnp.bfloat16

D_MODEL = 1024
RET_HEADS = 8
RET_DK = 64
RET_DV = 128
RET_PAIRS = RET_HEADS // 2
RET_CHUNK = 128
ROPE_BASE = 10000.0
ATT_WINDOWS = (128, 512, 2048)
ATT_DILATIONS = (1, 4, 16)
ATT_GROUPS = 3
ATT_HPG = 4
ATT_HD = 128
ATT_NK = 128
ATT_GW = ATT_HPG * ATT_HD
REL_BUCKETS = 32
REL_MAX_DIST = 2048
D_FF = 4 * D_MODEL
D_PLE = 256
NORM_EPS = 1e-6
RET_QK_W = RET_HEADS * RET_DK
RET_V_W = RET_HEADS * RET_DV
ATT_W = ATT_GROUPS * ATT_GW
COL_RET = 0
COL_ATT = 2 * RET_QK_W + 2 * RET_V_W
COL_GATE = COL_ATT + 3 * ATT_W
N_IN = COL_GATE + 2 * D_MODEL

VMEM_LIMIT_V7X = 56 * 1024 * 1024
VMEM_LIMIT_TAIL_V7X = 62 * 1024 * 1024
LANES = 128
TM_INPROJ = 512
TS_RETENTION = 512
ATTN_UNROLL = 8
PERM_ROWS = 256
SINGLE_BLOCK_GROUP = 4
DECODE_SEQS = 4


def _dot(a, b):
    return jnp.dot(a, b, preferred_element_type=F32)


def _dot_nt(a, b):
    return lax.dot_general(a, b, (((1,), (1,)), ((), ())), preferred_element_type=F32)


def _dot_tn(a, b):
    return lax.dot_general(a, b, (((0,), (0,)), ((), ())), preferred_element_type=F32)


def _rms(x, g):
    return x * lax.rsqrt(jnp.mean(x * x, axis=-1, keepdims=True) + NORM_EPS) * g


def _sigmoid(x):
    return 1.0 / (1.0 + jnp.exp(-x))


def _resident(shape):
    return pl.BlockSpec(shape, lambda *_: (0,) * len(shape), pipeline_mode=pl.Buffered(1))


def _col_window(rows, width, block_index):
    return pl.BlockSpec((rows, width), lambda *_: (0, block_index), pipeline_mode=pl.Buffered(1))


def _params(n_axes):
    return pltpu.CompilerParams(dimension_semantics=("arbitrary",) * n_axes,
                                vmem_limit_bytes=VMEM_LIMIT_V7X)


def _rope_inv_row():
    half = RET_DK // 2
    inv = ROPE_BASE ** (-jnp.arange(half, dtype=F32) / half)
    return jnp.tile(inv, LANES // half)[None, :]


def _ret_log_decay():
    return np.log1p(-np.exp2(-5.0 - np.arange(RET_HEADS, dtype=np.float32))).astype(np.float32)


def _ret_tables():
    c = RET_CHUNK
    lg = _ret_log_decay()
    i = np.arange(c, dtype=np.float32)
    diff = i[:, None] - i[None, :]
    dmask = np.where(diff[None] >= 0, np.exp(np.maximum(diff, 0.0)[None] * lg[:, None, None]), 0.0)
    q_decay = np.exp((i + 1.0)[:, None] * lg[None, :])
    k_decay = np.exp((c - 1.0 - i)[:, None] * lg[None, :])
    qdec = np.broadcast_to(q_decay.T[:, :, None], (RET_HEADS, c, RET_DV))
    kdec = np.repeat(k_decay, RET_DK, axis=1).reshape(c, RET_PAIRS, 2 * RET_DK).transpose(1, 0, 2)
    gc = np.repeat(np.exp(c * lg), RET_DV).reshape(RET_PAIRS, 1, 2 * RET_DV)
    return tuple(jnp.asarray(np.ascontiguousarray(t), F32) for t in (dmask, qdec, kdec, gc))


def _rel_buckets():
    max_exact = REL_BUCKETS // 2
    out = []
    for dil in ATT_DILATIONS:
        d = np.arange(ATT_NK, dtype=np.int32) * dil
        log_ratio = (np.log(np.maximum(d, 1).astype(np.float32) / np.float32(max_exact))
                     / np.float32(math.log(REL_MAX_DIST / max_exact)))
        large = max_exact + (log_ratio * np.float32(REL_BUCKETS - max_exact)).astype(np.int32)
        out.append(np.where(d < max_exact, d, np.minimum(large, REL_BUCKETS - 1)))
    return np.stack(out)


def _band_select():
    nk = ATT_NK
    buckets = _rel_buckets()
    onehot = np.zeros((ATT_GROUPS, REL_BUCKETS, 3 * nk), np.float32)
    for g in range(ATT_GROUPS):
        for k in range(nk, 2 * nk):
            onehot[g, buckets[g, 2 * nk - 1 - k], k] = 1.0
    mask = np.full((1, 3 * nk), -np.inf, np.float32)
    mask[0, nk:2 * nk] = 0.0
    return jnp.asarray(onehot, BF16), jnp.asarray(mask, F32)


def _slot_select():
    buckets = _rel_buckets()
    onehot = np.zeros((ATT_GROUPS, REL_BUCKETS, ATT_NK), np.float32)
    for g in range(ATT_GROUPS):
        for slot in range(ATT_NK):
            onehot[g, buckets[g, 0 if slot == 0 else ATT_NK - slot], slot] = 1.0
    return jnp.asarray(onehot, BF16)


def _split3(x):
    hi = x.astype(BF16)
    rem = x - hi.astype(F32)
    mid = rem.astype(BF16)
    lo = (rem - mid.astype(F32)).astype(BF16)
    return hi, mid, lo


def _inproj_ret_body(x_ref, xs_ref, ln_ref, w_ref, wg0_ref, wg1_ref, wg2_ref, wg3_ref, inv_ref, *refs,
                     tm, tiles, n_steps, sample_pos):
    outs, sample_outs = refs[:6], refs[6:12]
    cos_s, sin_s = refs[12:]
    i = pl.program_id(0)

    def first_half(rows):
        lane = lax.broadcasted_iota(jnp.int32, (rows, LANES), 1)
        return (lane % RET_DK) < (RET_DK // 2)

    def tables(pos):
        ang = pos.astype(F32) * inv_ref[...]
        sin = jnp.sin(ang)
        return jnp.cos(ang), jnp.where(first_half(pos.shape[0]), -sin, sin)

    def project(x, cos, sin, q_ref, k_ref, v_ref, g_ref, gr_ref, ga_ref):
        h = _rms(x, ln_ref[...]).astype(BF16)
        qk = _dot(h, w_ref[:, 0:2 * RET_QK_W])
        n_q = RET_QK_W // LANES
        fh = first_half(x.shape[0])
        for c in range(2 * n_q):
            xc = qk[:, c * LANES:(c + 1) * LANES]
            swapped = jnp.where(fh, pltpu.roll(xc, LANES - RET_DK // 2, 1), pltpu.roll(xc, RET_DK // 2, 1))
            r = xc * cos + swapped * sin
            if c < n_q:
                q_ref[:, c * LANES:(c + 1) * LANES] = r.astype(q_ref.dtype)
            else:
                k_ref[:, (c - n_q) * LANES:(c - n_q + 1) * LANES] = (r * (RET_DK ** -0.5)).astype(k_ref.dtype)
        o = 2 * RET_QK_W
        v_ref[...] = _dot(h, w_ref[:, o:o + RET_V_W]).astype(v_ref.dtype)
        o += RET_V_W
        g_ref[...] = _dot(h, w_ref[:, o:o + RET_V_W]).astype(g_ref.dtype)
        half = D_MODEL // 2
        gr_ref[:, :half] = _dot(h, wg0_ref[...]).astype(gr_ref.dtype)
        gr_ref[:, half:] = _dot(h, wg1_ref[...]).astype(gr_ref.dtype)
        ga_ref[:, :half] = _dot(h, wg2_ref[...]).astype(ga_ref.dtype)
        ga_ref[:, half:] = _dot(h, wg3_ref[...]).astype(ga_ref.dtype)

    trow = pl.ds(pl.multiple_of(lax.rem(i, tiles) * tm, tm), tm)

    @pl.when(i < tiles)
    def _():
        cos_s[trow, :], sin_s[trow, :] = tables(i * tm + lax.broadcasted_iota(jnp.int32, (tm, LANES), 0))
    project(x_ref[...], cos_s[trow, :], sin_s[trow, :], *outs)

    @pl.when(i == n_steps - 1)
    def _():
        ns = xs_ref.shape[0]
        project(xs_ref[...], *tables(jnp.full((ns, LANES), sample_pos, jnp.int32)), *sample_outs)


def _inproj_ret(x2d, xs2d, ln, w_in, inv_row, tm, seq, sample_pos):
    n = x2d.shape[0]
    ns = xs2d.shape[0]
    tiles = seq // tm
    n_steps = n // tm
    row = lambda width: pl.BlockSpec((tm, width), lambda i: (i, 0))
    widths = (RET_QK_W, RET_QK_W, RET_V_W, RET_V_W, D_MODEL, D_MODEL)
    half = D_MODEL // 2
    assert COL_GATE % half == 0
    return pl.pallas_call(
        functools.partial(_inproj_ret_body, tm=tm, tiles=tiles, n_steps=n_steps, sample_pos=sample_pos),
        grid=(n_steps,),
        in_specs=[row(D_MODEL), _resident(xs2d.shape), _resident((1, D_MODEL)), _col_window(D_MODEL, COL_ATT, 0)]
        + [_col_window(D_MODEL, half, COL_GATE // half + j) for j in range(4)] + [_resident((1, LANES))],
        out_specs=[row(wd) for wd in widths] + [pl.BlockSpec((ns, wd), lambda i: (0, 0)) for wd in widths],
        out_shape=[jax.ShapeDtypeStruct((n, wd), BF16) for wd in widths]
        + [jax.ShapeDtypeStruct((ns, wd), F32) for wd in widths],
        scratch_shapes=[pltpu.VMEM((seq, LANES), F32), pltpu.VMEM((seq, LANES), F32)],
        compiler_params=_params(1),
        name="inproj_ret",
    )(x2d, xs2d, ln, w_in, w_in, w_in, w_in, w_in, inv_row)


def _inproj_att_body(x_ref, xs_ref, ln_ref, wq_ref, wk_ref, wv_ref, *refs, tm, keeps, seq, n_steps):
    lowp = refs[:3 * ATT_GROUPS]
    full = refs[3 * ATT_GROUPS:5 * ATT_GROUPS]
    sample_ref = refs[5 * ATT_GROUPS]

    @pl.when(pl.program_id(0) == n_steps - 1)
    def _():
        hs = _rms(xs_ref[...], ln_ref[...]).astype(BF16)
        for kind, w_ref in enumerate((wq_ref, wk_ref, wv_ref)):
            for g in range(ATT_GROUPS):
                c = kind * ATT_GROUPS + g
                sample_ref[:, c * ATT_GW:(c + 1) * ATT_GW] = _dot(hs, w_ref[:, g * ATT_GW:(g + 1) * ATT_GW])

    h = _rms(x_ref[...], ln_ref[...]).astype(BF16)
    for g in reversed(range(ATT_GROUPS)):
        for kind in reversed(range(3)):
            dil = ATT_DILATIONS[g]
            dst = lowp[kind * ATT_GROUPS + g]
            r = _dot(h, (wq_ref, wk_ref, wv_ref)[kind][:, g * ATT_GW:(g + 1) * ATT_GW])
            if dil == 1:
                dst[0] = r.astype(dst.dtype)
            else:
                dst[...] = jnp.swapaxes(r.reshape(tm // dil, dil, ATT_GW), 0, 1).astype(dst.dtype)
            if kind > 0:
                cache = full[(kind - 1) * ATT_GROUPS + g]
                rows = tm if keeps[g] == seq else keeps[g]
                for hh in range(ATT_HPG):
                    cache[pl.ds(hh, rows, stride=ATT_HPG), :] = r[tm - rows:, hh * ATT_HD:(hh + 1) * ATT_HD]


def _inproj_att(x2d, xs2d, ln, w_in, tm, seq):
    n = x2d.shape[0]
    ns = xs2d.shape[0]
    batch = n // seq
    tiles = seq // tm
    keeps = tuple(min(wd, seq) for wd in ATT_WINDOWS)
    assert all(kp <= tm or kp == seq for kp in keeps) and seq % tm == 0
    out_specs, out_shape = [], []
    for _ in range(3):
        for g in range(ATT_GROUPS):
            dil = ATT_DILATIONS[g]
            out_specs.append(pl.BlockSpec((None, dil, tm // dil, ATT_GW), lambda i: (i // tiles, 0, i % tiles, 0)))
            out_shape.append(jax.ShapeDtypeStruct((batch, dil, seq // dil, ATT_GW), BF16))
    for _ in range(2):
        for g in range(ATT_GROUPS):
            if keeps[g] == seq:
                idx = lambda i: (i // tiles, i % tiles, 0)
                rows = tm
            else:
                idx = lambda i: (i // tiles, 0, 0)
                rows = keeps[g]
            out_specs.append(pl.BlockSpec((None, rows * ATT_HPG, ATT_HD), idx))
            out_shape.append(jax.ShapeDtypeStruct((batch, keeps[g] * ATT_HPG, ATT_HD), F32))
    out_specs.append(pl.BlockSpec((ns, 3 * ATT_W), lambda i: (0, 0)))
    out_shape.append(jax.ShapeDtypeStruct((ns, 3 * ATT_W), F32))
    return pl.pallas_call(
        functools.partial(_inproj_att_body, tm=tm, keeps=keeps, seq=seq, n_steps=n // tm),
        grid=(n // tm,),
        in_specs=[pl.BlockSpec((tm, D_MODEL), lambda i: (i, 0)), _resident(xs2d.shape), _resident((1, D_MODEL))]
        + [_col_window(D_MODEL, ATT_W, COL_ATT // ATT_W + kind) for kind in range(3)],
        out_specs=out_specs,
        out_shape=out_shape,
        compiler_params=_params(1),
        name="inproj_att",
    )(x2d, xs2d, ln, w_in, w_in, w_in)


def _gn_swish(o, gate, gn):
    mu = jnp.mean(o, axis=-1, keepdims=True)
    d = o - mu
    var = jnp.mean(d * d, axis=-1, keepdims=True)
    on = d * lax.rsqrt(var + NORM_EPS) * gn
    return gate * _sigmoid(gate) * on


def _retention_body(q_ref, k_ref, v_ref, g_ref, gn_ref, dm_ref, qdec_ref, kdec_ref, gc_ref,
                    out_ref, st_ref, state, *, n_chunks):
    c = RET_CHUNK

    @pl.when(pl.program_id(1) == 0)
    def _():
        state[...] = jnp.zeros_like(state)

    lane = lax.broadcasted_iota(jnp.int32, (c, 2 * RET_DK), 1)
    head0 = lane < RET_DK

    def chunk(ci, carry):
        rows = pl.ds(pl.multiple_of(ci * c, c), c)
        scores, cross, values = {}, {}, {}
        for p in range(RET_PAIRS):
            q2 = q_ref[rows, p * 2 * RET_DK:(p + 1) * 2 * RET_DK]
            k2 = k_ref[rows, p * 2 * RET_DK:(p + 1) * 2 * RET_DK]
            v2 = v_ref[rows, p * 2 * RET_DV:(p + 1) * 2 * RET_DV]
            pst = state[p]
            pst_lo = pst.astype(BF16)
            zero = jnp.zeros_like(q2)
            for hh in range(2):
                h = 2 * p + hh
                qm = jnp.where(head0 if hh == 0 else jnp.logical_not(head0), q2, zero)
                values[h] = v2[:, hh * RET_DV:(hh + 1) * RET_DV]
                scores[h] = _dot_nt(qm, k2)
                cross[h] = _dot(qm, pst_lo[:, hh * RET_DV:(hh + 1) * RET_DV])
            kd = (k2.astype(F32) * kdec_ref[p]).astype(BF16)
            state[p] = pst * gc_ref[p] + _dot_tn(kd, v2)
        for h in range(RET_HEADS):
            o = _dot((scores[h] * dm_ref[h]).astype(BF16), values[h]) + cross[h] * qdec_ref[h]
            gate = g_ref[rows, h * RET_DV:(h + 1) * RET_DV].astype(F32)
            res = _gn_swish(o, gate, gn_ref[:, h * RET_DV:(h + 1) * RET_DV])
            out_ref[rows, h * RET_DV:(h + 1) * RET_DV] = res.astype(out_ref.dtype)
        return carry

    lax.fori_loop(0, n_chunks, chunk, 0, unroll=True)
    for p in range(RET_PAIRS):
        pst = state[p]
        for hh in range(2):
            st_ref[2 * p + hh] = pst[hh * RET_DK:(hh + 1) * RET_DK, hh * RET_DV:(hh + 1) * RET_DV]


def _retention_prompt(q, k, v, g, gn, batch, seq, ts):
    n = q.shape[0]
    steps = seq // ts
    dmask, qdec, kdec, gc = _ret_tables()
    row = lambda width: pl.BlockSpec((ts, width), lambda b, s: (b * steps + s, 0))
    const = lambda a: pl.BlockSpec(a.shape, lambda b, s: (0,) * a.ndim)
    return pl.pallas_call(
        functools.partial(_retention_body, n_chunks=ts // RET_CHUNK),
        grid=(batch, steps),
        in_specs=[row(RET_QK_W), row(RET_QK_W), row(RET_V_W), row(RET_V_W), const(gn),
                  const(dmask), const(qdec), const(kdec), const(gc)],
        out_specs=[row(RET_V_W),
                   pl.BlockSpec((None, RET_HEADS, RET_DK, RET_DV), lambda b, s: (b, 0, 0, 0))],
        out_shape=[jax.ShapeDtypeStruct((n, RET_V_W), BF16),
                   jax.ShapeDtypeStruct((batch, RET_HEADS, RET_DK, RET_DV), F32)],
        scratch_shapes=[pltpu.VMEM((RET_PAIRS, 2 * RET_DK, 2 * RET_DV), F32)],
        compiler_params=_params(2),
        name="retention",
    )(q, k, v, g, gn, dmask, qdec, kdec, gc)


def _attn_body(q_ref, k_ref, v_ref, rb_ref, sel_ref, neg_ref, o_ref, lse_ref, *scratch, group, dil, n_blocks):
    nk = ATT_NK
    scale = ATT_HD ** -0.5
    lane = lax.broadcasted_iota(jnp.int32, (nk, LANES), 1)
    tabs = scratch[-1]

    @pl.when(pl.program_id(0) == 0)
    def _():
        band = neg_ref[...]
        for piece in _split3(rb_ref[group * ATT_HPG:(group + 1) * ATT_HPG, :]):
            band = band + _dot(piece, sel_ref[...])
        col = lax.broadcasted_iota(jnp.int32, (1, 2 * nk), 1)
        for i in range(nk):
            window = band[:, nk - 1 - i:3 * nk - 1 - i]
            for h in range(ATT_HPG):
                tabs[1, h, i:i + 1, :] = window[h:h + 1, :]
                tabs[0, h, i:i + 1, :] = jnp.where(col < nk, -jnp.inf, window[h:h + 1, :])
    if dil > 1:
        o_s, l_s = scratch[:2]

    def chain(s_raw, v_ext, sel, h):
        s = s_raw * scale + (tabs[sel, h] if n_blocks > 1 else tabs[1, h, :, nk:])
        m = jnp.max(s, axis=-1, keepdims=True)
        p = jnp.exp(s - m)
        pv = _dot(p.astype(BF16), v_ext)
        den = pv[:, ATT_HD:]
        return (pv[:, :ATT_HD] / den).astype(BF16), m + jnp.log(den)

    def write(rs, rows, outs):
        lse_tile = jnp.zeros((nk, LANES), F32)
        for h, (o, lse) in enumerate(outs):
            hc = slice(h * ATT_HD, (h + 1) * ATT_HD)
            if dil == 1:
                o_ref[rows, hc] = o
            else:
                o_s[rs, rows, hc] = o
            lse_tile = jnp.where(lane == h, lse, lse_tile)
        if dil == 1:
            lse_ref[rows, :] = lse_tile
        else:
            l_s[rs, rows, :] = lse_tile

    heads = [slice(h * ATT_HD, (h + 1) * ATT_HD) for h in range(ATT_HPG)]

    def block(rs, n, n_prev, sel):
        rows = pl.ds(pl.multiple_of(n * nk, nk), nk)
        prev_rows = pl.ds(pl.multiple_of(n_prev * nk, nk), nk)
        q = q_ref[rs, rows, :]
        k_all = jnp.concatenate([k_ref[rs, prev_rows, :], k_ref[rs, rows, :]], axis=0)
        v_all = jnp.concatenate([v_ref[rs, prev_rows, :], v_ref[rs, rows, :]], axis=0)
        ones = jnp.ones((2 * nk, ATT_HD), BF16)
        outs = []
        for h, hc in enumerate(heads):
            v_ext = jnp.concatenate([v_all[:, hc], ones], axis=1)
            outs.append(chain(_dot_nt(q[:, hc], k_all[:, hc]), v_ext, sel, h))
        write(rs, rows, outs)

    def single_blocks(streams):
        ones = jnp.ones((nk, ATT_HD), BF16)
        scores = [[_dot_nt(q_ref[rs, :, hc], k_ref[rs, :, hc]) for hc in heads] for rs in streams]
        for i, rs in enumerate(streams):
            outs = [chain(scores[i][h], jnp.concatenate([v_ref[rs, :, hc], ones], axis=1), 0, h)
                    for h, hc in enumerate(heads)]
            write(rs, slice(None), outs)

    if n_blocks == 1:
        for r0 in range(0, dil, SINGLE_BLOCK_GROUP):
            single_blocks(range(r0, min(r0 + SINGLE_BLOCK_GROUP, dil)))
    else:
        for rs in range(dil):
            def loop(n, carry, rs=rs):
                block(rs, n, jnp.maximum(n - 1, 0), jnp.minimum(n, 1))
                return carry
            lax.fori_loop(0, n_blocks, loop, 0, unroll=ATTN_UNROLL)

    if dil > 1:
        rows = o_ref.shape[0]
        for j in range(rows // PERM_ROWS):
            src = slice(j * (PERM_ROWS // dil), (j + 1) * (PERM_ROWS // dil))
            dst = slice(j * PERM_ROWS, (j + 1) * PERM_ROWS)
            o_ref[dst, :] = jnp.swapaxes(o_s[:, src, :], 0, 1).reshape(PERM_ROWS, ATT_GW).astype(BF16)
            lse_ref[dst, :] = jnp.swapaxes(l_s[:, src, :], 0, 1).reshape(PERM_ROWS, LANES)


def _attention_prompt(aq, ak, av, rel_bias_t, onehot, neg, g):
    batch, dil, length, _ = aq.shape
    seq = dil * length
    n_blocks = length // ATT_NK
    blk = pl.BlockSpec((None, dil, length, ATT_GW), lambda b: (b, 0, 0, 0))
    scratch = [pltpu.VMEM((dil, length, ATT_GW), BF16), pltpu.VMEM((dil, length, LANES), F32)] if dil > 1 else []
    scratch.append(pltpu.VMEM((2, ATT_HPG, ATT_NK, 2 * ATT_NK), F32))
    o, lse = pl.pallas_call(
        functools.partial(_attn_body, group=g, dil=dil, n_blocks=n_blocks),
        grid=(batch,),
        in_specs=[blk, blk, blk, pl.BlockSpec(rel_bias_t.shape, lambda b: (0, 0)),
                  pl.BlockSpec((None, REL_BUCKETS, 3 * ATT_NK), lambda b: (g, 0, 0)),
                  pl.BlockSpec((1, 3 * ATT_NK), lambda b: (0, 0))],
        out_specs=[pl.BlockSpec((None, seq, ATT_GW), lambda b: (b, 0, 0)),
                   pl.BlockSpec((None, seq, LANES), lambda b: (b, 0, 0))],
        out_shape=[jax.ShapeDtypeStruct((batch, seq, ATT_GW), BF16),
                   jax.ShapeDtypeStruct((batch, seq, LANES), F32)],
        scratch_shapes=scratch,
        compiler_params=_params(1),
        name="attention_g%d" % g,
    )(aq, ak, av, rel_bias_t, onehot, neg)
    return o.reshape(batch * seq, ATT_GW), lse.reshape(batch * seq, LANES)


def _tail_math(act_refs, w_refs, y_ref, combine, middle=None):
    wrb_ref, wab_ref, wo_ref, wu_ref, wd_ref, wpl_ref, wpg_ref, ln2_ref, lnf_ref = w_refs
    if combine:
        ret_ref, o0_ref, o1_ref, o2_ref, l0_ref, l1_ref, l2_ref, gr_ref, ga_ref, x_ref, ple_ref = act_refs
        lses = [l0_ref[...], l1_ref[...], l2_ref[...]]
        outs = [o0_ref, o1_ref, o2_ref]
        parts = []
        for h in range(ATT_HPG):
            lh = [l[:, h:h + 1] for l in lses]
            mx = jnp.maximum(jnp.maximum(lh[0], lh[1]), lh[2])
            e = [jnp.exp(l - mx) for l in lh]
            tot = e[0] + e[1] + e[2]
            acc = None
            for g in range(ATT_GROUPS):
                term = (e[g] / tot) * outs[g][:, h * ATT_HD:(h + 1) * ATT_HD].astype(F32)
                acc = term if acc is None else acc + term
            parts.append(acc)
        att = jnp.concatenate(parts, axis=1).astype(BF16)
    else:
        ret_ref, att_ref, gr_ref, ga_ref, x_ref, ple_ref = act_refs
        att = att_ref[...].astype(BF16)
    a = _dot(ret_ref[...].astype(BF16), wrb_ref[...])
    b = _dot(att, wab_ref[...])
    mixed = _sigmoid(gr_ref[...].astype(F32)) * a + _sigmoid(ga_ref[...].astype(F32)) * b
    x1 = x_ref[...] + _dot(mixed.astype(BF16), wo_ref[...])
    h2 = _rms(x1, ln2_ref[...]).astype(BF16)
    ff_chunk = D_MODEL
    acc = None
    for c in range(D_FF // ff_chunk):
        u = _dot(h2, wu_ref[:, c * ff_chunk:(c + 1) * ff_chunk])
        r = jnp.maximum(u, 0.0)
        t = _dot((r * r).astype(BF16), wd_ref[c * ff_chunk:(c + 1) * ff_chunk, :])
        acc = t if acc is None else acc + t
        if middle is not None and c == D_FF // ff_chunk // 2 - 1:
            middle()
    x2 = x1 + acc
    gate = _sigmoid(_dot(x2.astype(BF16), wpg_ref[...]))
    x3 = x2 + gate * _dot(ple_ref[...].astype(BF16), wpl_ref[...])
    y_ref[...] = _rms(x3, lnf_ref[...])


def _tail_shift_body(*refs, n_act, n_sample, n_w, n_steps):
    nc = 2 * ATT_GROUPS
    acts = refs[:n_act]
    sample_acts = refs[n_act:n_act + n_sample]
    base = n_act + n_sample
    ws = refs[base:base + n_w]
    new_ref = refs[base + n_w]
    base += n_w + 1
    old = refs[base:base + nc]
    y_ref, ys_ref = refs[base + nc:base + nc + 2]
    base += nc + 2
    out = refs[base:base + nc]
    stage = refs[base + nc:base + 2 * nc]
    sem_in, sem_out, sem_row = refs[base + 2 * nc:]
    s = pl.program_id(0)
    keys = [2 * g for g in range(ATT_GROUPS)]
    values = [2 * g + 1 for g in range(ATT_GROUPS)]

    def copy_in(i, seq):
        width = old[i].shape[1]
        return pltpu.make_async_copy(old[i].at[seq, pl.ds(1, width - 1)], stage[i], sem_in.at[i])

    def copy_out(i, seq):
        width = old[i].shape[1]
        return pltpu.make_async_copy(stage[i], out[i].at[seq, pl.ds(0, width - 1)], sem_out.at[i])

    def copy_row(i):
        g, kind = divmod(i, 2)
        return pltpu.make_async_copy(new_ref.at[s, kind + 1, g], out[i].at[s, old[i].shape[1] - 1], sem_row.at[i])

    @pl.when(s == 0)
    def _():
        for i in keys:
            copy_in(i, 0).start()
    for i in keys:
        copy_in(i, s).wait()
    for i in keys:
        copy_out(i, s).start()

    @pl.when(s > 0)
    def _():
        for i in values:
            copy_out(i, s - 1).wait()
    for i in values:
        copy_in(i, s).start()
    for i in range(nc):
        copy_row(i).start()

    def middle():
        for i in values:
            copy_in(i, s).wait()
        for i in values:
            copy_out(i, s).start()
        for i in keys:
            copy_out(i, s).wait()

        @pl.when(s < n_steps - 1)
        def _():
            for i in keys:
                copy_in(i, s + 1).start()

    _tail_math(acts, ws, y_ref, True, middle)

    @pl.when(s == n_steps - 1)
    def _():
        _tail_math(sample_acts, ws, ys_ref, False)

    for i in range(nc):
        copy_row(i).wait()

    @pl.when(s == n_steps - 1)
    def _():
        for i in values:
            copy_out(i, s).wait()


def _tail_shift(acts, sample_acts, weights, tm, new_qkv, caches):
    n = acts[0].shape[0]
    n_steps = n // tm
    assert n_steps == new_qkv.shape[0]
    row = lambda a: pl.BlockSpec((tm, a.shape[1]), lambda i: (i, 0))
    anyspec = pl.BlockSpec(memory_space=pl.ANY)
    nc = len(caches)
    new_rows = new_qkv.reshape(new_qkv.shape[0], 3, ATT_GROUPS, ATT_HPG, ATT_HD)
    ns = sample_acts[0].shape[0]
    res = pl.pallas_call(
        functools.partial(_tail_shift_body, n_act=len(acts), n_sample=len(sample_acts), n_w=len(weights),
                          n_steps=n_steps),
        grid=(n_steps,),
        in_specs=[row(a) for a in acts] + [_resident(a.shape) for a in sample_acts]
        + [_resident(w.shape) for w in weights] + [anyspec] * (nc + 1),
        out_specs=[pl.BlockSpec((tm, D_MODEL), lambda i: (i, 0)), pl.BlockSpec((ns, D_MODEL), lambda i: (0, 0))]
        + [anyspec] * nc,
        out_shape=[jax.ShapeDtypeStruct((n, D_MODEL), F32), jax.ShapeDtypeStruct((ns, D_MODEL), F32)]
        + [jax.ShapeDtypeStruct(c.shape, c.dtype) for c in caches],
        scratch_shapes=[pltpu.VMEM((c.shape[1] - 1, ATT_HPG, ATT_HD), F32) for c in caches]
        + [pltpu.SemaphoreType.DMA((nc,)), pltpu.SemaphoreType.DMA((nc,)), pltpu.SemaphoreType.DMA((nc,))],
        compiler_params=pltpu.CompilerParams(dimension_semantics=("arbitrary",),
                                             vmem_limit_bytes=VMEM_LIMIT_TAIL_V7X),
        name="tail",
    )(*acts, *sample_acts, *weights, new_rows, *caches)
    return res[0], res[1], res[2:]


def _decode_body(q_ref, k_ref, v_ref, g_ref, gn_ref, st_ref, gam_ref,
                 qkv_ref, rb_ref, slot_ref,
                 ck0_ref, cv0_ref, ck1_ref, cv1_ref, ck2_ref, cv2_ref,
                 ret_ref, att_ref, nst_ref, slot_bias, *, seqs):
    sub = lax.broadcasted_iota(jnp.int32, (8, 2 * RET_DK), 0)
    lane = lax.broadcasted_iota(jnp.int32, (8, 2 * RET_DK), 1)
    row0 = sub == 0
    srow = lax.broadcasted_iota(jnp.int32, (2 * RET_DK, RET_DV), 0)
    for i in range(seqs):
        for p in range(RET_PAIRS):
            pc = slice(p * 2 * RET_DK, (p + 1) * 2 * RET_DK)
            q2 = jnp.where(row0, jnp.broadcast_to(q_ref[i, :, pc], (8, 2 * RET_DK)), 0.0)
            k2 = jnp.where(row0, jnp.broadcast_to(k_ref[i, :, pc], (8, 2 * RET_DK)), 0.0)
            pst = st_ref[i, p]
            gam = gam_ref[p]
            outer = []
            for hh in range(2):
                h = 2 * p + hh
                hc = slice(h * RET_DV, (h + 1) * RET_DV)
                hsel = (lane < RET_DK) if hh == 0 else (lane >= RET_DK)
                qm = jnp.where(hsel, q2, 0.0)
                km = jnp.where(hsel, k2, 0.0)
                vh = v_ref[i, :, hc]
                v8 = jnp.where(row0[:, :RET_DV], jnp.broadcast_to(vh, (8, RET_DV)), 0.0)
                cross = _dot(qm.astype(BF16), (pst * gam).astype(BF16))[0:1, :]
                qk = jnp.sum(qm[0:1, :] * km[0:1, :], axis=-1, keepdims=True)
                o = cross + qk * vh
                ret_ref[i, :, hc] = _gn_swish(o, g_ref[i, :, hc], gn_ref[:, hc])
                outer.append(_dot_tn(k2.astype(BF16), v8.astype(BF16)))
            nst_ref[i, p] = pst * gam + jnp.where(srow < RET_DK, outer[0], outer[1])
    scale = ATT_HD ** -0.5
    caches = ((ck0_ref, cv0_ref), (ck1_ref, cv1_ref), (ck2_ref, cv2_ref))
    slot = lax.broadcasted_iota(jnp.int32, (ATT_NK, ATT_HD), 0)
    is_new = slot == 0
    first = row0[:, :ATT_HD]

    @pl.when(pl.program_id(0) == 0)
    def _():
        pieces = _split3(rb_ref[...])
        for g in range(ATT_GROUPS):
            rows = slice(g * ATT_HPG, (g + 1) * ATT_HPG)
            slot_bias[rows, :] = (_dot(pieces[0][rows], slot_ref[g]) + _dot(pieces[1][rows], slot_ref[g])
                                  + _dot(pieces[2][rows], slot_ref[g]))

    def one_row(x):
        return jnp.where(first, jnp.broadcast_to(x, (8, ATT_HD)), 0.0).astype(BF16)

    scores = {}
    for i in range(seqs):
        for h in range(ATT_HPG):
            for g in range(ATT_GROUPS):
                c0 = g * ATT_GW + h * ATT_HD
                kk = jnp.where(is_new, qkv_ref[i, :, ATT_W + c0:ATT_W + c0 + ATT_HD], caches[g][0][i, :, h, :])
                scores[i, g, h] = _dot_nt(one_row(qkv_ref[i, :, c0:c0 + ATT_HD]), kk.astype(BF16))[0:1, :]
    for i in range(seqs):
        for h in range(ATT_HPG):
            o_g, lse_g = [], []
            for g in range(ATT_GROUPS):
                c0 = g * ATT_GW + h * ATT_HD
                vv = jnp.where(is_new, qkv_ref[i, :, 2 * ATT_W + c0:2 * ATT_W + c0 + ATT_HD],
                               caches[g][1][i, :, h, :])
                gh = g * ATT_HPG + h
                s = scores[i, g, h] * scale + slot_bias[gh:gh + 1, :]
                m = jnp.max(s, axis=-1, keepdims=True)
                pr = jnp.exp(s - m)
                den = jnp.sum(pr, axis=-1, keepdims=True)
                o_g.append(_dot(one_row(pr), vv.astype(BF16))[0:1, :] / den)
                lse_g.append(m + jnp.log(den))
            mx = jnp.maximum(jnp.maximum(lse_g[0], lse_g[1]), lse_g[2])
            e = [jnp.exp(l - mx) for l in lse_g]
            tot = e[0] + e[1] + e[2]
            att_ref[i, :, h * ATT_HD:(h + 1) * ATT_HD] = ((e[0] / tot) * o_g[0] + (e[1] / tot) * o_g[1]
                                                          + (e[2] / tot) * o_g[2])


def _decode(q, k, v, g, gn, state, qkv, rel_bias_t, caches):
    batch = q.shape[0]
    gam = jnp.asarray(np.ascontiguousarray(np.broadcast_to(
        np.repeat(np.exp(_ret_log_decay()), RET_DK).reshape(RET_PAIRS, 2 * RET_DK, 1),
        (RET_PAIRS, 2 * RET_DK, RET_DV))), F32)
    vec = lambda a: a.reshape(batch, 1, a.shape[1])
    seqs = DECODE_SEQS
    assert batch % seqs == 0
    vspec = lambda width: pl.BlockSpec((seqs, 1, width), lambda b: (b, 0, 0))
    const = lambda a: pl.BlockSpec(a.shape, lambda b: (0,) * a.ndim)
    st_spec = pl.BlockSpec((seqs, RET_PAIRS, 2 * RET_DK, RET_DV), lambda b: (b, 0, 0, 0))
    cache_in, cache_specs = [], []
    for gi in range(ATT_GROUPS):
        dil = ATT_DILATIONS[gi]
        for c in caches[2 * gi:2 * gi + 2]:
            cache_in.append(c.reshape(batch, ATT_NK, dil, ATT_HPG, ATT_HD))
            cache_specs.append(pl.BlockSpec((seqs, ATT_NK, None, ATT_HPG, ATT_HD), lambda b: (b, 0, 0, 0, 0)))
    st_pairs = state.reshape(batch, RET_PAIRS, 2 * RET_DK, RET_DV)
    slot_sel = _slot_select()
    ret, att, nst = pl.pallas_call(
        functools.partial(_decode_body, seqs=seqs),
        grid=(batch // seqs,),
        in_specs=[vspec(RET_QK_W), vspec(RET_QK_W), vspec(RET_V_W), vspec(RET_V_W), const(gn), st_spec, const(gam),
                  vspec(3 * ATT_W), const(rel_bias_t), const(slot_sel)] + cache_specs,
        out_specs=[vspec(RET_V_W), vspec(ATT_GW), st_spec],
        out_shape=[jax.ShapeDtypeStruct((batch, 1, RET_V_W), F32),
                   jax.ShapeDtypeStruct((batch, 1, ATT_GW), F32),
                   jax.ShapeDtypeStruct(st_pairs.shape, F32)],
        scratch_shapes=[pltpu.VMEM((rel_bias_t.shape[0], ATT_NK), F32)],
        compiler_params=_params(1),
        name="decode",
    )(vec(q), vec(k), vec(v), vec(g), gn, st_pairs, gam, vec(qkv), rel_bias_t, slot_sel, *cache_in)
    return ret.reshape(batch, RET_V_W), att.reshape(batch, ATT_GW), nst.reshape(state.shape)


def kernel(x_prompt, x_sample, state_ret, cache_k_w128, cache_v_w128, cache_k_w512, cache_v_w512,
           cache_k_w2048, cache_v_w2048, p_prompt, p_sample, ln1_g, w_in, ret_gn_g, w_ret_br, w_att_br,
           w_out, ln2_g, w_up, w_down, w_ple, w_ple_gate, rel_bias, lnf_g):
    depth = w_in.shape[0]
    assert depth == 1
    batch, seq, _ = x_prompt.shape
    dec_batch, dec_seq, _ = x_sample.shape
    assert dec_seq == 1
    past_len = 16384
    l = 0
    ln1 = ln1_g[l][None, :]
    ln2 = ln2_g[l][None, :]
    lnf = lnf_g[None, :]
    gn = ret_gn_g[l][None, :]
    w_in_lo = w_in[l].astype(BF16)
    tail_w = (w_ret_br[l].astype(BF16), w_att_br[l].astype(BF16), w_out[l].astype(BF16), w_up[l].astype(BF16),
              w_down[l].astype(BF16), w_ple[l].astype(BF16), w_ple_gate[l].astype(BF16), ln2, lnf)
    inv_row = _rope_inv_row()

    caches = (cache_k_w128[l], cache_v_w128[l], cache_k_w512[l], cache_v_w512[l],
              cache_k_w2048[l], cache_v_w2048[l])

    xs = x_sample.reshape(dec_batch, D_MODEL)
    xp = x_prompt.reshape(batch * seq, D_MODEL)
    rq, rk, rv, rg, gr, ga, sq, sk, sv, sg, sgr, sga = _inproj_ret(xp, xs, ln1, w_in_lo, inv_row, TM_INPROJ, seq,
                                                                   past_len)
    att_o = _inproj_att(xp, xs, ln1, w_in_lo, TM_INPROJ, seq)
    aqs, aks, avs = att_o[0:3], att_o[3:6], att_o[6:9]
    kfull, vfull = att_o[9:12], att_o[12:15]
    s_qkv = att_o[15]

    rel_bias_t = rel_bias.T
    s_ret, s_attn, new_st = _decode(sq, sk, sv, sg, gn, state_ret[l], s_qkv, rel_bias_t, caches)
    ple_s = p_sample[l].reshape(dec_batch, D_PLE)

    ret_out, st_p = _retention_prompt(rq, rk, rv, rg, gn, batch, seq, TS_RETENTION)
    outs, lses = [], []
    band_onehot, band_neg = _band_select()
    for g in range(ATT_GROUPS):
        o, lse = _attention_prompt(aqs[g], aks[g], avs[g], rel_bias_t, band_onehot, band_neg, g)
        outs.append(o)
        lses.append(lse)
    ple_p = p_prompt[l].reshape(batch * seq, D_PLE)
    tm_tail = batch * seq // dec_batch
    y_p, y_s, new_caches = _tail_shift([ret_out] + outs + lses + [gr, ga, xp, ple_p],
                                       [s_ret, s_attn, sgr, sga, xs, ple_s], tail_w, tm_tail, s_qkv, caches)
    y_prompt = y_p.reshape(batch, seq, D_MODEL)
    y_sample = y_s.reshape(dec_batch, 1, D_MODEL)
    new_state_p = st_p[None]
    kv_p = []
    for g in range(ATT_GROUPS):
        shape = (1, batch, min(ATT_WINDOWS[g], seq), ATT_HPG, ATT_HD)
        kv_p.append(kfull[g].reshape(shape))
        kv_p.append(vfull[g].reshape(shape))
    kv_s = [c[None] for c in new_caches]

    return (y_prompt, y_sample, new_state_p, *kv_p, new_st[None], *kv_s)
```

```python
import functools
import math

import jax
import jax.numpy as jnp
import numpy as np
from jax import lax
from jax.experimental import pallas as pl
from jax.experimental.pallas import tpu as pltpu

F32 = jnp.float32
BF16 = jnp.bfloat16

D_MODEL = 1024
RET_HEADS = 8
RET_DK = 64
RET_DV = 128
RET_PAIRS = RET_HEADS // 2
RET_CHUNK = 128
ROPE_BASE = 10000.0
ATT_WINDOWS = (128, 512, 2048)
ATT_DILATIONS = (1, 4, 16)
ATT_GROUPS = 3
ATT_HPG = 4
ATT_HD = 128
ATT_NK = 128
ATT_GW = ATT_HPG * ATT_HD
REL_BUCKETS = 32
REL_MAX_DIST = 2048
D_FF = 4 * D_MODEL
D_PLE = 256
NORM_EPS = 1e-6
RET_QK_W = RET_HEADS * RET_DK
RET_V_W = RET_HEADS * RET_DV
ATT_W = ATT_GROUPS * ATT_GW
COL_RET = 0
COL_ATT = 2 * RET_QK_W + 2 * RET_V_W
COL_GATE = COL_ATT + 3 * ATT_W
N_IN = COL_GATE + 2 * D_MODEL

VMEM_LIMIT_V7X = 56 * 1024 * 1024
VMEM_LIMIT_TAIL_V7X = 62 * 1024 * 1024
LANES = 128
TM_INPROJ = 512
TS_RETENTION = 512
ATTN_UNROLL = 8
PERM_ROWS = 256
SINGLE_BLOCK_GROUP = 4
DECODE_SEQS = 4


def _dot(a, b):
    return jnp.dot(a, b, preferred_element_type=F32)


def _dot_nt(a, b):
    return lax.dot_general(a, b, (((1,), (1,)), ((), ())), preferred_element_type=F32)


def _dot_tn(a, b):
    return lax.dot_general(a, b, (((0,), (0,)), ((), ())), preferred_element_type=F32)


def _rms(x, g):
    return x * lax.rsqrt(jnp.mean(x * x, axis=-1, keepdims=True) + NORM_EPS) * g


def _sigmoid(x):
    return 1.0 / (1.0 + jnp.exp(-x))


def _resident(shape):
    return pl.BlockSpec(shape, lambda *_: (0,) * len(shape), pipeline_mode=pl.Buffered(1))


def _col_window(rows, width, block_index):
    return pl.BlockSpec((rows, width), lambda *_: (0, block_index), pipeline_mode=pl.Buffered(1))


def _params(n_axes):
    return pltpu.CompilerParams(dimension_semantics=("arbitrary",) * n_axes,
                                vmem_limit_bytes=VMEM_LIMIT_V7X)


def _rope_inv_row():
    half = RET_DK // 2
    inv = ROPE_BASE ** (-jnp.arange(half, dtype=F32) / half)
    return jnp.tile(inv, LANES // half)[None, :]


def _ret_log_decay():
    return np.log1p(-np.exp2(-5.0 - np.arange(RET_HEADS, dtype=np.float32))).astype(np.float32)


def _ret_tables():
    c = RET_CHUNK
    lg = _ret_log_decay()
    i = np.arange(c, dtype=np.float32)
    diff = i[:, None] - i[None, :]
    dmask = np.where(diff[None] >= 0, np.exp(np.maximum(diff, 0.0)[None] * lg[:, None, None]), 0.0)
    q_decay = np.exp((i + 1.0)[:, None] * lg[None, :])
    k_decay = np.exp((c - 1.0 - i)[:, None] * lg[None, :])
    qdec = np.broadcast_to(q_decay.T[:, :, None], (RET_HEADS, c, RET_DV))
    kdec = np.repeat(k_decay, RET_DK, axis=1).reshape(c, RET_PAIRS, 2 * RET_DK).transpose(1, 0, 2)
    gc = np.repeat(np.exp(c * lg), RET_DV).reshape(RET_PAIRS, 1, 2 * RET_DV)
    return tuple(jnp.asarray(np.ascontiguousarray(t), F32) for t in (dmask, qdec, kdec, gc))


def _rel_buckets():
    max_exact = REL_BUCKETS // 2
    out = []
    for dil in ATT_DILATIONS:
        d = np.arange(ATT_NK, dtype=np.int32) * dil
        log_ratio = (np.log(np.maximum(d, 1).astype(np.float32) / np.float32(max_exact))
                     / np.float32(math.log(REL_MAX_DIST / max_exact)))
        large = max_exact + (log_ratio * np.float32(REL_BUCKETS - max_exact)).astype(np.int32)
        out.append(np.where(d < max_exact, d, np.minimum(large, REL_BUCKETS - 1)))
    return np.stack(out)


def _band_select():
    nk = ATT_NK
    buckets = _rel_buckets()
    onehot = np.zeros((ATT_GROUPS, REL_BUCKETS, 3 * nk), np.float32)
    for g in range(ATT_GROUPS):
        for k in range(nk, 2 * nk):
            onehot[g, buckets[g, 2 * nk - 1 - k], k] = 1.0
    mask = np.full((1, 3 * nk), -np.inf, np.float32)
    mask[0, nk:2 * nk] = 0.0
    return jnp.asarray(onehot, BF16), jnp.asarray(mask, F32)


def _slot_select():
    buckets = _rel_buckets()
    onehot = np.zeros((ATT_GROUPS, REL_BUCKETS, ATT_NK), np.float32)
    for g in range(ATT_GROUPS):
        for slot in range(ATT_NK):
            onehot[g, buckets[g, 0 if slot == 0 else ATT_NK - slot], slot] = 1.0
    return jnp.asarray(onehot, BF16)


def _split3(x):
    hi = x.astype(BF16)
    rem = x - hi.astype(F32)
    mid = rem.astype(BF16)
    lo = (rem - mid.astype(F32)).astype(BF16)
    return hi, mid, lo


def _inproj_ret_body(x_ref, xs_ref, ln_ref, w_ref, wg0_ref, wg1_ref, wg2_ref, wg3_ref, inv_ref, *refs,
                     tm, tiles, n_steps, sample_pos):
    outs, sample_outs = refs[:6], refs[6:12]
    cos_s, sin_s = refs[12:]
    i = pl.program_id(0)

    def first_half(rows):
        lane = lax.broadcasted_iota(jnp.int32, (rows, LANES), 1)
        return (lane % RET_DK) < (RET_DK // 2)

    def tables(pos):
        ang = pos.astype(F32) * inv_ref[...]
        sin = jnp.sin(ang)
        return jnp.cos(ang), jnp.where(first_half(pos.shape[0]), -sin, sin)

    def project(x, cos, sin, q_ref, k_ref, v_ref, g_ref, gr_ref, ga_ref):
        h = _rms(x, ln_ref[...]).astype(BF16)
        qk = _dot(h, w_ref[:, 0:2 * RET_QK_W])
        n_q = RET_QK_W // LANES
        fh = first_half(x.shape[0])
        for c in range(2 * n_q):
            xc = qk[:, c * LANES:(c + 1) * LANES]
            swapped = jnp.where(fh, pltpu.roll(xc, LANES - RET_DK // 2, 1), pltpu.roll(xc, RET_DK // 2, 1))
            r = xc * cos + swapped * sin
            if c < n_q:
                q_ref[:, c * LANES:(c + 1) * LANES] = r.astype(q_ref.dtype)
            else:
                k_ref[:, (c - n_q) * LANES:(c - n_q + 1) * LANES] = (r * (RET_DK ** -0.5)).astype(k_ref.dtype)
        o = 2 * RET_QK_W
        v_ref[...] = _dot(h, w_ref[:, o:o + RET_V_W]).astype(v_ref.dtype)
        o += RET_V_W
        g_ref[...] = _dot(h, w_ref[:, o:o + RET_V_W]).astype(g_ref.dtype)
        half = D_MODEL // 2
        gr_ref[:, :half] = _dot(h, wg0_ref[...]).astype(gr_ref.dtype)
        gr_ref[:, half:] = _dot(h, wg1_ref[...]).astype(gr_ref.dtype)
        ga_ref[:, :half] = _dot(h, wg2_ref[...]).astype(ga_ref.dtype)
        ga_ref[:, half:] = _dot(h, wg3_ref[...]).astype(ga_ref.dtype)

    trow = pl.ds(pl.multiple_of(lax.rem(i, tiles) * tm, tm), tm)

    @pl.when(i < tiles)
    def _():
        cos_s[trow, :], sin_s[trow, :] = tables(i * tm + lax.broadcasted_iota(jnp.int32, (tm, LANES), 0))
    project(x_ref[...], cos_s[trow, :], sin_s[trow, :], *outs)

    @pl.when(i == n_steps - 1)
    def _():
        ns = xs_ref.shape[0]
        project(xs_ref[...], *tables(jnp.full((ns, LANES), sample_pos, jnp.int32)), *sample_outs)


def _inproj_ret(x2d, xs2d, ln, w_in, inv_row, tm, seq, sample_pos):
    n = x2d.shape[0]
    ns = xs2d.shape[0]
    tiles = seq // tm
    n_steps = n // tm
    row = lambda width: pl.BlockSpec((tm, width), lambda i: (i, 0))
    widths = (RET_QK_W, RET_QK_W, RET_V_W, RET_V_W, D_MODEL, D_MODEL)
    half = D_MODEL // 2
    assert COL_GATE % half == 0
    return pl.pallas_call(
        functools.partial(_inproj_ret_body, tm=tm, tiles=tiles, n_steps=n_steps, sample_pos=sample_pos),
        grid=(n_steps,),
        in_specs=[row(D_MODEL), _resident(xs2d.shape), _resident((1, D_MODEL)), _col_window(D_MODEL, COL_ATT, 0)]
        + [_col_window(D_MODEL, half, COL_GATE // half + j) for j in range(4)] + [_resident((1, LANES))],
        out_specs=[row(wd) for wd in widths] + [pl.BlockSpec((ns, wd), lambda i: (0, 0)) for wd in widths],
        out_shape=[jax.ShapeDtypeStruct((n, wd), BF16) for wd in widths]
        + [jax.ShapeDtypeStruct((ns, wd), F32) for wd in widths],
        scratch_shapes=[pltpu.VMEM((seq, LANES), F32), pltpu.VMEM((seq, LANES), F32)],
        compiler_params=_params(1),
        name="inproj_ret",
    )(x2d, xs2d, ln, w_in, w_in, w_in, w_in, w_in, inv_row)


def _inproj_att_body(x_ref, xs_ref, ln_ref, wq_ref, wk_ref, wv_ref, *refs, tm, keeps, seq, n_steps):
    lowp = refs[:3 * ATT_GROUPS]
    full = refs[3 * ATT_GROUPS:5 * ATT_GROUPS]
    sample_ref = refs[5 * ATT_GROUPS]

    @pl.when(pl.program_id(0) == n_steps - 1)
    def _():
        hs = _rms(xs_ref[...], ln_ref[...]).astype(BF16)
        for kind, w_ref in enumerate((wq_ref, wk_ref, wv_ref)):
            for g in range(ATT_GROUPS):
                c = kind * ATT_GROUPS + g
                sample_ref[:, c * ATT_GW:(c + 1) * ATT_GW] = _dot(hs, w_ref[:, g * ATT_GW:(g + 1) * ATT_GW])

    h = _rms(x_ref[...], ln_ref[...]).astype(BF16)
    for g in reversed(range(ATT_GROUPS)):
        for kind in reversed(range(3)):
            dil = ATT_DILATIONS[g]
            dst = lowp[kind * ATT_GROUPS + g]
            r = _dot(h, (wq_ref, wk_ref, wv_ref)[kind][:, g * ATT_GW:(g + 1) * ATT_GW])
            if dil == 1:
                dst[0] = r.astype(dst.dtype)
            else:
                dst[...] = jnp.swapaxes(r.reshape(tm // dil, dil, ATT_GW), 0, 1).astype(dst.dtype)
            if kind > 0:
                cache = full[(kind - 1) * ATT_GROUPS + g]
                rows = tm if keeps[g] == seq else keeps[g]
                for hh in range(ATT_HPG):
                    cache[pl.ds(hh, rows, stride=ATT_HPG), :] = r[tm - rows:, hh * ATT_HD:(hh + 1) * ATT_HD]


def _inproj_att(x2d, xs2d, ln, w_in, tm, seq):
    n = x2d.shape[0]
    ns = xs2d.shape[0]
    batch = n // seq
    tiles = seq // tm
    keeps = tuple(min(wd, seq) for wd in ATT_WINDOWS)
    assert all(kp <= tm or kp == seq for kp in keeps) and seq % tm == 0
    out_specs, out_shape = [], []
    for _ in range(3):
        for g in range(ATT_GROUPS):
            dil = ATT_DILATIONS[g]
            out_specs.append(pl.BlockSpec((None, dil, tm // dil, ATT_GW), lambda i: (i // tiles, 0, i % tiles, 0)))
            out_shape.append(jax.ShapeDtypeStruct((batch, dil, seq // dil, ATT_GW), BF16))
    for _ in range(2):
        for g in range(ATT_GROUPS):
            if keeps[g] == seq:
                idx = lambda i: (i // tiles, i % tiles, 0)
                rows = tm
            else:
                idx = lambda i: (i // tiles, 0, 0)
                rows = keeps[g]
            out_specs.append(pl.BlockSpec((None, rows * ATT_HPG, ATT_HD), idx))
            out_shape.append(jax.ShapeDtypeStruct((batch, keeps[g] * ATT_HPG, ATT_HD), F32))
    out_specs.append(pl.BlockSpec((ns, 3 * ATT_W), lambda i: (0, 0)))
    out_shape.append(jax.ShapeDtypeStruct((ns, 3 * ATT_W), F32))
    return pl.pallas_call(
        functools.partial(_inproj_att_body, tm=tm, keeps=keeps, seq=seq, n_steps=n // tm),
        grid=(n // tm,),
        in_specs=[pl.BlockSpec((tm, D_MODEL), lambda i: (i, 0)), _resident(xs2d.shape), _resident((1, D_MODEL))]
        + [_col_window(D_MODEL, ATT_W, COL_ATT // ATT_W + kind) for kind in range(3)],
        out_specs=out_specs,
        out_shape=out_shape,
        compiler_params=_params(1),
        name="inproj_att",
    )(x2d, xs2d, ln, w_in, w_in, w_in)


def _gn_swish(o, gate, gn):
    mu = jnp.mean(o, axis=-1, keepdims=True)
    d = o - mu
    var = jnp.mean(d * d, axis=-1, keepdims=True)
    on = d * lax.rsqrt(var + NORM_EPS) * gn
    return gate * _sigmoid(gate) * on


def _retention_body(q_ref, k_ref, v_ref, g_ref, gn_ref, dm_ref, qdec_ref, kdec_ref, gc_ref,
                    out_ref, st_ref, state, *, n_chunks):
    c = RET_CHUNK

    @pl.when(pl.program_id(1) == 0)
    def _():
        state[...] = jnp.zeros_like(state)

    lane = lax.broadcasted_iota(jnp.int32, (c, 2 * RET_DK), 1)
    head0 = lane < RET_DK

    def chunk(ci, carry):
        rows = pl.ds(pl.multiple_of(ci * c, c), c)
        scores, cross, values = {}, {}, {}
        for p in range(RET_PAIRS):
            q2 = q_ref[rows, p * 2 * RET_DK:(p + 1) * 2 * RET_DK]
            k2 = k_ref[rows, p * 2 * RET_DK:(p + 1) * 2 * RET_DK]
            v2 = v_ref[rows, p * 2 * RET_DV:(p + 1) * 2 * RET_DV]
            pst = state[p]
            pst_lo = pst.astype(BF16)
            zero = jnp.zeros_like(q2)
            for hh in range(2):
                h = 2 * p + hh
                qm = jnp.where(head0 if hh == 0 else jnp.logical_not(head0), q2, zero)
                values[h] = v2[:, hh * RET_DV:(hh + 1) * RET_DV]
                scores[h] = _dot_nt(qm, k2)
                cross[h] = _dot(qm, pst_lo[:, hh * RET_DV:(hh + 1) * RET_DV])
            kd = (k2.astype(F32) * kdec_ref[p]).astype(BF16)
            state[p] = pst * gc_ref[p] + _dot_tn(kd, v2)
        for h in range(RET_HEADS):
            o = _dot((scores[h] * dm_ref[h]).astype(BF16), values[h]) + cross[h] * qdec_ref[h]
            gate = g_ref[rows, h * RET_DV:(h + 1) * RET_DV].astype(F32)
            res = _gn_swish(o, gate, gn_ref[:, h * RET_DV:(h + 1) * RET_DV])
            out_ref[rows, h * RET_DV:(h + 1) * RET_DV] = res.astype(out_ref.dtype)
        return carry

    lax.fori_loop(0, n_chunks, chunk, 0, unroll=True)
    for p in range(RET_PAIRS):
        pst = state[p]
        for hh in range(2):
            st_ref[2 * p + hh] = pst[hh * RET_DK:(hh + 1) * RET_DK, hh * RET_DV:(hh + 1) * RET_DV]


def _retention_prompt(q, k, v, g, gn, batch, seq, ts):
    n = q.shape[0]
    steps = seq // ts
    dmask, qdec, kdec, gc = _ret_tables()
    row = lambda width: pl.BlockSpec((ts, width), lambda b, s: (b * steps + s, 0))
    const = lambda a: pl.BlockSpec(a.shape, lambda b, s: (0,) * a.ndim)
    return pl.pallas_call(
        functools.partial(_retention_body, n_chunks=ts // RET_CHUNK),
        grid=(batch, steps),
        in_specs=[row(RET_QK_W), row(RET_QK_W), row(RET_V_W), row(RET_V_W), const(gn),
                  const(dmask), const(qdec), const(kdec), const(gc)],
        out_specs=[row(RET_V_W),
                   pl.BlockSpec((None, RET_HEADS, RET_DK, RET_DV), lambda b, s: (b, 0, 0, 0))],
        out_shape=[jax.ShapeDtypeStruct((n, RET_V_W), BF16),
                   jax.ShapeDtypeStruct((batch, RET_HEADS, RET_DK, RET_DV), F32)],
        scratch_shapes=[pltpu.VMEM((RET_PAIRS, 2 * RET_DK, 2 * RET_DV), F32)],
        compiler_params=_params(2),
        name="retention",
    )(q, k, v, g, gn, dmask, qdec, kdec, gc)


def _attn_body(q_ref, k_ref, v_ref, rb_ref, sel_ref, neg_ref, o_ref, lse_ref, *scratch, group, dil, n_blocks):
    nk = ATT_NK
    scale = ATT_HD ** -0.5
    lane = lax.broadcasted_iota(jnp.int32, (nk, LANES), 1)
    tabs = scratch[-1]

    @pl.when(pl.program_id(0) == 0)
    def _():
        band = neg_ref[...]
        for piece in _split3(rb_ref[group * ATT_HPG:(group + 1) * ATT_HPG, :]):
            band = band + _dot(piece, sel_ref[...])
        col = lax.broadcasted_iota(jnp.int32, (1, 2 * nk), 1)
        for i in range(nk):
            window = band[:, nk - 1 - i:3 * nk - 1 - i]
            for h in range(ATT_HPG):
                tabs[1, h, i:i + 1, :] = window[h:h + 1, :]
                tabs[0, h, i:i + 1, :] = jnp.where(col < nk, -jnp.inf, window[h:h + 1, :])
    if dil > 1:
        o_s, l_s = scratch[:2]

    def chain(s_raw, v_ext, sel, h):
        s = s_raw * scale + (tabs[sel, h] if n_blocks > 1 else tabs[1, h, :, nk:])
        m = jnp.max(s, axis=-1, keepdims=True)
        p = jnp.exp(s - m)
        pv = _dot(p.astype(BF16), v_ext)
        den = pv[:, ATT_HD:]
        return (pv[:, :ATT_HD] / den).astype(BF16), m + jnp.log(den)

    def write(rs, rows, outs):
        lse_tile = jnp.zeros((nk, LANES), F32)
        for h, (o, lse) in enumerate(outs):
            hc = slice(h * ATT_HD, (h + 1) * ATT_HD)
            if dil == 1:
                o_ref[rows, hc] = o
            else:
                o_s[rs, rows, hc] = o
            lse_tile = jnp.where(lane == h, lse, lse_tile)
        if dil == 1:
            lse_ref[rows, :] = lse_tile
        else:
            l_s[rs, rows, :] = lse_tile

    heads = [slice(h * ATT_HD, (h + 1) * ATT_HD) for h in range(ATT_HPG)]

    def block(rs, n, n_prev, sel):
        rows = pl.ds(pl.multiple_of(n * nk, nk), nk)
        prev_rows = pl.ds(pl.multiple_of(n_prev * nk, nk), nk)
        q = q_ref[rs, rows, :]
        k_all = jnp.concatenate([k_ref[rs, prev_rows, :], k_ref[rs, rows, :]], axis=0)
        v_all = jnp.concatenate([v_ref[rs, prev_rows, :], v_ref[rs, rows, :]], axis=0)
        ones = jnp.ones((2 * nk, ATT_HD), BF16)
        outs = []
        for h, hc in enumerate(heads):
            v_ext = jnp.concatenate([v_all[:, hc], ones], axis=1)
            outs.append(chain(_dot_nt(q[:, hc], k_all[:, hc]), v_ext, sel, h))
        write(rs, rows, outs)

    def single_blocks(streams):
        ones = jnp.ones((nk, ATT_HD), BF16)
        scores = [[_dot_nt(q_ref[rs, :, hc], k_ref[rs, :, hc]) for hc in heads] for rs in streams]
        for i, rs in enumerate(streams):
            outs = [chain(scores[i][h], jnp.concatenate([v_ref[rs, :, hc], ones], axis=1), 0, h)
                    for h, hc in enumerate(heads)]
            write(rs, slice(None), outs)

    if n_blocks == 1:
        for r0 in range(0, dil, SINGLE_BLOCK_GROUP):
            single_blocks(range(r0, min(r0 + SINGLE_BLOCK_GROUP, dil)))
    else:
        for rs in range(dil):
            def loop(n, carry, rs=rs):
                block(rs, n, jnp.maximum(n - 1, 0), jnp.minimum(n, 1))
                return carry
            lax.fori_loop(0, n_blocks, loop, 0, unroll=ATTN_UNROLL)

    if dil > 1:
        rows = o_ref.shape[0]
        for j in range(rows // PERM_ROWS):
            src = slice(j * (PERM_ROWS // dil), (j + 1) * (PERM_ROWS // dil))
            dst = slice(j * PERM_ROWS, (j + 1) * PERM_ROWS)
            o_ref[dst, :] = jnp.swapaxes(o_s[:, src, :], 0, 1).reshape(PERM_ROWS, ATT_GW).astype(BF16)
            lse_ref[dst, :] = jnp.swapaxes(l_s[:, src, :], 0, 1).reshape(PERM_ROWS, LANES)


def _attention_prompt(aq, ak, av, rel_bias_t, onehot, neg, g):
    batch, dil, length, _ = aq.shape
    seq = dil * length
    n_blocks = length // ATT_NK
    blk = pl.BlockSpec((None, dil, length, ATT_GW), lambda b: (b, 0, 0, 0))
    scratch = [pltpu.VMEM((dil, length, ATT_GW), BF16), pltpu.VMEM((dil, length, LANES), F32)] if dil > 1 else []
    scratch.append(pltpu.VMEM((2, ATT_HPG, ATT_NK, 2 * ATT_NK), F32))
    o, lse = pl.pallas_call(
        functools.partial(_attn_body, group=g, dil=dil, n_blocks=n_blocks),
        grid=(batch,),
        in_specs=[blk, blk, blk, pl.BlockSpec(rel_bias_t.shape, lambda b: (0, 0)),
                  pl.BlockSpec((None, REL_BUCKETS, 3 * ATT_NK), lambda b: (g, 0, 0)),
                  pl.BlockSpec((1, 3 * ATT_NK), lambda b: (0, 0))],
        out_specs=[pl.BlockSpec((None, seq, ATT_GW), lambda b: (b, 0, 0)),
                   pl.BlockSpec((None, seq, LANES), lambda b: (b, 0, 0))],
        out_shape=[jax.ShapeDtypeStruct((batch, seq, ATT_GW), BF16),
                   jax.ShapeDtypeStruct((batch, seq, LANES), F32)],
        scratch_shapes=scratch,
        compiler_params=_params(1),
        name="attention_g%d" % g,
    )(aq, ak, av, rel_bias_t, onehot, neg)
    return o.reshape(batch * seq, ATT_GW), lse.reshape(batch * seq, LANES)


def _tail_math(act_refs, w_refs, y_ref, combine, middle=None):
    wrb_ref, wab_ref, wo_ref, wu_ref, wd_ref, wpl_ref, wpg_ref, ln2_ref, lnf_ref = w_refs
    if combine:
        ret_ref, o0_ref, o1_ref, o2_ref, l0_ref, l1_ref, l2_ref, gr_ref, ga_ref, x_ref, ple_ref = act_refs
        lses = [l0_ref[...], l1_ref[...], l2_ref[...]]
        outs = [o0_ref, o1_ref, o2_ref]
        parts = []
        for h in range(ATT_HPG):
            lh = [l[:, h:h + 1] for l in lses]
            mx = jnp.maximum(jnp.maximum(lh[0], lh[1]), lh[2])
            e = [jnp.exp(l - mx) for l in lh]
            tot = e[0] + e[1] + e[2]
            acc = None
            for g in range(ATT_GROUPS):
                term = (e[g] / tot) * outs[g][:, h * ATT_HD:(h + 1) * ATT_HD].astype(F32)
                acc = term if acc is None else acc + term
            parts.append(acc)
        att = jnp.concatenate(parts, axis=1).astype(BF16)
    else:
        ret_ref, att_ref, gr_ref, ga_ref, x_ref, ple_ref = act_refs
        att = att_ref[...].astype(BF16)
    a = _dot(ret_ref[...].astype(BF16), wrb_ref[...])
    b = _dot(att, wab_ref[...])
    mixed = _sigmoid(gr_ref[...].astype(F32)) * a + _sigmoid(ga_ref[...].astype(F32)) * b
    x1 = x_ref[...] + _dot(mixed.astype(BF16), wo_ref[...])
    h2 = _rms(x1, ln2_ref[...]).astype(BF16)
    ff_chunk = D_MODEL
    acc = None
    for c in range(D_FF // ff_chunk):
        u = _dot(h2, wu_ref[:, c * ff_chunk:(c + 1) * ff_chunk])
        r = jnp.maximum(u, 0.0)
        t = _dot((r * r).astype(BF16), wd_ref[c * ff_chunk:(c + 1) * ff_chunk, :])
        acc = t if acc is None else acc + t
        if middle is not None and c == D_FF // ff_chunk // 2 - 1:
            middle()
    x2 = x1 + acc
    gate = _sigmoid(_dot(x2.astype(BF16), wpg_ref[...]))
    x3 = x2 + gate * _dot(ple_ref[...].astype(BF16), wpl_ref[...])
    y_ref[...] = _rms(x3, lnf_ref[...])


def _tail_shift_body(*refs, n_act, n_sample, n_w, n_steps):
    nc = 2 * ATT_GROUPS
    acts = refs[:n_act]
    sample_acts = refs[n_act:n_act + n_sample]
    base = n_act + n_sample
    ws = refs[base:base + n_w]
    new_ref = refs[base + n_w]
    base += n_w + 1
    old = refs[base:base + nc]
    y_ref, ys_ref = refs[base + nc:base + nc + 2]
    base += nc + 2
    out = refs[base:base + nc]
    stage = refs[base + nc:base + 2 * nc]
    sem_in, sem_out, sem_row = refs[base + 2 * nc:]
    s = pl.program_id(0)
    keys = [2 * g for g in range(ATT_GROUPS)]
    values = [2 * g + 1 for g in range(ATT_GROUPS)]

    def copy_in(i, seq):
        width = old[i].shape[1]
        return pltpu.make_async_copy(old[i].at[seq, pl.ds(1, width - 1)], stage[i], sem_in.at[i])

    def copy_out(i, seq):
        width = old[i].shape[1]
        return pltpu.make_async_copy(stage[i], out[i].at[seq, pl.ds(0, width - 1)], sem_out.at[i])

    def copy_row(i):
        g, kind = divmod(i, 2)
        return pltpu.make_async_copy(new_ref.at[s, kind + 1, g], out[i].at[s, old[i].shape[1] - 1], sem_row.at[i])

    @pl.when(s == 0)
    def _():
        for i in keys:
            copy_in(i, 0).start()
    for i in keys:
        copy_in(i, s).wait()
    for i in keys:
        copy_out(i, s).start()

    @pl.when(s > 0)
    def _():
        for i in values:
            copy_out(i, s - 1).wait()
    for i in values:
        copy_in(i, s).start()
    for i in range(nc):
        copy_row(i).start()

    def middle():
        for i in values:
            copy_in(i, s).wait()
        for i in values:
            copy_out(i, s).start()
        for i in keys:
            copy_out(i, s).wait()

        @pl.when(s < n_steps - 1)
        def _():
            for i in keys:
                copy_in(i, s + 1).start()

    _tail_math(acts, ws, y_ref, True, middle)

    @pl.when(s == n_steps - 1)
    def _():
        _tail_math(sample_acts, ws, ys_ref, False)

    for i in range(nc):
        copy_row(i).wait()

    @pl.when(s == n_steps - 1)
    def _():
        for i in values:
            copy_out(i, s).wait()


def _tail_shift(acts, sample_acts, weights, tm, new_qkv, caches):
    n = acts[0].shape[0]
    n_steps = n // tm
    assert n_steps == new_qkv.shape[0]
    row = lambda a: pl.BlockSpec((tm, a.shape[1]), lambda i: (i, 0))
    anyspec = pl.BlockSpec(memory_space=pl.ANY)
    nc = len(caches)
    new_rows = new_qkv.reshape(new_qkv.shape[0], 3, ATT_GROUPS, ATT_HPG, ATT_HD)
    ns = sample_acts[0].shape[0]
    res = pl.pallas_call(
        functools.partial(_tail_shift_body, n_act=len(acts), n_sample=len(sample_acts), n_w=len(weights),
                          n_steps=n_steps),
        grid=(n_steps,),
        in_specs=[row(a) for a in acts] + [_resident(a.shape) for a in sample_acts]
        + [_resident(w.shape) for w in weights] + [anyspec] * (nc + 1),
        out_specs=[pl.BlockSpec((tm, D_MODEL), lambda i: (i, 0)), pl.BlockSpec((ns, D_MODEL), lambda i: (0, 0))]
        + [anyspec] * nc,
        out_shape=[jax.ShapeDtypeStruct((n, D_MODEL), F32), jax.ShapeDtypeStruct((ns, D_MODEL), F32)]
        + [jax.ShapeDtypeStruct(c.shape, c.dtype) for c in caches],
        scratch_shapes=[pltpu.VMEM((c.shape[1] - 1, ATT_HPG, ATT_HD), F32) for c in caches]
        + [pltpu.SemaphoreType.DMA((nc,)), pltpu.SemaphoreType.DMA((nc,)), pltpu.SemaphoreType.DMA((nc,))],
        compiler_params=pltpu.CompilerParams(dimension_semantics=("arbitrary",),
                                             vmem_limit_bytes=VMEM_LIMIT_TAIL_V7X),
        name="tail",
    )(*acts, *sample_acts, *weights, new_rows, *caches)
    return res[0], res[1], res[2:]


def _decode_body(q_ref, k_ref, v_ref, g_ref, gn_ref, st_ref, gam_ref,
                 qkv_ref, rb_ref, slot_ref,
                 ck0_ref, cv0_ref, ck1_ref, cv1_ref, ck2_ref, cv2_ref, *refs, seqs, n_cast):
    cast_in = refs[:n_cast]
    ret_ref, att_ref, nst_ref = refs[n_cast:n_cast + 3]
    cast_out = refs[n_cast + 3:2 * n_cast + 3]
    slot_bias = refs[2 * n_cast + 3]
    for src, dst in zip(cast_in, cast_out):
        dst[...] = src[...].astype(dst.dtype)
    sub = lax.broadcasted_iota(jnp.int32, (8, 2 * RET_DK), 0)
    lane = lax.broadcasted_iota(jnp.int32, (8, 2 * RET_DK), 1)
    row0 = sub == 0
    srow = lax.broadcasted_iota(jnp.int32, (2 * RET_DK, RET_DV), 0)
    for i in range(seqs):
        for p in range(RET_PAIRS):
            pc = slice(p * 2 * RET_DK, (p + 1) * 2 * RET_DK)
            q2 = jnp.where(row0, jnp.broadcast_to(q_ref[i, :, pc], (8, 2 * RET_DK)), 0.0)
            k2 = jnp.where(row0, jnp.broadcast_to(k_ref[i, :, pc], (8, 2 * RET_DK)), 0.0)
            pst = st_ref[i, p]
            gam = gam_ref[p]
            outer = []
            for hh in range(2):
                h = 2 * p + hh
                hc = slice(h * RET_DV, (h + 1) * RET_DV)
                hsel = (lane < RET_DK) if hh == 0 else (lane >= RET_DK)
                qm = jnp.where(hsel, q2, 0.0)
                km = jnp.where(hsel, k2, 0.0)
                vh = v_ref[i, :, hc]
                v8 = jnp.where(row0[:, :RET_DV], jnp.broadcast_to(vh, (8, RET_DV)), 0.0)
                cross = _dot(qm.astype(BF16), (pst * gam).astype(BF16))[0:1, :]
                qk = jnp.sum(qm[0:1, :] * km[0:1, :], axis=-1, keepdims=True)
                o = cross + qk * vh
                ret_ref[i, :, hc] = _gn_swish(o, g_ref[i, :, hc], gn_ref[:, hc])
                outer.append(_dot_tn(k2.astype(BF16), v8.astype(BF16)))
            nst_ref[i, p] = pst * gam + jnp.where(srow < RET_DK, outer[0], outer[1])
    scale = ATT_HD ** -0.5
    caches = ((ck0_ref, cv0_ref), (ck1_ref, cv1_ref), (ck2_ref, cv2_ref))
    slot = lax.broadcasted_iota(jnp.int32, (ATT_NK, ATT_HD), 0)
    is_new = slot == 0
    first = row0[:, :ATT_HD]

    @pl.when(pl.program_id(0) == 0)
    def _():
        pieces = _split3(rb_ref[...])
        for g in range(ATT_GROUPS):
            rows = slice(g * ATT_HPG, (g + 1) * ATT_HPG)
            slot_bias[rows, :] = (_dot(pieces[0][rows], slot_ref[g]) + _dot(pieces[1][rows], slot_ref[g])
                                  + _dot(pieces[2][rows], slot_ref[g]))

    def one_row(x):
        return jnp.where(first, jnp.broadcast_to(x, (8, ATT_HD)), 0.0).astype(BF16)

    scores = {}
    for i in range(seqs):
        for h in range(ATT_HPG):
            for g in range(ATT_GROUPS):
                c0 = g * ATT_GW + h * ATT_HD
                kk = jnp.where(is_new, qkv_ref[i, :, ATT_W + c0:ATT_W + c0 + ATT_HD], caches[g][0][i, :, h, :])
                scores[i, g, h] = _dot_nt(one_row(qkv_ref[i, :, c0:c0 + ATT_HD]), kk.astype(BF16))[0:1, :]
    for i in range(seqs):
        for h in range(ATT_HPG):
            o_g, lse_g = [], []
            for g in range(ATT_GROUPS):
                c0 = g * ATT_GW + h * ATT_HD
                vv = jnp.where(is_new, qkv_ref[i, :, 2 * ATT_W + c0:2 * ATT_W + c0 + ATT_HD],
                               caches[g][1][i, :, h, :])
                gh = g * ATT_HPG + h
                s = scores[i, g, h] * scale + slot_bias[gh:gh + 1, :]
                m = jnp.max(s, axis=-1, keepdims=True)
                pr = jnp.exp(s - m)
                den = jnp.sum(pr, axis=-1, keepdims=True)
                o_g.append(_dot(one_row(pr), vv.astype(BF16))[0:1, :] / den)
                lse_g.append(m + jnp.log(den))
            mx = jnp.maximum(jnp.maximum(lse_g[0], lse_g[1]), lse_g[2])
            e = [jnp.exp(l - mx) for l in lse_g]
            tot = e[0] + e[1] + e[2]
            att_ref[i, :, h * ATT_HD:(h + 1) * ATT_HD] = ((e[0] / tot) * o_g[0] + (e[1] / tot) * o_g[1]
                                                          + (e[2] / tot) * o_g[2])


def _decode(q, k, v, g, gn, state, qkv, rel_bias_t, caches, cast):
    batch = q.shape[0]
    gam = jnp.asarray(np.ascontiguousarray(np.broadcast_to(
        np.repeat(np.exp(_ret_log_decay()), RET_DK).reshape(RET_PAIRS, 2 * RET_DK, 1),
        (RET_PAIRS, 2 * RET_DK, RET_DV))), F32)
    vec = lambda a: a.reshape(batch, 1, a.shape[1])
    seqs = DECODE_SEQS
    assert batch % seqs == 0
    vspec = lambda width: pl.BlockSpec((seqs, 1, width), lambda b: (b, 0, 0))
    const = lambda a: pl.BlockSpec(a.shape, lambda b: (0,) * a.ndim)
    st_spec = pl.BlockSpec((seqs, RET_PAIRS, 2 * RET_DK, RET_DV), lambda b: (b, 0, 0, 0))
    cache_in, cache_specs = [], []
    for gi in range(ATT_GROUPS):
        dil = ATT_DILATIONS[gi]
        for c in caches[2 * gi:2 * gi + 2]:
            cache_in.append(c.reshape(batch, ATT_NK, dil, ATT_HPG, ATT_HD))
            cache_specs.append(pl.BlockSpec((seqs, ATT_NK, None, ATT_HPG, ATT_HD), lambda b: (b, 0, 0, 0, 0)))
    st_pairs = state.reshape(batch, RET_PAIRS, 2 * RET_DK, RET_DV)
    slot_sel = _slot_select()
    steps = batch // seqs
    slab = lambda w: pl.BlockSpec((w.shape[0] // steps, w.shape[1]), lambda b: (b, 0))
    assert all(w.shape[0] % (16 * steps) == 0 for w in cast)
    res = pl.pallas_call(
        functools.partial(_decode_body, seqs=seqs, n_cast=len(cast)),
        grid=(steps,),
        in_specs=[vspec(RET_QK_W), vspec(RET_QK_W), vspec(RET_V_W), vspec(RET_V_W), const(gn), st_spec, const(gam),
                  vspec(3 * ATT_W), const(rel_bias_t), const(slot_sel)] + cache_specs + [slab(w) for w in cast],
        out_specs=[vspec(RET_V_W), vspec(ATT_GW), st_spec] + [slab(w) for w in cast],
        out_shape=[jax.ShapeDtypeStruct((batch, 1, RET_V_W), F32),
                   jax.ShapeDtypeStruct((batch, 1, ATT_GW), F32),
                   jax.ShapeDtypeStruct(st_pairs.shape, F32)]
        + [jax.ShapeDtypeStruct(w.shape, BF16) for w in cast],
        scratch_shapes=[pltpu.VMEM((rel_bias_t.shape[0], ATT_NK), F32)],
        compiler_params=_params(1),
        name="decode",
    )(vec(q), vec(k), vec(v), vec(g), gn, st_pairs, gam, vec(qkv), rel_bias_t, slot_sel, *cache_in, *cast)
    ret, att, nst = res[:3]
    return ret.reshape(batch, RET_V_W), att.reshape(batch, ATT_GW), nst.reshape(state.shape), tuple(res[3:])


def kernel(x_prompt, x_sample, state_ret, cache_k_w128, cache_v_w128, cache_k_w512, cache_v_w512,
           cache_k_w2048, cache_v_w2048, p_prompt, p_sample, ln1_g, w_in, ret_gn_g, w_ret_br, w_att_br,
           w_out, ln2_g, w_up, w_down, w_ple, w_ple_gate, rel_bias, lnf_g):
    depth = w_in.shape[0]
    assert depth == 1
    batch, seq, _ = x_prompt.shape
    dec_batch, dec_seq, _ = x_sample.shape
    assert dec_seq == 1
    past_len = 16384
    l = 0
    ln1 = ln1_g[l][None, :]
    ln2 = ln2_g[l][None, :]
    lnf = lnf_g[None, :]
    gn = ret_gn_g[l][None, :]
    w_in_lo = w_in[l].astype(BF16)
    tail_w_f32 = (w_ret_br[l], w_att_br[l], w_out[l], w_up[l], w_down[l], w_ple[l], w_ple_gate[l])
    inv_row = _rope_inv_row()

    caches = (cache_k_w128[l], cache_v_w128[l], cache_k_w512[l], cache_v_w512[l],
              cache_k_w2048[l], cache_v_w2048[l])

    xs = x_sample.reshape(dec_batch, D_MODEL)
    xp = x_prompt.reshape(batch * seq, D_MODEL)
    rq, rk, rv, rg, gr, ga, sq, sk, sv, sg, sgr, sga = _inproj_ret(xp, xs, ln1, w_in_lo, inv_row, TM_INPROJ, seq,
                                                                   past_len)
    att_o = _inproj_att(xp, xs, ln1, w_in_lo, TM_INPROJ, seq)
    aqs, aks, avs = att_o[0:3], att_o[3:6], att_o[6:9]
    kfull, vfull = att_o[9:12], att_o[12:15]
    s_qkv = att_o[15]

    rel_bias_t = rel_bias.T
    s_ret, s_attn, new_st, tail_w_lo = _decode(sq, sk, sv, sg, gn, state_ret[l], s_qkv, rel_bias_t, caches,
                                               tail_w_f32)
    tail_w = tail_w_lo + (ln2, lnf)
    ple_s = p_sample[l].reshape(dec_batch, D_PLE)

    ret_out, st_p = _retention_prompt(rq, rk, rv, rg, gn, batch, seq, TS_RETENTION)
    outs, lses = [], []
    band_onehot, band_neg = _band_select()
    for g in range(ATT_GROUPS):
        o, lse = _attention_prompt(aqs[g], aks[g], avs[g], rel_bias_t, band_onehot, band_neg, g)
        outs.append(o)
        lses.append(lse)
    ple_p = p_prompt[l].reshape(batch * seq, D_PLE)
    tm_tail = batch * seq // dec_batch
    y_p, y_s, new_caches = _tail_shift([ret_out] + outs + lses + [gr, ga, xp, ple_p],
                                       [s_ret, s_attn, sgr, sga, xs, ple_s], tail_w, tm_tail, s_qkv, caches)
    y_prompt = y_p.reshape(batch, seq, D_MODEL)
    y_sample = y_s.reshape(dec_batch, 1, D_MODEL)
    new_state_p = st_p[None]
    kv_p = []
    for g in range(ATT_GROUPS):
        shape = (1, batch, min(ATT_WINDOWS[g], seq), ATT_HPG, ATT_HD)
        kv_p.append(kfull[g].reshape(shape))
        kv_p.append(vfull[g].reshape(shape))
    kv_s = [c[None] for c in new_caches]

    return (y_prompt, y_sample, new_state_p, *kv_p, new_st[None], *kv_s)
```

```python
import functools
import math

import jax
import jax.numpy as jnp
import numpy as np
from jax import lax
from jax.experimental import pallas as pl
from jax.experimental.pallas import tpu as pltpu

F32 = jnp.float32
BF16 = jnp.bfloat16

D_MODEL = 1024
RET_HEADS = 8
RET_DK = 64
RET_DV = 128
RET_PAIRS = RET_HEADS // 2
RET_CHUNK = 128
ROPE_BASE = 10000.0
ATT_WINDOWS = (128, 512, 2048)
ATT_DILATIONS = (1, 4, 16)
ATT_GROUPS = 3
ATT_HPG = 4
ATT_HD = 128
ATT_NK = 128
ATT_GW = ATT_HPG * ATT_HD
REL_BUCKETS = 32
REL_MAX_DIST = 2048
D_FF = 4 * D_MODEL
D_PLE = 256
NORM_EPS = 1e-6
RET_QK_W = RET_HEADS * RET_DK
RET_V_W = RET_HEADS * RET_DV
ATT_W = ATT_GROUPS * ATT_GW
COL_RET = 0
COL_ATT = 2 * RET_QK_W + 2 * RET_V_W
COL_GATE = COL_ATT + 3 * ATT_W
N_IN = COL_GATE + 2 * D_MODEL

VMEM_LIMIT_V7X = 56 * 1024 * 1024
VMEM_LIMIT_TAIL_V7X = 62 * 1024 * 1024
LANES = 128
TM_INPROJ = 512
TS_RETENTION = 512
ATTN_UNROLL = 8
PERM_ROWS = 256
SINGLE_BLOCK_GROUP = 4
DECODE_SEQS = 4


def _dot(a, b):
    return jnp.dot(a, b, preferred_element_type=F32)


def _dot_nt(a, b):
    return lax.dot_general(a, b, (((1,), (1,)), ((), ())), preferred_element_type=F32)


def _dot_tn(a, b):
    return lax.dot_general(a, b, (((0,), (0,)), ((), ())), preferred_element_type=F32)


def _rms(x, g):
    return x * lax.rsqrt(jnp.mean(x * x, axis=-1, keepdims=True) + NORM_EPS) * g


def _sigmoid(x):
    return 1.0 / (1.0 + jnp.exp(-x))


def _resident(shape):
    return pl.BlockSpec(shape, lambda *_: (0,) * len(shape), pipeline_mode=pl.Buffered(1))


def _col_window(rows, width, block_index):
    return pl.BlockSpec((rows, width), lambda *_: (0, block_index), pipeline_mode=pl.Buffered(1))


def _params(n_axes):
    return pltpu.CompilerParams(dimension_semantics=("arbitrary",) * n_axes,
                                vmem_limit_bytes=VMEM_LIMIT_V7X)


def _rope_inv_row():
    half = RET_DK // 2
    inv = ROPE_BASE ** (-jnp.arange(half, dtype=F32) / half)
    return jnp.tile(inv, LANES // half)[None, :]


def _ret_log_decay():
    return np.log1p(-np.exp2(-5.0 - np.arange(RET_HEADS, dtype=np.float32))).astype(np.float32)


def _ret_tables():
    c = RET_CHUNK
    lg = _ret_log_decay()
    i = np.arange(c, dtype=np.float32)
    diff = i[:, None] - i[None, :]
    dmask = np.where(diff[None] >= 0, np.exp(np.maximum(diff, 0.0)[None] * lg[:, None, None]), 0.0)
    q_decay = np.exp((i + 1.0)[:, None] * lg[None, :])
    k_decay = np.exp((c - 1.0 - i)[:, None] * lg[None, :])
    qdec = np.broadcast_to(q_decay.T[:, :, None], (RET_HEADS, c, RET_DV))
    kdec = np.repeat(k_decay, RET_DK, axis=1).reshape(c, RET_PAIRS, 2 * RET_DK).transpose(1, 0, 2)
    gc = np.repeat(np.exp(c * lg), RET_DV).reshape(RET_PAIRS, 1, 2 * RET_DV)
    return tuple(jnp.asarray(np.ascontiguousarray(t), F32) for t in (dmask, qdec, kdec, gc))


def _rel_buckets():
    max_exact = REL_BUCKETS // 2
    out = []
    for dil in ATT_DILATIONS:
        d = np.arange(ATT_NK, dtype=np.int32) * dil
        log_ratio = (np.log(np.maximum(d, 1).astype(np.float32) / np.float32(max_exact))
                     / np.float32(math.log(REL_MAX_DIST / max_exact)))
        large = max_exact + (log_ratio * np.float32(REL_BUCKETS - max_exact)).astype(np.int32)
        out.append(np.where(d < max_exact, d, np.minimum(large, REL_BUCKETS - 1)))
    return np.stack(out)


def _band_select():
    nk = ATT_NK
    buckets = _rel_buckets()
    onehot = np.zeros((ATT_GROUPS, REL_BUCKETS, 3 * nk), np.float32)
    for g in range(ATT_GROUPS):
        for k in range(nk, 2 * nk):
            onehot[g, buckets[g, 2 * nk - 1 - k], k] = 1.0
    mask = np.full((1, 3 * nk), -np.inf, np.float32)
    mask[0, nk:2 * nk] = 0.0
    return jnp.asarray(onehot, BF16), jnp.asarray(mask, F32)


def _slot_select():
    buckets = _rel_buckets()
    onehot = np.zeros((ATT_GROUPS, REL_BUCKETS, ATT_NK), np.float32)
    for g in range(ATT_GROUPS):
        for slot in range(ATT_NK):
            onehot[g, buckets[g, 0 if slot == 0 else ATT_NK - slot], slot] = 1.0
    return jnp.asarray(onehot, BF16)


def _split3(x):
    hi = x.astype(BF16)
    rem = x - hi.astype(F32)
    mid = rem.astype(BF16)
    lo = (rem - mid.astype(F32)).astype(BF16)
    return hi, mid, lo


def _inproj_ret_body(x_ref, xs_ref, ln_ref, w_ref, wg0_ref, wg1_ref, wg2_ref, wg3_ref, inv_ref, *refs,
                     tm, tiles, n_steps, sample_pos):
    outs, sample_outs = refs[:6], refs[6:12]
    cos_s, sin_s = refs[12:]
    i = pl.program_id(0)

    def first_half(rows):
        lane = lax.broadcasted_iota(jnp.int32, (rows, LANES), 1)
        return (lane % RET_DK) < (RET_DK // 2)

    def tables(pos):
        ang = pos.astype(F32) * inv_ref[...]
        sin = jnp.sin(ang)
        return jnp.cos(ang), jnp.where(first_half(pos.shape[0]), -sin, sin)

    def project(x, cos, sin, q_ref, k_ref, v_ref, g_ref, gr_ref, ga_ref):
        h = _rms(x, ln_ref[...]).astype(BF16)
        qk = _dot(h, w_ref[:, 0:2 * RET_QK_W])
        n_q = RET_QK_W // LANES
        fh = first_half(x.shape[0])
        for c in range(2 * n_q):
            xc = qk[:, c * LANES:(c + 1) * LANES]
            swapped = jnp.where(fh, pltpu.roll(xc, LANES - RET_DK // 2, 1), pltpu.roll(xc, RET_DK // 2, 1))
            r = xc * cos + swapped * sin
            if c < n_q:
                q_ref[:, c * LANES:(c + 1) * LANES] = r.astype(q_ref.dtype)
            else:
                k_ref[:, (c - n_q) * LANES:(c - n_q + 1) * LANES] = (r * (RET_DK ** -0.5)).astype(k_ref.dtype)
        o = 2 * RET_QK_W
        v_ref[...] = _dot(h, w_ref[:, o:o + RET_V_W]).astype(v_ref.dtype)
        o += RET_V_W
        g_ref[...] = _dot(h, w_ref[:, o:o + RET_V_W]).astype(g_ref.dtype)
        half = D_MODEL // 2
        gr_ref[:, :half] = _dot(h, wg0_ref[...]).astype(gr_ref.dtype)
        gr_ref[:, half:] = _dot(h, wg1_ref[...]).astype(gr_ref.dtype)
        ga_ref[:, :half] = _dot(h, wg2_ref[...]).astype(ga_ref.dtype)
        ga_ref[:, half:] = _dot(h, wg3_ref[...]).astype(ga_ref.dtype)

    trow = pl.ds(pl.multiple_of(lax.rem(i, tiles) * tm, tm), tm)

    @pl.when(i < tiles)
    def _():
        cos_s[trow, :], sin_s[trow, :] = tables(i * tm + lax.broadcasted_iota(jnp.int32, (tm, LANES), 0))
    project(x_ref[...], cos_s[trow, :], sin_s[trow, :], *outs)

    @pl.when(i == n_steps - 1)
    def _():
        ns = xs_ref.shape[0]
        project(xs_ref[...], *tables(jnp.full((ns, LANES), sample_pos, jnp.int32)), *sample_outs)


def _inproj_ret(x2d, xs2d, ln, w_in, inv_row, tm, seq, sample_pos):
    n = x2d.shape[0]
    ns = xs2d.shape[0]
    tiles = seq // tm
    n_steps = n // tm
    row = lambda width: pl.BlockSpec((tm, width), lambda i: (i, 0))
    widths = (RET_QK_W, RET_QK_W, RET_V_W, RET_V_W, D_MODEL, D_MODEL)
    half = D_MODEL // 2
    assert COL_GATE % half == 0
    return pl.pallas_call(
        functools.partial(_inproj_ret_body, tm=tm, tiles=tiles, n_steps=n_steps, sample_pos=sample_pos),
        grid=(n_steps,),
        in_specs=[row(D_MODEL), _resident(xs2d.shape), _resident((1, D_MODEL)), _col_window(D_MODEL, COL_ATT, 0)]
        + [_col_window(D_MODEL, half, COL_GATE // half + j) for j in range(4)] + [_resident((1, LANES))],
        out_specs=[row(wd) for wd in widths] + [pl.BlockSpec((ns, wd), lambda i: (0, 0)) for wd in widths],
        out_shape=[jax.ShapeDtypeStruct((n, wd), BF16) for wd in widths]
        + [jax.ShapeDtypeStruct((ns, wd), F32) for wd in widths],
        scratch_shapes=[pltpu.VMEM((seq, LANES), F32), pltpu.VMEM((seq, LANES), F32)],
        compiler_params=_params(1),
        name="inproj_ret",
    )(x2d, xs2d, ln, w_in, w_in, w_in, w_in, w_in, inv_row)


def _inproj_att_body(x_ref, xs_ref, ln_ref, wq_ref, wk_ref, wv_ref, *refs, tm, keeps, seq, n_steps):
    lowp = refs[:3 * ATT_GROUPS]
    full = refs[3 * ATT_GROUPS:5 * ATT_GROUPS]
    sample_ref = refs[5 * ATT_GROUPS]

    @pl.when(pl.program_id(0) == n_steps - 1)
    def _():
        hs = _rms(xs_ref[...], ln_ref[...]).astype(BF16)
        for kind, w_ref in enumerate((wq_ref, wk_ref, wv_ref)):
            for g in range(ATT_GROUPS):
                c = kind * ATT_GROUPS + g
                sample_ref[:, c * ATT_GW:(c + 1) * ATT_GW] = _dot(hs, w_ref[:, g * ATT_GW:(g + 1) * ATT_GW])

    h = _rms(x_ref[...], ln_ref[...]).astype(BF16)
    for g in reversed(range(ATT_GROUPS)):
        for kind in reversed(range(3)):
            dil = ATT_DILATIONS[g]
            dst = lowp[kind * ATT_GROUPS + g]
            r = _dot(h, (wq_ref, wk_ref, wv_ref)[kind][:, g * ATT_GW:(g + 1) * ATT_GW])
            if dil == 1:
                dst[0] = r.astype(dst.dtype)
            else:
                dst[...] = jnp.swapaxes(r.reshape(tm // dil, dil, ATT_GW), 0, 1).astype(dst.dtype)
            if kind > 0:
                cache = full[(kind - 1) * ATT_GROUPS + g]
                rows = tm if keeps[g] == seq else keeps[g]
                for hh in range(ATT_HPG):
                    cache[pl.ds(hh, rows, stride=ATT_HPG), :] = r[tm - rows:, hh * ATT_HD:(hh + 1) * ATT_HD]


def _inproj_att(x2d, xs2d, ln, w_in, tm, seq):
    n = x2d.shape[0]
    ns = xs2d.shape[0]
    batch = n // seq
    tiles = seq // tm
    keeps = tuple(min(wd, seq) for wd in ATT_WINDOWS)
    assert all(kp <= tm or kp == seq for kp in keeps) and seq % tm == 0
    out_specs, out_shape = [], []
    for _ in range(3):
        for g in range(ATT_GROUPS):
            dil = ATT_DILATIONS[g]
            out_specs.append(pl.BlockSpec((None, dil, tm // dil, ATT_GW), lambda i: (i // tiles, 0, i % tiles, 0)))
            out_shape.append(jax.ShapeDtypeStruct((batch, dil, seq // dil, ATT_GW), BF16))
    for _ in range(2):
        for g in range(ATT_GROUPS):
            if keeps[g] == seq:
                idx = lambda i: (i // tiles, i % tiles, 0)
                rows = tm
            else:
                idx = lambda i: (i // tiles, 0, 0)
                rows = keeps[g]
            out_specs.append(pl.BlockSpec((None, rows * ATT_HPG, ATT_HD), idx))
            out_shape.append(jax.ShapeDtypeStruct((batch, keeps[g] * ATT_HPG, ATT_HD), F32))
    out_specs.append(pl.BlockSpec((ns, 3 * ATT_W), lambda i: (0, 0)))
    out_shape.append(jax.ShapeDtypeStruct((ns, 3 * ATT_W), F32))
    return pl.pallas_call(
        functools.partial(_inproj_att_body, tm=tm, keeps=keeps, seq=seq, n_steps=n // tm),
        grid=(n // tm,),
        in_specs=[pl.BlockSpec((tm, D_MODEL), lambda i: (i, 0)), _resident(xs2d.shape), _resident((1, D_MODEL))]
        + [_col_window(D_MODEL, ATT_W, COL_ATT // ATT_W + kind) for kind in range(3)],
        out_specs=out_specs,
        out_shape=out_shape,
        compiler_params=_params(1),
        name="inproj_att",
    )(x2d, xs2d, ln, w_in, w_in, w_in)


def _gn_swish(o, gate, gn):
    mu = jnp.mean(o, axis=-1, keepdims=True)
    d = o - mu
    var = jnp.mean(d * d, axis=-1, keepdims=True)
    on = d * lax.rsqrt(var + NORM_EPS) * gn
    return gate * _sigmoid(gate) * on


def _retention_body(q_ref, k_ref, v_ref, g_ref, gn_ref, dm_ref, qdec_ref, kdec_ref, gc_ref, *refs,
                    n_chunks, n_cast):
    cast_in = refs[:n_cast]
    out_ref, st_ref = refs[n_cast:n_cast + 2]
    cast_out = refs[n_cast + 2:2 * n_cast + 2]
    state = refs[2 * n_cast + 2]
    for src, dst in zip(cast_in, cast_out):
        dst[...] = src[...].astype(dst.dtype)
    c = RET_CHUNK

    @pl.when(pl.program_id(1) == 0)
    def _():
        state[...] = jnp.zeros_like(state)

    lane = lax.broadcasted_iota(jnp.int32, (c, 2 * RET_DK), 1)
    head0 = lane < RET_DK

    def chunk(ci, carry):
        rows = pl.ds(pl.multiple_of(ci * c, c), c)
        scores, cross, values = {}, {}, {}
        for p in range(RET_PAIRS):
            q2 = q_ref[rows, p * 2 * RET_DK:(p + 1) * 2 * RET_DK]
            k2 = k_ref[rows, p * 2 * RET_DK:(p + 1) * 2 * RET_DK]
            v2 = v_ref[rows, p * 2 * RET_DV:(p + 1) * 2 * RET_DV]
            pst = state[p]
            pst_lo = pst.astype(BF16)
            zero = jnp.zeros_like(q2)
            for hh in range(2):
                h = 2 * p + hh
                qm = jnp.where(head0 if hh == 0 else jnp.logical_not(head0), q2, zero)
                values[h] = v2[:, hh * RET_DV:(hh + 1) * RET_DV]
                scores[h] = _dot_nt(qm, k2)
                cross[h] = _dot(qm, pst_lo[:, hh * RET_DV:(hh + 1) * RET_DV])
            kd = (k2.astype(F32) * kdec_ref[p]).astype(BF16)
            state[p] = pst * gc_ref[p] + _dot_tn(kd, v2)
        for h in range(RET_HEADS):
            o = _dot((scores[h] * dm_ref[h]).astype(BF16), values[h]) + cross[h] * qdec_ref[h]
            gate = g_ref[rows, h * RET_DV:(h + 1) * RET_DV].astype(F32)
            res = _gn_swish(o, gate, gn_ref[:, h * RET_DV:(h + 1) * RET_DV])
            out_ref[rows, h * RET_DV:(h + 1) * RET_DV] = res.astype(out_ref.dtype)
        return carry

    lax.fori_loop(0, n_chunks, chunk, 0, unroll=True)
    for p in range(RET_PAIRS):
        pst = state[p]
        for hh in range(2):
            st_ref[2 * p + hh] = pst[hh * RET_DK:(hh + 1) * RET_DK, hh * RET_DV:(hh + 1) * RET_DV]


def _retention_prompt(q, k, v, g, gn, batch, seq, ts, cast):
    n = q.shape[0]
    steps = seq // ts
    dmask, qdec, kdec, gc = _ret_tables()
    row = lambda width: pl.BlockSpec((ts, width), lambda b, s: (b * steps + s, 0))
    const = lambda a: pl.BlockSpec(a.shape, lambda b, s: (0,) * a.ndim)
    n_steps = batch * steps
    slab = lambda w: pl.BlockSpec((w.shape[0] // n_steps, w.shape[1]), lambda b, s: (b * steps + s, 0))
    assert all(w.shape[0] % (16 * n_steps) == 0 for w in cast)
    res = pl.pallas_call(
        functools.partial(_retention_body, n_chunks=ts // RET_CHUNK, n_cast=len(cast)),
        grid=(batch, steps),
        in_specs=[row(RET_QK_W), row(RET_QK_W), row(RET_V_W), row(RET_V_W), const(gn),
                  const(dmask), const(qdec), const(kdec), const(gc)] + [slab(w) for w in cast],
        out_specs=[row(RET_V_W),
                   pl.BlockSpec((None, RET_HEADS, RET_DK, RET_DV), lambda b, s: (b, 0, 0, 0))]
        + [slab(w) for w in cast],
        out_shape=[jax.ShapeDtypeStruct((n, RET_V_W), BF16),
                   jax.ShapeDtypeStruct((batch, RET_HEADS, RET_DK, RET_DV), F32)]
        + [jax.ShapeDtypeStruct(w.shape, BF16) for w in cast],
        scratch_shapes=[pltpu.VMEM((RET_PAIRS, 2 * RET_DK, 2 * RET_DV), F32)],
        compiler_params=_params(2),
        name="retention",
    )(q, k, v, g, gn, dmask, qdec, kdec, gc, *cast)
    return res[0], res[1], tuple(res[2:])


def _attn_body(q_ref, k_ref, v_ref, rb_ref, sel_ref, neg_ref, o_ref, lse_ref, *scratch, group, dil, n_blocks):
    nk = ATT_NK
    scale = ATT_HD ** -0.5
    lane = lax.broadcasted_iota(jnp.int32, (nk, LANES), 1)
    tabs = scratch[-1]

    @pl.when(pl.program_id(0) == 0)
    def _():
        band = neg_ref[...]
        for piece in _split3(rb_ref[group * ATT_HPG:(group + 1) * ATT_HPG, :]):
            band = band + _dot(piece, sel_ref[...])
        col = lax.broadcasted_iota(jnp.int32, (1, 2 * nk), 1)
        for i in range(nk):
            window = band[:, nk - 1 - i:3 * nk - 1 - i]
            for h in range(ATT_HPG):
                tabs[1, h, i:i + 1, :] = window[h:h + 1, :]
                tabs[0, h, i:i + 1, :] = jnp.where(col < nk, -jnp.inf, window[h:h + 1, :])
    if dil > 1:
        o_s, l_s = scratch[:2]

    def chain(s_raw, v_ext, sel, h):
        s = s_raw * scale + (tabs[sel, h] if n_blocks > 1 else tabs[1, h, :, nk:])
        m = jnp.max(s, axis=-1, keepdims=True)
        p = jnp.exp(s - m)
        pv = _dot(p.astype(BF16), v_ext)
        den = pv[:, ATT_HD:]
        return (pv[:, :ATT_HD] / den).astype(BF16), m + jnp.log(den)

    def write(rs, rows, outs):
        lse_tile = jnp.zeros((nk, LANES), F32)
        for h, (o, lse) in enumerate(outs):
            hc = slice(h * ATT_HD, (h + 1) * ATT_HD)
            if dil == 1:
                o_ref[rows, hc] = o
            else:
                o_s[rs, rows, hc] = o
            lse_tile = jnp.where(lane == h, lse, lse_tile)
        if dil == 1:
            lse_ref[rows, :] = lse_tile
        else:
            l_s[rs, rows, :] = lse_tile

    heads = [slice(h * ATT_HD, (h + 1) * ATT_HD) for h in range(ATT_HPG)]

    def block(rs, n, n_prev, sel):
        rows = pl.ds(pl.multiple_of(n * nk, nk), nk)
        prev_rows = pl.ds(pl.multiple_of(n_prev * nk, nk), nk)
        q = q_ref[rs, rows, :]
        k_all = jnp.concatenate([k_ref[rs, prev_rows, :], k_ref[rs, rows, :]], axis=0)
        v_all = jnp.concatenate([v_ref[rs, prev_rows, :], v_ref[rs, rows, :]], axis=0)
        ones = jnp.ones((2 * nk, ATT_HD), BF16)
        outs = []
        for h, hc in enumerate(heads):
            v_ext = jnp.concatenate([v_all[:, hc], ones], axis=1)
            outs.append(chain(_dot_nt(q[:, hc], k_all[:, hc]), v_ext, sel, h))
        write(rs, rows, outs)

    def single_blocks(streams):
        ones = jnp.ones((nk, ATT_HD), BF16)
        scores = [[_dot_nt(q_ref[rs, :, hc], k_ref[rs, :, hc]) for hc in heads] for rs in streams]
        for i, rs in enumerate(streams):
            outs = [chain(scores[i][h], jnp.concatenate([v_ref[rs, :, hc], ones], axis=1), 0, h)
                    for h, hc in enumerate(heads)]
            write(rs, slice(None), outs)

    if n_blocks == 1:
        for r0 in range(0, dil, SINGLE_BLOCK_GROUP):
            single_blocks(range(r0, min(r0 + SINGLE_BLOCK_GROUP, dil)))
    else:
        for rs in range(dil):
            def loop(n, carry, rs=rs):
                block(rs, n, jnp.maximum(n - 1, 0), jnp.minimum(n, 1))
                return carry
            lax.fori_loop(0, n_blocks, loop, 0, unroll=ATTN_UNROLL)

    if dil > 1:
        rows = o_ref.shape[0]
        for j in range(rows // PERM_ROWS):
            src = slice(j * (PERM_ROWS // dil), (j + 1) * (PERM_ROWS // dil))
            dst = slice(j * PERM_ROWS, (j + 1) * PERM_ROWS)
            o_ref[dst, :] = jnp.swapaxes(o_s[:, src, :], 0, 1).reshape(PERM_ROWS, ATT_GW).astype(BF16)
            lse_ref[dst, :] = jnp.swapaxes(l_s[:, src, :], 0, 1).reshape(PERM_ROWS, LANES)


def _attention_prompt(aq, ak, av, rel_bias_t, onehot, neg, g):
    batch, dil, length, _ = aq.shape
    seq = dil * length
    n_blocks = length // ATT_NK
    blk = pl.BlockSpec((None, dil, length, ATT_GW), lambda b: (b, 0, 0, 0))
    scratch = [pltpu.VMEM((dil, length, ATT_GW), BF16), pltpu.VMEM((dil, length, LANES), F32)] if dil > 1 else []
    scratch.append(pltpu.VMEM((2, ATT_HPG, ATT_NK, 2 * ATT_NK), F32))
    o, lse = pl.pallas_call(
        functools.partial(_attn_body, group=g, dil=dil, n_blocks=n_blocks),
        grid=(batch,),
        in_specs=[blk, blk, blk, pl.BlockSpec(rel_bias_t.shape, lambda b: (0, 0)),
                  pl.BlockSpec((None, REL_BUCKETS, 3 * ATT_NK), lambda b: (g, 0, 0)),
                  pl.BlockSpec((1, 3 * ATT_NK), lambda b: (0, 0))],
        out_specs=[pl.BlockSpec((None, seq, ATT_GW), lambda b: (b, 0, 0)),
                   pl.BlockSpec((None, seq, LANES), lambda b: (b, 0, 0))],
        out_shape=[jax.ShapeDtypeStruct((batch, seq, ATT_GW), BF16),
                   jax.ShapeDtypeStruct((batch, seq, LANES), F32)],
        scratch_shapes=scratch,
        compiler_params=_params(1),
        name="attention_g%d" % g,
    )(aq, ak, av, rel_bias_t, onehot, neg)
    return o.reshape(batch * seq, ATT_GW), lse.reshape(batch * seq, LANES)


def _tail_math(act_refs, w_refs, y_ref, combine, middle=None):
    wrb_ref, wab_ref, wo_ref, wu_ref, wd_ref, wpl_ref, wpg_ref, ln2_ref, lnf_ref = w_refs
    if combine:
        ret_ref, o0_ref, o1_ref, o2_ref, l0_ref, l1_ref, l2_ref, gr_ref, ga_ref, x_ref, ple_ref = act_refs
        lses = [l0_ref[...], l1_ref[...], l2_ref[...]]
        outs = [o0_ref, o1_ref, o2_ref]
        parts = []
        for h in range(ATT_HPG):
            lh = [l[:, h:h + 1] for l in lses]
            mx = jnp.maximum(jnp.maximum(lh[0], lh[1]), lh[2])
            e = [jnp.exp(l - mx) for l in lh]
            tot = e[0] + e[1] + e[2]
            acc = None
            for g in range(ATT_GROUPS):
                term = (e[g] / tot) * outs[g][:, h * ATT_HD:(h + 1) * ATT_HD].astype(F32)
                acc = term if acc is None else acc + term
            parts.append(acc)
        att = jnp.concatenate(parts, axis=1).astype(BF16)
    else:
        ret_ref, att_ref, gr_ref, ga_ref, x_ref, ple_ref = act_refs
        att = att_ref[...].astype(BF16)
    a = _dot(ret_ref[...].astype(BF16), wrb_ref[...])
    b = _dot(att, wab_ref[...])
    mixed = _sigmoid(gr_ref[...].astype(F32)) * a + _sigmoid(ga_ref[...].astype(F32)) * b
    x1 = x_ref[...] + _dot(mixed.astype(BF16), wo_ref[...])
    h2 = _rms(x1, ln2_ref[...]).astype(BF16)
    ff_chunk = D_MODEL
    acc = None
    for c in range(D_FF // ff_chunk):
        u = _dot(h2, wu_ref[:, c * ff_chunk:(c + 1) * ff_chunk])
        r = jnp.maximum(u, 0.0)
        t = _dot((r * r).astype(BF16), wd_ref[c * ff_chunk:(c + 1) * ff_chunk, :])
        acc = t if acc is None else acc + t
        if middle is not None and c == D_FF // ff_chunk // 2 - 1:
            middle()
    x2 = x1 + acc
    gate = _sigmoid(_dot(x2.astype(BF16), wpg_ref[...]))
    x3 = x2 + gate * _dot(ple_ref[...].astype(BF16), wpl_ref[...])
    y_ref[...] = _rms(x3, lnf_ref[...])


def _tail_shift_body(*refs, n_act, n_sample, n_w, n_steps):
    nc = 2 * ATT_GROUPS
    acts = refs[:n_act]
    sample_acts = refs[n_act:n_act + n_sample]
    base = n_act + n_sample
    ws = refs[base:base + n_w]
    new_ref = refs[base + n_w]
    base += n_w + 1
    old = refs[base:base + nc]
    y_ref, ys_ref = refs[base + nc:base + nc + 2]
    base += nc + 2
    out = refs[base:base + nc]
    stage = refs[base + nc:base + 2 * nc]
    sem_in, sem_out, sem_row = refs[base + 2 * nc:]
    s = pl.program_id(0)
    keys = [2 * g for g in range(ATT_GROUPS)]
    values = [2 * g + 1 for g in range(ATT_GROUPS)]

    def copy_in(i, seq):
        width = old[i].shape[1]
        return pltpu.make_async_copy(old[i].at[seq, pl.ds(1, width - 1)], stage[i], sem_in.at[i])

    def copy_out(i, seq):
        width = old[i].shape[1]
        return pltpu.make_async_copy(stage[i], out[i].at[seq, pl.ds(0, width - 1)], sem_out.at[i])

    def copy_row(i):
        g, kind = divmod(i, 2)
        return pltpu.make_async_copy(new_ref.at[s, kind + 1, g], out[i].at[s, old[i].shape[1] - 1], sem_row.at[i])

    @pl.when(s == 0)
    def _():
        for i in keys:
            copy_in(i, 0).start()
    for i in keys:
        copy_in(i, s).wait()
    for i in keys:
        copy_out(i, s).start()

    @pl.when(s > 0)
    def _():
        for i in values:
            copy_out(i, s - 1).wait()
    for i in values:
        copy_in(i, s).start()
    for i in range(nc):
        copy_row(i).start()

    def middle():
        for i in values:
            copy_in(i, s).wait()
        for i in values:
            copy_out(i, s).start()
        for i in keys:
            copy_out(i, s).wait()

        @pl.when(s < n_steps - 1)
        def _():
            for i in keys:
                copy_in(i, s + 1).start()

    _tail_math(acts, ws, y_ref, True, middle)

    @pl.when(s == n_steps - 1)
    def _():
        _tail_math(sample_acts, ws, ys_ref, False)

    for i in range(nc):
        copy_row(i).wait()

    @pl.when(s == n_steps - 1)
    def _():
        for i in values:
            copy_out(i, s).wait()


def _tail_shift(acts, sample_acts, weights, tm, new_qkv, caches):
    n = acts[0].shape[0]
    n_steps = n // tm
    assert n_steps == new_qkv.shape[0]
    row = lambda a: pl.BlockSpec((tm, a.shape[1]), lambda i: (i, 0))
    anyspec = pl.BlockSpec(memory_space=pl.ANY)
    nc = len(caches)
    new_rows = new_qkv.reshape(new_qkv.shape[0], 3, ATT_GROUPS, ATT_HPG, ATT_HD)
    ns = sample_acts[0].shape[0]
    res = pl.pallas_call(
        functools.partial(_tail_shift_body, n_act=len(acts), n_sample=len(sample_acts), n_w=len(weights),
                          n_steps=n_steps),
        grid=(n_steps,),
        in_specs=[row(a) for a in acts] + [_resident(a.shape) for a in sample_acts]
        + [_resident(w.shape) for w in weights] + [anyspec] * (nc + 1),
        out_specs=[pl.BlockSpec((tm, D_MODEL), lambda i: (i, 0)), pl.BlockSpec((ns, D_MODEL), lambda i: (0, 0))]
        + [anyspec] * nc,
        out_shape=[jax.ShapeDtypeStruct((n, D_MODEL), F32), jax.ShapeDtypeStruct((ns, D_MODEL), F32)]
        + [jax.ShapeDtypeStruct(c.shape, c.dtype) for c in caches],
        scratch_shapes=[pltpu.VMEM((c.shape[1] - 1, ATT_HPG, ATT_HD), F32) for c in caches]
        + [pltpu.SemaphoreType.DMA((nc,)), pltpu.SemaphoreType.DMA((nc,)), pltpu.SemaphoreType.DMA((nc,))],
        compiler_params=pltpu.CompilerParams(dimension_semantics=("arbitrary",),
                                             vmem_limit_bytes=VMEM_LIMIT_TAIL_V7X),
        name="tail",
    )(*acts, *sample_acts, *weights, new_rows, *caches)
    return res[0], res[1], res[2:]


def _decode_body(q_ref, k_ref, v_ref, g_ref, gn_ref, st_ref, gam_ref,
                 qkv_ref, rb_ref, slot_ref,
                 ck0_ref, cv0_ref, ck1_ref, cv1_ref, ck2_ref, cv2_ref, *refs, seqs, n_cast):
    cast_in = refs[:n_cast]
    ret_ref, att_ref, nst_ref = refs[n_cast:n_cast + 3]
    cast_out = refs[n_cast + 3:2 * n_cast + 3]
    slot_bias = refs[2 * n_cast + 3]
    for src, dst in zip(cast_in, cast_out):
        dst[...] = src[...].astype(dst.dtype)
    sub = lax.broadcasted_iota(jnp.int32, (8, 2 * RET_DK), 0)
    lane = lax.broadcasted_iota(jnp.int32, (8, 2 * RET_DK), 1)
    row0 = sub == 0
    srow = lax.broadcasted_iota(jnp.int32, (2 * RET_DK, RET_DV), 0)
    for i in range(seqs):
        for p in range(RET_PAIRS):
            pc = slice(p * 2 * RET_DK, (p + 1) * 2 * RET_DK)
            q2 = jnp.where(row0, jnp.broadcast_to(q_ref[i, :, pc], (8, 2 * RET_DK)), 0.0)
            k2 = jnp.where(row0, jnp.broadcast_to(k_ref[i, :, pc], (8, 2 * RET_DK)), 0.0)
            pst = st_ref[i, p]
            gam = gam_ref[p]
            outer = []
            for hh in range(2):
                h = 2 * p + hh
                hc = slice(h * RET_DV, (h + 1) * RET_DV)
                hsel = (lane < RET_DK) if hh == 0 else (lane >= RET_DK)
                qm = jnp.where(hsel, q2, 0.0)
                km = jnp.where(hsel, k2, 0.0)
                vh = v_ref[i, :, hc]
                v8 = jnp.where(row0[:, :RET_DV], jnp.broadcast_to(vh, (8, RET_DV)), 0.0)
                cross = _dot(qm.astype(BF16), (pst * gam).astype(BF16))[0:1, :]
                qk = jnp.sum(qm[0:1, :] * km[0:1, :], axis=-1, keepdims=True)
                o = cross + qk * vh
                ret_ref[i, :, hc] = _gn_swish(o, g_ref[i, :, hc], gn_ref[:, hc])
                outer.append(_dot_tn(k2.astype(BF16), v8.astype(BF16)))
            nst_ref[i, p] = pst * gam + jnp.where(srow < RET_DK, outer[0], outer[1])
    scale = ATT_HD ** -0.5
    caches = ((ck0_ref, cv0_ref), (ck1_ref, cv1_ref), (ck2_ref, cv2_ref))
    slot = lax.broadcasted_iota(jnp.int32, (ATT_NK, ATT_HD), 0)
    is_new = slot == 0
    first = row0[:, :ATT_HD]

    @pl.when(pl.program_id(0) == 0)
    def _():
        pieces = _split3(rb_ref[...])
        for g in range(ATT_GROUPS):
            rows = slice(g * ATT_HPG, (g + 1) * ATT_HPG)
            slot_bias[rows, :] = (_dot(pieces[0][rows], slot_ref[g]) + _dot(pieces[1][rows], slot_ref[g])
                                  + _dot(pieces[2][rows], slot_ref[g]))

    def one_row(x):
        return jnp.where(first, jnp.broadcast_to(x, (8, ATT_HD)), 0.0).astype(BF16)

    scores = {}
    for i in range(seqs):
        for h in range(ATT_HPG):
            for g in range(ATT_GROUPS):
                c0 = g * ATT_GW + h * ATT_HD
                kk = jnp.where(is_new, qkv_ref[i, :, ATT_W + c0:ATT_W + c0 + ATT_HD], caches[g][0][i, :, h, :])
                scores[i, g, h] = _dot_nt(one_row(qkv_ref[i, :, c0:c0 + ATT_HD]), kk.astype(BF16))[0:1, :]
    for i in range(seqs):
        for h in range(ATT_HPG):
            o_g, lse_g = [], []
            for g in range(ATT_GROUPS):
                c0 = g * ATT_GW + h * ATT_HD
                vv = jnp.where(is_new, qkv_ref[i, :, 2 * ATT_W + c0:2 * ATT_W + c0 + ATT_HD],
                               caches[g][1][i, :, h, :])
                gh = g * ATT_HPG + h
                s = scores[i, g, h] * scale + slot_bias[gh:gh + 1, :]
                m = jnp.max(s, axis=-1, keepdims=True)
                pr = jnp.exp(s - m)
                den = jnp.sum(pr, axis=-1, keepdims=True)
                o_g.append(_dot(one_row(pr), vv.astype(BF16))[0:1, :] / den)
                lse_g.append(m + jnp.log(den))
            mx = jnp.maximum(jnp.maximum(lse_g[0], lse_g[1]), lse_g[2])
            e = [jnp.exp(l - mx) for l in lse_g]
            tot = e[0] + e[1] + e[2]
            att_ref[i, :, h * ATT_HD:(h + 1) * ATT_HD] = ((e[0] / tot) * o_g[0] + (e[1] / tot) * o_g[1]
                                                          + (e[2] / tot) * o_g[2])


def _decode(q, k, v, g, gn, state, qkv, rel_bias_t, caches, cast):
    batch = q.shape[0]
    gam = jnp.asarray(np.ascontiguousarray(np.broadcast_to(
        np.repeat(np.exp(_ret_log_decay()), RET_DK).reshape(RET_PAIRS, 2 * RET_DK, 1),
        (RET_PAIRS, 2 * RET_DK, RET_DV))), F32)
    vec = lambda a: a.reshape(batch, 1, a.shape[1])
    seqs = DECODE_SEQS
    assert batch % seqs == 0
    vspec = lambda width: pl.BlockSpec((seqs, 1, width), lambda b: (b, 0, 0))
    const = lambda a: pl.BlockSpec(a.shape, lambda b: (0,) * a.ndim)
    st_spec = pl.BlockSpec((seqs, RET_PAIRS, 2 * RET_DK, RET_DV), lambda b: (b, 0, 0, 0))
    cache_in, cache_specs = [], []
    for gi in range(ATT_GROUPS):
        dil = ATT_DILATIONS[gi]
        for c in caches[2 * gi:2 * gi + 2]:
            cache_in.append(c.reshape(batch, ATT_NK, dil, ATT_HPG, ATT_HD))
            cache_specs.append(pl.BlockSpec((seqs, ATT_NK, None, ATT_HPG, ATT_HD), lambda b: (b, 0, 0, 0, 0)))
    st_pairs = state.reshape(batch, RET_PAIRS, 2 * RET_DK, RET_DV)
    slot_sel = _slot_select()
    steps = batch // seqs
    slab = lambda w: pl.BlockSpec((w.shape[0] // steps, w.shape[1]), lambda b: (b, 0))
    assert all(w.shape[0] % (16 * steps) == 0 for w in cast)
    res = pl.pallas_call(
        functools.partial(_decode_body, seqs=seqs, n_cast=len(cast)),
        grid=(steps,),
        in_specs=[vspec(RET_QK_W), vspec(RET_QK_W), vspec(RET_V_W), vspec(RET_V_W), const(gn), st_spec, const(gam),
                  vspec(3 * ATT_W), const(rel_bias_t), const(slot_sel)] + cache_specs + [slab(w) for w in cast],
        out_specs=[vspec(RET_V_W), vspec(ATT_GW), st_spec] + [slab(w) for w in cast],
        out_shape=[jax.ShapeDtypeStruct((batch, 1, RET_V_W), F32),
                   jax.ShapeDtypeStruct((batch, 1, ATT_GW), F32),
                   jax.ShapeDtypeStruct(st_pairs.shape, F32)]
        + [jax.ShapeDtypeStruct(w.shape, BF16) for w in cast],
        scratch_shapes=[pltpu.VMEM((rel_bias_t.shape[0], ATT_NK), F32)],
        compiler_params=_params(1),
        name="decode",
    )(vec(q), vec(k), vec(v), vec(g), gn, st_pairs, gam, vec(qkv), rel_bias_t, slot_sel, *cache_in, *cast)
    ret, att, nst = res[:3]
    return ret.reshape(batch, RET_V_W), att.reshape(batch, ATT_GW), nst.reshape(state.shape), tuple(res[3:])


def kernel(x_prompt, x_sample, state_ret, cache_k_w128, cache_v_w128, cache_k_w512, cache_v_w512,
           cache_k_w2048, cache_v_w2048, p_prompt, p_sample, ln1_g, w_in, ret_gn_g, w_ret_br, w_att_br,
           w_out, ln2_g, w_up, w_down, w_ple, w_ple_gate, rel_bias, lnf_g):
    depth = w_in.shape[0]
    assert depth == 1
    batch, seq, _ = x_prompt.shape
    dec_batch, dec_seq, _ = x_sample.shape
    assert dec_seq == 1
    past_len = 16384
    l = 0
    ln1 = ln1_g[l][None, :]
    ln2 = ln2_g[l][None, :]
    lnf = lnf_g[None, :]
    gn = ret_gn_g[l][None, :]
    w_in_lo = w_in[l].astype(BF16)
    inv_row = _rope_inv_row()

    caches = (cache_k_w128[l], cache_v_w128[l], cache_k_w512[l], cache_v_w512[l],
              cache_k_w2048[l], cache_v_w2048[l])

    xs = x_sample.reshape(dec_batch, D_MODEL)
    xp = x_prompt.reshape(batch * seq, D_MODEL)
    rq, rk, rv, rg, gr, ga, sq, sk, sv, sg, sgr, sga = _inproj_ret(xp, xs, ln1, w_in_lo, inv_row, TM_INPROJ, seq,
                                                                   past_len)
    att_o = _inproj_att(xp, xs, ln1, w_in_lo, TM_INPROJ, seq)
    aqs, aks, avs = att_o[0:3], att_o[3:6], att_o[6:9]
    kfull, vfull = att_o[9:12], att_o[12:15]
    s_qkv = att_o[15]

    rel_bias_t = rel_bias.T
    s_ret, s_attn, new_st, (wrb, wab, wo, wpl, wpg) = _decode(
        sq, sk, sv, sg, gn, state_ret[l], s_qkv, rel_bias_t, caches,
        (w_ret_br[l], w_att_br[l], w_out[l], w_ple[l], w_ple_gate[l]))
    ple_s = p_sample[l].reshape(dec_batch, D_PLE)

    ret_out, st_p, (wu, wd) = _retention_prompt(rq, rk, rv, rg, gn, batch, seq, TS_RETENTION, (w_up[l], w_down[l]))
    tail_w = (wrb, wab, wo, wu, wd, wpl, wpg, ln2, lnf)
    outs, lses = [], []
    band_onehot, band_neg = _band_select()
    for g in range(ATT_GROUPS):
        o, lse = _attention_prompt(aqs[g], aks[g], avs[g], rel_bias_t, band_onehot, band_neg, g)
        outs.append(o)
        lses.append(lse)
    ple_p = p_prompt[l].reshape(batch * seq, D_PLE)
    tm_tail = batch * seq // dec_batch
    y_p, y_s, new_caches = _tail_shift([ret_out] + outs + lses + [gr, ga, xp, ple_p],
                                       [s_ret, s_attn, sgr, sga, xs, ple_s], tail_w, tm_tail, s_qkv, caches)
    y_prompt = y_p.reshape(batch, seq, D_MODEL)
    y_sample = y_s.reshape(dec_batch, 1, D_MODEL)
    new_state_p = st_p[None]
    kv_p = []
    for g in range(ATT_GROUPS):
        shape = (1, batch, min(ATT_WINDOWS[g], seq), ATT_HPG, ATT_HD)
        kv_p.append(kfull[g].reshape(shape))
        kv_p.append(vfull[g].reshape(shape))
    kv_s = [c[None] for c in new_caches]

    return (y_prompt, y_sample, new_state_p, *kv_p, new_st[None], *kv_s)
```

```python
import functools
import math

import jax
import jax.numpy as jnp
import numpy as np
from jax import lax
from jax.experimental import pallas as pl
from jax.experimental.pallas import tpu as pltpu

F32 = jnp.float32
BF16 = jnp.bfloat16

D_MODEL = 1024
RET_HEADS = 8
RET_DK = 64
RET_DV = 128
RET_PAIRS = RET_HEADS // 2
RET_CHUNK = 128
ROPE_BASE = 10000.0
ATT_WINDOWS = (128, 512, 2048)
ATT_DILATIONS = (1, 4, 16)
ATT_GROUPS = 3
ATT_HPG = 4
ATT_HD = 128
ATT_NK = 128
ATT_GW = ATT_HPG * ATT_HD
REL_BUCKETS = 32
REL_MAX_DIST = 2048
D_FF = 4 * D_MODEL
D_PLE = 256
NORM_EPS = 1e-6
RET_QK_W = RET_HEADS * RET_DK
RET_V_W = RET_HEADS * RET_DV
ATT_W = ATT_GROUPS * ATT_GW
COL_ATT = 2 * RET_QK_W + 2 * RET_V_W
COL_GATE = COL_ATT + 3 * ATT_W
N_IN = COL_GATE + 2 * D_MODEL
PAST_LEN = 16384

VMEM_LIMIT_V7X = 56 * 1024 * 1024
VMEM_LIMIT_TAIL_V7X = 62 * 1024 * 1024
LANES = 128
TM_INPROJ = 512
TS_RETENTION = 512
ATTN_UNROLL = 8
PERM_ROWS = 256
SINGLE_BLOCK_GROUP = 4
DECODE_SEQS = 4
CAST_STEPS = 8


def _dot(a, b):
    return jnp.dot(a, b, preferred_element_type=F32)


def _dot_nt(a, b):
    return lax.dot_general(a, b, (((1,), (1,)), ((), ())), preferred_element_type=F32)


def _dot_tn(a, b):
    return lax.dot_general(a, b, (((0,), (0,)), ((), ())), preferred_element_type=F32)


def _rms(x, g):
    return x * lax.rsqrt(jnp.mean(x * x, axis=-1, keepdims=True) + NORM_EPS) * g


def _sigmoid(x):
    return 1.0 / (1.0 + jnp.exp(-x))


def _resident(shape):
    return pl.BlockSpec(shape, lambda *_: (0,) * len(shape), pipeline_mode=pl.Buffered(1))


def _params(n_axes):
    return pltpu.CompilerParams(dimension_semantics=("arbitrary",) * n_axes,
                                vmem_limit_bytes=VMEM_LIMIT_V7X)


def _rope_inv_row():
    half = RET_DK // 2
    inv = ROPE_BASE ** (-jnp.arange(half, dtype=F32) / half)
    return jnp.tile(inv, LANES // half)[None, :]


def _ret_log_decay():
    return np.log1p(-np.exp2(-5.0 - np.arange(RET_HEADS, dtype=np.float32))).astype(np.float32)


def _ret_tables():
    c = RET_CHUNK
    lg = _ret_log_decay()
    i = np.arange(c, dtype=np.float32)
    diff = i[:, None] - i[None, :]
    dmask = np.where(diff[None] >= 0, np.exp(np.maximum(diff, 0.0)[None] * lg[:, None, None]), 0.0)
    q_decay = np.exp((i + 1.0)[:, None] * lg[None, :])
    k_decay = np.exp((c - 1.0 - i)[:, None] * lg[None, :])
    qdec = np.broadcast_to(q_decay.T[:, :, None], (RET_HEADS, c, RET_DV))
    kdec = np.repeat(k_decay, RET_DK, axis=1).reshape(c, RET_PAIRS, 2 * RET_DK).transpose(1, 0, 2)
    gc = np.repeat(np.exp(c * lg), RET_DV).reshape(RET_PAIRS, 1, 2 * RET_DV)
    return tuple(jnp.asarray(np.ascontiguousarray(t), F32) for t in (dmask, qdec, kdec, gc))


def _rel_buckets():
    max_exact = REL_BUCKETS // 2
    out = []
    for dil in ATT_DILATIONS:
        d = np.arange(ATT_NK, dtype=np.int32) * dil
        log_ratio = (np.log(np.maximum(d, 1).astype(np.float32) / np.float32(max_exact))
                     / np.float32(math.log(REL_MAX_DIST / max_exact)))
        large = max_exact + (log_ratio * np.float32(REL_BUCKETS - max_exact)).astype(np.int32)
        out.append(np.where(d < max_exact, d, np.minimum(large, REL_BUCKETS - 1)))
    return np.stack(out)


def _band_select():
    nk = ATT_NK
    buckets = _rel_buckets()
    onehot = np.zeros((ATT_GROUPS, REL_BUCKETS, 3 * nk), np.float32)
    for g in range(ATT_GROUPS):
        for k in range(nk, 2 * nk):
            onehot[g, buckets[g, 2 * nk - 1 - k], k] = 1.0
    mask = np.full((1, 3 * nk), -np.inf, np.float32)
    mask[0, nk:2 * nk] = 0.0
    return jnp.asarray(onehot, BF16), jnp.asarray(mask, F32)


def _slot_select():
    buckets = _rel_buckets()
    onehot = np.zeros((ATT_GROUPS, REL_BUCKETS, ATT_NK), np.float32)
    for g in range(ATT_GROUPS):
        for slot in range(ATT_NK):
            onehot[g, buckets[g, 0 if slot == 0 else ATT_NK - slot], slot] = 1.0
    return jnp.asarray(onehot, BF16)


def _split3(x):
    hi = x.astype(BF16)
    rem = x - hi.astype(F32)
    mid = rem.astype(BF16)
    lo = (rem - mid.astype(F32)).astype(BF16)
    return hi, mid, lo


def _cast_body(*refs):
    n = len(refs) // 2
    for src, dst in zip(refs[:n], refs[n:]):
        dst[...] = src[...].astype(dst.dtype)


def _cast_ret_weights(w_in):
    rows = D_MODEL // CAST_STEPS
    half = D_MODEL // 2
    assert COL_GATE % half == 0
    widths_blocks = [(COL_ATT, 0)] + [(half, COL_GATE // half + j) for j in range(4)]
    window = lambda width, blk: pl.BlockSpec((rows, width), lambda i: (i, blk))
    return pl.pallas_call(
        _cast_body,
        grid=(CAST_STEPS,),
        in_specs=[window(wd, blk) for wd, blk in widths_blocks],
        out_specs=[window(wd, 0) for wd, _ in widths_blocks],
        out_shape=[jax.ShapeDtypeStruct((D_MODEL, wd), BF16) for wd, _ in widths_blocks],
        compiler_params=_params(1),
        name="cast_ret_weights",
    )(*([w_in] * len(widths_blocks)))


def _inproj_ret_body(x_ref, xs_ref, ln_ref, w_ref, wg0_ref, wg1_ref, wg2_ref, wg3_ref, inv_ref, *refs,
                     tm, tiles, n_steps, sample_pos):
    cast_in = refs[:3]
    outs, sample_outs = refs[3:9], refs[9:15]
    cast_out = refs[15:18]
    cos_s, sin_s = refs[18:]
    for src, dst in zip(cast_in, cast_out):
        dst[...] = src[...].astype(dst.dtype)
    i = pl.program_id(0)

    def first_half(rows):
        lane = lax.broadcasted_iota(jnp.int32, (rows, LANES), 1)
        return (lane % RET_DK) < (RET_DK // 2)

    def tables(pos):
        ang = pos.astype(F32) * inv_ref[...]
        sin = jnp.sin(ang)
        return jnp.cos(ang), jnp.where(first_half(pos.shape[0]), -sin, sin)

    def project(x, cos, sin, q_ref, k_ref, v_ref, g_ref, gr_ref, ga_ref):
        h = _rms(x, ln_ref[...]).astype(BF16)
        qk = _dot(h, w_ref[:, 0:2 * RET_QK_W])
        n_q = RET_QK_W // LANES
        fh = first_half(x.shape[0])
        for c in range(2 * n_q):
            xc = qk[:, c * LANES:(c + 1) * LANES]
            swapped = jnp.where(fh, pltpu.roll(xc, LANES - RET_DK // 2, 1), pltpu.roll(xc, RET_DK // 2, 1))
            r = xc * cos + swapped * sin
            if c < n_q:
                q_ref[:, c * LANES:(c + 1) * LANES] = r.astype(q_ref.dtype)
            else:
                k_ref[:, (c - n_q) * LANES:(c - n_q + 1) * LANES] = (r * (RET_DK ** -0.5)).astype(k_ref.dtype)
        o = 2 * RET_QK_W
        v_ref[...] = _dot(h, w_ref[:, o:o + RET_V_W]).astype(v_ref.dtype)
        o += RET_V_W
        g_ref[...] = _dot(h, w_ref[:, o:o + RET_V_W]).astype(g_ref.dtype)
        half = D_MODEL // 2
        gr_ref[:, :half] = _dot(h, wg0_ref[...]).astype(gr_ref.dtype)
        gr_ref[:, half:] = _dot(h, wg1_ref[...]).astype(gr_ref.dtype)
        ga_ref[:, :half] = _dot(h, wg2_ref[...]).astype(ga_ref.dtype)
        ga_ref[:, half:] = _dot(h, wg3_ref[...]).astype(ga_ref.dtype)

    trow = pl.ds(pl.multiple_of(lax.rem(i, tiles) * tm, tm), tm)

    @pl.when(i < tiles)
    def _():
        cos_s[trow, :], sin_s[trow, :] = tables(i * tm + lax.broadcasted_iota(jnp.int32, (tm, LANES), 0))
    project(x_ref[...], cos_s[trow, :], sin_s[trow, :], *outs)

    @pl.when(i == n_steps - 1)
    def _():
        ns = xs_ref.shape[0]
        project(xs_ref[...], *tables(jnp.full((ns, LANES), sample_pos, jnp.int32)), *sample_outs)


def _inproj_ret(x2d, xs2d, ln, w_ret, w_gates, w_in, inv_row, tm, seq, sample_pos):
    n = x2d.shape[0]
    ns = xs2d.shape[0]
    tiles = seq // tm
    n_steps = n // tm
    row = lambda width: pl.BlockSpec((tm, width), lambda i: (i, 0))
    widths = (RET_QK_W, RET_QK_W, RET_V_W, RET_V_W, D_MODEL, D_MODEL)
    slab = D_MODEL // n_steps
    assert slab % 16 == 0 and COL_ATT % ATT_W == 0
    res = pl.pallas_call(
        functools.partial(_inproj_ret_body, tm=tm, tiles=tiles, n_steps=n_steps, sample_pos=sample_pos),
        grid=(n_steps,),
        in_specs=[row(D_MODEL), _resident(xs2d.shape), _resident((1, D_MODEL)), _resident(w_ret.shape)]
        + [_resident(w.shape) for w in w_gates] + [_resident((1, LANES))]
        + [pl.BlockSpec((slab, ATT_W), lambda i, kind=kind: (i, COL_ATT // ATT_W + kind)) for kind in range(3)],
        out_specs=[row(wd) for wd in widths] + [pl.BlockSpec((ns, wd), lambda i: (0, 0)) for wd in widths]
        + [pl.BlockSpec((slab, ATT_W), lambda i: (i, 0)) for _ in range(3)],
        out_shape=[jax.ShapeDtypeStruct((n, wd), BF16) for wd in widths]
        + [jax.ShapeDtypeStruct((ns, wd), F32) for wd in widths]
        + [jax.ShapeDtypeStruct((D_MODEL, ATT_W), BF16) for _ in range(3)],
        scratch_shapes=[pltpu.VMEM((seq, LANES), F32), pltpu.VMEM((seq, LANES), F32)],
        compiler_params=_params(1),
        name="inproj_ret",
    )(x2d, xs2d, ln, w_ret, *w_gates, inv_row, w_in, w_in, w_in)
    return res[:12], res[12:]


def _inproj_att_body(x_ref, xs_ref, ln_ref, wq_ref, wk_ref, wv_ref, *refs, tm, keeps, seq, n_steps):
    lowp = refs[:3 * ATT_GROUPS]
    full = refs[3 * ATT_GROUPS:5 * ATT_GROUPS]
    sample_ref = refs[5 * ATT_GROUPS]

    @pl.when(pl.program_id(0) == n_steps - 1)
    def _():
        hs = _rms(xs_ref[...], ln_ref[...]).astype(BF16)
        for kind, w_ref in enumerate((wq_ref, wk_ref, wv_ref)):
            for g in range(ATT_GROUPS):
                c = kind * ATT_GROUPS + g
                sample_ref[:, c * ATT_GW:(c + 1) * ATT_GW] = _dot(hs, w_ref[:, g * ATT_GW:(g + 1) * ATT_GW])

    h = _rms(x_ref[...], ln_ref[...]).astype(BF16)
    for g in reversed(range(ATT_GROUPS)):
        for kind in reversed(range(3)):
            dil = ATT_DILATIONS[g]
            dst = lowp[kind * ATT_GROUPS + g]
            r = _dot(h, (wq_ref, wk_ref, wv_ref)[kind][:, g * ATT_GW:(g + 1) * ATT_GW])
            if dil == 1:
                dst[0] = r.astype(dst.dtype)
            else:
                dst[...] = jnp.swapaxes(r.reshape(tm // dil, dil, ATT_GW), 0, 1).astype(dst.dtype)
            if kind > 0:
                cache = full[(kind - 1) * ATT_GROUPS + g]
                rows = tm if keeps[g] == seq else keeps[g]
                for hh in range(ATT_HPG):
                    cache[pl.ds(hh, rows, stride=ATT_HPG), :] = r[tm - rows:, hh * ATT_HD:(hh + 1) * ATT_HD]


def _inproj_att(x2d, xs2d, ln, w_qkv, tm, seq):
    n = x2d.shape[0]
    ns = xs2d.shape[0]
    batch = n // seq
    tiles = seq // tm
    keeps = tuple(min(wd, seq) for wd in ATT_WINDOWS)
    assert all(kp <= tm or kp == seq for kp in keeps) and seq % tm == 0
    out_specs, out_shape = [], []
    for _ in range(3):
        for g in range(ATT_GROUPS):
            dil = ATT_DILATIONS[g]
            out_specs.append(pl.BlockSpec((None, dil, tm // dil, ATT_GW), lambda i: (i // tiles, 0, i % tiles, 0)))
            out_shape.append(jax.ShapeDtypeStruct((batch, dil, seq // dil, ATT_GW), BF16))
    for _ in range(2):
        for g in range(ATT_GROUPS):
            if keeps[g] == seq:
                idx = lambda i: (i // tiles, i % tiles, 0)
                rows = tm
            else:
                idx = lambda i: (i // tiles, 0, 0)
                rows = keeps[g]
            out_specs.append(pl.BlockSpec((None, rows * ATT_HPG, ATT_HD), idx))
            out_shape.append(jax.ShapeDtypeStruct((batch, keeps[g] * ATT_HPG, ATT_HD), F32))
    out_specs.append(pl.BlockSpec((ns, 3 * ATT_W), lambda i: (0, 0)))
    out_shape.append(jax.ShapeDtypeStruct((ns, 3 * ATT_W), F32))
    return pl.pallas_call(
        functools.partial(_inproj_att_body, tm=tm, keeps=keeps, seq=seq, n_steps=n // tm),
        grid=(n // tm,),
        in_specs=[pl.BlockSpec((tm, D_MODEL), lambda i: (i, 0)), _resident(xs2d.shape), _resident((1, D_MODEL))]
        + [_resident(w.shape) for w in w_qkv],
        out_specs=out_specs,
        out_shape=out_shape,
        compiler_params=_params(1),
        name="inproj_att",
    )(x2d, xs2d, ln, *w_qkv)


def _gn_swish(o, gate, gn):
    mu = jnp.mean(o, axis=-1, keepdims=True)
    d = o - mu
    var = jnp.mean(d * d, axis=-1, keepdims=True)
    on = d * lax.rsqrt(var + NORM_EPS) * gn
    return gate * _sigmoid(gate) * on


def _retention_body(q_ref, k_ref, v_ref, g_ref, gn_ref, dm_ref, qdec_ref, kdec_ref, gc_ref, *refs,
                    n_chunks, n_cast):
    cast_in = refs[:n_cast]
    out_ref, st_ref = refs[n_cast:n_cast + 2]
    cast_out = refs[n_cast + 2:2 * n_cast + 2]
    state = refs[2 * n_cast + 2]
    for src, dst in zip(cast_in, cast_out):
        dst[...] = src[...].astype(dst.dtype)
    c = RET_CHUNK

    @pl.when(pl.program_id(1) == 0)
    def _():
        state[...] = jnp.zeros_like(state)

    lane = lax.broadcasted_iota(jnp.int32, (c, 2 * RET_DK), 1)
    head0 = lane < RET_DK

    def chunk(ci, carry):
        rows = pl.ds(pl.multiple_of(ci * c, c), c)
        scores, cross, values = {}, {}, {}
        for p in range(RET_PAIRS):
            q2 = q_ref[rows, p * 2 * RET_DK:(p + 1) * 2 * RET_DK]
            k2 = k_ref[rows, p * 2 * RET_DK:(p + 1) * 2 * RET_DK]
            v2 = v_ref[rows, p * 2 * RET_DV:(p + 1) * 2 * RET_DV]
            pst = state[p]
            pst_lo = pst.astype(BF16)
            zero = jnp.zeros_like(q2)
            for hh in range(2):
                h = 2 * p + hh
                qm = jnp.where(head0 if hh == 0 else jnp.logical_not(head0), q2, zero)
                values[h] = v2[:, hh * RET_DV:(hh + 1) * RET_DV]
                scores[h] = _dot_nt(qm, k2)
                cross[h] = _dot(qm, pst_lo[:, hh * RET_DV:(hh + 1) * RET_DV])
            kd = (k2.astype(F32) * kdec_ref[p]).astype(BF16)
            state[p] = pst * gc_ref[p] + _dot_tn(kd, v2)
        for h in range(RET_HEADS):
            o = _dot((scores[h] * dm_ref[h]).astype(BF16), values[h]) + cross[h] * qdec_ref[h]
            gate = g_ref[rows, h * RET_DV:(h + 1) * RET_DV].astype(F32)
            res = _gn_swish(o, gate, gn_ref[:, h * RET_DV:(h + 1) * RET_DV])
            out_ref[rows, h * RET_DV:(h + 1) * RET_DV] = res.astype(out_ref.dtype)
        return carry

    lax.fori_loop(0, n_chunks, chunk, 0, unroll=True)
    for p in range(RET_PAIRS):
        pst = state[p]
        for hh in range(2):
            st_ref[2 * p + hh] = pst[hh * RET_DK:(hh + 1) * RET_DK, hh * RET_DV:(hh + 1) * RET_DV]


def _retention_prompt(q, k, v, g, gn, batch, seq, ts, cast):
    n = q.shape[0]
    steps = seq // ts
    dmask, qdec, kdec, gc = _ret_tables()
    row = lambda width: pl.BlockSpec((ts, width), lambda b, s: (b * steps + s, 0))
    const = lambda a: pl.BlockSpec(a.shape, lambda b, s: (0,) * a.ndim)
    n_steps = batch * steps
    slab = lambda w: pl.BlockSpec((w.shape[0] // n_steps, w.shape[1]), lambda b, s: (b * steps + s, 0))
    assert all(w.shape[0] % (16 * n_steps) == 0 for w in cast)
    res = pl.pallas_call(
        functools.partial(_retention_body, n_chunks=ts // RET_CHUNK, n_cast=len(cast)),
        grid=(batch, steps),
        in_specs=[row(RET_QK_W), row(RET_QK_W), row(RET_V_W), row(RET_V_W), const(gn),
                  const(dmask), const(qdec), const(kdec), const(gc)] + [slab(w) for w in cast],
        out_specs=[row(RET_V_W),
                   pl.BlockSpec((None, RET_HEADS, RET_DK, RET_DV), lambda b, s: (b, 0, 0, 0))]
        + [slab(w) for w in cast],
        out_shape=[jax.ShapeDtypeStruct((n, RET_V_W), BF16),
                   jax.ShapeDtypeStruct((batch, RET_HEADS, RET_DK, RET_DV), F32)]
        + [jax.ShapeDtypeStruct(w.shape, BF16) for w in cast],
        scratch_shapes=[pltpu.VMEM((RET_PAIRS, 2 * RET_DK, 2 * RET_DV), F32)],
        compiler_params=_params(2),
        name="retention",
    )(q, k, v, g, gn, dmask, qdec, kdec, gc, *cast)
    return res[0], res[1], tuple(res[2:])


def _attn_body(q_ref, k_ref, v_ref, rb_ref, sel_ref, neg_ref, o_ref, lse_ref, *scratch, group, dil, n_blocks):
    nk = ATT_NK
    scale = ATT_HD ** -0.5
    lane = lax.broadcasted_iota(jnp.int32, (nk, LANES), 1)
    tabs = scratch[-1]

    @pl.when(pl.program_id(0) == 0)
    def _():
        band = neg_ref[...]
        for piece in _split3(rb_ref[group * ATT_HPG:(group + 1) * ATT_HPG, :]):
            band = band + _dot(piece, sel_ref[...])
        col = lax.broadcasted_iota(jnp.int32, (1, 2 * nk), 1)
        for i in range(nk):
            window = band[:, nk - 1 - i:3 * nk - 1 - i]
            for h in range(ATT_HPG):
                tabs[1, h, i:i + 1, :] = window[h:h + 1, :]
                tabs[0, h, i:i + 1, :] = jnp.where(col < nk, -jnp.inf, window[h:h + 1, :])
    if dil > 1:
        o_s, l_s = scratch[:2]

    def chain(s_raw, v_ext, sel, h):
        s = s_raw * scale + (tabs[sel, h] if n_blocks > 1 else tabs[1, h, :, nk:])
        m = jnp.max(s, axis=-1, keepdims=True)
        p = jnp.exp(s - m)
        pv = _dot(p.astype(BF16), v_ext)
        den = pv[:, ATT_HD:]
        return (pv[:, :ATT_HD] / den).astype(BF16), m + jnp.log(den)

    def write(rs, rows, outs):
        lse_tile = jnp.zeros((nk, LANES), F32)
        for h, (o, lse) in enumerate(outs):
            hc = slice(h * ATT_HD, (h + 1) * ATT_HD)
            if dil == 1:
                o_ref[rows, hc] = o
            else:
                o_s[rs, rows, hc] = o
            lse_tile = jnp.where(lane == h, lse, lse_tile)
        if dil == 1:
            lse_ref[rows, :] = lse_tile
        else:
            l_s[rs, rows, :] = lse_tile

    heads = [slice(h * ATT_HD, (h + 1) * ATT_HD) for h in range(ATT_HPG)]

    def block(rs, n, n_prev, sel):
        rows = pl.ds(pl.multiple_of(n * nk, nk), nk)
        prev_rows = pl.ds(pl.multiple_of(n_prev * nk, nk), nk)
        q = q_ref[rs, rows, :]
        k_all = jnp.concatenate([k_ref[rs, prev_rows, :], k_ref[rs, rows, :]], axis=0)
        v_all = jnp.concatenate([v_ref[rs, prev_rows, :], v_ref[rs, rows, :]], axis=0)
        ones = jnp.ones((2 * nk, ATT_HD), BF16)
        outs = []
        for h, hc in enumerate(heads):
            v_ext = jnp.concatenate([v_all[:, hc], ones], axis=1)
            outs.append(chain(_dot_nt(q[:, hc], k_all[:, hc]), v_ext, sel, h))
        write(rs, rows, outs)

    def single_blocks(streams):
        ones = jnp.ones((nk, ATT_HD), BF16)
        scores = [[_dot_nt(q_ref[rs, :, hc], k_ref[rs, :, hc]) for hc in heads] for rs in streams]
        for i, rs in enumerate(streams):
            outs = [chain(scores[i][h], jnp.concatenate([v_ref[rs, :, hc], ones], axis=1), 0, h)
                    for h, hc in enumerate(heads)]
            write(rs, slice(None), outs)

    if n_blocks == 1:
        for r0 in range(0, dil, SINGLE_BLOCK_GROUP):
            single_blocks(range(r0, min(r0 + SINGLE_BLOCK_GROUP, dil)))
    else:
        for rs in range(dil):
            def loop(n, carry, rs=rs):
                block(rs, n, jnp.maximum(n - 1, 0), jnp.minimum(n, 1))
                return carry
            lax.fori_loop(0, n_blocks, loop, 0, unroll=ATTN_UNROLL)

    if dil > 1:
        rows = o_ref.shape[0]
        for j in range(rows // PERM_ROWS):
            src = slice(j * (PERM_ROWS // dil), (j + 1) * (PERM_ROWS // dil))
            dst = slice(j * PERM_ROWS, (j + 1) * PERM_ROWS)
            o_ref[dst, :] = jnp.swapaxes(o_s[:, src, :], 0, 1).reshape(PERM_ROWS, ATT_GW).astype(BF16)
            lse_ref[dst, :] = jnp.swapaxes(l_s[:, src, :], 0, 1).reshape(PERM_ROWS, LANES)


def _attention_prompt(aq, ak, av, rel_bias_t, onehot, neg, g):
    batch, dil, length, _ = aq.shape
    seq = dil * length
    n_blocks = length // ATT_NK
    blk = pl.BlockSpec((None, dil, length, ATT_GW), lambda b: (b, 0, 0, 0))
    scratch = [pltpu.VMEM((dil, length, ATT_GW), BF16), pltpu.VMEM((dil, length, LANES), F32)] if dil > 1 else []
    scratch.append(pltpu.VMEM((2, ATT_HPG, ATT_NK, 2 * ATT_NK), F32))
    o, lse = pl.pallas_call(
        functools.partial(_attn_body, group=g, dil=dil, n_blocks=n_blocks),
        grid=(batch,),
        in_specs=[blk, blk, blk, pl.BlockSpec(rel_bias_t.shape, lambda b: (0, 0)),
                  pl.BlockSpec((None, REL_BUCKETS, 3 * ATT_NK), lambda b: (g, 0, 0)),
                  pl.BlockSpec((1, 3 * ATT_NK), lambda b: (0, 0))],
        out_specs=[pl.BlockSpec((None, seq, ATT_GW), lambda b: (b, 0, 0)),
                   pl.BlockSpec((None, seq, LANES), lambda b: (b, 0, 0))],
        out_shape=[jax.ShapeDtypeStruct((batch, seq, ATT_GW), BF16),
                   jax.ShapeDtypeStruct((batch, seq, LANES), F32)],
        scratch_shapes=scratch,
        compiler_params=_params(1),
        name="attention_g%d" % g,
    )(aq, ak, av, rel_bias_t, onehot, neg)
    return o.reshape(batch * seq, ATT_GW), lse.reshape(batch * seq, LANES)


def _tail_math(act_refs, w_refs, y_ref, combine, middle=None):
    wrb_ref, wab_ref, wo_ref, wu_ref, wd_ref, wpl_ref, wpg_ref, ln2_ref, lnf_ref = w_refs
    if combine:
        ret_ref, o0_ref, o1_ref, o2_ref, l0_ref, l1_ref, l2_ref, gr_ref, ga_ref, x_ref, ple_ref = act_refs
        lses = [l0_ref[...], l1_ref[...], l2_ref[...]]
        outs = [o0_ref, o1_ref, o2_ref]
        parts = []
        for h in range(ATT_HPG):
            lh = [l[:, h:h + 1] for l in lses]
            mx = jnp.maximum(jnp.maximum(lh[0], lh[1]), lh[2])
            e = [jnp.exp(l - mx) for l in lh]
            tot = e[0] + e[1] + e[2]
            acc = None
            for g in range(ATT_GROUPS):
                term = (e[g] / tot) * outs[g][:, h * ATT_HD:(h + 1) * ATT_HD].astype(F32)
                acc = term if acc is None else acc + term
            parts.append(acc)
        att = jnp.concatenate(parts, axis=1).astype(BF16)
    else:
        ret_ref, att_ref, gr_ref, ga_ref, x_ref, ple_ref = act_refs
        att = att_ref[...].astype(BF16)
    a = _dot(ret_ref[...].astype(BF16), wrb_ref[...])
    b = _dot(att, wab_ref[...])
    mixed = _sigmoid(gr_ref[...].astype(F32)) * a + _sigmoid(ga_ref[...].astype(F32)) * b
    x1 = x_ref[...] + _dot(mixed.astype(BF16), wo_ref[...])
    h2 = _rms(x1, ln2_ref[...]).astype(BF16)
    ff_chunk = D_MODEL
    acc = None
    for c in range(D_FF // ff_chunk):
        u = _dot(h2, wu_ref[:, c * ff_chunk:(c + 1) * ff_chunk])
        r = jnp.maximum(u, 0.0)
        t = _dot((r * r).astype(BF16), wd_ref[c * ff_chunk:(c + 1) * ff_chunk, :])
        acc = t if acc is None else acc + t
        if middle is not None and c == D_FF // ff_chunk // 2 - 1:
            middle()
    x2 = x1 + acc
    gate = _sigmoid(_dot(x2.astype(BF16), wpg_ref[...]))
    x3 = x2 + gate * _dot(ple_ref[...].astype(BF16), wpl_ref[...])
    y_ref[...] = _rms(x3, lnf_ref[...])


def _tail_shift_body(*refs, n_act, n_sample, n_w, n_steps):
    nc = 2 * ATT_GROUPS
    acts = refs[:n_act]
    sample_acts = refs[n_act:n_act + n_sample]
    base = n_act + n_sample
    ws = refs[base:base + n_w]
    new_ref = refs[base + n_w]
    base += n_w + 1
    old = refs[base:base + nc]
    y_ref, ys_ref = refs[base + nc:base + nc + 2]
    base += nc + 2
    out = refs[base:base + nc]
    stage = refs[base + nc:base + 2 * nc]
    sem_in, sem_out, sem_row = refs[base + 2 * nc:]
    s = pl.program_id(0)
    keys = [2 * g for g in range(ATT_GROUPS)]
    values = [2 * g + 1 for g in range(ATT_GROUPS)]

    def copy_in(i, seq):
        width = old[i].shape[1]
        return pltpu.make_async_copy(old[i].at[seq, pl.ds(1, width - 1)], stage[i], sem_in.at[i])

    def copy_out(i, seq):
        width = old[i].shape[1]
        return pltpu.make_async_copy(stage[i], out[i].at[seq, pl.ds(0, width - 1)], sem_out.at[i])

    def copy_row(i):
        g, kind = divmod(i, 2)
        return pltpu.make_async_copy(new_ref.at[s, kind + 1, g], out[i].at[s, old[i].shape[1] - 1], sem_row.at[i])

    @pl.when(s == 0)
    def _():
        for i in keys:
            copy_in(i, 0).start()
    for i in keys:
        copy_in(i, s).wait()
    for i in keys:
        copy_out(i, s).start()

    @pl.when(s > 0)
    def _():
        for i in values:
            copy_out(i, s - 1).wait()
    for i in values:
        copy_in(i, s).start()
    for i in range(nc):
        copy_row(i).start()

    def middle():
        for i in values:
            copy_in(i, s).wait()
        for i in values:
            copy_out(i, s).start()
        for i in keys:
            copy_out(i, s).wait()

        @pl.when(s < n_steps - 1)
        def _():
            for i in keys:
                copy_in(i, s + 1).start()

    _tail_math(acts, ws, y_ref, True, middle)

    @pl.when(s == n_steps - 1)
    def _():
        _tail_math(sample_acts, ws, ys_ref, False)

    for i in range(nc):
        copy_row(i).wait()

    @pl.when(s == n_steps - 1)
    def _():
        for i in values:
            copy_out(i, s).wait()


def _tail_shift(acts, sample_acts, weights, tm, new_qkv, caches):
    n = acts[0].shape[0]
    n_steps = n // tm
    assert n_steps == new_qkv.shape[0]
    row = lambda a: pl.BlockSpec((tm, a.shape[1]), lambda i: (i, 0))
    anyspec = pl.BlockSpec(memory_space=pl.ANY)
    nc = len(caches)
    new_rows = new_qkv.reshape(new_qkv.shape[0], 3, ATT_GROUPS, ATT_HPG, ATT_HD)
    ns = sample_acts[0].shape[0]
    res = pl.pallas_call(
        functools.partial(_tail_shift_body, n_act=len(acts), n_sample=len(sample_acts), n_w=len(weights),
                          n_steps=n_steps),
        grid=(n_steps,),
        in_specs=[row(a) for a in acts] + [_resident(a.shape) for a in sample_acts]
        + [_resident(w.shape) for w in weights] + [anyspec] * (nc + 1),
        out_specs=[pl.BlockSpec((tm, D_MODEL), lambda i: (i, 0)), pl.BlockSpec((ns, D_MODEL), lambda i: (0, 0))]
        + [anyspec] * nc,
        out_shape=[jax.ShapeDtypeStruct((n, D_MODEL), F32), jax.ShapeDtypeStruct((ns, D_MODEL), F32)]
        + [jax.ShapeDtypeStruct(c.shape, c.dtype) for c in caches],
        scratch_shapes=[pltpu.VMEM((c.shape[1] - 1, ATT_HPG, ATT_HD), F32) for c in caches]
        + [pltpu.SemaphoreType.DMA((nc,)), pltpu.SemaphoreType.DMA((nc,)), pltpu.SemaphoreType.DMA((nc,))],
        compiler_params=pltpu.CompilerParams(dimension_semantics=("arbitrary",),
                                             vmem_limit_bytes=VMEM_LIMIT_TAIL_V7X),
        name="tail",
    )(*acts, *sample_acts, *weights, new_rows, *caches)
    return res[0], res[1], res[2:]


def _decode_body(q_ref, k_ref, v_ref, g_ref, gn_ref, st_ref, gam_ref,
                 qkv_ref, rb_ref, slot_ref,
                 ck0_ref, cv0_ref, ck1_ref, cv1_ref, ck2_ref, cv2_ref, *refs, seqs, n_cast):
    cast_in = refs[:n_cast]
    ret_ref, att_ref, nst_ref = refs[n_cast:n_cast + 3]
    cast_out = refs[n_cast + 3:2 * n_cast + 3]
    slot_bias = refs[2 * n_cast + 3]
    for src, dst in zip(cast_in, cast_out):
        dst[...] = src[...].astype(dst.dtype)
    sub = lax.broadcasted_iota(jnp.int32, (8, 2 * RET_DK), 0)
    lane = lax.broadcasted_iota(jnp.int32, (8, 2 * RET_DK), 1)
    row0 = sub == 0
    srow = lax.broadcasted_iota(jnp.int32, (2 * RET_DK, RET_DV), 0)
    for i in range(seqs):
        for p in range(RET_PAIRS):
            pc = slice(p * 2 * RET_DK, (p + 1) * 2 * RET_DK)
            q2 = jnp.where(row0, jnp.broadcast_to(q_ref[i, :, pc], (8, 2 * RET_DK)), 0.0)
            k2 = jnp.where(row0, jnp.broadcast_to(k_ref[i, :, pc], (8, 2 * RET_DK)), 0.0)
            pst = st_ref[i, p]
            gam = gam_ref[p]
            outer = []
            for hh in range(2):
                h = 2 * p + hh
                hc = slice(h * RET_DV, (h + 1) * RET_DV)
                hsel = (lane < RET_DK) if hh == 0 else (lane >= RET_DK)
                qm = jnp.where(hsel, q2, 0.0)
                km = jnp.where(hsel, k2, 0.0)
                vh = v_ref[i, :, hc]
                v8 = jnp.where(row0[:, :RET_DV], jnp.broadcast_to(vh, (8, RET_DV)), 0.0)
                cross = _dot(qm.astype(BF16), (pst * gam).astype(BF16))[0:1, :]
                qk = jnp.sum(qm[0:1, :] * km[0:1, :], axis=-1, keepdims=True)
                o = cross + qk * vh
                ret_ref[i, :, hc] = _gn_swish(o, g_ref[i, :, hc], gn_ref[:, hc])
                outer.append(_dot_tn(k2.astype(BF16), v8.astype(BF16)))
            nst_ref[i, p] = pst * gam + jnp.where(srow < RET_DK, outer[0], outer[1])
    scale = ATT_HD ** -0.5
    caches = ((ck0_ref, cv0_ref), (ck1_ref, cv1_ref), (ck2_ref, cv2_ref))
    slot = lax.broadcasted_iota(jnp.int32, (ATT_NK, ATT_HD), 0)
    is_new = slot == 0
    first = row0[:, :ATT_HD]

    @pl.when(pl.program_id(0) == 0)
    def _():
        pieces = _split3(rb_ref[...])
        for g in range(ATT_GROUPS):
            rows = slice(g * ATT_HPG, (g + 1) * ATT_HPG)
            slot_bias[rows, :] = (_dot(pieces[0][rows], slot_ref[g]) + _dot(pieces[1][rows], slot_ref[g])
                                  + _dot(pieces[2][rows], slot_ref[g]))

    def one_row(x):
        return jnp.where(first, jnp.broadcast_to(x, (8, ATT_HD)), 0.0).astype(BF16)

    scores = {}
    for i in range(seqs):
        for h in range(ATT_HPG):
            for g in range(ATT_GROUPS):
                c0 = g * ATT_GW + h * ATT_HD
                kk = jnp.where(is_new, qkv_ref[i, :, ATT_W + c0:ATT_W + c0 + ATT_HD], caches[g][0][i, :, h, :])
                scores[i, g, h] = _dot_nt(one_row(qkv_ref[i, :, c0:c0 + ATT_HD]), kk.astype(BF16))[0:1, :]
    for i in range(seqs):
        for h in range(ATT_HPG):
            o_g, lse_g = [], []
            for g in range(ATT_GROUPS):
                c0 = g * ATT_GW + h * ATT_HD
                vv = jnp.where(is_new, qkv_ref[i, :, 2 * ATT_W + c0:2 * ATT_W + c0 + ATT_HD],
                               caches[g][1][i, :, h, :])
                gh = g * ATT_HPG + h
                s = scores[i, g, h] * scale + slot_bias[gh:gh + 1, :]
                m = jnp.max(s, axis=-1, keepdims=True)
                pr = jnp.exp(s - m)
                den = jnp.sum(pr, axis=-1, keepdims=True)
                o_g.append(_dot(one_row(pr), vv.astype(BF16))[0:1, :] / den)
                lse_g.append(m + jnp.log(den))
            mx = jnp.maximum(jnp.maximum(lse_g[0], lse_g[1]), lse_g[2])
            e = [jnp.exp(l - mx) for l in lse_g]
            tot = e[0] + e[1] + e[2]
            att_ref[i, :, h * ATT_HD:(h + 1) * ATT_HD] = ((e[0] / tot) * o_g[0] + (e[1] / tot) * o_g[1]
                                                          + (e[2] / tot) * o_g[2])


def _decode(q, k, v, g, gn, state, qkv, rel_bias_t, caches, cast):
    batch = q.shape[0]
    gam = jnp.asarray(np.ascontiguousarray(np.broadcast_to(
        np.repeat(np.exp(_ret_log_decay()), RET_DK).reshape(RET_PAIRS, 2 * RET_DK, 1),
        (RET_PAIRS, 2 * RET_DK, RET_DV))), F32)
    vec = lambda a: a.reshape(batch, 1, a.shape[1])
    seqs = DECODE_SEQS
    assert batch % seqs == 0
    vspec = lambda width: pl.BlockSpec((seqs, 1, width), lambda b: (b, 0, 0))
    const = lambda a: pl.BlockSpec(a.shape, lambda b: (0,) * a.ndim)
    st_spec = pl.BlockSpec((seqs, RET_PAIRS, 2 * RET_DK, RET_DV), lambda b: (b, 0, 0, 0))
    cache_in, cache_specs = [], []
    for gi in range(ATT_GROUPS):
        dil = ATT_DILATIONS[gi]
        for c in caches[2 * gi:2 * gi + 2]:
            cache_in.append(c.reshape(batch, ATT_NK, dil, ATT_HPG, ATT_HD))
            cache_specs.append(pl.BlockSpec((seqs, ATT_NK, None, ATT_HPG, ATT_HD), lambda b: (b, 0, 0, 0, 0)))
    st_pairs = state.reshape(batch, RET_PAIRS, 2 * RET_DK, RET_DV)
    slot_sel = _slot_select()
    steps = batch // seqs
    slab = lambda w: pl.BlockSpec((w.shape[0] // steps, w.shape[1]), lambda b: (b, 0))
    assert all(w.shape[0] % (16 * steps) == 0 for w in cast)
    res = pl.pallas_call(
        functools.partial(_decode_body, seqs=seqs, n_cast=len(cast)),
        grid=(steps,),
        in_specs=[vspec(RET_QK_W), vspec(RET_QK_W), vspec(RET_V_W), vspec(RET_V_W), const(gn), st_spec, const(gam),
                  vspec(3 * ATT_W), const(rel_bias_t), const(slot_sel)] + cache_specs + [slab(w) for w in cast],
        out_specs=[vspec(RET_V_W), vspec(ATT_GW), st_spec] + [slab(w) for w in cast],
        out_shape=[jax.ShapeDtypeStruct((batch, 1, RET_V_W), F32),
                   jax.ShapeDtypeStruct((batch, 1, ATT_GW), F32),
                   jax.ShapeDtypeStruct(st_pairs.shape, F32)]
        + [jax.ShapeDtypeStruct(w.shape, BF16) for w in cast],
        scratch_shapes=[pltpu.VMEM((rel_bias_t.shape[0], ATT_NK), F32)],
        compiler_params=_params(1),
        name="decode",
    )(vec(q), vec(k), vec(v), vec(g), gn, st_pairs, gam, vec(qkv), rel_bias_t, slot_sel, *cache_in, *cast)
    ret, att, nst = res[:3]
    return ret.reshape(batch, RET_V_W), att.reshape(batch, ATT_GW), nst.reshape(state.shape), tuple(res[3:])


def kernel(x_prompt, x_sample, state_ret, cache_k_w128, cache_v_w128, cache_k_w512, cache_v_w512,
           cache_k_w2048, cache_v_w2048, p_prompt, p_sample, ln1_g, w_in, ret_gn_g, w_ret_br, w_att_br,
           w_out, ln2_g, w_up, w_down, w_ple, w_ple_gate, rel_bias, lnf_g):
    depth, _, n_in = w_in.shape
    assert depth == 1 and n_in == N_IN
    batch, seq, _ = x_prompt.shape
    dec_batch, dec_seq, _ = x_sample.shape
    assert dec_seq == 1
    l = 0
    ln1 = ln1_g[l][None, :]
    ln2 = ln2_g[l][None, :]
    lnf = lnf_g[None, :]
    gn = ret_gn_g[l][None, :]
    w_ret, *w_gates = _cast_ret_weights(w_in[l])
    inv_row = _rope_inv_row()

    caches = (cache_k_w128[l], cache_v_w128[l], cache_k_w512[l], cache_v_w512[l],
              cache_k_w2048[l], cache_v_w2048[l])

    xs = x_sample.reshape(dec_batch, D_MODEL)
    xp = x_prompt.reshape(batch * seq, D_MODEL)
    (rq, rk, rv, rg, gr, ga, sq, sk, sv, sg, sgr, sga), w_qkv = _inproj_ret(
        xp, xs, ln1, w_ret, w_gates, w_in[l], inv_row, TM_INPROJ, seq, PAST_LEN)
    att_o = _inproj_att(xp, xs, ln1, w_qkv, TM_INPROJ, seq)
    aqs, aks, avs = att_o[0:3], att_o[3:6], att_o[6:9]
    kfull, vfull = att_o[9:12], att_o[12:15]
    s_qkv = att_o[15]

    rel_bias_t = rel_bias.T
    s_ret, s_attn, new_st, (wrb, wab, wo, wpl, wpg) = _decode(
        sq, sk, sv, sg, gn, state_ret[l], s_qkv, rel_bias_t, caches,
        (w_ret_br[l], w_att_br[l], w_out[l], w_ple[l], w_ple_gate[l]))
    ple_s = p_sample[l].reshape(dec_batch, D_PLE)

    ret_out, st_p, (wu, wd) = _retention_prompt(rq, rk, rv, rg, gn, batch, seq, TS_RETENTION, (w_up[l], w_down[l]))
    tail_w = (wrb, wab, wo, wu, wd, wpl, wpg, ln2, lnf)
    outs, lses = [], []
    band_onehot, band_neg = _band_select()
    for g in range(ATT_GROUPS):
        o, lse = _attention_prompt(aqs[g], aks[g], avs[g], rel_bias_t, band_onehot, band_neg, g)
        outs.append(o)
        lses.append(lse)
    ple_p = p_prompt[l].reshape(batch * seq, D_PLE)
    tm_tail = batch * seq // dec_batch
    y_p, y_s, new_caches = _tail_shift([ret_out] + outs + lses + [gr, ga, xp, ple_p],
                                       [s_ret, s_attn, sgr, sga, xs, ple_s], tail_w, tm_tail, s_qkv, caches)
    y_prompt = y_p.reshape(batch, seq, D_MODEL)
    y_sample = y_s.reshape(dec_batch, 1, D_MODEL)
    new_state_p = st_p[None]
    kv_p = []
    for g in range(ATT_GROUPS):
        shape = (1, batch, min(ATT_WINDOWS[g], seq), ATT_HPG, ATT_HD)
        kv_p.append(kfull[g].reshape(shape))
        kv_p.append(vfull[g].reshape(shape))
    kv_s = [c[None] for c in new_caches]

    return (y_prompt, y_sample, new_state_p, *kv_p, new_st[None], *kv_s)
```

```python
import functools
import math

import jax
import jax.numpy as jnp
import numpy as np
from jax import lax
from jax.experimental import pallas as pl
from jax.experimental.pallas import tpu as pltpu

F32 = jnp.float32
BF16 = jnp.bfloat16

D_MODEL = 1024
RET_HEADS = 8
RET_DK = 64
RET_DV = 128
RET_PAIRS = RET_HEADS // 2
RET_CHUNK = 128
ROPE_BASE = 10000.0
ATT_WINDOWS = (128, 512, 2048)
ATT_DILATIONS = (1, 4, 16)
ATT_GROUPS = 3
ATT_HPG = 4
ATT_HD = 128
ATT_NK = 128
ATT_GW = ATT_HPG * ATT_HD
REL_BUCKETS = 32
REL_MAX_DIST = 2048
D_FF = 4 * D_MODEL
D_PLE = 256
NORM_EPS = 1e-6
RET_QK_W = RET_HEADS * RET_DK
RET_V_W = RET_HEADS * RET_DV
ATT_W = ATT_GROUPS * ATT_GW
COL_ATT = 2 * RET_QK_W + 2 * RET_V_W
COL_GATE = COL_ATT + 3 * ATT_W
N_IN = COL_GATE + 2 * D_MODEL
PAST_LEN = 16384

VMEM_LIMIT_V7X = 56 * 1024 * 1024
VMEM_LIMIT_TAIL_V7X = 62 * 1024 * 1024
LANES = 128
TM_INPROJ = 512
TS_RETENTION = 512
ATTN_UNROLL = 8
PERM_ROWS = 256
SINGLE_BLOCK_GROUP = 4
DECODE_SEQS = 4
CAST_STEPS = 8


def _dot(a, b):
    return jnp.dot(a, b, preferred_element_type=F32)


def _dot_nt(a, b):
    return lax.dot_general(a, b, (((1,), (1,)), ((), ())), preferred_element_type=F32)


def _dot_tn(a, b):
    return lax.dot_general(a, b, (((0,), (0,)), ((), ())), preferred_element_type=F32)


def _rms(x, g):
    return x * lax.rsqrt(jnp.mean(x * x, axis=-1, keepdims=True) + NORM_EPS) * g


def _sigmoid(x):
    return 1.0 / (1.0 + jnp.exp(-x))


def _resident(shape):
    return pl.BlockSpec(shape, lambda *_: (0,) * len(shape), pipeline_mode=pl.Buffered(1))


def _params(n_axes):
    return pltpu.CompilerParams(dimension_semantics=("arbitrary",) * n_axes,
                                vmem_limit_bytes=VMEM_LIMIT_V7X)


def _rope_inv_row():
    half = RET_DK // 2
    inv = ROPE_BASE ** (-jnp.arange(half, dtype=F32) / half)
    return jnp.tile(inv, LANES // half)[None, :]


def _ret_log_decay():
    return np.log1p(-np.exp2(-5.0 - np.arange(RET_HEADS, dtype=np.float32))).astype(np.float32)


def _ret_tables():
    c = RET_CHUNK
    lg = _ret_log_decay()
    i = np.arange(c, dtype=np.float32)
    diff = i[:, None] - i[None, :]
    dmask = np.where(diff[None] >= 0, np.exp(np.maximum(diff, 0.0)[None] * lg[:, None, None]), 0.0)
    q_decay = np.exp((i + 1.0)[:, None] * lg[None, :])
    k_decay = np.exp((c - 1.0 - i)[:, None] * lg[None, :])
    qdec = np.broadcast_to(q_decay.T[:, :, None], (RET_HEADS, c, RET_DV))
    kdec = np.repeat(k_decay, RET_DK, axis=1).reshape(c, RET_PAIRS, 2 * RET_DK).transpose(1, 0, 2)
    gc = np.repeat(np.exp(c * lg), RET_DV).reshape(RET_PAIRS, 1, 2 * RET_DV)
    return tuple(jnp.asarray(np.ascontiguousarray(t), F32) for t in (dmask, qdec, kdec, gc))


def _rel_buckets():
    max_exact = REL_BUCKETS // 2
    out = []
    for dil in ATT_DILATIONS:
        d = np.arange(ATT_NK, dtype=np.int32) * dil
        log_ratio = (np.log(np.maximum(d, 1).astype(np.float32) / np.float32(max_exact))
                     / np.float32(math.log(REL_MAX_DIST / max_exact)))
        large = max_exact + (log_ratio * np.float32(REL_BUCKETS - max_exact)).astype(np.int32)
        out.append(np.where(d < max_exact, d, np.minimum(large, REL_BUCKETS - 1)))
    return np.stack(out)


def _band_select():
    nk = ATT_NK
    buckets = _rel_buckets()
    onehot = np.zeros((ATT_GROUPS, REL_BUCKETS, 3 * nk), np.float32)
    for g in range(ATT_GROUPS):
        for k in range(nk, 2 * nk):
            onehot[g, buckets[g, 2 * nk - 1 - k], k] = 1.0
    mask = np.full((1, 3 * nk), -np.inf, np.float32)
    mask[0, nk:2 * nk] = 0.0
    return jnp.asarray(onehot, BF16), jnp.asarray(mask, F32)


def _slot_select():
    buckets = _rel_buckets()
    onehot = np.zeros((ATT_GROUPS, REL_BUCKETS, ATT_NK), np.float32)
    for g in range(ATT_GROUPS):
        for slot in range(ATT_NK):
            onehot[g, buckets[g, 0 if slot == 0 else ATT_NK - slot], slot] = 1.0
    return jnp.asarray(onehot, BF16)


def _split3(x):
    hi = x.astype(BF16)
    rem = x - hi.astype(F32)
    mid = rem.astype(BF16)
    lo = (rem - mid.astype(F32)).astype(BF16)
    return hi, mid, lo


def _cast_body(*refs):
    n = len(refs) // 2
    for src, dst in zip(refs[:n], refs[n:]):
        dst[...] = src[...].astype(dst.dtype)


def _cast_ret_weights(w_in):
    rows = D_MODEL // CAST_STEPS
    half = D_MODEL // 2
    assert COL_GATE % half == 0
    widths_blocks = [(COL_ATT, 0)] + [(half, COL_GATE // half + j) for j in range(4)]
    window = lambda width, blk: pl.BlockSpec((rows, width), lambda i: (i, blk))
    return pl.pallas_call(
        _cast_body,
        grid=(CAST_STEPS,),
        in_specs=[window(wd, blk) for wd, blk in widths_blocks],
        out_specs=[window(wd, 0) for wd, _ in widths_blocks],
        out_shape=[jax.ShapeDtypeStruct((D_MODEL, wd), BF16) for wd, _ in widths_blocks],
        compiler_params=_params(1),
        name="cast_ret_weights",
    )(*([w_in] * len(widths_blocks)))


def _inproj_ret_body(x_ref, xs_ref, ln_ref, w_ref, wg0_ref, wg1_ref, wg2_ref, wg3_ref, inv_ref, *refs,
                     tm, tiles, n_steps, sample_pos):
    cast_in = refs[:3]
    outs, sample_outs = refs[3:9], refs[9:15]
    cast_out = refs[15:18]
    cos_s, sin_s = refs[18:]
    for src, dst in zip(cast_in, cast_out):
        dst[...] = src[...].astype(dst.dtype)
    i = pl.program_id(0)

    def first_half(rows):
        lane = lax.broadcasted_iota(jnp.int32, (rows, LANES), 1)
        return (lane % RET_DK) < (RET_DK // 2)

    def tables(pos):
        ang = pos.astype(F32) * inv_ref[...]
        sin = jnp.sin(ang)
        return jnp.cos(ang), jnp.where(first_half(pos.shape[0]), -sin, sin)

    def project(x, cos, sin, q_ref, k_ref, v_ref, g_ref, gr_ref, ga_ref):
        h = _rms(x, ln_ref[...]).astype(BF16)
        qk = _dot(h, w_ref[:, 0:2 * RET_QK_W])
        n_q = RET_QK_W // LANES
        fh = first_half(x.shape[0])
        for c in range(2 * n_q):
            xc = qk[:, c * LANES:(c + 1) * LANES]
            swapped = jnp.where(fh, pltpu.roll(xc, LANES - RET_DK // 2, 1), pltpu.roll(xc, RET_DK // 2, 1))
            r = xc * cos + swapped * sin
            if c < n_q:
                q_ref[:, c * LANES:(c + 1) * LANES] = r.astype(q_ref.dtype)
            else:
                k_ref[:, (c - n_q) * LANES:(c - n_q + 1) * LANES] = (r * (RET_DK ** -0.5)).astype(k_ref.dtype)
        o = 2 * RET_QK_W
        v_ref[...] = _dot(h, w_ref[:, o:o + RET_V_W]).astype(v_ref.dtype)
        o += RET_V_W
        g_ref[...] = _dot(h, w_ref[:, o:o + RET_V_W]).astype(g_ref.dtype)
        half = D_MODEL // 2
        gr_ref[:, :half] = _dot(h, wg0_ref[...]).astype(gr_ref.dtype)
        gr_ref[:, half:] = _dot(h, wg1_ref[...]).astype(gr_ref.dtype)
        ga_ref[:, :half] = _dot(h, wg2_ref[...]).astype(ga_ref.dtype)
        ga_ref[:, half:] = _dot(h, wg3_ref[...]).astype(ga_ref.dtype)

    trow = pl.ds(pl.multiple_of(lax.rem(i, tiles) * tm, tm), tm)

    @pl.when(i < tiles)
    def _():
        cos_s[trow, :], sin_s[trow, :] = tables(i * tm + lax.broadcasted_iota(jnp.int32, (tm, LANES), 0))
    project(x_ref[...], cos_s[trow, :], sin_s[trow, :], *outs)

    @pl.when(i == n_steps - 1)
    def _():
        ns = xs_ref.shape[0]
        project(xs_ref[...], *tables(jnp.full((ns, LANES), sample_pos, jnp.int32)), *sample_outs)


def _inproj_ret(x2d, xs2d, ln, w_ret, w_gates, w_in, inv_row, tm, seq, sample_pos):
    n = x2d.shape[0]
    ns = xs2d.shape[0]
    tiles = seq // tm
    n_steps = n // tm
    row = lambda width: pl.BlockSpec((tm, width), lambda i: (i, 0))
    widths = (RET_QK_W, RET_QK_W, RET_V_W, RET_V_W, D_MODEL, D_MODEL)
    slab = D_MODEL // n_steps
    assert slab % 16 == 0 and COL_ATT % ATT_W == 0
    res = pl.pallas_call(
        functools.partial(_inproj_ret_body, tm=tm, tiles=tiles, n_steps=n_steps, sample_pos=sample_pos),
        grid=(n_steps,),
        in_specs=[row(D_MODEL), _resident(xs2d.shape), _resident((1, D_MODEL)), _resident(w_ret.shape)]
        + [_resident(w.shape) for w in w_gates] + [_resident((1, LANES))]
        + [pl.BlockSpec((slab, ATT_W), lambda i, kind=kind: (i, COL_ATT // ATT_W + kind)) for kind in range(3)],
        out_specs=[row(wd) for wd in widths] + [pl.BlockSpec((ns, wd), lambda i: (0, 0)) for wd in widths]
        + [pl.BlockSpec((slab, ATT_W), lambda i: (i, 0)) for _ in range(3)],
        out_shape=[jax.ShapeDtypeStruct((n, wd), BF16) for wd in widths]
        + [jax.ShapeDtypeStruct((ns, wd), F32) for wd in widths]
        + [jax.ShapeDtypeStruct((D_MODEL, ATT_W), BF16) for _ in range(3)],
        scratch_shapes=[pltpu.VMEM((seq, LANES), F32), pltpu.VMEM((seq, LANES), F32)],
        compiler_params=_params(1),
        name="inproj_ret",
    )(x2d, xs2d, ln, w_ret, *w_gates, inv_row, w_in, w_in, w_in)
    return res[:12], res[12:]


def _inproj_att_body(x_ref, xs_ref, ln_ref, wq_ref, wk_ref, wv_ref, *refs, tm, keeps, seq, n_steps):
    lowp = refs[:3 * ATT_GROUPS]
    full = refs[3 * ATT_GROUPS:5 * ATT_GROUPS]
    sample_ref = refs[5 * ATT_GROUPS]

    @pl.when(pl.program_id(0) == n_steps - 1)
    def _():
        hs = _rms(xs_ref[...], ln_ref[...]).astype(BF16)
        for kind, w_ref in enumerate((wq_ref, wk_ref, wv_ref)):
            for g in range(ATT_GROUPS):
                c = kind * ATT_GROUPS + g
                sample_ref[:, c * ATT_GW:(c + 1) * ATT_GW] = _dot(hs, w_ref[:, g * ATT_GW:(g + 1) * ATT_GW])

    h = _rms(x_ref[...], ln_ref[...]).astype(BF16)
    for g in reversed(range(ATT_GROUPS)):
        for kind in reversed(range(3)):
            dil = ATT_DILATIONS[g]
            dst = lowp[kind * ATT_GROUPS + g]
            r = _dot(h, (wq_ref, wk_ref, wv_ref)[kind][:, g * ATT_GW:(g + 1) * ATT_GW])
            if dil == 1:
                dst[0] = r.astype(dst.dtype)
            else:
                dst[...] = jnp.swapaxes(r.reshape(tm // dil, dil, ATT_GW), 0, 1).astype(dst.dtype)
            if kind > 0:
                cache = full[(kind - 1) * ATT_GROUPS + g]
                rows = tm if keeps[g] == seq else keeps[g]
                for hh in range(ATT_HPG):
                    cache[pl.ds(hh, rows, stride=ATT_HPG), :] = r[tm - rows:, hh * ATT_HD:(hh + 1) * ATT_HD]


def _inproj_att(x2d, xs2d, ln, w_qkv, tm, seq):
    n = x2d.shape[0]
    ns = xs2d.shape[0]
    batch = n // seq
    tiles = seq // tm
    keeps = tuple(min(wd, seq) for wd in ATT_WINDOWS)
    assert all(kp <= tm or kp == seq for kp in keeps) and seq % tm == 0
    out_specs, out_shape = [], []
    for _ in range(3):
        for g in range(ATT_GROUPS):
            dil = ATT_DILATIONS[g]
            out_specs.append(pl.BlockSpec((None, dil, tm // dil, ATT_GW), lambda i: (i // tiles, 0, i % tiles, 0)))
            out_shape.append(jax.ShapeDtypeStruct((batch, dil, seq // dil, ATT_GW), BF16))
    for _ in range(2):
        for g in range(ATT_GROUPS):
            if keeps[g] == seq:
                idx = lambda i: (i // tiles, i % tiles, 0)
                rows = tm
            else:
                idx = lambda i: (i // tiles, 0, 0)
                rows = keeps[g]
            out_specs.append(pl.BlockSpec((None, rows * ATT_HPG, ATT_HD), idx))
            out_shape.append(jax.ShapeDtypeStruct((batch, keeps[g] * ATT_HPG, ATT_HD), F32))
    out_specs.append(pl.BlockSpec((ns, 3 * ATT_W), lambda i: (0, 0)))
    out_shape.append(jax.ShapeDtypeStruct((ns, 3 * ATT_W), F32))
    return pl.pallas_call(
        functools.partial(_inproj_att_body, tm=tm, keeps=keeps, seq=seq, n_steps=n // tm),
        grid=(n // tm,),
        in_specs=[pl.BlockSpec((tm, D_MODEL), lambda i: (i, 0)), _resident(xs2d.shape), _resident((1, D_MODEL))]
        + [_resident(w.shape) for w in w_qkv],
        out_specs=out_specs,
        out_shape=out_shape,
        compiler_params=_params(1),
        name="inproj_att",
    )(x2d, xs2d, ln, *w_qkv)


def _gn_swish(o, gate, gn):
    mu = jnp.mean(o, axis=-1, keepdims=True)
    d = o - mu
    var = jnp.mean(d * d, axis=-1, keepdims=True)
    on = d * lax.rsqrt(var + NORM_EPS) * gn
    return gate * _sigmoid(gate) * on


def _retention_body(q_ref, k_ref, v_ref, g_ref, gn_ref, dm_ref, qdec_ref, kdec_ref, gc_ref, *refs,
                    n_chunks, n_cast):
    cast_in = refs[:n_cast]
    out_ref, st_ref = refs[n_cast:n_cast + 2]
    cast_out = refs[n_cast + 2:2 * n_cast + 2]
    state = refs[2 * n_cast + 2]
    for src, dst in zip(cast_in, cast_out):
        dst[...] = src[...].astype(dst.dtype)
    c = RET_CHUNK

    @pl.when(pl.program_id(1) == 0)
    def _():
        state[...] = jnp.zeros_like(state)

    lane = lax.broadcasted_iota(jnp.int32, (c, 2 * RET_DK), 1)
    head0 = lane < RET_DK

    def chunk(ci, carry):
        rows = pl.ds(pl.multiple_of(ci * c, c), c)
        scores, cross, values = {}, {}, {}
        for p in range(RET_PAIRS):
            q2 = q_ref[rows, p * 2 * RET_DK:(p + 1) * 2 * RET_DK]
            k2 = k_ref[rows, p * 2 * RET_DK:(p + 1) * 2 * RET_DK]
            v2 = v_ref[rows, p * 2 * RET_DV:(p + 1) * 2 * RET_DV]
            pst = state[p]
            pst_lo = pst.astype(BF16)
            zero = jnp.zeros_like(q2)
            for hh in range(2):
                h = 2 * p + hh
                qm = jnp.where(head0 if hh == 0 else jnp.logical_not(head0), q2, zero)
                values[h] = v2[:, hh * RET_DV:(hh + 1) * RET_DV]
                scores[h] = _dot_nt(qm, k2)
                cross[h] = _dot(qm, pst_lo[:, hh * RET_DV:(hh + 1) * RET_DV])
            kd = (k2.astype(F32) * kdec_ref[p]).astype(BF16)
            state[p] = pst * gc_ref[p] + _dot_tn(kd, v2)
        for h in range(RET_HEADS):
            o = _dot((scores[h] * dm_ref[h]).astype(BF16), values[h]) + cross[h] * qdec_ref[h]
            gate = g_ref[rows, h * RET_DV:(h + 1) * RET_DV].astype(F32)
            res = _gn_swish(o, gate, gn_ref[:, h * RET_DV:(h + 1) * RET_DV])
            out_ref[rows, h * RET_DV:(h + 1) * RET_DV] = res.astype(out_ref.dtype)
        return carry

    lax.fori_loop(0, n_chunks, chunk, 0, unroll=True)
    for p in range(RET_PAIRS):
        pst = state[p]
        for hh in range(2):
            st_ref[2 * p + hh] = pst[hh * RET_DK:(hh + 1) * RET_DK, hh * RET_DV:(hh + 1) * RET_DV]


def _retention_prompt(q, k, v, g, gn, batch, seq, ts, cast):
    n = q.shape[0]
    steps = seq // ts
    dmask, qdec, kdec, gc = _ret_tables()
    row = lambda width: pl.BlockSpec((ts, width), lambda b, s: (b * steps + s, 0))
    const = lambda a: pl.BlockSpec(a.shape, lambda b, s: (0,) * a.ndim)
    n_steps = batch * steps
    slab = lambda w: pl.BlockSpec((w.shape[0] // n_steps, w.shape[1]), lambda b, s: (b * steps + s, 0))
    assert all(w.shape[0] % (16 * n_steps) == 0 for w in cast)
    res = pl.pallas_call(
        functools.partial(_retention_body, n_chunks=ts // RET_CHUNK, n_cast=len(cast)),
        grid=(batch, steps),
        in_specs=[row(RET_QK_W), row(RET_QK_W), row(RET_V_W), row(RET_V_W), const(gn),
                  const(dmask), const(qdec), const(kdec), const(gc)] + [slab(w) for w in cast],
        out_specs=[row(RET_V_W),
                   pl.BlockSpec((None, RET_HEADS, RET_DK, RET_DV), lambda b, s: (b, 0, 0, 0))]
        + [slab(w) for w in cast],
        out_shape=[jax.ShapeDtypeStruct((n, RET_V_W), BF16),
                   jax.ShapeDtypeStruct((batch, RET_HEADS, RET_DK, RET_DV), F32)]
        + [jax.ShapeDtypeStruct(w.shape, BF16) for w in cast],
        scratch_shapes=[pltpu.VMEM((RET_PAIRS, 2 * RET_DK, 2 * RET_DV), F32)],
        compiler_params=_params(2),
        name="retention",
    )(q, k, v, g, gn, dmask, qdec, kdec, gc, *cast)
    return res[0], res[1], tuple(res[2:])


def _attn_body(q_ref, k_ref, v_ref, rb_ref, sel_ref, neg_ref, o_ref, lse_ref, *scratch, group, dil, n_blocks):
    nk = ATT_NK
    scale = ATT_HD ** -0.5
    lane = lax.broadcasted_iota(jnp.int32, (nk, LANES), 1)
    tabs = scratch[-1]

    @pl.when(pl.program_id(0) == 0)
    def _():
        band = neg_ref[...]
        for piece in _split3(rb_ref[group * ATT_HPG:(group + 1) * ATT_HPG, :]):
            band = band + _dot(piece, sel_ref[...])
        for i in range(nk):
            window = band[:, nk - 1 - i:3 * nk - 1 - i]
            for h in range(ATT_HPG):
                tabs[1, h, i:i + 1, :] = window[h:h + 1, :]
        col = lax.broadcasted_iota(jnp.int32, (nk, 2 * nk), 1)
        for h in range(ATT_HPG):
            tabs[0, h] = jnp.where(col < nk, -jnp.inf, tabs[1, h])
    if dil > 1:
        o_s, l_s = scratch[:2]

    def chain(s_raw, v_ext, sel, h):
        s = s_raw * scale + (tabs[sel, h] if n_blocks > 1 else tabs[1, h, :, nk:])
        m = jnp.max(s, axis=-1, keepdims=True)
        p = jnp.exp(s - m)
        pv = _dot(p.astype(BF16), v_ext)
        den = pv[:, ATT_HD:]
        return (pv[:, :ATT_HD] / den).astype(BF16), m + jnp.log(den)

    def write(rs, rows, outs):
        lse_tile = jnp.zeros((nk, LANES), F32)
        for h, (o, lse) in enumerate(outs):
            hc = slice(h * ATT_HD, (h + 1) * ATT_HD)
            if dil == 1:
                o_ref[rows, hc] = o
            else:
                o_s[rs, rows, hc] = o
            lse_tile = jnp.where(lane == h, lse, lse_tile)
        if dil == 1:
            lse_ref[rows, :] = lse_tile
        else:
            l_s[rs, rows, :] = lse_tile

    heads = [slice(h * ATT_HD, (h + 1) * ATT_HD) for h in range(ATT_HPG)]

    def block(rs, n, n_prev, sel):
        rows = pl.ds(pl.multiple_of(n * nk, nk), nk)
        prev_rows = pl.ds(pl.multiple_of(n_prev * nk, nk), nk)
        q = q_ref[rs, rows, :]
        k_all = jnp.concatenate([k_ref[rs, prev_rows, :], k_ref[rs, rows, :]], axis=0)
        v_all = jnp.concatenate([v_ref[rs, prev_rows, :], v_ref[rs, rows, :]], axis=0)
        ones = jnp.ones((2 * nk, ATT_HD), BF16)
        outs = []
        for h, hc in enumerate(heads):
            v_ext = jnp.concatenate([v_all[:, hc], ones], axis=1)
            outs.append(chain(_dot_nt(q[:, hc], k_all[:, hc]), v_ext, sel, h))
        write(rs, rows, outs)

    def single_blocks(streams):
        ones = jnp.ones((nk, ATT_HD), BF16)
        scores = [[_dot_nt(q_ref[rs, :, hc], k_ref[rs, :, hc]) for hc in heads] for rs in streams]
        for i, rs in enumerate(streams):
            outs = [chain(scores[i][h], jnp.concatenate([v_ref[rs, :, hc], ones], axis=1), 0, h)
                    for h, hc in enumerate(heads)]
            write(rs, slice(None), outs)

    if n_blocks == 1:
        for r0 in range(0, dil, SINGLE_BLOCK_GROUP):
            single_blocks(range(r0, min(r0 + SINGLE_BLOCK_GROUP, dil)))
    else:
        for rs in range(dil):
            def loop(n, carry, rs=rs):
                block(rs, n, jnp.maximum(n - 1, 0), jnp.minimum(n, 1))
                return carry
            lax.fori_loop(0, n_blocks, loop, 0, unroll=ATTN_UNROLL)

    if dil > 1:
        rows = o_ref.shape[0]
        for j in range(rows // PERM_ROWS):
            src = slice(j * (PERM_ROWS // dil), (j + 1) * (PERM_ROWS // dil))
            dst = slice(j * PERM_ROWS, (j + 1) * PERM_ROWS)
            o_ref[dst, :] = jnp.swapaxes(o_s[:, src, :], 0, 1).reshape(PERM_ROWS, ATT_GW).astype(BF16)
            lse_ref[dst, :] = jnp.swapaxes(l_s[:, src, :], 0, 1).reshape(PERM_ROWS, LANES)


def _attention_prompt(aq, ak, av, rel_bias_t, onehot, neg, g):
    batch, dil, length, _ = aq.shape
    seq = dil * length
    n_blocks = length // ATT_NK
    blk = pl.BlockSpec((None, dil, length, ATT_GW), lambda b: (b, 0, 0, 0))
    scratch = [pltpu.VMEM((dil, length, ATT_GW), BF16), pltpu.VMEM((dil, length, LANES), F32)] if dil > 1 else []
    scratch.append(pltpu.VMEM((2, ATT_HPG, ATT_NK, 2 * ATT_NK), F32))
    o, lse = pl.pallas_call(
        functools.partial(_attn_body, group=g, dil=dil, n_blocks=n_blocks),
        grid=(batch,),
        in_specs=[blk, blk, blk, pl.BlockSpec(rel_bias_t.shape, lambda b: (0, 0)),
                  pl.BlockSpec((None, REL_BUCKETS, 3 * ATT_NK), lambda b: (g, 0, 0)),
                  pl.BlockSpec((1, 3 * ATT_NK), lambda b: (0, 0))],
        out_specs=[pl.BlockSpec((None, seq, ATT_GW), lambda b: (b, 0, 0)),
                   pl.BlockSpec((None, seq, LANES), lambda b: (b, 0, 0))],
        out_shape=[jax.ShapeDtypeStruct((batch, seq, ATT_GW), BF16),
                   jax.ShapeDtypeStruct((batch, seq, LANES), F32)],
        scratch_shapes=scratch,
        compiler_params=_params(1),
        name="attention_g%d" % g,
    )(aq, ak, av, rel_bias_t, onehot, neg)
    return o.reshape(batch * seq, ATT_GW), lse.reshape(batch * seq, LANES)


def _tail_math(act_refs, w_refs, y_ref, combine, middle=None):
    wrb_ref, wab_ref, wo_ref, wu_ref, wd_ref, wpl_ref, wpg_ref, ln2_ref, lnf_ref = w_refs
    if combine:
        ret_ref, o0_ref, o1_ref, o2_ref, l0_ref, l1_ref, l2_ref, gr_ref, ga_ref, x_ref, ple_ref = act_refs
        lses = [l0_ref[...], l1_ref[...], l2_ref[...]]
        outs = [o0_ref, o1_ref, o2_ref]
        parts = []
        for h in range(ATT_HPG):
            lh = [l[:, h:h + 1] for l in lses]
            mx = jnp.maximum(jnp.maximum(lh[0], lh[1]), lh[2])
            e = [jnp.exp(l - mx) for l in lh]
            tot = e[0] + e[1] + e[2]
            acc = None
            for g in range(ATT_GROUPS):
                term = (e[g] / tot) * outs[g][:, h * ATT_HD:(h + 1) * ATT_HD].astype(F32)
                acc = term if acc is None else acc + term
            parts.append(acc)
        att = jnp.concatenate(parts, axis=1).astype(BF16)
    else:
        ret_ref, att_ref, gr_ref, ga_ref, x_ref, ple_ref = act_refs
        att = att_ref[...].astype(BF16)
    a = _dot(ret_ref[...].astype(BF16), wrb_ref[...])
    b = _dot(att, wab_ref[...])
    mixed = _sigmoid(gr_ref[...].astype(F32)) * a + _sigmoid(ga_ref[...].astype(F32)) * b
    x1 = x_ref[...] + _dot(mixed.astype(BF16), wo_ref[...])
    h2 = _rms(x1, ln2_ref[...]).astype(BF16)
    ff_chunk = D_MODEL
    acc = None
    for c in range(D_FF // ff_chunk):
        u = _dot(h2, wu_ref[:, c * ff_chunk:(c + 1) * ff_chunk])
        r = jnp.maximum(u, 0.0)
        t = _dot((r * r).astype(BF16), wd_ref[c * ff_chunk:(c + 1) * ff_chunk, :])
        acc = t if acc is None else acc + t
        if middle is not None and c == D_FF // ff_chunk // 2 - 1:
            middle()
    x2 = x1 + acc
    gate = _sigmoid(_dot(x2.astype(BF16), wpg_ref[...]))
    x3 = x2 + gate * _dot(ple_ref[...].astype(BF16), wpl_ref[...])
    y_ref[...] = _rms(x3, lnf_ref[...])


def _tail_shift_body(*refs, n_act, n_sample, n_w, n_steps):
    nc = 2 * ATT_GROUPS
    acts = refs[:n_act]
    sample_acts = refs[n_act:n_act + n_sample]
    base = n_act + n_sample
    ws = refs[base:base + n_w]
    new_ref = refs[base + n_w]
    base += n_w + 1
    old = refs[base:base + nc]
    y_ref, ys_ref = refs[base + nc:base + nc + 2]
    base += nc + 2
    out = refs[base:base + nc]
    stage = refs[base + nc:base + 2 * nc]
    sem_in, sem_out, sem_row = refs[base + 2 * nc:]
    s = pl.program_id(0)
    keys = [2 * g for g in range(ATT_GROUPS)]
    values = [2 * g + 1 for g in range(ATT_GROUPS)]

    def copy_in(i, seq):
        width = old[i].shape[1]
        return pltpu.make_async_copy(old[i].at[seq, pl.ds(1, width - 1)], stage[i], sem_in.at[i])

    def copy_out(i, seq):
        width = old[i].shape[1]
        return pltpu.make_async_copy(stage[i], out[i].at[seq, pl.ds(0, width - 1)], sem_out.at[i])

    def copy_row(i):
        g, kind = divmod(i, 2)
        return pltpu.make_async_copy(new_ref.at[s, kind + 1, g], out[i].at[s, old[i].shape[1] - 1], sem_row.at[i])

    @pl.when(s == 0)
    def _():
        for i in keys:
            copy_in(i, 0).start()
    for i in keys:
        copy_in(i, s).wait()
    for i in keys:
        copy_out(i, s).start()

    @pl.when(s > 0)
    def _():
        for i in values:
            copy_out(i, s - 1).wait()
    for i in values:
        copy_in(i, s).start()
    for i in range(nc):
        copy_row(i).start()

    def middle():
        for i in values:
            copy_in(i, s).wait()
        for i in values:
            copy_out(i, s).start()
        for i in keys:
            copy_out(i, s).wait()

        @pl.when(s < n_steps - 1)
        def _():
            for i in keys:
                copy_in(i, s + 1).start()

    _tail_math(acts, ws, y_ref, True, middle)

    @pl.when(s == n_steps - 1)
    def _():
        _tail_math(sample_acts, ws, ys_ref, False)

    for i in range(nc):
        copy_row(i).wait()

    @pl.when(s == n_steps - 1)
    def _():
        for i in values:
            copy_out(i, s).wait()


def _tail_shift(acts, sample_acts, weights, tm, new_qkv, caches):
    n = acts[0].shape[0]
    n_steps = n // tm
    assert n_steps == new_qkv.shape[0]
    row = lambda a: pl.BlockSpec((tm, a.shape[1]), lambda i: (i, 0))
    anyspec = pl.BlockSpec(memory_space=pl.ANY)
    nc = len(caches)
    new_rows = new_qkv.reshape(new_qkv.shape[0], 3, ATT_GROUPS, ATT_HPG, ATT_HD)
    ns = sample_acts[0].shape[0]
    res = pl.pallas_call(
        functools.partial(_tail_shift_body, n_act=len(acts), n_sample=len(sample_acts), n_w=len(weights),
                          n_steps=n_steps),
        grid=(n_steps,),
        in_specs=[row(a) for a in acts] + [_resident(a.shape) for a in sample_acts]
        + [_resident(w.shape) for w in weights] + [anyspec] * (nc + 1),
        out_specs=[pl.BlockSpec((tm, D_MODEL), lambda i: (i, 0)), pl.BlockSpec((ns, D_MODEL), lambda i: (0, 0))]
        + [anyspec] * nc,
        out_shape=[jax.ShapeDtypeStruct((n, D_MODEL), F32), jax.ShapeDtypeStruct((ns, D_MODEL), F32)]
        + [jax.ShapeDtypeStruct(c.shape, c.dtype) for c in caches],
        scratch_shapes=[pltpu.VMEM((c.shape[1] - 1, ATT_HPG, ATT_HD), F32) for c in caches]
        + [pltpu.SemaphoreType.DMA((nc,)), pltpu.SemaphoreType.DMA((nc,)), pltpu.SemaphoreType.DMA((nc,))],
        compiler_params=pltpu.CompilerParams(dimension_semantics=("arbitrary",),
                                             vmem_limit_bytes=VMEM_LIMIT_TAIL_V7X),
        name="tail",
    )(*acts, *sample_acts, *weights, new_rows, *caches)
    return res[0], res[1], res[2:]


def _decode_body(q_ref, k_ref, v_ref, g_ref, gn_ref, st_ref, gam_ref,
                 qkv_ref, rb_ref, slot_ref,
                 ck0_ref, cv0_ref, ck1_ref, cv1_ref, ck2_ref, cv2_ref, *refs, seqs, n_cast):
    cast_in = refs[:n_cast]
    ret_ref, att_ref, nst_ref = refs[n_cast:n_cast + 3]
    cast_out = refs[n_cast + 3:2 * n_cast + 3]
    slot_bias = refs[2 * n_cast + 3]
    for src, dst in zip(cast_in, cast_out):
        dst[...] = src[...].astype(dst.dtype)
    sub = lax.broadcasted_iota(jnp.int32, (8, 2 * RET_DK), 0)
    lane = lax.broadcasted_iota(jnp.int32, (8, 2 * RET_DK), 1)
    row0 = sub == 0
    srow = lax.broadcasted_iota(jnp.int32, (2 * RET_DK, RET_DV), 0)
    for i in range(seqs):
        for p in range(RET_PAIRS):
            pc = slice(p * 2 * RET_DK, (p + 1) * 2 * RET_DK)
            q2 = jnp.where(row0, jnp.broadcast_to(q_ref[i, :, pc], (8, 2 * RET_DK)), 0.0)
            k2 = jnp.where(row0, jnp.broadcast_to(k_ref[i, :, pc], (8, 2 * RET_DK)), 0.0)
            pst = st_ref[i, p]
            gam = gam_ref[p]
            outer = []
            for hh in range(2):
                h = 2 * p + hh
                hc = slice(h * RET_DV, (h + 1) * RET_DV)
                hsel = (lane < RET_DK) if hh == 0 else (lane >= RET_DK)
                qm = jnp.where(hsel, q2, 0.0)
                km = jnp.where(hsel, k2, 0.0)
                vh = v_ref[i, :, hc]
                v8 = jnp.where(row0[:, :RET_DV], jnp.broadcast_to(vh, (8, RET_DV)), 0.0)
                cross = _dot(qm.astype(BF16), (pst * gam).astype(BF16))[0:1, :]
                qk = jnp.sum(qm[0:1, :] * km[0:1, :], axis=-1, keepdims=True)
                o = cross + qk * vh
                ret_ref[i, :, hc] = _gn_swish(o, g_ref[i, :, hc], gn_ref[:, hc])
                outer.append(_dot_tn(k2.astype(BF16), v8.astype(BF16)))
            nst_ref[i, p] = pst * gam + jnp.where(srow < RET_DK, outer[0], outer[1])
    scale = ATT_HD ** -0.5
    caches = ((ck0_ref, cv0_ref), (ck1_ref, cv1_ref), (ck2_ref, cv2_ref))
    slot = lax.broadcasted_iota(jnp.int32, (ATT_NK, ATT_HD), 0)
    is_new = slot == 0
    first = row0[:, :ATT_HD]

    @pl.when(pl.program_id(0) == 0)
    def _():
        pieces = _split3(rb_ref[...])
        for g in range(ATT_GROUPS):
            rows = slice(g * ATT_HPG, (g + 1) * ATT_HPG)
            slot_bias[rows, :] = (_dot(pieces[0][rows], slot_ref[g]) + _dot(pieces[1][rows], slot_ref[g])
                                  + _dot(pieces[2][rows], slot_ref[g]))

    def one_row(x):
        return jnp.where(first, jnp.broadcast_to(x, (8, ATT_HD)), 0.0).astype(BF16)

    scores = {}
    for i in range(seqs):
        for h in range(ATT_HPG):
            for g in range(ATT_GROUPS):
                c0 = g * ATT_GW + h * ATT_HD
                kk = jnp.where(is_new, qkv_ref[i, :, ATT_W + c0:ATT_W + c0 + ATT_HD], caches[g][0][i, :, h, :])
                scores[i, g, h] = _dot_nt(one_row(qkv_ref[i, :, c0:c0 + ATT_HD]), kk.astype(BF16))[0:1, :]
    for i in range(seqs):
        for h in range(ATT_HPG):
            o_g, lse_g = [], []
            for g in range(ATT_GROUPS):
                c0 = g * ATT_GW + h * ATT_HD
                vv = jnp.where(is_new, qkv_ref[i, :, 2 * ATT_W + c0:2 * ATT_W + c0 + ATT_HD],
                               caches[g][1][i, :, h, :])
                gh = g * ATT_HPG + h
                s = scores[i, g, h] * scale + slot_bias[gh:gh + 1, :]
                m = jnp.max(s, axis=-1, keepdims=True)
                pr = jnp.exp(s - m)
                den = jnp.sum(pr, axis=-1, keepdims=True)
                o_g.append(_dot(one_row(pr), vv.astype(BF16))[0:1, :] / den)
                lse_g.append(m + jnp.log(den))
            mx = jnp.maximum(jnp.maximum(lse_g[0], lse_g[1]), lse_g[2])
            e = [jnp.exp(l - mx) for l in lse_g]
            tot = e[0] + e[1] + e[2]
            att_ref[i, :, h * ATT_HD:(h + 1) * ATT_HD] = ((e[0] / tot) * o_g[0] + (e[1] / tot) * o_g[1]
                                                          + (e[2] / tot) * o_g[2])


def _decode(q, k, v, g, gn, state, qkv, rel_bias_t, caches, cast):
    batch = q.shape[0]
    gam = jnp.asarray(np.ascontiguousarray(np.broadcast_to(
        np.repeat(np.exp(_ret_log_decay()), RET_DK).reshape(RET_PAIRS, 2 * RET_DK, 1),
        (RET_PAIRS, 2 * RET_DK, RET_DV))), F32)
    vec = lambda a: a.reshape(batch, 1, a.shape[1])
    seqs = DECODE_SEQS
    assert batch % seqs == 0
    vspec = lambda width: pl.BlockSpec((seqs, 1, width), lambda b: (b, 0, 0))
    const = lambda a: pl.BlockSpec(a.shape, lambda b: (0,) * a.ndim)
    st_spec = pl.BlockSpec((seqs, RET_PAIRS, 2 * RET_DK, RET_DV), lambda b: (b, 0, 0, 0))
    cache_in, cache_specs = [], []
    for gi in range(ATT_GROUPS):
        dil = ATT_DILATIONS[gi]
        for c in caches[2 * gi:2 * gi + 2]:
            cache_in.append(c.reshape(batch, ATT_NK, dil, ATT_HPG, ATT_HD))
            cache_specs.append(pl.BlockSpec((seqs, ATT_NK, None, ATT_HPG, ATT_HD), lambda b: (b, 0, 0, 0, 0)))
    st_pairs = state.reshape(batch, RET_PAIRS, 2 * RET_DK, RET_DV)
    slot_sel = _slot_select()
    steps = batch // seqs
    slab = lambda w: pl.BlockSpec((w.shape[0] // steps, w.shape[1]), lambda b: (b, 0))
    assert all(w.shape[0] % (16 * steps) == 0 for w in cast)
    res = pl.pallas_call(
        functools.partial(_decode_body, seqs=seqs, n_cast=len(cast)),
        grid=(steps,),
        in_specs=[vspec(RET_QK_W), vspec(RET_QK_W), vspec(RET_V_W), vspec(RET_V_W), const(gn), st_spec, const(gam),
                  vspec(3 * ATT_W), const(rel_bias_t), const(slot_sel)] + cache_specs + [slab(w) for w in cast],
        out_specs=[vspec(RET_V_W), vspec(ATT_GW), st_spec] + [slab(w) for w in cast],
        out_shape=[jax.ShapeDtypeStruct((batch, 1, RET_V_W), F32),
                   jax.ShapeDtypeStruct((batch, 1, ATT_GW), F32),
                   jax.ShapeDtypeStruct(st_pairs.shape, F32)]
        + [jax.ShapeDtypeStruct(w.shape, BF16) for w in cast],
        scratch_shapes=[pltpu.VMEM((rel_bias_t.shape[0], ATT_NK), F32)],
        compiler_params=_params(1),
        name="decode",
    )(vec(q), vec(k), vec(v), vec(g), gn, st_pairs, gam, vec(qkv), rel_bias_t, slot_sel, *cache_in, *cast)
    ret, att, nst = res[:3]
    return ret.reshape(batch, RET_V_W), att.reshape(batch, ATT_GW), nst.reshape(state.shape), tuple(res[3:])


def kernel(x_prompt, x_sample, state_ret, cache_k_w128, cache_v_w128, cache_k_w512, cache_v_w512,
           cache_k_w2048, cache_v_w2048, p_prompt, p_sample, ln1_g, w_in, ret_gn_g, w_ret_br, w_att_br,
           w_out, ln2_g, w_up, w_down, w_ple, w_ple_gate, rel_bias, lnf_g):
    depth, _, n_in = w_in.shape
    assert depth == 1 and n_in == N_IN
    batch, seq, _ = x_prompt.shape
    dec_batch, dec_seq, _ = x_sample.shape
    assert dec_seq == 1
    l = 0
    ln1 = ln1_g[l][None, :]
    ln2 = ln2_g[l][None, :]
    lnf = lnf_g[None, :]
    gn = ret_gn_g[l][None, :]
    w_ret, *w_gates = _cast_ret_weights(w_in[l])
    inv_row = _rope_inv_row()

    caches = (cache_k_w128[l], cache_v_w128[l], cache_k_w512[l], cache_v_w512[l],
              cache_k_w2048[l], cache_v_w2048[l])

    xs = x_sample.reshape(dec_batch, D_MODEL)
    xp = x_prompt.reshape(batch * seq, D_MODEL)
    (rq, rk, rv, rg, gr, ga, sq, sk, sv, sg, sgr, sga), w_qkv = _inproj_ret(
        xp, xs, ln1, w_ret, w_gates, w_in[l], inv_row, TM_INPROJ, seq, PAST_LEN)
    att_o = _inproj_att(xp, xs, ln1, w_qkv, TM_INPROJ, seq)
    aqs, aks, avs = att_o[0:3], att_o[3:6], att_o[6:9]
    kfull, vfull = att_o[9:12], att_o[12:15]
    s_qkv = att_o[15]

    rel_bias_t = rel_bias.T
    s_ret, s_attn, new_st, (wrb, wab, wo, wpl, wpg) = _decode(
        sq, sk, sv, sg, gn, state_ret[l], s_qkv, rel_bias_t, caches,
        (w_ret_br[l], w_att_br[l], w_out[l], w_ple[l], w_ple_gate[l]))
    ple_s = p_sample[l].reshape(dec_batch, D_PLE)

    ret_out, st_p, (wu, wd) = _retention_prompt(rq, rk, rv, rg, gn, batch, seq, TS_RETENTION, (w_up[l], w_down[l]))
    tail_w = (wrb, wab, wo, wu, wd, wpl, wpg, ln2, lnf)
    outs, lses = [], []
    band_onehot, band_neg = _band_select()
    for g in range(ATT_GROUPS):
        o, lse = _attention_prompt(aqs[g], aks[g], avs[g], rel_bias_t, band_onehot, band_neg, g)
        outs.append(o)
        lses.append(lse)
    ple_p = p_prompt[l].reshape(batch * seq, D_PLE)
    tm_tail = batch * seq // dec_batch
    y_p, y_s, new_caches = _tail_shift([ret_out] + outs + lses + [gr, ga, xp, ple_p],
                                       [s_ret, s_attn, sgr, sga, xs, ple_s], tail_w, tm_tail, s_qkv, caches)
    y_prompt = y_p.reshape(batch, seq, D_MODEL)
    y_sample = y_s.reshape(dec_batch, 1, D_MODEL)
    new_state_p = st_p[None]
    kv_p = []
    for g in range(ATT_GROUPS):
        shape = (1, batch, min(ATT_WINDOWS[g], seq), ATT_HPG, ATT_HD)
        kv_p.append(kfull[g].reshape(shape))
        kv_p.append(vfull[g].reshape(shape))
    kv_s = [c[None] for c in new_caches]

    return (y_prompt, y_sample, new_state_p, *kv_p, new_st[None], *kv_s)
```

```python
import functools
import math

import jax
import jax.numpy as jnp
import numpy as np
from jax import lax
from jax.experimental import pallas as pl
from jax.experimental.pallas import tpu as pltpu

F32 = jnp.float32
BF16 = jnp.bfloat16

D_MODEL = 1024
RET_HEADS = 8
RET_DK = 64
RET_DV = 128
RET_PAIRS = RET_HEADS // 2
RET_CHUNK = 128
ROPE_BASE = 10000.0
ATT_WINDOWS = (128, 512, 2048)
ATT_DILATIONS = (1, 4, 16)
ATT_GROUPS = 3
ATT_HPG = 4
ATT_HD = 128
ATT_NK = 128
ATT_GW = ATT_HPG * ATT_HD
REL_BUCKETS = 32
REL_MAX_DIST = 2048
D_FF = 4 * D_MODEL
D_PLE = 256
NORM_EPS = 1e-6
RET_QK_W = RET_HEADS * RET_DK
RET_V_W = RET_HEADS * RET_DV
ATT_W = ATT_GROUPS * ATT_GW
COL_ATT = 2 * RET_QK_W + 2 * RET_V_W
COL_GATE = COL_ATT + 3 * ATT_W
N_IN = COL_GATE + 2 * D_MODEL
PAST_LEN = 16384

VMEM_LIMIT_V7X = 56 * 1024 * 1024
VMEM_LIMIT_TAIL_V7X = 62 * 1024 * 1024
LANES = 128
TM_INPROJ = 512
TS_RETENTION = 512
ATTN_UNROLL = 8
PERM_ROWS = 256
SINGLE_BLOCK_GROUP = 4
DECODE_SEQS = 4
CAST_STEPS = 8


def _dot(a, b):
    return jnp.dot(a, b, preferred_element_type=F32)


def _dot_nt(a, b):
    return lax.dot_general(a, b, (((1,), (1,)), ((), ())), preferred_element_type=F32)


def _dot_tn(a, b):
    return lax.dot_general(a, b, (((0,), (0,)), ((), ())), preferred_element_type=F32)


def _rms(x, g):
    return x * lax.rsqrt(jnp.mean(x * x, axis=-1, keepdims=True) + NORM_EPS) * g


def _sigmoid(x):
    return 1.0 / (1.0 + jnp.exp(-x))


def _resident(shape):
    return pl.BlockSpec(shape, lambda *_: (0,) * len(shape), pipeline_mode=pl.Buffered(1))


def _params(n_axes):
    return pltpu.CompilerParams(dimension_semantics=("arbitrary",) * n_axes,
                                vmem_limit_bytes=VMEM_LIMIT_V7X)


def _rope_inv_row():
    half = RET_DK // 2
    inv = ROPE_BASE ** (-jnp.arange(half, dtype=F32) / half)
    return jnp.tile(inv, LANES // half)[None, :]


def _ret_log_decay():
    return np.log1p(-np.exp2(-5.0 - np.arange(RET_HEADS, dtype=np.float32))).astype(np.float32)


def _ret_tables():
    c = RET_CHUNK
    lg = _ret_log_decay()
    i = np.arange(c, dtype=np.float32)
    diff = i[:, None] - i[None, :]
    dmask = np.where(diff[None] >= 0, np.exp(np.maximum(diff, 0.0)[None] * lg[:, None, None]), 0.0)
    q_decay = np.exp((i + 1.0)[:, None] * lg[None, :])
    k_decay = np.exp((c - 1.0 - i)[:, None] * lg[None, :])
    qdec = np.broadcast_to(q_decay.T[:, :, None], (RET_HEADS, c, RET_DV))
    kdec = np.repeat(k_decay, RET_DK, axis=1).reshape(c, RET_PAIRS, 2 * RET_DK).transpose(1, 0, 2)
    gc = np.repeat(np.exp(c * lg), RET_DV).reshape(RET_PAIRS, 1, 2 * RET_DV)
    return tuple(jnp.asarray(np.ascontiguousarray(t), F32) for t in (dmask, qdec, kdec, gc))


def _rel_buckets():
    max_exact = REL_BUCKETS // 2
    out = []
    for dil in ATT_DILATIONS:
        d = np.arange(ATT_NK, dtype=np.int32) * dil
        log_ratio = (np.log(np.maximum(d, 1).astype(np.float32) / np.float32(max_exact))
                     / np.float32(math.log(REL_MAX_DIST / max_exact)))
        large = max_exact + (log_ratio * np.float32(REL_BUCKETS - max_exact)).astype(np.int32)
        out.append(np.where(d < max_exact, d, np.minimum(large, REL_BUCKETS - 1)))
    return np.stack(out)


def _band_select():
    nk = ATT_NK
    buckets = _rel_buckets()
    onehot = np.zeros((ATT_GROUPS, REL_BUCKETS, 3 * nk), np.float32)
    for g in range(ATT_GROUPS):
        for k in range(nk, 2 * nk):
            onehot[g, buckets[g, 2 * nk - 1 - k], k] = 1.0
    mask = np.full((1, 3 * nk), -np.inf, np.float32)
    mask[0, nk:2 * nk] = 0.0
    return jnp.asarray(onehot, BF16), jnp.asarray(mask, F32)


def _slot_order(dil):
    slots = np.arange(ATT_NK)
    return np.concatenate([slots[0::2], slots[1::2]]) if dil == 1 else slots


def _slot_select():
    buckets = _rel_buckets()
    onehot = np.zeros((ATT_GROUPS, REL_BUCKETS, ATT_NK), np.float32)
    for g in range(ATT_GROUPS):
        for col, slot in enumerate(_slot_order(ATT_DILATIONS[g])):
            onehot[g, buckets[g, 0 if slot == 0 else ATT_NK - slot], col] = 1.0
    return jnp.asarray(onehot, BF16)


def _split3(x):
    hi = x.astype(BF16)
    rem = x - hi.astype(F32)
    mid = rem.astype(BF16)
    lo = (rem - mid.astype(F32)).astype(BF16)
    return hi, mid, lo


def _cast_body(*refs):
    n = len(refs) // 2
    for src, dst in zip(refs[:n], refs[n:]):
        dst[...] = src[...].astype(dst.dtype)


def _cast_ret_weights(w_in):
    rows = D_MODEL // CAST_STEPS
    half = D_MODEL // 2
    assert COL_GATE % half == 0
    widths_blocks = [(COL_ATT, 0)] + [(half, COL_GATE // half + j) for j in range(4)]
    window = lambda width, blk: pl.BlockSpec((rows, width), lambda i: (i, blk))
    return pl.pallas_call(
        _cast_body,
        grid=(CAST_STEPS,),
        in_specs=[window(wd, blk) for wd, blk in widths_blocks],
        out_specs=[window(wd, 0) for wd, _ in widths_blocks],
        out_shape=[jax.ShapeDtypeStruct((D_MODEL, wd), BF16) for wd, _ in widths_blocks],
        compiler_params=_params(1),
        name="cast_ret_weights",
    )(*([w_in] * len(widths_blocks)))


def _inproj_ret_body(x_ref, xs_ref, ln_ref, w_ref, wg0_ref, wg1_ref, wg2_ref, wg3_ref, inv_ref, *refs,
                     tm, tiles, n_steps, sample_pos):
    cast_in = refs[:3]
    outs, sample_outs = refs[3:9], refs[9:15]
    cast_out = refs[15:18]
    cos_s, sin_s = refs[18:]
    for src, dst in zip(cast_in, cast_out):
        dst[...] = src[...].astype(dst.dtype)
    i = pl.program_id(0)

    def first_half(rows):
        lane = lax.broadcasted_iota(jnp.int32, (rows, LANES), 1)
        return (lane % RET_DK) < (RET_DK // 2)

    def tables(pos):
        ang = pos.astype(F32) * inv_ref[...]
        sin = jnp.sin(ang)
        return jnp.cos(ang), jnp.where(first_half(pos.shape[0]), -sin, sin)

    def project(x, cos, sin, q_ref, k_ref, v_ref, g_ref, gr_ref, ga_ref):
        h = _rms(x, ln_ref[...]).astype(BF16)
        qk = _dot(h, w_ref[:, 0:2 * RET_QK_W])
        n_q = RET_QK_W // LANES
        fh = first_half(x.shape[0])
        for c in range(2 * n_q):
            xc = qk[:, c * LANES:(c + 1) * LANES]
            swapped = jnp.where(fh, pltpu.roll(xc, LANES - RET_DK // 2, 1), pltpu.roll(xc, RET_DK // 2, 1))
            r = xc * cos + swapped * sin
            if c < n_q:
                q_ref[:, c * LANES:(c + 1) * LANES] = r.astype(q_ref.dtype)
            else:
                k_ref[:, (c - n_q) * LANES:(c - n_q + 1) * LANES] = (r * (RET_DK ** -0.5)).astype(k_ref.dtype)
        o = 2 * RET_QK_W
        v_ref[...] = _dot(h, w_ref[:, o:o + RET_V_W]).astype(v_ref.dtype)
        o += RET_V_W
        g_ref[...] = _dot(h, w_ref[:, o:o + RET_V_W]).astype(g_ref.dtype)
        half = D_MODEL // 2
        gr_ref[:, :half] = _dot(h, wg0_ref[...]).astype(gr_ref.dtype)
        gr_ref[:, half:] = _dot(h, wg1_ref[...]).astype(gr_ref.dtype)
        ga_ref[:, :half] = _dot(h, wg2_ref[...]).astype(ga_ref.dtype)
        ga_ref[:, half:] = _dot(h, wg3_ref[...]).astype(ga_ref.dtype)

    trow = pl.ds(pl.multiple_of(lax.rem(i, tiles) * tm, tm), tm)

    @pl.when(i < tiles)
    def _():
        cos_s[trow, :], sin_s[trow, :] = tables(i * tm + lax.broadcasted_iota(jnp.int32, (tm, LANES), 0))
    project(x_ref[...], cos_s[trow, :], sin_s[trow, :], *outs)

    @pl.when(i == n_steps - 1)
    def _():
        ns = xs_ref.shape[0]
        project(xs_ref[...], *tables(jnp.full((ns, LANES), sample_pos, jnp.int32)), *sample_outs)


def _inproj_ret(x2d, xs2d, ln, w_ret, w_gates, w_in, inv_row, tm, seq, sample_pos):
    n = x2d.shape[0]
    ns = xs2d.shape[0]
    tiles = seq // tm
    n_steps = n // tm
    row = lambda width: pl.BlockSpec((tm, width), lambda i: (i, 0))
    widths = (RET_QK_W, RET_QK_W, RET_V_W, RET_V_W, D_MODEL, D_MODEL)
    slab = D_MODEL // n_steps
    assert slab % 16 == 0 and COL_ATT % ATT_W == 0
    res = pl.pallas_call(
        functools.partial(_inproj_ret_body, tm=tm, tiles=tiles, n_steps=n_steps, sample_pos=sample_pos),
        grid=(n_steps,),
        in_specs=[row(D_MODEL), _resident(xs2d.shape), _resident((1, D_MODEL)), _resident(w_ret.shape)]
        + [_resident(w.shape) for w in w_gates] + [_resident((1, LANES))]
        + [pl.BlockSpec((slab, ATT_W), lambda i, kind=kind: (i, COL_ATT // ATT_W + kind)) for kind in range(3)],
        out_specs=[row(wd) for wd in widths] + [pl.BlockSpec((ns, wd), lambda i: (0, 0)) for wd in widths]
        + [pl.BlockSpec((slab, ATT_W), lambda i: (i, 0)) for _ in range(3)],
        out_shape=[jax.ShapeDtypeStruct((n, wd), BF16) for wd in widths]
        + [jax.ShapeDtypeStruct((ns, wd), F32) for wd in widths]
        + [jax.ShapeDtypeStruct((D_MODEL, ATT_W), BF16) for _ in range(3)],
        scratch_shapes=[pltpu.VMEM((seq, LANES), F32), pltpu.VMEM((seq, LANES), F32)],
        compiler_params=_params(1),
        name="inproj_ret",
    )(x2d, xs2d, ln, w_ret, *w_gates, inv_row, w_in, w_in, w_in)
    return res[:12], res[12:]


def _inproj_att_body(x_ref, xs_ref, ln_ref, wq_ref, wk_ref, wv_ref, *refs, tm, keeps, seq, n_steps):
    lowp = refs[:3 * ATT_GROUPS]
    full = refs[3 * ATT_GROUPS:5 * ATT_GROUPS]
    sample_ref = refs[5 * ATT_GROUPS]

    @pl.when(pl.program_id(0) == n_steps - 1)
    def _():
        hs = _rms(xs_ref[...], ln_ref[...]).astype(BF16)
        for kind, w_ref in enumerate((wq_ref, wk_ref, wv_ref)):
            for g in range(ATT_GROUPS):
                c = kind * ATT_GROUPS + g
                sample_ref[:, c * ATT_GW:(c + 1) * ATT_GW] = _dot(hs, w_ref[:, g * ATT_GW:(g + 1) * ATT_GW])

    h = _rms(x_ref[...], ln_ref[...]).astype(BF16)
    for g in reversed(range(ATT_GROUPS)):
        for kind in reversed(range(3)):
            dil = ATT_DILATIONS[g]
            dst = lowp[kind * ATT_GROUPS + g]
            r = _dot(h, (wq_ref, wk_ref, wv_ref)[kind][:, g * ATT_GW:(g + 1) * ATT_GW])
            if dil == 1:
                dst[0] = r.astype(dst.dtype)
            else:
                dst[...] = jnp.swapaxes(r.reshape(tm // dil, dil, ATT_GW), 0, 1).astype(dst.dtype)
            if kind > 0:
                cache = full[(kind - 1) * ATT_GROUPS + g]
                rows = tm if keeps[g] == seq else keeps[g]
                for hh in range(ATT_HPG):
                    cache[pl.ds(hh, rows, stride=ATT_HPG), :] = r[tm - rows:, hh * ATT_HD:(hh + 1) * ATT_HD]


def _inproj_att(x2d, xs2d, ln, w_qkv, tm, seq):
    n = x2d.shape[0]
    ns = xs2d.shape[0]
    batch = n // seq
    tiles = seq // tm
    keeps = tuple(min(wd, seq) for wd in ATT_WINDOWS)
    assert all(kp <= tm or kp == seq for kp in keeps) and seq % tm == 0
    out_specs, out_shape = [], []
    for _ in range(3):
        for g in range(ATT_GROUPS):
            dil = ATT_DILATIONS[g]
            out_specs.append(pl.BlockSpec((None, dil, tm // dil, ATT_GW), lambda i: (i // tiles, 0, i % tiles, 0)))
            out_shape.append(jax.ShapeDtypeStruct((batch, dil, seq // dil, ATT_GW), BF16))
    for _ in range(2):
        for g in range(ATT_GROUPS):
            if keeps[g] == seq:
                idx = lambda i: (i // tiles, i % tiles, 0)
                rows = tm
            else:
                idx = lambda i: (i // tiles, 0, 0)
                rows = keeps[g]
            out_specs.append(pl.BlockSpec((None, rows * ATT_HPG, ATT_HD), idx))
            out_shape.append(jax.ShapeDtypeStruct((batch, keeps[g] * ATT_HPG, ATT_HD), F32))
    out_specs.append(pl.BlockSpec((ns, 3 * ATT_W), lambda i: (0, 0)))
    out_shape.append(jax.ShapeDtypeStruct((ns, 3 * ATT_W), F32))
    return pl.pallas_call(
        functools.partial(_inproj_att_body, tm=tm, keeps=keeps, seq=seq, n_steps=n // tm),
        grid=(n // tm,),
        in_specs=[pl.BlockSpec((tm, D_MODEL), lambda i: (i, 0)), _resident(xs2d.shape), _resident((1, D_MODEL))]
        + [_resident(w.shape) for w in w_qkv],
        out_specs=out_specs,
        out_shape=out_shape,
        compiler_params=_params(1),
        name="inproj_att",
    )(x2d, xs2d, ln, *w_qkv)


def _gn_swish(o, gate, gn):
    mu = jnp.mean(o, axis=-1, keepdims=True)
    d = o - mu
    var = jnp.mean(d * d, axis=-1, keepdims=True)
    on = d * lax.rsqrt(var + NORM_EPS) * gn
    return gate * _sigmoid(gate) * on


def _retention_body(q_ref, k_ref, v_ref, g_ref, gn_ref, dm_ref, qdec_ref, kdec_ref, gc_ref, *refs,
                    n_chunks, n_cast):
    cast_in = refs[:n_cast]
    out_ref, st_ref = refs[n_cast:n_cast + 2]
    cast_out = refs[n_cast + 2:2 * n_cast + 2]
    state = refs[2 * n_cast + 2]
    for src, dst in zip(cast_in, cast_out):
        dst[...] = src[...].astype(dst.dtype)
    c = RET_CHUNK

    @pl.when(pl.program_id(1) == 0)
    def _():
        state[...] = jnp.zeros_like(state)

    lane = lax.broadcasted_iota(jnp.int32, (c, 2 * RET_DK), 1)
    head0 = lane < RET_DK

    def chunk(ci, carry):
        rows = pl.ds(pl.multiple_of(ci * c, c), c)
        def independent(p):
            q2 = q_ref[rows, p * 2 * RET_DK:(p + 1) * 2 * RET_DK]
            k2 = k_ref[rows, p * 2 * RET_DK:(p + 1) * 2 * RET_DK]
            v2 = v_ref[rows, p * 2 * RET_DV:(p + 1) * 2 * RET_DV]
            pst = state[p]
            pst_lo = pst.astype(BF16)
            zero = jnp.zeros_like(q2)
            parts = []
            for hh in range(2):
                qm = jnp.where(head0 if hh == 0 else jnp.logical_not(head0), q2, zero)
                hc = slice(hh * RET_DV, (hh + 1) * RET_DV)
                parts.append((_dot_nt(qm, k2), _dot(qm, pst_lo[:, hc]), v2[:, hc]))
            kd = (k2.astype(F32) * kdec_ref[p]).astype(BF16)
            state[p] = pst * gc_ref[p] + _dot_tn(kd, v2)
            return parts

        def readout(p, parts):
            for hh, (scores, cross, values) in enumerate(parts):
                h = 2 * p + hh
                o = _dot((scores * dm_ref[h]).astype(BF16), values) + cross * qdec_ref[h]
                gate = g_ref[rows, h * RET_DV:(h + 1) * RET_DV].astype(F32)
                res = _gn_swish(o, gate, gn_ref[:, h * RET_DV:(h + 1) * RET_DV])
                out_ref[rows, h * RET_DV:(h + 1) * RET_DV] = res.astype(out_ref.dtype)

        for p in range(RET_PAIRS):
            readout(p, independent(p))
        return carry

    lax.fori_loop(0, n_chunks, chunk, 0, unroll=True)
    for p in range(RET_PAIRS):
        pst = state[p]
        for hh in range(2):
            st_ref[2 * p + hh] = pst[hh * RET_DK:(hh + 1) * RET_DK, hh * RET_DV:(hh + 1) * RET_DV]


def _retention_prompt(q, k, v, g, gn, batch, seq, ts, cast):
    n = q.shape[0]
    steps = seq // ts
    dmask, qdec, kdec, gc = _ret_tables()
    row = lambda width: pl.BlockSpec((ts, width), lambda b, s: (b * steps + s, 0))
    const = lambda a: pl.BlockSpec(a.shape, lambda b, s: (0,) * a.ndim)
    n_steps = batch * steps
    slab = lambda w: pl.BlockSpec((w.shape[0] // n_steps, w.shape[1]), lambda b, s: (b * steps + s, 0))
    assert all(w.shape[0] % (16 * n_steps) == 0 for w in cast)
    res = pl.pallas_call(
        functools.partial(_retention_body, n_chunks=ts // RET_CHUNK, n_cast=len(cast)),
        grid=(batch, steps),
        in_specs=[row(RET_QK_W), row(RET_QK_W), row(RET_V_W), row(RET_V_W), const(gn),
                  const(dmask), const(qdec), const(kdec), const(gc)] + [slab(w) for w in cast],
        out_specs=[row(RET_V_W),
                   pl.BlockSpec((None, RET_HEADS, RET_DK, RET_DV), lambda b, s: (b, 0, 0, 0))]
        + [slab(w) for w in cast],
        out_shape=[jax.ShapeDtypeStruct((n, RET_V_W), BF16),
                   jax.ShapeDtypeStruct((batch, RET_HEADS, RET_DK, RET_DV), F32)]
        + [jax.ShapeDtypeStruct(w.shape, BF16) for w in cast],
        scratch_shapes=[pltpu.VMEM((RET_PAIRS, 2 * RET_DK, 2 * RET_DV), F32)],
        compiler_params=_params(2),
        name="retention",
    )(q, k, v, g, gn, dmask, qdec, kdec, gc, *cast)
    return res[0], res[1], tuple(res[2:])


def _attn_body(q_ref, k_ref, v_ref, rb_ref, sel_ref, neg_ref, w_ref, o_ref, lse_ref, wb_ref, *scratch,
               group, dil, n_blocks):
    nk = ATT_NK
    wb_ref[...] = w_ref[...].astype(wb_ref.dtype)
    scale = ATT_HD ** -0.5
    lane = lax.broadcasted_iota(jnp.int32, (nk, LANES), 1)
    tabs = scratch[-1]

    @pl.when(pl.program_id(0) == 0)
    def _():
        band = neg_ref[...]
        for piece in _split3(rb_ref[group * ATT_HPG:(group + 1) * ATT_HPG, :]):
            band = band + _dot(piece, sel_ref[...])
        for i in range(nk):
            window = band[:, nk - 1 - i:3 * nk - 1 - i]
            for h in range(ATT_HPG):
                tabs[1, h, i:i + 1, :] = window[h:h + 1, :]
        col = lax.broadcasted_iota(jnp.int32, (nk, 2 * nk), 1)
        for h in range(ATT_HPG):
            tabs[0, h] = jnp.where(col < nk, -jnp.inf, tabs[1, h])
    if dil > 1:
        o_s, l_s = scratch[:2]

    def chain(s_raw, v_ext, sel, h):
        s = s_raw * scale + (tabs[sel, h] if n_blocks > 1 else tabs[1, h, :, nk:])
        m = jnp.max(s, axis=-1, keepdims=True)
        p = jnp.exp(s - m)
        pv = _dot(p.astype(BF16), v_ext)
        den = pv[:, ATT_HD:]
        return (pv[:, :ATT_HD] / den).astype(BF16), m + jnp.log(den)

    def write(rs, rows, outs):
        lse_tile = jnp.zeros((nk, LANES), F32)
        for h, (o, lse) in enumerate(outs):
            hc = slice(h * ATT_HD, (h + 1) * ATT_HD)
            if dil == 1:
                o_ref[rows, hc] = o
            else:
                o_s[rs, rows, hc] = o
            lse_tile = jnp.where(lane == h, lse, lse_tile)
        if dil == 1:
            lse_ref[rows, :] = lse_tile
        else:
            l_s[rs, rows, :] = lse_tile

    heads = [slice(h * ATT_HD, (h + 1) * ATT_HD) for h in range(ATT_HPG)]

    def block(rs, n, n_prev, sel):
        rows = pl.ds(pl.multiple_of(n * nk, nk), nk)
        prev_rows = pl.ds(pl.multiple_of(n_prev * nk, nk), nk)
        q = q_ref[rs, rows, :]
        k_all = jnp.concatenate([k_ref[rs, prev_rows, :], k_ref[rs, rows, :]], axis=0)
        v_all = jnp.concatenate([v_ref[rs, prev_rows, :], v_ref[rs, rows, :]], axis=0)
        ones = jnp.ones((2 * nk, ATT_HD), BF16)
        outs = []
        for h, hc in enumerate(heads):
            v_ext = jnp.concatenate([v_all[:, hc], ones], axis=1)
            outs.append(chain(_dot_nt(q[:, hc], k_all[:, hc]), v_ext, sel, h))
        write(rs, rows, outs)

    def single_blocks(streams):
        ones = jnp.ones((nk, ATT_HD), BF16)
        scores = [[_dot_nt(q_ref[rs, :, hc], k_ref[rs, :, hc]) for hc in heads] for rs in streams]
        for i, rs in enumerate(streams):
            outs = [chain(scores[i][h], jnp.concatenate([v_ref[rs, :, hc], ones], axis=1), 0, h)
                    for h, hc in enumerate(heads)]
            write(rs, slice(None), outs)

    if n_blocks == 1:
        for r0 in range(0, dil, SINGLE_BLOCK_GROUP):
            single_blocks(range(r0, min(r0 + SINGLE_BLOCK_GROUP, dil)))
    else:
        for rs in range(dil):
            def loop(n, carry, rs=rs):
                block(rs, n, jnp.maximum(n - 1, 0), jnp.minimum(n, 1))
                return carry
            lax.fori_loop(0, n_blocks, loop, 0, unroll=ATTN_UNROLL)

    if dil > 1:
        rows = o_ref.shape[0]
        for j in range(rows // PERM_ROWS):
            src = slice(j * (PERM_ROWS // dil), (j + 1) * (PERM_ROWS // dil))
            dst = slice(j * PERM_ROWS, (j + 1) * PERM_ROWS)
            o_ref[dst, :] = jnp.swapaxes(o_s[:, src, :], 0, 1).reshape(PERM_ROWS, ATT_GW).astype(BF16)
            lse_ref[dst, :] = jnp.swapaxes(l_s[:, src, :], 0, 1).reshape(PERM_ROWS, LANES)


def _attention_prompt(aq, ak, av, rel_bias_t, onehot, neg, g, cast):
    batch, dil, length, _ = aq.shape
    seq = dil * length
    n_blocks = length // ATT_NK
    blk = pl.BlockSpec((None, dil, length, ATT_GW), lambda b: (b, 0, 0, 0))
    assert cast.shape[0] % (16 * batch) == 0
    slab = pl.BlockSpec((cast.shape[0] // batch, cast.shape[1]), lambda b: (b, 0))
    scratch = [pltpu.VMEM((dil, length, ATT_GW), BF16), pltpu.VMEM((dil, length, LANES), F32)] if dil > 1 else []
    scratch.append(pltpu.VMEM((2, ATT_HPG, ATT_NK, 2 * ATT_NK), F32))
    o, lse, cast_bf16 = pl.pallas_call(
        functools.partial(_attn_body, group=g, dil=dil, n_blocks=n_blocks),
        grid=(batch,),
        in_specs=[blk, blk, blk, pl.BlockSpec(rel_bias_t.shape, lambda b: (0, 0)),
                  pl.BlockSpec((None, REL_BUCKETS, 3 * ATT_NK), lambda b: (g, 0, 0)),
                  pl.BlockSpec((1, 3 * ATT_NK), lambda b: (0, 0)), slab],
        out_specs=[pl.BlockSpec((None, seq, ATT_GW), lambda b: (b, 0, 0)),
                   pl.BlockSpec((None, seq, LANES), lambda b: (b, 0, 0)), slab],
        out_shape=[jax.ShapeDtypeStruct((batch, seq, ATT_GW), BF16),
                   jax.ShapeDtypeStruct((batch, seq, LANES), F32),
                   jax.ShapeDtypeStruct(cast.shape, BF16)],
        scratch_shapes=scratch,
        compiler_params=_params(1),
        name="attention_g%d" % g,
    )(aq, ak, av, rel_bias_t, onehot, neg, cast)
    return o.reshape(batch * seq, ATT_GW), lse.reshape(batch * seq, LANES), cast_bf16


def _tail_math(act_refs, w_refs, y_ref, combine, middle=None):
    wrb_ref, wab_ref, wo_ref, wu_ref, wd_ref, wpl_ref, wpg_ref, ln2_ref, lnf_ref = w_refs
    if combine:
        ret_ref, o0_ref, o1_ref, o2_ref, l0_ref, l1_ref, l2_ref, gr_ref, ga_ref, x_ref, ple_ref = act_refs
        lses = [l0_ref[...], l1_ref[...], l2_ref[...]]
        outs = [o0_ref, o1_ref, o2_ref]
        parts = []
        for h in range(ATT_HPG):
            lh = [l[:, h:h + 1] for l in lses]
            mx = jnp.maximum(jnp.maximum(lh[0], lh[1]), lh[2])
            e = [jnp.exp(l - mx) for l in lh]
            tot = e[0] + e[1] + e[2]
            acc = None
            for g in range(ATT_GROUPS):
                term = (e[g] / tot) * outs[g][:, h * ATT_HD:(h + 1) * ATT_HD].astype(F32)
                acc = term if acc is None else acc + term
            parts.append(acc)
        att = jnp.concatenate(parts, axis=1).astype(BF16)
    else:
        ret_ref, att_ref, gr_ref, ga_ref, x_ref, ple_ref = act_refs
        att = att_ref[...].astype(BF16)
    a = _dot(ret_ref[...].astype(BF16), wrb_ref[...])
    b = _dot(att, wab_ref[...])
    mixed = _sigmoid(gr_ref[...].astype(F32)) * a + _sigmoid(ga_ref[...].astype(F32)) * b
    x1 = x_ref[...] + _dot(mixed.astype(BF16), wo_ref[...])
    h2 = _rms(x1, ln2_ref[...]).astype(BF16)
    ff_chunk = D_MODEL
    acc = None
    for c in range(D_FF // ff_chunk):
        u = _dot(h2, wu_ref[:, c * ff_chunk:(c + 1) * ff_chunk])
        r = jnp.maximum(u, 0.0)
        t = _dot((r * r).astype(BF16), wd_ref[c * ff_chunk:(c + 1) * ff_chunk, :])
        acc = t if acc is None else acc + t
        if middle is not None and c == D_FF // ff_chunk // 2 - 1:
            middle()
    x2 = x1 + acc
    gate = _sigmoid(_dot(x2.astype(BF16), wpg_ref[...]))
    x3 = x2 + gate * _dot(ple_ref[...].astype(BF16), wpl_ref[...])
    y_ref[...] = _rms(x3, lnf_ref[...])


def _tail_shift_body(*refs, n_act, n_sample, n_w, n_steps):
    nc = 2 * ATT_GROUPS
    acts = refs[:n_act]
    sample_acts = refs[n_act:n_act + n_sample]
    base = n_act + n_sample
    ws = refs[base:base + n_w]
    new_ref = refs[base + n_w]
    base += n_w + 1
    old = refs[base:base + nc]
    y_ref, ys_ref = refs[base + nc:base + nc + 2]
    base += nc + 2
    out = refs[base:base + nc]
    stage = refs[base + nc:base + 2 * nc]
    sem_in, sem_out, sem_row = refs[base + 2 * nc:]
    s = pl.program_id(0)
    keys = [2 * g for g in range(ATT_GROUPS)]
    values = [2 * g + 1 for g in range(ATT_GROUPS)]

    def copy_in(i, seq):
        width = old[i].shape[1]
        return pltpu.make_async_copy(old[i].at[seq, pl.ds(1, width - 1)], stage[i], sem_in.at[i])

    def copy_out(i, seq):
        width = old[i].shape[1]
        return pltpu.make_async_copy(stage[i], out[i].at[seq, pl.ds(0, width - 1)], sem_out.at[i])

    def copy_row(i):
        g, kind = divmod(i, 2)
        return pltpu.make_async_copy(new_ref.at[s, kind + 1, g], out[i].at[s, old[i].shape[1] - 1], sem_row.at[i])

    @pl.when(s == 0)
    def _():
        for i in keys:
            copy_in(i, 0).start()
    for i in keys:
        copy_in(i, s).wait()
    for i in keys:
        copy_out(i, s).start()

    @pl.when(s > 0)
    def _():
        for i in values:
            copy_out(i, s - 1).wait()
    for i in values:
        copy_in(i, s).start()
    for i in range(nc):
        copy_row(i).start()

    def middle():
        for i in values:
            copy_in(i, s).wait()
        for i in values:
            copy_out(i, s).start()
        for i in keys:
            copy_out(i, s).wait()

        @pl.when(s < n_steps - 1)
        def _():
            for i in keys:
                copy_in(i, s + 1).start()

    _tail_math(acts, ws, y_ref, True, middle)

    @pl.when(s == n_steps - 1)
    def _():
        _tail_math(sample_acts, ws, ys_ref, False)

    for i in range(nc):
        copy_row(i).wait()

    @pl.when(s == n_steps - 1)
    def _():
        for i in values:
            copy_out(i, s).wait()


def _tail_shift(acts, sample_acts, weights, tm, new_qkv, caches):
    n = acts[0].shape[0]
    n_steps = n // tm
    assert n_steps == new_qkv.shape[0]
    row = lambda a: pl.BlockSpec((tm, a.shape[1]), lambda i: (i, 0))
    anyspec = pl.BlockSpec(memory_space=pl.ANY)
    nc = len(caches)
    new_rows = new_qkv.reshape(new_qkv.shape[0], 3, ATT_GROUPS, ATT_HPG, ATT_HD)
    ns = sample_acts[0].shape[0]
    res = pl.pallas_call(
        functools.partial(_tail_shift_body, n_act=len(acts), n_sample=len(sample_acts), n_w=len(weights),
                          n_steps=n_steps),
        grid=(n_steps,),
        in_specs=[row(a) for a in acts] + [_resident(a.shape) for a in sample_acts]
        + [_resident(w.shape) for w in weights] + [anyspec] * (nc + 1),
        out_specs=[pl.BlockSpec((tm, D_MODEL), lambda i: (i, 0)), pl.BlockSpec((ns, D_MODEL), lambda i: (0, 0))]
        + [anyspec] * nc,
        out_shape=[jax.ShapeDtypeStruct((n, D_MODEL), F32), jax.ShapeDtypeStruct((ns, D_MODEL), F32)]
        + [jax.ShapeDtypeStruct(c.shape, c.dtype) for c in caches],
        scratch_shapes=[pltpu.VMEM((c.shape[1] - 1, ATT_HPG, ATT_HD), F32) for c in caches]
        + [pltpu.SemaphoreType.DMA((nc,)), pltpu.SemaphoreType.DMA((nc,)), pltpu.SemaphoreType.DMA((nc,))],
        compiler_params=pltpu.CompilerParams(dimension_semantics=("arbitrary",),
                                             vmem_limit_bytes=VMEM_LIMIT_TAIL_V7X),
        name="tail",
    )(*acts, *sample_acts, *weights, new_rows, *caches)
    return res[0], res[1], res[2:]


def _decode_body(q_ref, k_ref, v_ref, g_ref, gn_ref, st_ref, gam_ref,
                 qkv_ref, rb_ref, slot_ref,
                 ck0_ref, cv0_ref, ck1_ref, cv1_ref, ck2_ref, cv2_ref, *refs, seqs, n_cast):
    cast_in = refs[:n_cast]
    ret_ref, att_ref, nst_ref = refs[n_cast:n_cast + 3]
    cast_out = refs[n_cast + 3:2 * n_cast + 3]
    slot_bias = refs[2 * n_cast + 3]
    for src, dst in zip(cast_in, cast_out):
        dst[...] = src[...].astype(dst.dtype)
    sub = lax.broadcasted_iota(jnp.int32, (8, 2 * RET_DK), 0)
    lane = lax.broadcasted_iota(jnp.int32, (8, 2 * RET_DK), 1)
    row0 = sub == 0
    srow = lax.broadcasted_iota(jnp.int32, (2 * RET_DK, RET_DV), 0)
    for i in range(seqs):
        for p in range(RET_PAIRS):
            pc = slice(p * 2 * RET_DK, (p + 1) * 2 * RET_DK)
            q2 = jnp.where(row0, jnp.broadcast_to(q_ref[i, :, pc], (8, 2 * RET_DK)), 0.0)
            k2 = jnp.where(row0, jnp.broadcast_to(k_ref[i, :, pc], (8, 2 * RET_DK)), 0.0)
            pst = st_ref[i, p]
            gam = gam_ref[p]
            outer = []
            for hh in range(2):
                h = 2 * p + hh
                hc = slice(h * RET_DV, (h + 1) * RET_DV)
                hsel = (lane < RET_DK) if hh == 0 else (lane >= RET_DK)
                qm = jnp.where(hsel, q2, 0.0)
                km = jnp.where(hsel, k2, 0.0)
                vh = v_ref[i, :, hc]
                v8 = jnp.where(row0[:, :RET_DV], jnp.broadcast_to(vh, (8, RET_DV)), 0.0)
                cross = _dot(qm.astype(BF16), (pst * gam).astype(BF16))[0:1, :]
                qk = jnp.sum(qm[0:1, :] * km[0:1, :], axis=-1, keepdims=True)
                o = cross + qk * vh
                ret_ref[i, :, hc] = _gn_swish(o, g_ref[i, :, hc], gn_ref[:, hc])
                outer.append(_dot_tn(k2.astype(BF16), v8.astype(BF16)))
            nst_ref[i, p] = pst * gam + jnp.where(srow < RET_DK, outer[0], outer[1])
    scale = ATT_HD ** -0.5
    caches = ((ck0_ref, cv0_ref), (ck1_ref, cv1_ref), (ck2_ref, cv2_ref))
    slot = lax.broadcasted_iota(jnp.int32, (ATT_NK, ATT_HD), 0)
    is_new = slot == 0
    first = row0[:, :ATT_HD]

    @pl.when(pl.program_id(0) == 0)
    def _():
        pieces = _split3(rb_ref[...])
        for g in range(ATT_GROUPS):
            rows = slice(g * ATT_HPG, (g + 1) * ATT_HPG)
            slot_bias[rows, :] = (_dot(pieces[0][rows], slot_ref[g]) + _dot(pieces[1][rows], slot_ref[g])
                                  + _dot(pieces[2][rows], slot_ref[g]))

    def one_row(x):
        return jnp.where(first, jnp.broadcast_to(x, (8, ATT_HD)), 0.0).astype(BF16)

    def head_tiles(ref, i):
        tiles = jnp.swapaxes(ref[i], 0, 1)
        if ref.shape[1] == ATT_NK:
            return [tiles[h] for h in range(ATT_HPG)]
        return [jnp.concatenate([tiles[h], tiles[ATT_HPG + h]], axis=0) for h in range(ATT_HPG)]

    for i in range(seqs):
        scores = {}
        k_tiles = [head_tiles(caches[g][0], i) for g in range(ATT_GROUPS)]
        v_tiles = [head_tiles(caches[g][1], i) for g in range(ATT_GROUPS)]
        for h in range(ATT_HPG):
            for g in range(ATT_GROUPS):
                c0 = g * ATT_GW + h * ATT_HD
                kk = jnp.where(is_new, qkv_ref[i, :, ATT_W + c0:ATT_W + c0 + ATT_HD], k_tiles[g][h])
                scores[g, h] = _dot_nt(one_row(qkv_ref[i, :, c0:c0 + ATT_HD]), kk.astype(BF16))[0:1, :]
        for h in range(ATT_HPG):
            o_g, lse_g = [], []
            for g in range(ATT_GROUPS):
                c0 = g * ATT_GW + h * ATT_HD
                vv = jnp.where(is_new, qkv_ref[i, :, 2 * ATT_W + c0:2 * ATT_W + c0 + ATT_HD], v_tiles[g][h])
                gh = g * ATT_HPG + h
                s = scores[g, h] * scale + slot_bias[gh:gh + 1, :]
                m = jnp.max(s, axis=-1, keepdims=True)
                pr = jnp.exp(s - m)
                den = jnp.sum(pr, axis=-1, keepdims=True)
                o_g.append(_dot(one_row(pr), vv.astype(BF16))[0:1, :] / den)
                lse_g.append(m + jnp.log(den))
            mx = jnp.maximum(jnp.maximum(lse_g[0], lse_g[1]), lse_g[2])
            e = [jnp.exp(l - mx) for l in lse_g]
            tot = e[0] + e[1] + e[2]
            att_ref[i, :, h * ATT_HD:(h + 1) * ATT_HD] = ((e[0] / tot) * o_g[0] + (e[1] / tot) * o_g[1]
                                                          + (e[2] / tot) * o_g[2])


def _decode(q, k, v, g, gn, state, qkv, rel_bias_t, caches, cast):
    batch = q.shape[0]
    gam = jnp.asarray(np.ascontiguousarray(np.broadcast_to(
        np.repeat(np.exp(_ret_log_decay()), RET_DK).reshape(RET_PAIRS, 2 * RET_DK, 1),
        (RET_PAIRS, 2 * RET_DK, RET_DV))), F32)
    vec = lambda a: a.reshape(batch, 1, a.shape[1])
    seqs = DECODE_SEQS
    assert batch % seqs == 0
    vspec = lambda width: pl.BlockSpec((seqs, 1, width), lambda b: (b, 0, 0))
    const = lambda a: pl.BlockSpec(a.shape, lambda b: (0,) * a.ndim)
    st_spec = pl.BlockSpec((seqs, RET_PAIRS, 2 * RET_DK, RET_DV), lambda b: (b, 0, 0, 0))
    cache_in, cache_specs = [], []
    for gi in range(ATT_GROUPS):
        dil = ATT_DILATIONS[gi]
        for c in caches[2 * gi:2 * gi + 2]:
            tiles = ATT_NK if dil > 1 else ATT_NK // 2
            cache_in.append(c.reshape(batch, tiles, c.shape[1] * ATT_HPG // tiles, ATT_HD))
            cache_specs.append(pl.BlockSpec((seqs, tiles, 2 * ATT_HPG, ATT_HD), lambda b: (b, 0, 0, 0)))
    st_pairs = state.reshape(batch, RET_PAIRS, 2 * RET_DK, RET_DV)
    slot_sel = _slot_select()
    steps = batch // seqs
    slab = lambda w: pl.BlockSpec((w.shape[0] // steps, w.shape[1]), lambda b: (b, 0))
    assert all(w.shape[0] % (16 * steps) == 0 for w in cast)
    res = pl.pallas_call(
        functools.partial(_decode_body, seqs=seqs, n_cast=len(cast)),
        grid=(steps,),
        in_specs=[vspec(RET_QK_W), vspec(RET_QK_W), vspec(RET_V_W), vspec(RET_V_W), const(gn), st_spec, const(gam),
                  vspec(3 * ATT_W), const(rel_bias_t), const(slot_sel)] + cache_specs + [slab(w) for w in cast],
        out_specs=[vspec(RET_V_W), vspec(ATT_GW), st_spec] + [slab(w) for w in cast],
        out_shape=[jax.ShapeDtypeStruct((batch, 1, RET_V_W), F32),
                   jax.ShapeDtypeStruct((batch, 1, ATT_GW), F32),
                   jax.ShapeDtypeStruct(st_pairs.shape, F32)]
        + [jax.ShapeDtypeStruct(w.shape, BF16) for w in cast],
        scratch_shapes=[pltpu.VMEM((rel_bias_t.shape[0], ATT_NK), F32)],
        compiler_params=_params(1),
        name="decode",
    )(vec(q), vec(k), vec(v), vec(g), gn, st_pairs, gam, vec(qkv), rel_bias_t, slot_sel, *cache_in, *cast)
    ret, att, nst = res[:3]
    return ret.reshape(batch, RET_V_W), att.reshape(batch, ATT_GW), nst.reshape(state.shape), tuple(res[3:])


def kernel(x_prompt, x_sample, state_ret, cache_k_w128, cache_v_w128, cache_k_w512, cache_v_w512,
           cache_k_w2048, cache_v_w2048, p_prompt, p_sample, ln1_g, w_in, ret_gn_g, w_ret_br, w_att_br,
           w_out, ln2_g, w_up, w_down, w_ple, w_ple_gate, rel_bias, lnf_g):
    depth, _, n_in = w_in.shape
    assert depth == 1 and n_in == N_IN
    batch, seq, _ = x_prompt.shape
    dec_batch, dec_seq, _ = x_sample.shape
    assert dec_seq == 1
    l = 0
    ln1 = ln1_g[l][None, :]
    ln2 = ln2_g[l][None, :]
    lnf = lnf_g[None, :]
    gn = ret_gn_g[l][None, :]
    w_ret, *w_gates = _cast_ret_weights(w_in[l])
    inv_row = _rope_inv_row()

    caches = (cache_k_w128[l], cache_v_w128[l], cache_k_w512[l], cache_v_w512[l],
              cache_k_w2048[l], cache_v_w2048[l])

    xs = x_sample.reshape(dec_batch, D_MODEL)
    xp = x_prompt.reshape(batch * seq, D_MODEL)
    (rq, rk, rv, rg, gr, ga, sq, sk, sv, sg, sgr, sga), w_qkv = _inproj_ret(
        xp, xs, ln1, w_ret, w_gates, w_in[l], inv_row, TM_INPROJ, seq, PAST_LEN)
    att_o = _inproj_att(xp, xs, ln1, w_qkv, TM_INPROJ, seq)
    aqs, aks, avs = att_o[0:3], att_o[3:6], att_o[6:9]
    kfull, vfull = att_o[9:12], att_o[12:15]
    s_qkv = att_o[15]

    rel_bias_t = rel_bias.T
    s_ret, s_attn, new_st, (wpl, wpg) = _decode(
        sq, sk, sv, sg, gn, state_ret[l], s_qkv, rel_bias_t, caches, (w_ple[l], w_ple_gate[l]))
    ple_s = p_sample[l].reshape(dec_batch, D_PLE)

    ret_out, st_p, (wu, wd) = _retention_prompt(rq, rk, rv, rg, gn, batch, seq, TS_RETENTION, (w_up[l], w_down[l]))
    outs, lses, branch_w = [], [], []
    band_onehot, band_neg = _band_select()
    assert ATT_GROUPS == 3
    for g, w in enumerate((w_ret_br[l], w_att_br[l], w_out[l])):
        o, lse, wb = _attention_prompt(aqs[g], aks[g], avs[g], rel_bias_t, band_onehot, band_neg, g, w)
        outs.append(o)
        lses.append(lse)
        branch_w.append(wb)
    wrb, wab, wo = branch_w
    tail_w = (wrb, wab, wo, wu, wd, wpl, wpg, ln2, lnf)
    ple_p = p_prompt[l].reshape(batch * seq, D_PLE)
    tm_tail = batch * seq // dec_batch
    y_p, y_s, new_caches = _tail_shift([ret_out] + outs + lses + [gr, ga, xp, ple_p],
                                       [s_ret, s_attn, sgr, sga, xs, ple_s], tail_w, tm_tail, s_qkv, caches)
    y_prompt = y_p.reshape(batch, seq, D_MODEL)
    y_sample = y_s.reshape(dec_batch, 1, D_MODEL)
    new_state_p = st_p[None]
    kv_p = []
    for g in range(ATT_GROUPS):
        shape = (1, batch, min(ATT_WINDOWS[g], seq), ATT_HPG, ATT_HD)
        kv_p.append(kfull[g].reshape(shape))
        kv_p.append(vfull[g].reshape(shape))
    kv_s = [c[None] for c in new_caches]

    return (y_prompt, y_sample, new_state_p, *kv_p, new_st[None], *kv_s)
```

```python
import functools
import math

import jax
import jax.numpy as jnp
import numpy as np
from jax import lax
from jax.experimental import pallas as pl
from jax.experimental.pallas import tpu as pltpu

F32 = jnp.float32
BF16 = jnp.bfloat16

D_MODEL = 1024
RET_HEADS = 8
RET_DK = 64
RET_DV = 128
RET_PAIRS = RET_HEADS // 2
RET_CHUNK = 128
ROPE_BASE = 10000.0
ATT_WINDOWS = (128, 512, 2048)
ATT_DILATIONS = (1, 4, 16)
ATT_GROUPS = 3
ATT_HPG = 4
ATT_HD = 128
ATT_NK = 128
ATT_GW = ATT_HPG * ATT_HD
REL_BUCKETS = 32
REL_MAX_DIST = 2048
D_FF = 4 * D_MODEL
D_PLE = 256
NORM_EPS = 1e-6
RET_QK_W = RET_HEADS * RET_DK
RET_V_W = RET_HEADS * RET_DV
ATT_W = ATT_GROUPS * ATT_GW
COL_ATT = 2 * RET_QK_W + 2 * RET_V_W
COL_GATE = COL_ATT + 3 * ATT_W
N_IN = COL_GATE + 2 * D_MODEL
PAST_LEN = 16384

VMEM_LIMIT_V7X = 56 * 1024 * 1024
VMEM_LIMIT_TAIL_V7X = 62 * 1024 * 1024
LANES = 128
TM_INPROJ = 512
TS_RETENTION = 512
ATTN_UNROLL = 8
PERM_ROWS = 256
SINGLE_BLOCK_GROUP = 4
DECODE_SEQS = 4
DECODE_RING = 3
CAST_STEPS = 8


def _dot(a, b):
    return jnp.dot(a, b, preferred_element_type=F32)


def _dot_nt(a, b):
    return lax.dot_general(a, b, (((1,), (1,)), ((), ())), preferred_element_type=F32)


def _dot_tn(a, b):
    return lax.dot_general(a, b, (((0,), (0,)), ((), ())), preferred_element_type=F32)


def _rms(x, g):
    return x * lax.rsqrt(jnp.mean(x * x, axis=-1, keepdims=True) + NORM_EPS) * g


def _sigmoid(x):
    return 1.0 / (1.0 + jnp.exp(-x))


def _resident(shape):
    return pl.BlockSpec(shape, lambda *_: (0,) * len(shape), pipeline_mode=pl.Buffered(1))


def _params(n_axes):
    return pltpu.CompilerParams(dimension_semantics=("arbitrary",) * n_axes,
                                vmem_limit_bytes=VMEM_LIMIT_V7X)


def _rope_inv_row():
    half = RET_DK // 2
    inv = ROPE_BASE ** (-jnp.arange(half, dtype=F32) / half)
    return jnp.tile(inv, LANES // half)[None, :]


def _ret_log_decay():
    return np.log1p(-np.exp2(-5.0 - np.arange(RET_HEADS, dtype=np.float32))).astype(np.float32)


def _ret_tables():
    c = RET_CHUNK
    lg = _ret_log_decay()
    i = np.arange(c, dtype=np.float32)
    diff = i[:, None] - i[None, :]
    dmask = np.where(diff[None] >= 0, np.exp(np.maximum(diff, 0.0)[None] * lg[:, None, None]), 0.0)
    q_decay = np.exp((i + 1.0)[:, None] * lg[None, :])
    k_decay = np.exp((c - 1.0 - i)[:, None] * lg[None, :])
    qdec = np.broadcast_to(q_decay.T[:, :, None], (RET_HEADS, c, RET_DV))
    kdec = np.repeat(k_decay, RET_DK, axis=1).reshape(c, RET_PAIRS, 2 * RET_DK).transpose(1, 0, 2)
    gc = np.repeat(np.exp(c * lg), RET_DV).reshape(RET_PAIRS, 1, 2 * RET_DV)
    return tuple(jnp.asarray(np.ascontiguousarray(t), F32) for t in (dmask, qdec, kdec, gc))


def _rel_buckets():
    max_exact = REL_BUCKETS // 2
    out = []
    for dil in ATT_DILATIONS:
        d = np.arange(ATT_NK, dtype=np.int32) * dil
        log_ratio = (np.log(np.maximum(d, 1).astype(np.float32) / np.float32(max_exact))
                     / np.float32(math.log(REL_MAX_DIST / max_exact)))
        large = max_exact + (log_ratio * np.float32(REL_BUCKETS - max_exact)).astype(np.int32)
        out.append(np.where(d < max_exact, d, np.minimum(large, REL_BUCKETS - 1)))
    return np.stack(out)


def _band_select():
    nk = ATT_NK
    buckets = _rel_buckets()
    onehot = np.zeros((ATT_GROUPS, REL_BUCKETS, 3 * nk), np.float32)
    for g in range(ATT_GROUPS):
        for k in range(nk, 2 * nk):
            onehot[g, buckets[g, 2 * nk - 1 - k], k] = 1.0
    mask = np.full((1, 3 * nk), -np.inf, np.float32)
    mask[0, nk:2 * nk] = 0.0
    return jnp.asarray(onehot, BF16), jnp.asarray(mask, F32)


def _slot_order(dil):
    slots = np.arange(ATT_NK)
    return np.concatenate([slots[0::2], slots[1::2]]) if dil == 1 else slots


def _slot_select():
    buckets = _rel_buckets()
    onehot = np.zeros((ATT_GROUPS, REL_BUCKETS, ATT_NK), np.float32)
    for g in range(ATT_GROUPS):
        for col, slot in enumerate(_slot_order(ATT_DILATIONS[g])):
            onehot[g, buckets[g, 0 if slot == 0 else ATT_NK - slot], col] = 1.0
    return jnp.asarray(onehot, BF16)


def _split3(x):
    hi = x.astype(BF16)
    rem = x - hi.astype(F32)
    mid = rem.astype(BF16)
    lo = (rem - mid.astype(F32)).astype(BF16)
    return hi, mid, lo


def _cast_body(*refs):
    n = len(refs) // 2
    for src, dst in zip(refs[:n], refs[n:]):
        dst[...] = src[...].astype(dst.dtype)


def _cast_ret_weights(w_in):
    rows = D_MODEL // CAST_STEPS
    half = D_MODEL // 2
    assert COL_GATE % half == 0
    widths_blocks = [(COL_ATT, 0)] + [(half, COL_GATE // half + j) for j in range(4)]
    window = lambda width, blk: pl.BlockSpec((rows, width), lambda i: (i, blk))
    return pl.pallas_call(
        _cast_body,
        grid=(CAST_STEPS,),
        in_specs=[window(wd, blk) for wd, blk in widths_blocks],
        out_specs=[window(wd, 0) for wd, _ in widths_blocks],
        out_shape=[jax.ShapeDtypeStruct((D_MODEL, wd), BF16) for wd, _ in widths_blocks],
        compiler_params=_params(1),
        name="cast_ret_weights",
    )(*([w_in] * len(widths_blocks)))


def _inproj_ret_body(x_ref, xs_ref, ln_ref, w_ref, wg0_ref, wg1_ref, wg2_ref, wg3_ref, inv_ref, *refs,
                     tm, tiles, n_steps, sample_pos):
    cast_in = refs[:3]
    outs, sample_outs = refs[3:9], refs[9:15]
    cast_out = refs[15:18]
    cos_s, sin_s = refs[18:]
    for src, dst in zip(cast_in, cast_out):
        dst[...] = src[...].astype(dst.dtype)
    i = pl.program_id(0)

    def first_half(rows):
        lane = lax.broadcasted_iota(jnp.int32, (rows, LANES), 1)
        return (lane % RET_DK) < (RET_DK // 2)

    def tables(pos):
        ang = pos.astype(F32) * inv_ref[...]
        sin = jnp.sin(ang)
        return jnp.cos(ang), jnp.where(first_half(pos.shape[0]), -sin, sin)

    def project(x, cos, sin, q_ref, k_ref, v_ref, g_ref, gr_ref, ga_ref):
        h = _rms(x, ln_ref[...]).astype(BF16)
        qk = _dot(h, w_ref[:, 0:2 * RET_QK_W])
        n_q = RET_QK_W // LANES
        fh = first_half(x.shape[0])
        for c in range(2 * n_q):
            xc = qk[:, c * LANES:(c + 1) * LANES]
            swapped = jnp.where(fh, pltpu.roll(xc, LANES - RET_DK // 2, 1), pltpu.roll(xc, RET_DK // 2, 1))
            r = xc * cos + swapped * sin
            if c < n_q:
                q_ref[:, c * LANES:(c + 1) * LANES] = r.astype(q_ref.dtype)
            else:
                k_ref[:, (c - n_q) * LANES:(c - n_q + 1) * LANES] = (r * (RET_DK ** -0.5)).astype(k_ref.dtype)
        o = 2 * RET_QK_W
        v_ref[...] = _dot(h, w_ref[:, o:o + RET_V_W]).astype(v_ref.dtype)
        o += RET_V_W
        g_ref[...] = _dot(h, w_ref[:, o:o + RET_V_W]).astype(g_ref.dtype)
        half = D_MODEL // 2
        gr_ref[:, :half] = _dot(h, wg0_ref[...]).astype(gr_ref.dtype)
        gr_ref[:, half:] = _dot(h, wg1_ref[...]).astype(gr_ref.dtype)
        ga_ref[:, :half] = _dot(h, wg2_ref[...]).astype(ga_ref.dtype)
        ga_ref[:, half:] = _dot(h, wg3_ref[...]).astype(ga_ref.dtype)

    trow = pl.ds(pl.multiple_of(lax.rem(i, tiles) * tm, tm), tm)

    @pl.when(i < tiles)
    def _():
        cos_s[trow, :], sin_s[trow, :] = tables(i * tm + lax.broadcasted_iota(jnp.int32, (tm, LANES), 0))
    project(x_ref[...], cos_s[trow, :], sin_s[trow, :], *outs)

    @pl.when(i == n_steps - 1)
    def _():
        ns = xs_ref.shape[0]
        project(xs_ref[...], *tables(jnp.full((ns, LANES), sample_pos, jnp.int32)), *sample_outs)


def _inproj_ret(x2d, xs2d, ln, w_ret, w_gates, w_in, inv_row, tm, seq, sample_pos):
    n = x2d.shape[0]
    ns = xs2d.shape[0]
    tiles = seq // tm
    n_steps = n // tm
    row = lambda width: pl.BlockSpec((tm, width), lambda i: (i, 0))
    widths = (RET_QK_W, RET_QK_W, RET_V_W, RET_V_W, D_MODEL, D_MODEL)
    slab = D_MODEL // n_steps
    assert slab % 16 == 0 and COL_ATT % ATT_W == 0
    res = pl.pallas_call(
        functools.partial(_inproj_ret_body, tm=tm, tiles=tiles, n_steps=n_steps, sample_pos=sample_pos),
        grid=(n_steps,),
        in_specs=[row(D_MODEL), _resident(xs2d.shape), _resident((1, D_MODEL)), _resident(w_ret.shape)]
        + [_resident(w.shape) for w in w_gates] + [_resident((1, LANES))]
        + [pl.BlockSpec((slab, ATT_W), lambda i, kind=kind: (i, COL_ATT // ATT_W + kind)) for kind in range(3)],
        out_specs=[row(wd) for wd in widths] + [pl.BlockSpec((ns, wd), lambda i: (0, 0)) for wd in widths]
        + [pl.BlockSpec((slab, ATT_W), lambda i: (i, 0)) for _ in range(3)],
        out_shape=[jax.ShapeDtypeStruct((n, wd), BF16) for wd in widths]
        + [jax.ShapeDtypeStruct((ns, wd), F32) for wd in widths]
        + [jax.ShapeDtypeStruct((D_MODEL, ATT_W), BF16) for _ in range(3)],
        scratch_shapes=[pltpu.VMEM((seq, LANES), F32), pltpu.VMEM((seq, LANES), F32)],
        compiler_params=_params(1),
        name="inproj_ret",
    )(x2d, xs2d, ln, w_ret, *w_gates, inv_row, w_in, w_in, w_in)
    return res[:12], res[12:]


def _inproj_att_body(x_ref, xs_ref, ln_ref, wq_ref, wk_ref, wv_ref, *refs, tm, keeps, seq, n_steps):
    lowp = refs[:3 * ATT_GROUPS]
    full = refs[3 * ATT_GROUPS:5 * ATT_GROUPS]
    sample_ref = refs[5 * ATT_GROUPS]

    @pl.when(pl.program_id(0) == n_steps - 1)
    def _():
        hs = _rms(xs_ref[...], ln_ref[...]).astype(BF16)
        for kind, w_ref in enumerate((wq_ref, wk_ref, wv_ref)):
            for g in range(ATT_GROUPS):
                c = kind * ATT_GROUPS + g
                sample_ref[:, c * ATT_GW:(c + 1) * ATT_GW] = _dot(hs, w_ref[:, g * ATT_GW:(g + 1) * ATT_GW])

    h = _rms(x_ref[...], ln_ref[...]).astype(BF16)
    for g in reversed(range(ATT_GROUPS)):
        for kind in reversed(range(3)):
            dil = ATT_DILATIONS[g]
            dst = lowp[kind * ATT_GROUPS + g]
            r = _dot(h, (wq_ref, wk_ref, wv_ref)[kind][:, g * ATT_GW:(g + 1) * ATT_GW])
            if dil == 1:
                dst[0] = r.astype(dst.dtype)
            else:
                dst[...] = jnp.swapaxes(r.reshape(tm // dil, dil, ATT_GW), 0, 1).astype(dst.dtype)
            if kind > 0:
                cache = full[(kind - 1) * ATT_GROUPS + g]
                rows = tm if keeps[g] == seq else keeps[g]
                for hh in range(ATT_HPG):
                    cache[pl.ds(hh, rows, stride=ATT_HPG), :] = r[tm - rows:, hh * ATT_HD:(hh + 1) * ATT_HD]


def _inproj_att(x2d, xs2d, ln, w_qkv, tm, seq):
    n = x2d.shape[0]
    ns = xs2d.shape[0]
    batch = n // seq
    tiles = seq // tm
    keeps = tuple(min(wd, seq) for wd in ATT_WINDOWS)
    assert all(kp <= tm or kp == seq for kp in keeps) and seq % tm == 0
    out_specs, out_shape = [], []
    for _ in range(3):
        for g in range(ATT_GROUPS):
            dil = ATT_DILATIONS[g]
            out_specs.append(pl.BlockSpec((None, dil, tm // dil, ATT_GW), lambda i: (i // tiles, 0, i % tiles, 0)))
            out_shape.append(jax.ShapeDtypeStruct((batch, dil, seq // dil, ATT_GW), BF16))
    for _ in range(2):
        for g in range(ATT_GROUPS):
            if keeps[g] == seq:
                idx = lambda i: (i // tiles, i % tiles, 0)
                rows = tm
            else:
                idx = lambda i: (i // tiles, 0, 0)
                rows = keeps[g]
            out_specs.append(pl.BlockSpec((None, rows * ATT_HPG, ATT_HD), idx))
            out_shape.append(jax.ShapeDtypeStruct((batch, keeps[g] * ATT_HPG, ATT_HD), F32))
    out_specs.append(pl.BlockSpec((ns, 3 * ATT_W), lambda i: (0, 0)))
    out_shape.append(jax.ShapeDtypeStruct((ns, 3 * ATT_W), F32))
    return pl.pallas_call(
        functools.partial(_inproj_att_body, tm=tm, keeps=keeps, seq=seq, n_steps=n // tm),
        grid=(n // tm,),
        in_specs=[pl.BlockSpec((tm, D_MODEL), lambda i: (i, 0)), _resident(xs2d.shape), _resident((1, D_MODEL))]
        + [_resident(w.shape) for w in w_qkv],
        out_specs=out_specs,
        out_shape=out_shape,
        compiler_params=_params(1),
        name="inproj_att",
    )(x2d, xs2d, ln, *w_qkv)


def _gn_swish(o, gate, gn):
    mu = jnp.mean(o, axis=-1, keepdims=True)
    d = o - mu
    var = jnp.mean(d * d, axis=-1, keepdims=True)
    on = d * lax.rsqrt(var + NORM_EPS) * gn
    return gate * _sigmoid(gate) * on


def _retention_body(q_ref, k_ref, v_ref, g_ref, gn_ref, dm_ref, qdec_ref, kdec_ref, gc_ref, *refs,
                    n_chunks, n_cast):
    cast_in = refs[:n_cast]
    out_ref, st_ref = refs[n_cast:n_cast + 2]
    cast_out = refs[n_cast + 2:2 * n_cast + 2]
    state = refs[2 * n_cast + 2]
    for src, dst in zip(cast_in, cast_out):
        dst[...] = src[...].astype(dst.dtype)
    c = RET_CHUNK

    @pl.when(pl.program_id(1) == 0)
    def _():
        state[...] = jnp.zeros_like(state)

    lane = lax.broadcasted_iota(jnp.int32, (c, 2 * RET_DK), 1)
    head0 = lane < RET_DK

    def chunk(ci, carry):
        rows = pl.ds(pl.multiple_of(ci * c, c), c)
        def independent(p):
            q2 = q_ref[rows, p * 2 * RET_DK:(p + 1) * 2 * RET_DK]
            k2 = k_ref[rows, p * 2 * RET_DK:(p + 1) * 2 * RET_DK]
            v2 = v_ref[rows, p * 2 * RET_DV:(p + 1) * 2 * RET_DV]
            pst = state[p]
            pst_lo = pst.astype(BF16)
            zero = jnp.zeros_like(q2)
            parts = []
            for hh in range(2):
                qm = jnp.where(head0 if hh == 0 else jnp.logical_not(head0), q2, zero)
                hc = slice(hh * RET_DV, (hh + 1) * RET_DV)
                parts.append((_dot_nt(qm, k2), _dot(qm, pst_lo[:, hc]), v2[:, hc]))
            kd = (k2.astype(F32) * kdec_ref[p]).astype(BF16)
            state[p] = pst * gc_ref[p] + _dot_tn(kd, v2)
            return parts

        def readout(p, parts):
            for hh, (scores, cross, values) in enumerate(parts):
                h = 2 * p + hh
                o = _dot((scores * dm_ref[h]).astype(BF16), values) + cross * qdec_ref[h]
                gate = g_ref[rows, h * RET_DV:(h + 1) * RET_DV].astype(F32)
                res = _gn_swish(o, gate, gn_ref[:, h * RET_DV:(h + 1) * RET_DV])
                out_ref[rows, h * RET_DV:(h + 1) * RET_DV] = res.astype(out_ref.dtype)

        for p in range(RET_PAIRS):
            readout(p, independent(p))
        return carry

    lax.fori_loop(0, n_chunks, chunk, 0, unroll=True)
    for p in range(RET_PAIRS):
        pst = state[p]
        for hh in range(2):
            st_ref[2 * p + hh] = pst[hh * RET_DK:(hh + 1) * RET_DK, hh * RET_DV:(hh + 1) * RET_DV]


def _retention_prompt(q, k, v, g, gn, batch, seq, ts, cast):
    n = q.shape[0]
    steps = seq // ts
    dmask, qdec, kdec, gc = _ret_tables()
    row = lambda width: pl.BlockSpec((ts, width), lambda b, s: (b * steps + s, 0))
    const = lambda a: pl.BlockSpec(a.shape, lambda b, s: (0,) * a.ndim)
    n_steps = batch * steps
    slab = lambda w: pl.BlockSpec((w.shape[0] // n_steps, w.shape[1]), lambda b, s: (b * steps + s, 0))
    assert all(w.shape[0] % (16 * n_steps) == 0 for w in cast)
    res = pl.pallas_call(
        functools.partial(_retention_body, n_chunks=ts // RET_CHUNK, n_cast=len(cast)),
        grid=(batch, steps),
        in_specs=[row(RET_QK_W), row(RET_QK_W), row(RET_V_W), row(RET_V_W), const(gn),
                  const(dmask), const(qdec), const(kdec), const(gc)] + [slab(w) for w in cast],
        out_specs=[row(RET_V_W),
                   pl.BlockSpec((None, RET_HEADS, RET_DK, RET_DV), lambda b, s: (b, 0, 0, 0))]
        + [slab(w) for w in cast],
        out_shape=[jax.ShapeDtypeStruct((n, RET_V_W), BF16),
                   jax.ShapeDtypeStruct((batch, RET_HEADS, RET_DK, RET_DV), F32)]
        + [jax.ShapeDtypeStruct(w.shape, BF16) for w in cast],
        scratch_shapes=[pltpu.VMEM((RET_PAIRS, 2 * RET_DK, 2 * RET_DV), F32)],
        compiler_params=_params(2),
        name="retention",
    )(q, k, v, g, gn, dmask, qdec, kdec, gc, *cast)
    return res[0], res[1], tuple(res[2:])


def _attn_body(q_ref, k_ref, v_ref, rb_ref, sel_ref, neg_ref, w_ref, o_ref, lse_ref, wb_ref, *scratch,
               group, dil, n_blocks):
    nk = ATT_NK
    wb_ref[...] = w_ref[...].astype(wb_ref.dtype)
    scale = ATT_HD ** -0.5
    lane = lax.broadcasted_iota(jnp.int32, (nk, LANES), 1)
    tabs = scratch[-1]

    @pl.when(pl.program_id(0) == 0)
    def _():
        band = neg_ref[...]
        for piece in _split3(rb_ref[group * ATT_HPG:(group + 1) * ATT_HPG, :]):
            band = band + _dot(piece, sel_ref[...])
        for i in range(nk):
            window = band[:, nk - 1 - i:3 * nk - 1 - i]
            for h in range(ATT_HPG):
                tabs[1, h, i:i + 1, :] = window[h:h + 1, :]
        col = lax.broadcasted_iota(jnp.int32, (nk, 2 * nk), 1)
        for h in range(ATT_HPG):
            tabs[0, h] = jnp.where(col < nk, -jnp.inf, tabs[1, h])
    if dil > 1:
        o_s, l_s = scratch[:2]

    def chain(s_raw, v_ext, sel, h):
        s = s_raw * scale + (tabs[sel, h] if n_blocks > 1 else tabs[1, h, :, nk:])
        m = jnp.max(s, axis=-1, keepdims=True)
        p = jnp.exp(s - m)
        pv = _dot(p.astype(BF16), v_ext)
        den = pv[:, ATT_HD:]
        return (pv[:, :ATT_HD] / den).astype(BF16), m + jnp.log(den)

    def write(rs, rows, outs):
        lse_tile = jnp.zeros((nk, LANES), F32)
        for h, (o, lse) in enumerate(outs):
            hc = slice(h * ATT_HD, (h + 1) * ATT_HD)
            if dil == 1:
                o_ref[rows, hc] = o
            else:
                o_s[rs, rows, hc] = o
            lse_tile = jnp.where(lane == h, lse, lse_tile)
        if dil == 1:
            lse_ref[rows, :] = lse_tile
        else:
            l_s[rs, rows, :] = lse_tile

    heads = [slice(h * ATT_HD, (h + 1) * ATT_HD) for h in range(ATT_HPG)]

    def block(rs, n, n_prev, sel):
        rows = pl.ds(pl.multiple_of(n * nk, nk), nk)
        prev_rows = pl.ds(pl.multiple_of(n_prev * nk, nk), nk)
        q = q_ref[rs, rows, :]
        k_all = jnp.concatenate([k_ref[rs, prev_rows, :], k_ref[rs, rows, :]], axis=0)
        v_all = jnp.concatenate([v_ref[rs, prev_rows, :], v_ref[rs, rows, :]], axis=0)
        ones = jnp.ones((2 * nk, ATT_HD), BF16)
        outs = []
        for h, hc in enumerate(heads):
            v_ext = jnp.concatenate([v_all[:, hc], ones], axis=1)
            outs.append(chain(_dot_nt(q[:, hc], k_all[:, hc]), v_ext, sel, h))
        write(rs, rows, outs)

    def single_blocks(streams):
        ones = jnp.ones((nk, ATT_HD), BF16)
        scores = [[_dot_nt(q_ref[rs, :, hc], k_ref[rs, :, hc]) for hc in heads] for rs in streams]
        for i, rs in enumerate(streams):
            outs = [chain(scores[i][h], jnp.concatenate([v_ref[rs, :, hc], ones], axis=1), 0, h)
                    for h, hc in enumerate(heads)]
            write(rs, slice(None), outs)

    if n_blocks == 1:
        for r0 in range(0, dil, SINGLE_BLOCK_GROUP):
            single_blocks(range(r0, min(r0 + SINGLE_BLOCK_GROUP, dil)))
    else:
        for rs in range(dil):
            def loop(n, carry, rs=rs):
                block(rs, n, jnp.maximum(n - 1, 0), jnp.minimum(n, 1))
                return carry
            lax.fori_loop(0, n_blocks, loop, 0, unroll=ATTN_UNROLL)

    if dil > 1:
        rows = o_ref.shape[0]
        for j in range(rows // PERM_ROWS):
            src = slice(j * (PERM_ROWS // dil), (j + 1) * (PERM_ROWS // dil))
            dst = slice(j * PERM_ROWS, (j + 1) * PERM_ROWS)
            o_ref[dst, :] = jnp.swapaxes(o_s[:, src, :], 0, 1).reshape(PERM_ROWS, ATT_GW).astype(BF16)
            lse_ref[dst, :] = jnp.swapaxes(l_s[:, src, :], 0, 1).reshape(PERM_ROWS, LANES)


def _attention_prompt(aq, ak, av, rel_bias_t, onehot, neg, g, cast):
    batch, dil, length, _ = aq.shape
    seq = dil * length
    n_blocks = length // ATT_NK
    blk = pl.BlockSpec((None, dil, length, ATT_GW), lambda b: (b, 0, 0, 0))
    assert cast.shape[0] % (16 * batch) == 0
    slab = pl.BlockSpec((cast.shape[0] // batch, cast.shape[1]), lambda b: (b, 0))
    scratch = [pltpu.VMEM((dil, length, ATT_GW), BF16), pltpu.VMEM((dil, length, LANES), F32)] if dil > 1 else []
    scratch.append(pltpu.VMEM((2, ATT_HPG, ATT_NK, 2 * ATT_NK), F32))
    o, lse, cast_bf16 = pl.pallas_call(
        functools.partial(_attn_body, group=g, dil=dil, n_blocks=n_blocks),
        grid=(batch,),
        in_specs=[blk, blk, blk, pl.BlockSpec(rel_bias_t.shape, lambda b: (0, 0)),
                  pl.BlockSpec((None, REL_BUCKETS, 3 * ATT_NK), lambda b: (g, 0, 0)),
                  pl.BlockSpec((1, 3 * ATT_NK), lambda b: (0, 0)), slab],
        out_specs=[pl.BlockSpec((None, seq, ATT_GW), lambda b: (b, 0, 0)),
                   pl.BlockSpec((None, seq, LANES), lambda b: (b, 0, 0)), slab],
        out_shape=[jax.ShapeDtypeStruct((batch, seq, ATT_GW), BF16),
                   jax.ShapeDtypeStruct((batch, seq, LANES), F32),
                   jax.ShapeDtypeStruct(cast.shape, BF16)],
        scratch_shapes=scratch,
        compiler_params=_params(1),
        name="attention_g%d" % g,
    )(aq, ak, av, rel_bias_t, onehot, neg, cast)
    return o.reshape(batch * seq, ATT_GW), lse.reshape(batch * seq, LANES), cast_bf16


def _tail_math(act_refs, w_refs, y_ref, combine, middle=None):
    wrb_ref, wab_ref, wo_ref, wu_ref, wd_ref, wpl_ref, wpg_ref, ln2_ref, lnf_ref = w_refs
    if combine:
        ret_ref, o0_ref, o1_ref, o2_ref, l0_ref, l1_ref, l2_ref, gr_ref, ga_ref, x_ref, ple_ref = act_refs
        lses = [l0_ref[...], l1_ref[...], l2_ref[...]]
        outs = [o0_ref, o1_ref, o2_ref]
        parts = []
        for h in range(ATT_HPG):
            lh = [l[:, h:h + 1] for l in lses]
            mx = jnp.maximum(jnp.maximum(lh[0], lh[1]), lh[2])
            e = [jnp.exp(l - mx) for l in lh]
            tot = e[0] + e[1] + e[2]
            acc = None
            for g in range(ATT_GROUPS):
                term = (e[g] / tot) * outs[g][:, h * ATT_HD:(h + 1) * ATT_HD].astype(F32)
                acc = term if acc is None else acc + term
            parts.append(acc)
        att = jnp.concatenate(parts, axis=1).astype(BF16)
    else:
        ret_ref, att_ref, gr_ref, ga_ref, x_ref, ple_ref = act_refs
        att = att_ref[...].astype(BF16)
    a = _dot(ret_ref[...].astype(BF16), wrb_ref[...])
    b = _dot(att, wab_ref[...])
    mixed = _sigmoid(gr_ref[...].astype(F32)) * a + _sigmoid(ga_ref[...].astype(F32)) * b
    x1 = x_ref[...] + _dot(mixed.astype(BF16), wo_ref[...])
    h2 = _rms(x1, ln2_ref[...]).astype(BF16)
    ff_chunk = D_MODEL
    acc = None
    for c in range(D_FF // ff_chunk):
        u = _dot(h2, wu_ref[:, c * ff_chunk:(c + 1) * ff_chunk])
        r = jnp.maximum(u, 0.0)
        t = _dot((r * r).astype(BF16), wd_ref[c * ff_chunk:(c + 1) * ff_chunk, :])
        acc = t if acc is None else acc + t
        if middle is not None and c == D_FF // ff_chunk // 2 - 1:
            middle()
    x2 = x1 + acc
    gate = _sigmoid(_dot(x2.astype(BF16), wpg_ref[...]))
    x3 = x2 + gate * _dot(ple_ref[...].astype(BF16), wpl_ref[...])
    y_ref[...] = _rms(x3, lnf_ref[...])


def _tail_shift_body(*refs, n_act, n_sample, n_w, n_steps):
    nc = 2 * ATT_GROUPS
    acts = refs[:n_act]
    sample_acts = refs[n_act:n_act + n_sample]
    base = n_act + n_sample
    ws = refs[base:base + n_w]
    new_ref = refs[base + n_w]
    base += n_w + 1
    old = refs[base:base + nc]
    y_ref, ys_ref = refs[base + nc:base + nc + 2]
    base += nc + 2
    out = refs[base:base + nc]
    stage = refs[base + nc:base + 2 * nc]
    sem_in, sem_out, sem_row = refs[base + 2 * nc:]
    s = pl.program_id(0)
    keys = [2 * g for g in range(ATT_GROUPS)]
    values = [2 * g + 1 for g in range(ATT_GROUPS)]

    def copy_in(i, seq):
        width = old[i].shape[1]
        return pltpu.make_async_copy(old[i].at[seq, pl.ds(1, width - 1)], stage[i], sem_in.at[i])

    def copy_out(i, seq):
        width = old[i].shape[1]
        return pltpu.make_async_copy(stage[i], out[i].at[seq, pl.ds(0, width - 1)], sem_out.at[i])

    def copy_row(i):
        g, kind = divmod(i, 2)
        return pltpu.make_async_copy(new_ref.at[s, kind + 1, g], out[i].at[s, old[i].shape[1] - 1], sem_row.at[i])

    @pl.when(s == 0)
    def _():
        for i in keys:
            copy_in(i, 0).start()
    for i in keys:
        copy_in(i, s).wait()
    for i in keys:
        copy_out(i, s).start()

    @pl.when(s > 0)
    def _():
        for i in values:
            copy_out(i, s - 1).wait()
    for i in values:
        copy_in(i, s).start()
    for i in range(nc):
        copy_row(i).start()

    def middle():
        for i in values:
            copy_in(i, s).wait()
        for i in values:
            copy_out(i, s).start()
        for i in keys:
            copy_out(i, s).wait()

        @pl.when(s < n_steps - 1)
        def _():
            for i in keys:
                copy_in(i, s + 1).start()

    _tail_math(acts, ws, y_ref, True, middle)

    @pl.when(s == n_steps - 1)
    def _():
        _tail_math(sample_acts, ws, ys_ref, False)

    for i in range(nc):
        copy_row(i).wait()

    @pl.when(s == n_steps - 1)
    def _():
        for i in values:
            copy_out(i, s).wait()


def _tail_shift(acts, sample_acts, weights, tm, new_qkv, caches):
    n = acts[0].shape[0]
    n_steps = n // tm
    assert n_steps == new_qkv.shape[0]
    row = lambda a: pl.BlockSpec((tm, a.shape[1]), lambda i: (i, 0))
    anyspec = pl.BlockSpec(memory_space=pl.ANY)
    nc = len(caches)
    new_rows = new_qkv.reshape(new_qkv.shape[0], 3, ATT_GROUPS, ATT_HPG, ATT_HD)
    ns = sample_acts[0].shape[0]
    res = pl.pallas_call(
        functools.partial(_tail_shift_body, n_act=len(acts), n_sample=len(sample_acts), n_w=len(weights),
                          n_steps=n_steps),
        grid=(n_steps,),
        in_specs=[row(a) for a in acts] + [_resident(a.shape) for a in sample_acts]
        + [_resident(w.shape) for w in weights] + [anyspec] * (nc + 1),
        out_specs=[pl.BlockSpec((tm, D_MODEL), lambda i: (i, 0)), pl.BlockSpec((ns, D_MODEL), lambda i: (0, 0))]
        + [anyspec] * nc,
        out_shape=[jax.ShapeDtypeStruct((n, D_MODEL), F32), jax.ShapeDtypeStruct((ns, D_MODEL), F32)]
        + [jax.ShapeDtypeStruct(c.shape, c.dtype) for c in caches],
        scratch_shapes=[pltpu.VMEM((c.shape[1] - 1, ATT_HPG, ATT_HD), F32) for c in caches]
        + [pltpu.SemaphoreType.DMA((nc,)), pltpu.SemaphoreType.DMA((nc,)), pltpu.SemaphoreType.DMA((nc,))],
        compiler_params=pltpu.CompilerParams(dimension_semantics=("arbitrary",),
                                             vmem_limit_bytes=VMEM_LIMIT_TAIL_V7X),
        name="tail",
    )(*acts, *sample_acts, *weights, new_rows, *caches)
    return res[0], res[1], res[2:]


def _decode_body(q_ref, k_ref, v_ref, g_ref, gn_ref, st_ref, gam_ref,
                 qkv_ref, rb_ref, slot_ref,
                 ck0_ref, cv0_ref, ck1_ref, cv1_ref, ck2_ref, cv2_ref, *refs, seqs, n_cast):
    cast_in = refs[:n_cast]
    ret_ref, att_ref, nst_ref = refs[n_cast:n_cast + 3]
    cast_out = refs[n_cast + 3:2 * n_cast + 3]
    slot_bias = refs[2 * n_cast + 3]
    cache_refs = (ck0_ref, cv0_ref, ck1_ref, cv1_ref, ck2_ref, cv2_ref)
    bufs = refs[2 * n_cast + 4:2 * n_cast + 4 + len(cache_refs)]
    sem = refs[2 * n_cast + 4 + len(cache_refs)]
    step = pl.program_id(0)

    def fetch(s):
        ring = lax.rem(s, DECODE_RING)
        return [pltpu.make_async_copy(c.at[pl.ds(s * seqs, seqs), :, pl.ds(0, 2 * ATT_HPG), :], b.at[ring],
                                      sem.at[j, ring])
                for j, (c, b) in enumerate(zip(cache_refs, bufs))]

    @pl.when(step == 0)
    def _():
        for s in range(DECODE_RING - 1):
            for cp in fetch(s):
                cp.start()

    @pl.when(step + (DECODE_RING - 1) < pl.num_programs(0))
    def _():
        for cp in fetch(step + (DECODE_RING - 1)):
            cp.start()

    for src, dst in zip(cast_in, cast_out):
        dst[...] = src[...].astype(dst.dtype)
    sub = lax.broadcasted_iota(jnp.int32, (8, 2 * RET_DK), 0)
    lane = lax.broadcasted_iota(jnp.int32, (8, 2 * RET_DK), 1)
    row0 = sub == 0
    srow = lax.broadcasted_iota(jnp.int32, (2 * RET_DK, RET_DV), 0)
    for i in range(seqs):
        for p in range(RET_PAIRS):
            pc = slice(p * 2 * RET_DK, (p + 1) * 2 * RET_DK)
            q2 = jnp.where(row0, jnp.broadcast_to(q_ref[i, :, pc], (8, 2 * RET_DK)), 0.0)
            k2 = jnp.where(row0, jnp.broadcast_to(k_ref[i, :, pc], (8, 2 * RET_DK)), 0.0)
            pst = st_ref[i, p]
            gam = gam_ref[p]
            outer = []
            for hh in range(2):
                h = 2 * p + hh
                hc = slice(h * RET_DV, (h + 1) * RET_DV)
                hsel = (lane < RET_DK) if hh == 0 else (lane >= RET_DK)
                qm = jnp.where(hsel, q2, 0.0)
                km = jnp.where(hsel, k2, 0.0)
                vh = v_ref[i, :, hc]
                v8 = jnp.where(row0[:, :RET_DV], jnp.broadcast_to(vh, (8, RET_DV)), 0.0)
                cross = _dot(qm.astype(BF16), (pst * gam).astype(BF16))[0:1, :]
                qk = jnp.sum(qm[0:1, :] * km[0:1, :], axis=-1, keepdims=True)
                o = cross + qk * vh
                ret_ref[i, :, hc] = _gn_swish(o, g_ref[i, :, hc], gn_ref[:, hc])
                outer.append(_dot_tn(k2.astype(BF16), v8.astype(BF16)))
            nst_ref[i, p] = pst * gam + jnp.where(srow < RET_DK, outer[0], outer[1])
    scale = ATT_HD ** -0.5
    for cp in fetch(step):
        cp.wait()
    ring = lax.rem(step, DECODE_RING)
    caches = [(bufs[2 * g].at[ring], bufs[2 * g + 1].at[ring]) for g in range(ATT_GROUPS)]
    slot = lax.broadcasted_iota(jnp.int32, (ATT_NK, ATT_HD), 0)
    is_new = slot == 0
    first = row0[:, :ATT_HD]

    @pl.when(pl.program_id(0) == 0)
    def _():
        pieces = _split3(rb_ref[...])
        for g in range(ATT_GROUPS):
            rows = slice(g * ATT_HPG, (g + 1) * ATT_HPG)
            slot_bias[rows, :] = (_dot(pieces[0][rows], slot_ref[g]) + _dot(pieces[1][rows], slot_ref[g])
                                  + _dot(pieces[2][rows], slot_ref[g]))

    def one_row(x):
        return jnp.where(first, jnp.broadcast_to(x, (8, ATT_HD)), 0.0).astype(BF16)

    def head_tiles(ref, i):
        tiles = jnp.swapaxes(ref[i], 0, 1)
        if ref.shape[1] == ATT_NK:
            return [tiles[h] for h in range(ATT_HPG)]
        return [jnp.concatenate([tiles[h], tiles[ATT_HPG + h]], axis=0) for h in range(ATT_HPG)]

    for i in range(seqs):
        scores = {}
        k_tiles = [head_tiles(caches[g][0], i) for g in range(ATT_GROUPS)]
        v_tiles = [head_tiles(caches[g][1], i) for g in range(ATT_GROUPS)]
        for h in range(ATT_HPG):
            for g in range(ATT_GROUPS):
                c0 = g * ATT_GW + h * ATT_HD
                kk = jnp.where(is_new, qkv_ref[i, :, ATT_W + c0:ATT_W + c0 + ATT_HD], k_tiles[g][h])
                scores[g, h] = _dot_nt(one_row(qkv_ref[i, :, c0:c0 + ATT_HD]), kk.astype(BF16))[0:1, :]
        for h in range(ATT_HPG):
            o_g, lse_g = [], []
            for g in range(ATT_GROUPS):
                c0 = g * ATT_GW + h * ATT_HD
                vv = jnp.where(is_new, qkv_ref[i, :, 2 * ATT_W + c0:2 * ATT_W + c0 + ATT_HD], v_tiles[g][h])
                gh = g * ATT_HPG + h
                s = scores[g, h] * scale + slot_bias[gh:gh + 1, :]
                m = jnp.max(s, axis=-1, keepdims=True)
                pr = jnp.exp(s - m)
                den = jnp.sum(pr, axis=-1, keepdims=True)
                o_g.append(_dot(one_row(pr), vv.astype(BF16))[0:1, :] / den)
                lse_g.append(m + jnp.log(den))
            mx = jnp.maximum(jnp.maximum(lse_g[0], lse_g[1]), lse_g[2])
            e = [jnp.exp(l - mx) for l in lse_g]
            tot = e[0] + e[1] + e[2]
            att_ref[i, :, h * ATT_HD:(h + 1) * ATT_HD] = ((e[0] / tot) * o_g[0] + (e[1] / tot) * o_g[1]
                                                          + (e[2] / tot) * o_g[2])


def _decode(q, k, v, g, gn, state, qkv, rel_bias_t, caches, cast):
    batch = q.shape[0]
    gam = jnp.asarray(np.ascontiguousarray(np.broadcast_to(
        np.repeat(np.exp(_ret_log_decay()), RET_DK).reshape(RET_PAIRS, 2 * RET_DK, 1),
        (RET_PAIRS, 2 * RET_DK, RET_DV))), F32)
    vec = lambda a: a.reshape(batch, 1, a.shape[1])
    seqs = DECODE_SEQS
    assert batch % seqs == 0
    vspec = lambda width: pl.BlockSpec((seqs, 1, width), lambda b: (b, 0, 0))
    const = lambda a: pl.BlockSpec(a.shape, lambda b: (0,) * a.ndim)
    st_spec = pl.BlockSpec((seqs, RET_PAIRS, 2 * RET_DK, RET_DV), lambda b: (b, 0, 0, 0))
    cache_in, cache_bufs = [], []
    for gi in range(ATT_GROUPS):
        dil = ATT_DILATIONS[gi]
        for c in caches[2 * gi:2 * gi + 2]:
            tiles = ATT_NK if dil > 1 else ATT_NK // 2
            cache_in.append(c.reshape(batch, tiles, c.shape[1] * ATT_HPG // tiles, ATT_HD))
            cache_bufs.append(pltpu.VMEM((DECODE_RING, seqs, tiles, 2 * ATT_HPG, ATT_HD), F32))
    cache_specs = [pl.BlockSpec(memory_space=pl.ANY)] * len(cache_in)
    st_pairs = state.reshape(batch, RET_PAIRS, 2 * RET_DK, RET_DV)
    slot_sel = _slot_select()
    steps = batch // seqs
    assert steps >= DECODE_RING - 1
    slab = lambda w: pl.BlockSpec((w.shape[0] // steps, w.shape[1]), lambda b: (b, 0))
    assert all(w.shape[0] % (16 * steps) == 0 for w in cast)
    res = pl.pallas_call(
        functools.partial(_decode_body, seqs=seqs, n_cast=len(cast)),
        grid=(steps,),
        in_specs=[vspec(RET_QK_W), vspec(RET_QK_W), vspec(RET_V_W), vspec(RET_V_W), const(gn), st_spec, const(gam),
                  vspec(3 * ATT_W), const(rel_bias_t), const(slot_sel)] + cache_specs + [slab(w) for w in cast],
        out_specs=[vspec(RET_V_W), vspec(ATT_GW), st_spec] + [slab(w) for w in cast],
        out_shape=[jax.ShapeDtypeStruct((batch, 1, RET_V_W), F32),
                   jax.ShapeDtypeStruct((batch, 1, ATT_GW), F32),
                   jax.ShapeDtypeStruct(st_pairs.shape, F32)]
        + [jax.ShapeDtypeStruct(w.shape, BF16) for w in cast],
        scratch_shapes=[pltpu.VMEM((rel_bias_t.shape[0], ATT_NK), F32)] + cache_bufs
        + [pltpu.SemaphoreType.DMA((len(cache_in), DECODE_RING))],
        compiler_params=_params(1),
        name="decode",
    )(vec(q), vec(k), vec(v), vec(g), gn, st_pairs, gam, vec(qkv), rel_bias_t, slot_sel, *cache_in, *cast)
    ret, att, nst = res[:3]
    return ret.reshape(batch, RET_V_W), att.reshape(batch, ATT_GW), nst.reshape(state.shape), tuple(res[3:])


def kernel(x_prompt, x_sample, state_ret, cache_k_w128, cache_v_w128, cache_k_w512, cache_v_w512,
           cache_k_w2048, cache_v_w2048, p_prompt, p_sample, ln1_g, w_in, ret_gn_g, w_ret_br, w_att_br,
           w_out, ln2_g, w_up, w_down, w_ple, w_ple_gate, rel_bias, lnf_g):
    depth, _, n_in = w_in.shape
    assert depth == 1 and n_in == N_IN
    batch, seq, _ = x_prompt.shape
    dec_batch, dec_seq, _ = x_sample.shape
    assert dec_seq == 1
    l = 0
    ln1 = ln1_g[l][None, :]
    ln2 = ln2_g[l][None, :]
    lnf = lnf_g[None, :]
    gn = ret_gn_g[l][None, :]
    w_ret, *w_gates = _cast_ret_weights(w_in[l])
    inv_row = _rope_inv_row()

    caches = (cache_k_w128[l], cache_v_w128[l], cache_k_w512[l], cache_v_w512[l],
              cache_k_w2048[l], cache_v_w2048[l])

    xs = x_sample.reshape(dec_batch, D_MODEL)
    xp = x_prompt.reshape(batch * seq, D_MODEL)
    (rq, rk, rv, rg, gr, ga, sq, sk, sv, sg, sgr, sga), w_qkv = _inproj_ret(
        xp, xs, ln1, w_ret, w_gates, w_in[l], inv_row, TM_INPROJ, seq, PAST_LEN)
    att_o = _inproj_att(xp, xs, ln1, w_qkv, TM_INPROJ, seq)
    aqs, aks, avs = att_o[0:3], att_o[3:6], att_o[6:9]
    kfull, vfull = att_o[9:12], att_o[12:15]
    s_qkv = att_o[15]

    rel_bias_t = rel_bias.T
    s_ret, s_attn, new_st, (wpl, wpg) = _decode(
        sq, sk, sv, sg, gn, state_ret[l], s_qkv, rel_bias_t, caches, (w_ple[l], w_ple_gate[l]))
    ple_s = p_sample[l].reshape(dec_batch, D_PLE)

    ret_out, st_p, (wu, wd) = _retention_prompt(rq, rk, rv, rg, gn, batch, seq, TS_RETENTION, (w_up[l], w_down[l]))
    outs, lses, branch_w = [], [], []
    band_onehot, band_neg = _band_select()
    assert ATT_GROUPS == 3
    for g, w in enumerate((w_ret_br[l], w_att_br[l], w_out[l])):
        o, lse, wb = _attention_prompt(aqs[g], aks[g], avs[g], rel_bias_t, band_onehot, band_neg, g, w)
        outs.append(o)
        lses.append(lse)
        branch_w.append(wb)
    wrb, wab, wo = branch_w
    tail_w = (wrb, wab, wo, wu, wd, wpl, wpg, ln2, lnf)
    ple_p = p_prompt[l].reshape(batch * seq, D_PLE)
    tm_tail = batch * seq // dec_batch
    y_p, y_s, new_caches = _tail_shift([ret_out] + outs + lses + [gr, ga, xp, ple_p],
                                       [s_ret, s_attn, sgr, sga, xs, ple_s], tail_w, tm_tail, s_qkv, caches)
    y_prompt = y_p.reshape(batch, seq, D_MODEL)
    y_sample = y_s.reshape(dec_batch, 1, D_MODEL)
    new_state_p = st_p[None]
    kv_p = []
    for g in range(ATT_GROUPS):
        shape = (1, batch, min(ATT_WINDOWS[g], seq), ATT_HPG, ATT_HD)
        kv_p.append(kfull[g].reshape(shape))
        kv_p.append(vfull[g].reshape(shape))
    kv_s = [c[None] for c in new_caches]

    return (y_prompt, y_sample, new_state_p, *kv_p, new_st[None], *kv_s)
```

```python
import functools
import math

import jax
import jax.numpy as jnp
import numpy as np
from jax import lax
from jax.experimental import pallas as pl
from jax.experimental.pallas import tpu as pltpu

F32 = jnp.float32
BF16 = jnp.bfloat16

D_MODEL = 1024
RET_HEADS = 8
RET_DK = 64
RET_DV = 128
RET_PAIRS = RET_HEADS // 2
RET_CHUNK = 128
ROPE_BASE = 10000.0
ATT_WINDOWS = (128, 512, 2048)
ATT_DILATIONS = (1, 4, 16)
ATT_GROUPS = 3
ATT_HPG = 4
ATT_HD = 128
ATT_NK = 128
ATT_GW = ATT_HPG * ATT_HD
REL_BUCKETS = 32
REL_MAX_DIST = 2048
D_FF = 4 * D_MODEL
D_PLE = 256
NORM_EPS = 1e-6
RET_QK_W = RET_HEADS * RET_DK
RET_V_W = RET_HEADS * RET_DV
ATT_W = ATT_GROUPS * ATT_GW
COL_ATT = 2 * RET_QK_W + 2 * RET_V_W
COL_GATE = COL_ATT + 3 * ATT_W
N_IN = COL_GATE + 2 * D_MODEL
PAST_LEN = 16384

VMEM_LIMIT_V7X = 56 * 1024 * 1024
VMEM_LIMIT_TAIL_V7X = 62 * 1024 * 1024
LANES = 128
TM_INPROJ = 512
TS_RETENTION = 512
ATTN_UNROLL = 8
PERM_ROWS = 256
SINGLE_BLOCK_GROUP = 4
DECODE_SEQS = 4
DECODE_RING = 3
CAST_STEPS = 8


def _dot(a, b):
    return jnp.dot(a, b, preferred_element_type=F32)


def _dot_nt(a, b):
    return lax.dot_general(a, b, (((1,), (1,)), ((), ())), preferred_element_type=F32)


def _dot_tn(a, b):
    return lax.dot_general(a, b, (((0,), (0,)), ((), ())), preferred_element_type=F32)


def _rms(x, g):
    return x * lax.rsqrt(jnp.mean(x * x, axis=-1, keepdims=True) + NORM_EPS) * g


def _sigmoid(x):
    return 1.0 / (1.0 + jnp.exp(-x))


def _resident(shape):
    return pl.BlockSpec(shape, lambda *_: (0,) * len(shape), pipeline_mode=pl.Buffered(1))


def _params(n_axes):
    return pltpu.CompilerParams(dimension_semantics=("arbitrary",) * n_axes,
                                vmem_limit_bytes=VMEM_LIMIT_V7X)


def _rope_inv_row():
    half = RET_DK // 2
    inv = ROPE_BASE ** (-jnp.arange(half, dtype=F32) / half)
    return jnp.tile(inv, LANES // half)[None, :]


def _ret_log_decay():
    return np.log1p(-np.exp2(-5.0 - np.arange(RET_HEADS, dtype=np.float32))).astype(np.float32)


def _ret_tables():
    c = RET_CHUNK
    lg = _ret_log_decay()
    i = np.arange(c, dtype=np.float32)
    diff = i[:, None] - i[None, :]
    dmask = np.where(diff[None] >= 0, np.exp(np.maximum(diff, 0.0)[None] * lg[:, None, None]), 0.0)
    q_decay = np.exp((i + 1.0)[:, None] * lg[None, :])
    k_decay = np.exp((c - 1.0 - i)[:, None] * lg[None, :])
    qdec = np.broadcast_to(q_decay.T[:, :, None], (RET_HEADS, c, RET_DV))
    kdec = np.repeat(k_decay, RET_DK, axis=1).reshape(c, RET_PAIRS, 2 * RET_DK).transpose(1, 0, 2)
    gc = np.repeat(np.exp(c * lg), RET_DV).reshape(RET_PAIRS, 1, 2 * RET_DV)
    return tuple(jnp.asarray(np.ascontiguousarray(t), F32) for t in (dmask, qdec, kdec, gc))


def _rel_buckets():
    max_exact = REL_BUCKETS // 2
    out = []
    for dil in ATT_DILATIONS:
        d = np.arange(ATT_NK, dtype=np.int32) * dil
        log_ratio = (np.log(np.maximum(d, 1).astype(np.float32) / np.float32(max_exact))
                     / np.float32(math.log(REL_MAX_DIST / max_exact)))
        large = max_exact + (log_ratio * np.float32(REL_BUCKETS - max_exact)).astype(np.int32)
        out.append(np.where(d < max_exact, d, np.minimum(large, REL_BUCKETS - 1)))
    return np.stack(out)


def _band_select():
    nk = ATT_NK
    buckets = _rel_buckets()
    onehot = np.zeros((ATT_GROUPS, REL_BUCKETS, 3 * nk), np.float32)
    for g in range(ATT_GROUPS):
        for k in range(nk, 2 * nk):
            onehot[g, buckets[g, 2 * nk - 1 - k], k] = 1.0
    mask = np.full((1, 3 * nk), -np.inf, np.float32)
    mask[0, nk:2 * nk] = 0.0
    return jnp.asarray(onehot, BF16), jnp.asarray(mask, F32)


def _slot_order(dil):
    slots = np.arange(ATT_NK)
    return np.concatenate([slots[0::2], slots[1::2]]) if dil == 1 else slots


def _slot_select():
    buckets = _rel_buckets()
    onehot = np.zeros((ATT_GROUPS, REL_BUCKETS, ATT_NK), np.float32)
    for g in range(ATT_GROUPS):
        for col, slot in enumerate(_slot_order(ATT_DILATIONS[g])):
            onehot[g, buckets[g, 0 if slot == 0 else ATT_NK - slot], col] = 1.0
    return jnp.asarray(onehot, BF16)


def _split3(x):
    hi = x.astype(BF16)
    rem = x - hi.astype(F32)
    mid = rem.astype(BF16)
    lo = (rem - mid.astype(F32)).astype(BF16)
    return hi, mid, lo


def _cast_body(*refs):
    n = len(refs) // 2
    for src, dst in zip(refs[:n], refs[n:]):
        dst[...] = src[...].astype(dst.dtype)


def _cast_ret_weights(w_in):
    rows = D_MODEL // CAST_STEPS
    half = D_MODEL // 2
    assert COL_GATE % half == 0
    widths_blocks = [(COL_ATT, 0)] + [(half, COL_GATE // half + j) for j in range(4)]
    window = lambda width, blk: pl.BlockSpec((rows, width), lambda i: (i, blk))
    return pl.pallas_call(
        _cast_body,
        grid=(CAST_STEPS,),
        in_specs=[window(wd, blk) for wd, blk in widths_blocks],
        out_specs=[window(wd, 0) for wd, _ in widths_blocks],
        out_shape=[jax.ShapeDtypeStruct((D_MODEL, wd), BF16) for wd, _ in widths_blocks],
        compiler_params=_params(1),
        name="cast_ret_weights",
    )(*([w_in] * len(widths_blocks)))


def _inproj_ret_body(x_ref, xs_ref, ln_ref, w_ref, wg0_ref, wg1_ref, wg2_ref, wg3_ref, inv_ref, *refs,
                     tm, tiles, n_steps, sample_pos):
    cast_in = refs[:3]
    outs, sample_outs = refs[3:9], refs[9:15]
    cast_out = refs[15:18]
    cos_s, sin_s = refs[18:]
    for src, dst in zip(cast_in, cast_out):
        dst[...] = src[...].astype(dst.dtype)
    i = pl.program_id(0)

    def first_half(rows):
        lane = lax.broadcasted_iota(jnp.int32, (rows, LANES), 1)
        return (lane % RET_DK) < (RET_DK // 2)

    def tables(pos):
        ang = pos.astype(F32) * inv_ref[...]
        sin = jnp.sin(ang)
        return jnp.cos(ang), jnp.where(first_half(pos.shape[0]), -sin, sin)

    def project(x, cos, sin, q_ref, k_ref, v_ref, g_ref, gr_ref, ga_ref):
        h = _rms(x, ln_ref[...]).astype(BF16)
        qk = _dot(h, w_ref[:, 0:2 * RET_QK_W])
        n_q = RET_QK_W // LANES
        fh = first_half(x.shape[0])
        for c in range(2 * n_q):
            xc = qk[:, c * LANES:(c + 1) * LANES]
            swapped = jnp.where(fh, pltpu.roll(xc, LANES - RET_DK // 2, 1), pltpu.roll(xc, RET_DK // 2, 1))
            r = xc * cos + swapped * sin
            if c < n_q:
                q_ref[:, c * LANES:(c + 1) * LANES] = r.astype(q_ref.dtype)
            else:
                k_ref[:, (c - n_q) * LANES:(c - n_q + 1) * LANES] = (r * (RET_DK ** -0.5)).astype(k_ref.dtype)
        o = 2 * RET_QK_W
        v_ref[...] = _dot(h, w_ref[:, o:o + RET_V_W]).astype(v_ref.dtype)
        o += RET_V_W
        g_ref[...] = _dot(h, w_ref[:, o:o + RET_V_W]).astype(g_ref.dtype)
        half = D_MODEL // 2
        gr_ref[:, :half] = _dot(h, wg0_ref[...]).astype(gr_ref.dtype)
        gr_ref[:, half:] = _dot(h, wg1_ref[...]).astype(gr_ref.dtype)
        ga_ref[:, :half] = _dot(h, wg2_ref[...]).astype(ga_ref.dtype)
        ga_ref[:, half:] = _dot(h, wg3_ref[...]).astype(ga_ref.dtype)

    trow = pl.ds(pl.multiple_of(lax.rem(i, tiles) * tm, tm), tm)

    @pl.when(i < tiles)
    def _():
        cos_s[trow, :], sin_s[trow, :] = tables(i * tm + lax.broadcasted_iota(jnp.int32, (tm, LANES), 0))
    project(x_ref[...], cos_s[trow, :], sin_s[trow, :], *outs)

    @pl.when(i == n_steps - 1)
    def _():
        ns = xs_ref.shape[0]
        project(xs_ref[...], *tables(jnp.full((ns, LANES), sample_pos, jnp.int32)), *sample_outs)


def _inproj_ret(x2d, xs2d, ln, w_ret, w_gates, w_in, inv_row, tm, seq, sample_pos):
    n = x2d.shape[0]
    ns = xs2d.shape[0]
    tiles = seq // tm
    n_steps = n // tm
    row = lambda width: pl.BlockSpec((tm, width), lambda i: (i, 0))
    widths = (RET_QK_W, RET_QK_W, RET_V_W, RET_V_W, D_MODEL, D_MODEL)
    slab = D_MODEL // n_steps
    assert slab % 16 == 0 and COL_ATT % ATT_W == 0
    res = pl.pallas_call(
        functools.partial(_inproj_ret_body, tm=tm, tiles=tiles, n_steps=n_steps, sample_pos=sample_pos),
        grid=(n_steps,),
        in_specs=[row(D_MODEL), _resident(xs2d.shape), _resident((1, D_MODEL)), _resident(w_ret.shape)]
        + [_resident(w.shape) for w in w_gates] + [_resident((1, LANES))]
        + [pl.BlockSpec((slab, ATT_W), lambda i, kind=kind: (i, COL_ATT // ATT_W + kind)) for kind in range(3)],
        out_specs=[row(wd) for wd in widths] + [pl.BlockSpec((ns, wd), lambda i: (0, 0)) for wd in widths]
        + [pl.BlockSpec((slab, ATT_W), lambda i: (i, 0)) for _ in range(3)],
        out_shape=[jax.ShapeDtypeStruct((n, wd), BF16) for wd in widths]
        + [jax.ShapeDtypeStruct((ns, wd), F32) for wd in widths]
        + [jax.ShapeDtypeStruct((D_MODEL, ATT_W), BF16) for _ in range(3)],
        scratch_shapes=[pltpu.VMEM((seq, LANES), F32), pltpu.VMEM((seq, LANES), F32)],
        compiler_params=_params(1),
        name="inproj_ret",
    )(x2d, xs2d, ln, w_ret, *w_gates, inv_row, w_in, w_in, w_in)
    return res[:12], res[12:]


def _inproj_att_body(x_ref, xs_ref, ln_ref, wq_ref, wk_ref, wv_ref, *refs, tm, keeps, seq, n_steps):
    lowp = refs[:3 * ATT_GROUPS]
    full = refs[3 * ATT_GROUPS:5 * ATT_GROUPS]
    sample_ref = refs[5 * ATT_GROUPS]

    @pl.when(pl.program_id(0) == n_steps - 1)
    def _():
        hs = _rms(xs_ref[...], ln_ref[...]).astype(BF16)
        for kind, w_ref in enumerate((wq_ref, wk_ref, wv_ref)):
            for g in range(ATT_GROUPS):
                c = kind * ATT_GROUPS + g
                sample_ref[:, c * ATT_GW:(c + 1) * ATT_GW] = _dot(hs, w_ref[:, g * ATT_GW:(g + 1) * ATT_GW])

    h = _rms(x_ref[...], ln_ref[...]).astype(BF16)
    for g in reversed(range(ATT_GROUPS)):
        for kind in reversed(range(3)):
            dil = ATT_DILATIONS[g]
            dst = lowp[kind * ATT_GROUPS + g]
            r = _dot(h, (wq_ref, wk_ref, wv_ref)[kind][:, g * ATT_GW:(g + 1) * ATT_GW])
            if dil == 1:
                dst[0] = r.astype(dst.dtype)
            else:
                dst[...] = jnp.swapaxes(r.reshape(tm // dil, dil, ATT_GW), 0, 1).astype(dst.dtype)
            if kind > 0:
                cache = full[(kind - 1) * ATT_GROUPS + g]
                rows = tm if keeps[g] == seq else keeps[g]
                for hh in range(ATT_HPG):
                    cache[pl.ds(hh, rows, stride=ATT_HPG), :] = r[tm - rows:, hh * ATT_HD:(hh + 1) * ATT_HD]


def _inproj_att(x2d, xs2d, ln, w_qkv, tm, seq):
    n = x2d.shape[0]
    ns = xs2d.shape[0]
    batch = n // seq
    tiles = seq // tm
    keeps = tuple(min(wd, seq) for wd in ATT_WINDOWS)
    assert all(kp <= tm or kp == seq for kp in keeps) and seq % tm == 0
    out_specs, out_shape = [], []
    for _ in range(3):
        for g in range(ATT_GROUPS):
            dil = ATT_DILATIONS[g]
            out_specs.append(pl.BlockSpec((None, dil, tm // dil, ATT_GW), lambda i: (i // tiles, 0, i % tiles, 0)))
            out_shape.append(jax.ShapeDtypeStruct((batch, dil, seq // dil, ATT_GW), BF16))
    for _ in range(2):
        for g in range(ATT_GROUPS):
            if keeps[g] == seq:
                idx = lambda i: (i // tiles, i % tiles, 0)
                rows = tm
            else:
                idx = lambda i: (i // tiles, 0, 0)
                rows = keeps[g]
            out_specs.append(pl.BlockSpec((None, rows * ATT_HPG, ATT_HD), idx))
            out_shape.append(jax.ShapeDtypeStruct((batch, keeps[g] * ATT_HPG, ATT_HD), F32))
    out_specs.append(pl.BlockSpec((ns, 3 * ATT_W), lambda i: (0, 0)))
    out_shape.append(jax.ShapeDtypeStruct((ns, 3 * ATT_W), F32))
    return pl.pallas_call(
        functools.partial(_inproj_att_body, tm=tm, keeps=keeps, seq=seq, n_steps=n // tm),
        grid=(n // tm,),
        in_specs=[pl.BlockSpec((tm, D_MODEL), lambda i: (i, 0)), _resident(xs2d.shape), _resident((1, D_MODEL))]
        + [_resident(w.shape) for w in w_qkv],
        out_specs=out_specs,
        out_shape=out_shape,
        compiler_params=_params(1),
        name="inproj_att",
    )(x2d, xs2d, ln, *w_qkv)


def _gn_swish(o, gate, gn):
    mu = jnp.mean(o, axis=-1, keepdims=True)
    d = o - mu
    var = jnp.mean(d * d, axis=-1, keepdims=True)
    on = d * lax.rsqrt(var + NORM_EPS) * gn
    return gate * _sigmoid(gate) * on


def _retention_body(q_ref, k_ref, v_ref, g_ref, gn_ref, dm_ref, qdec_ref, kdec_ref, gc_ref, *refs,
                    n_chunks, n_cast):
    cast_in = refs[:n_cast]
    out_ref, st_ref = refs[n_cast:n_cast + 2]
    cast_out = refs[n_cast + 2:2 * n_cast + 2]
    state = refs[2 * n_cast + 2]
    for src, dst in zip(cast_in, cast_out):
        dst[...] = src[...].astype(dst.dtype)
    c = RET_CHUNK

    @pl.when(pl.program_id(1) == 0)
    def _():
        state[...] = jnp.zeros_like(state)

    lane = lax.broadcasted_iota(jnp.int32, (c, 2 * RET_DK), 1)
    head0 = lane < RET_DK

    def chunk(ci, carry):
        rows = pl.ds(pl.multiple_of(ci * c, c), c)
        def independent(p):
            q2 = q_ref[rows, p * 2 * RET_DK:(p + 1) * 2 * RET_DK]
            k2 = k_ref[rows, p * 2 * RET_DK:(p + 1) * 2 * RET_DK]
            v2 = v_ref[rows, p * 2 * RET_DV:(p + 1) * 2 * RET_DV]
            pst = state[p]
            pst_lo = pst.astype(BF16)
            zero = jnp.zeros_like(q2)
            parts = []
            for hh in range(2):
                qm = jnp.where(head0 if hh == 0 else jnp.logical_not(head0), q2, zero)
                hc = slice(hh * RET_DV, (hh + 1) * RET_DV)
                parts.append((_dot_nt(qm, k2), _dot(qm, pst_lo[:, hc]), v2[:, hc]))
            kd = (k2.astype(F32) * kdec_ref[p]).astype(BF16)
            state[p] = pst * gc_ref[p] + _dot_tn(kd, v2)
            return parts

        def readout(p, parts):
            for hh, (scores, cross, values) in enumerate(parts):
                h = 2 * p + hh
                o = _dot((scores * dm_ref[h]).astype(BF16), values) + cross * qdec_ref[h]
                gate = g_ref[rows, h * RET_DV:(h + 1) * RET_DV].astype(F32)
                res = _gn_swish(o, gate, gn_ref[:, h * RET_DV:(h + 1) * RET_DV])
                out_ref[rows, h * RET_DV:(h + 1) * RET_DV] = res.astype(out_ref.dtype)

        for p in range(RET_PAIRS):
            readout(p, independent(p))
        return carry

    lax.fori_loop(0, n_chunks, chunk, 0, unroll=True)
    for p in range(RET_PAIRS):
        pst = state[p]
        for hh in range(2):
            st_ref[2 * p + hh] = pst[hh * RET_DK:(hh + 1) * RET_DK, hh * RET_DV:(hh + 1) * RET_DV]


def _retention_prompt(q, k, v, g, gn, batch, seq, ts, cast):
    n = q.shape[0]
    steps = seq // ts
    dmask, qdec, kdec, gc = _ret_tables()
    row = lambda width: pl.BlockSpec((ts, width), lambda b, s: (b * steps + s, 0))
    const = lambda a: pl.BlockSpec(a.shape, lambda b, s: (0,) * a.ndim)
    n_steps = batch * steps
    slab = lambda w: pl.BlockSpec((w.shape[0] // n_steps, w.shape[1]), lambda b, s: (b * steps + s, 0))
    assert all(w.shape[0] % (16 * n_steps) == 0 for w in cast)
    res = pl.pallas_call(
        functools.partial(_retention_body, n_chunks=ts // RET_CHUNK, n_cast=len(cast)),
        grid=(batch, steps),
        in_specs=[row(RET_QK_W), row(RET_QK_W), row(RET_V_W), row(RET_V_W), const(gn),
                  const(dmask), const(qdec), const(kdec), const(gc)] + [slab(w) for w in cast],
        out_specs=[row(RET_V_W),
                   pl.BlockSpec((None, RET_HEADS, RET_DK, RET_DV), lambda b, s: (b, 0, 0, 0))]
        + [slab(w) for w in cast],
        out_shape=[jax.ShapeDtypeStruct((n, RET_V_W), BF16),
                   jax.ShapeDtypeStruct((batch, RET_HEADS, RET_DK, RET_DV), F32)]
        + [jax.ShapeDtypeStruct(w.shape, BF16) for w in cast],
        scratch_shapes=[pltpu.VMEM((RET_PAIRS, 2 * RET_DK, 2 * RET_DV), F32)],
        compiler_params=_params(2),
        name="retention",
    )(q, k, v, g, gn, dmask, qdec, kdec, gc, *cast)
    return res[0], res[1], tuple(res[2:])


def _attn_body(q_ref, k_ref, v_ref, rb_ref, sel_ref, neg_ref, w_ref, o_ref, lse_ref, wb_ref, *scratch,
               group, dil, n_blocks):
    nk = ATT_NK
    wb_ref[...] = w_ref[...].astype(wb_ref.dtype)
    scale = ATT_HD ** -0.5
    lane = lax.broadcasted_iota(jnp.int32, (nk, LANES), 1)
    tabs = scratch[-1]

    @pl.when(pl.program_id(0) == 0)
    def _():
        band = neg_ref[...]
        for piece in _split3(rb_ref[group * ATT_HPG:(group + 1) * ATT_HPG, :]):
            band = band + _dot(piece, sel_ref[...])
        for i in range(nk):
            window = band[:, nk - 1 - i:3 * nk - 1 - i]
            for h in range(ATT_HPG):
                tabs[1, h, i:i + 1, :] = window[h:h + 1, :]
        col = lax.broadcasted_iota(jnp.int32, (nk, 2 * nk), 1)
        for h in range(ATT_HPG):
            tabs[0, h] = jnp.where(col < nk, -jnp.inf, tabs[1, h])
    if dil > 1:
        o_s, l_s = scratch[:2]

    def chain(s_raw, v_ext, sel, h):
        s = s_raw * scale + (tabs[sel, h] if n_blocks > 1 else tabs[1, h, :, nk:])
        m = jnp.max(s, axis=-1, keepdims=True)
        p = jnp.exp(s - m)
        pv = _dot(p.astype(BF16), v_ext)
        den = pv[:, ATT_HD:]
        return (pv[:, :ATT_HD] / den).astype(BF16), m + jnp.log(den)

    def write(rs, rows, outs):
        lse_tile = jnp.zeros((nk, LANES), F32)
        for h, (o, lse) in enumerate(outs):
            hc = slice(h * ATT_HD, (h + 1) * ATT_HD)
            if dil == 1:
                o_ref[rows, hc] = o
            else:
                o_s[rs, rows, hc] = o
            lse_tile = jnp.where(lane == h, lse, lse_tile)
        if dil == 1:
            lse_ref[rows, :] = lse_tile
        else:
            l_s[rs, rows, :] = lse_tile

    heads = [slice(h * ATT_HD, (h + 1) * ATT_HD) for h in range(ATT_HPG)]

    def block(rs, n, n_prev, sel):
        rows = pl.ds(pl.multiple_of(n * nk, nk), nk)
        prev_rows = pl.ds(pl.multiple_of(n_prev * nk, nk), nk)
        q = q_ref[rs, rows, :]
        k_all = jnp.concatenate([k_ref[rs, prev_rows, :], k_ref[rs, rows, :]], axis=0)
        v_all = jnp.concatenate([v_ref[rs, prev_rows, :], v_ref[rs, rows, :]], axis=0)
        ones = jnp.ones((2 * nk, ATT_HD), BF16)
        outs = []
        for h, hc in enumerate(heads):
            v_ext = jnp.concatenate([v_all[:, hc], ones], axis=1)
            outs.append(chain(_dot_nt(q[:, hc], k_all[:, hc]), v_ext, sel, h))
        write(rs, rows, outs)

    def single_blocks(streams):
        ones = jnp.ones((nk, ATT_HD), BF16)
        scores = [[_dot_nt(q_ref[rs, :, hc], k_ref[rs, :, hc]) for hc in heads] for rs in streams]
        for i, rs in enumerate(streams):
            outs = [chain(scores[i][h], jnp.concatenate([v_ref[rs, :, hc], ones], axis=1), 0, h)
                    for h, hc in enumerate(heads)]
            write(rs, slice(None), outs)

    if n_blocks == 1:
        for r0 in range(0, dil, SINGLE_BLOCK_GROUP):
            single_blocks(range(r0, min(r0 + SINGLE_BLOCK_GROUP, dil)))
    else:
        for rs in range(dil):
            def loop(n, carry, rs=rs):
                block(rs, n, jnp.maximum(n - 1, 0), jnp.minimum(n, 1))
                return carry
            lax.fori_loop(0, n_blocks, loop, 0, unroll=ATTN_UNROLL)

    if dil > 1:
        rows = o_ref.shape[0]
        for j in range(rows // PERM_ROWS):
            src = slice(j * (PERM_ROWS // dil), (j + 1) * (PERM_ROWS // dil))
            dst = slice(j * PERM_ROWS, (j + 1) * PERM_ROWS)
            o_ref[dst, :] = jnp.swapaxes(o_s[:, src, :], 0, 1).reshape(PERM_ROWS, ATT_GW).astype(BF16)
            lse_ref[dst, :] = jnp.swapaxes(l_s[:, src, :], 0, 1).reshape(PERM_ROWS, LANES)


def _attention_prompt(aq, ak, av, rel_bias_t, onehot, neg, g, cast):
    batch, dil, length, _ = aq.shape
    seq = dil * length
    n_blocks = length // ATT_NK
    blk = pl.BlockSpec((None, dil, length, ATT_GW), lambda b: (b, 0, 0, 0))
    assert cast.shape[0] % (16 * batch) == 0
    slab = pl.BlockSpec((cast.shape[0] // batch, cast.shape[1]), lambda b: (b, 0))
    scratch = [pltpu.VMEM((dil, length, ATT_GW), BF16), pltpu.VMEM((dil, length, LANES), F32)] if dil > 1 else []
    scratch.append(pltpu.VMEM((2, ATT_HPG, ATT_NK, 2 * ATT_NK), F32))
    o, lse, cast_bf16 = pl.pallas_call(
        functools.partial(_attn_body, group=g, dil=dil, n_blocks=n_blocks),
        grid=(batch,),
        in_specs=[blk, blk, blk, pl.BlockSpec(rel_bias_t.shape, lambda b: (0, 0)),
                  pl.BlockSpec((None, REL_BUCKETS, 3 * ATT_NK), lambda b: (g, 0, 0)),
                  pl.BlockSpec((1, 3 * ATT_NK), lambda b: (0, 0)), slab],
        out_specs=[pl.BlockSpec((None, seq, ATT_GW), lambda b: (b, 0, 0)),
                   pl.BlockSpec((None, seq, LANES), lambda b: (b, 0, 0)), slab],
        out_shape=[jax.ShapeDtypeStruct((batch, seq, ATT_GW), BF16),
                   jax.ShapeDtypeStruct((batch, seq, LANES), F32),
                   jax.ShapeDtypeStruct(cast.shape, BF16)],
        scratch_shapes=scratch,
        compiler_params=_params(1),
        name="attention_g%d" % g,
    )(aq, ak, av, rel_bias_t, onehot, neg, cast)
    return o.reshape(batch * seq, ATT_GW), lse.reshape(batch * seq, LANES), cast_bf16


def _tail_math(act_refs, w_refs, y_ref, combine, middle=None):
    wrb_ref, wab_ref, wo_ref, wu_ref, wd_ref, wpl_ref, wpg_ref, ln2_ref, lnf_ref = w_refs
    if combine:
        ret_ref, o0_ref, o1_ref, o2_ref, l0_ref, l1_ref, l2_ref, gr_ref, ga_ref, x_ref, ple_ref = act_refs
        lses = [l0_ref[...], l1_ref[...], l2_ref[...]]
        outs = [o0_ref, o1_ref, o2_ref]
        parts = []
        for h in range(ATT_HPG):
            lh = [l[:, h:h + 1] for l in lses]
            mx = jnp.maximum(jnp.maximum(lh[0], lh[1]), lh[2])
            e = [jnp.exp(l - mx) for l in lh]
            tot = e[0] + e[1] + e[2]
            acc = None
            for g in range(ATT_GROUPS):
                term = (e[g] / tot) * outs[g][:, h * ATT_HD:(h + 1) * ATT_HD].astype(F32)
                acc = term if acc is None else acc + term
            parts.append(acc)
        att = jnp.concatenate(parts, axis=1).astype(BF16)
    else:
        ret_ref, att_ref, gr_ref, ga_ref, x_ref, ple_ref = act_refs
        att = att_ref[...].astype(BF16)
    a = _dot(ret_ref[...].astype(BF16), wrb_ref[...])
    b = _dot(att, wab_ref[...])
    mixed = _sigmoid(gr_ref[...].astype(F32)) * a + _sigmoid(ga_ref[...].astype(F32)) * b
    x1 = x_ref[...] + _dot(mixed.astype(BF16), wo_ref[...])
    h2 = _rms(x1, ln2_ref[...]).astype(BF16)
    ff_chunk = D_MODEL
    acc = None
    for c in range(D_FF // ff_chunk):
        u = _dot(h2, wu_ref[:, c * ff_chunk:(c + 1) * ff_chunk])
        r = jnp.maximum(u, 0.0)
        t = _dot((r * r).astype(BF16), wd_ref[c * ff_chunk:(c + 1) * ff_chunk, :])
        acc = t if acc is None else acc + t
        if middle is not None and c == D_FF // ff_chunk // 2 - 1:
            middle()
    x2 = x1 + acc
    gate = _sigmoid(_dot(x2.astype(BF16), wpg_ref[...]))
    x3 = x2 + gate * _dot(ple_ref[...].astype(BF16), wpl_ref[...])
    y_ref[...] = _rms(x3, lnf_ref[...])


def _tail_shift_body(*refs, n_act, n_sample, n_w, n_steps):
    nc = 2 * ATT_GROUPS
    acts = refs[:n_act]
    sample_acts = refs[n_act:n_act + n_sample]
    base = n_act + n_sample
    ws = refs[base:base + n_w]
    new_ref = refs[base + n_w]
    base += n_w + 1
    old = refs[base:base + nc]
    y_ref, ys_ref = refs[base + nc:base + nc + 2]
    base += nc + 2
    out = refs[base:base + nc]
    stage = refs[base + nc:base + 2 * nc]
    sem_in, sem_out, sem_row = refs[base + 2 * nc:]
    s = pl.program_id(0)
    keys = [2 * g for g in range(ATT_GROUPS)]
    values = [2 * g + 1 for g in range(ATT_GROUPS)]

    def copy_in(i, seq):
        width = old[i].shape[1]
        return pltpu.make_async_copy(old[i].at[seq, pl.ds(1, width - 1)], stage[i], sem_in.at[i])

    def copy_out(i, seq):
        width = old[i].shape[1]
        return pltpu.make_async_copy(stage[i], out[i].at[seq, pl.ds(0, width - 1)], sem_out.at[i])

    def copy_row(i):
        g, kind = divmod(i, 2)
        return pltpu.make_async_copy(new_ref.at[s, kind + 1, g], out[i].at[s, old[i].shape[1] - 1], sem_row.at[i])

    @pl.when(s == 0)
    def _():
        for i in keys:
            copy_in(i, 0).start()
    for i in keys:
        copy_in(i, s).wait()
    for i in keys:
        copy_out(i, s).start()

    @pl.when(s > 0)
    def _():
        for i in values:
            copy_out(i, s - 1).wait()
    for i in values:
        copy_in(i, s).start()
    for i in range(nc):
        copy_row(i).start()

    def middle():
        for i in values:
            copy_in(i, s).wait()
        for i in values:
            copy_out(i, s).start()
        for i in keys:
            copy_out(i, s).wait()

        @pl.when(s < n_steps - 1)
        def _():
            for i in keys:
                copy_in(i, s + 1).start()

    _tail_math(acts, ws, y_ref, True, middle)

    @pl.when(s == n_steps - 1)
    def _():
        _tail_math(sample_acts, ws, ys_ref, False)

    for i in range(nc):
        copy_row(i).wait()

    @pl.when(s == n_steps - 1)
    def _():
        for i in values:
            copy_out(i, s).wait()


def _tail_shift(acts, sample_acts, weights, tm, new_qkv, caches):
    n = acts[0].shape[0]
    n_steps = n // tm
    assert n_steps == new_qkv.shape[0]
    row = lambda a: pl.BlockSpec((tm, a.shape[1]), lambda i: (i, 0))
    anyspec = pl.BlockSpec(memory_space=pl.ANY)
    nc = len(caches)
    new_rows = new_qkv.reshape(new_qkv.shape[0], 3, ATT_GROUPS, ATT_HPG, ATT_HD)
    ns = sample_acts[0].shape[0]
    res = pl.pallas_call(
        functools.partial(_tail_shift_body, n_act=len(acts), n_sample=len(sample_acts), n_w=len(weights),
                          n_steps=n_steps),
        grid=(n_steps,),
        in_specs=[row(a) for a in acts] + [_resident(a.shape) for a in sample_acts]
        + [_resident(w.shape) for w in weights] + [anyspec] * (nc + 1),
        out_specs=[pl.BlockSpec((tm, D_MODEL), lambda i: (i, 0)), pl.BlockSpec((ns, D_MODEL), lambda i: (0, 0))]
        + [anyspec] * nc,
        out_shape=[jax.ShapeDtypeStruct((n, D_MODEL), F32), jax.ShapeDtypeStruct((ns, D_MODEL), F32)]
        + [jax.ShapeDtypeStruct(c.shape, c.dtype) for c in caches],
        scratch_shapes=[pltpu.VMEM((c.shape[1] - 1, ATT_HPG, ATT_HD), F32) for c in caches]
        + [pltpu.SemaphoreType.DMA((nc,)), pltpu.SemaphoreType.DMA((nc,)), pltpu.SemaphoreType.DMA((nc,))],
        compiler_params=pltpu.CompilerParams(dimension_semantics=("arbitrary",),
                                             vmem_limit_bytes=VMEM_LIMIT_TAIL_V7X),
        name="tail",
    )(*acts, *sample_acts, *weights, new_rows, *caches)
    return res[0], res[1], res[2:]


def _decode_body(q_ref, k_ref, v_ref, g_ref, gn_ref, st_ref, gam_ref,
                 qkv_ref, rb_ref, slot_ref,
                 ck0_ref, cv0_ref, ck1_ref, cv1_ref, ck2_ref, cv2_ref, *refs, seqs, n_cast):
    cast_in = refs[:n_cast]
    ret_ref, att_ref, nst_ref = refs[n_cast:n_cast + 3]
    cast_out = refs[n_cast + 3:2 * n_cast + 3]
    slot_bias = refs[2 * n_cast + 3]
    cache_refs = (ck0_ref, cv0_ref, ck1_ref, cv1_ref, ck2_ref, cv2_ref)
    bufs = refs[2 * n_cast + 4:2 * n_cast + 4 + len(cache_refs)]
    sem = refs[2 * n_cast + 4 + len(cache_refs)]
    step = pl.program_id(0)

    def fetch(s):
        ring = lax.rem(s, DECODE_RING)
        return [pltpu.make_async_copy(c.at[pl.ds(s * seqs, seqs), :, pl.ds(0, wanted_rows(c)), :],
                                      b.at[ring, :, :, pl.ds(0, wanted_rows(c)), :], sem.at[j, ring])
                for j, (c, b) in enumerate(zip(cache_refs, bufs))]

    def wanted_rows(c):
        return 2 * ATT_HPG if c.shape[2] == 2 * ATT_HPG else ATT_HPG

    @pl.when(step == 0)
    def _():
        for c, b in zip(cache_refs, bufs):
            if wanted_rows(c) < 2 * ATT_HPG:
                b[...] = jnp.zeros(b.shape, b.dtype)
        for s in range(DECODE_RING - 1):
            for cp in fetch(s):
                cp.start()

    @pl.when(step + (DECODE_RING - 1) < pl.num_programs(0))
    def _():
        for cp in fetch(step + (DECODE_RING - 1)):
            cp.start()

    for src, dst in zip(cast_in, cast_out):
        dst[...] = src[...].astype(dst.dtype)
    sub = lax.broadcasted_iota(jnp.int32, (8, 2 * RET_DK), 0)
    lane = lax.broadcasted_iota(jnp.int32, (8, 2 * RET_DK), 1)
    row0 = sub == 0
    srow = lax.broadcasted_iota(jnp.int32, (2 * RET_DK, RET_DV), 0)
    for i in range(seqs):
        for p in range(RET_PAIRS):
            pc = slice(p * 2 * RET_DK, (p + 1) * 2 * RET_DK)
            q2 = jnp.where(row0, jnp.broadcast_to(q_ref[i, :, pc], (8, 2 * RET_DK)), 0.0)
            k2 = jnp.where(row0, jnp.broadcast_to(k_ref[i, :, pc], (8, 2 * RET_DK)), 0.0)
            pst = st_ref[i, p]
            gam = gam_ref[p]
            outer = []
            for hh in range(2):
                h = 2 * p + hh
                hc = slice(h * RET_DV, (h + 1) * RET_DV)
                hsel = (lane < RET_DK) if hh == 0 else (lane >= RET_DK)
                qm = jnp.where(hsel, q2, 0.0)
                km = jnp.where(hsel, k2, 0.0)
                vh = v_ref[i, :, hc]
                v8 = jnp.where(row0[:, :RET_DV], jnp.broadcast_to(vh, (8, RET_DV)), 0.0)
                cross = _dot(qm.astype(BF16), (pst * gam).astype(BF16))[0:1, :]
                qk = jnp.sum(qm[0:1, :] * km[0:1, :], axis=-1, keepdims=True)
                o = cross + qk * vh
                ret_ref[i, :, hc] = _gn_swish(o, g_ref[i, :, hc], gn_ref[:, hc])
                outer.append(_dot_tn(k2.astype(BF16), v8.astype(BF16)))
            nst_ref[i, p] = pst * gam + jnp.where(srow < RET_DK, outer[0], outer[1])
    scale = ATT_HD ** -0.5
    for cp in fetch(step):
        cp.wait()
    ring = lax.rem(step, DECODE_RING)
    caches = [(bufs[2 * g].at[ring], bufs[2 * g + 1].at[ring]) for g in range(ATT_GROUPS)]
    slot = lax.broadcasted_iota(jnp.int32, (ATT_NK, ATT_HD), 0)
    is_new = slot == 0
    first = row0[:, :ATT_HD]

    @pl.when(pl.program_id(0) == 0)
    def _():
        pieces = _split3(rb_ref[...])
        for g in range(ATT_GROUPS):
            rows = slice(g * ATT_HPG, (g + 1) * ATT_HPG)
            slot_bias[rows, :] = (_dot(pieces[0][rows], slot_ref[g]) + _dot(pieces[1][rows], slot_ref[g])
                                  + _dot(pieces[2][rows], slot_ref[g]))

    def one_row(x):
        return jnp.where(first, jnp.broadcast_to(x, (8, ATT_HD)), 0.0).astype(BF16)

    def head_tiles(ref, i):
        tiles = jnp.swapaxes(ref[i], 0, 1)
        if ref.shape[1] == ATT_NK:
            return [tiles[h] for h in range(ATT_HPG)]
        return [jnp.concatenate([tiles[h], tiles[ATT_HPG + h]], axis=0) for h in range(ATT_HPG)]

    for i in range(seqs):
        scores = {}
        k_tiles = [head_tiles(caches[g][0], i) for g in range(ATT_GROUPS)]
        v_tiles = [head_tiles(caches[g][1], i) for g in range(ATT_GROUPS)]
        for h in range(ATT_HPG):
            for g in range(ATT_GROUPS):
                c0 = g * ATT_GW + h * ATT_HD
                kk = jnp.where(is_new, qkv_ref[i, :, ATT_W + c0:ATT_W + c0 + ATT_HD], k_tiles[g][h])
                scores[g, h] = _dot_nt(one_row(qkv_ref[i, :, c0:c0 + ATT_HD]), kk.astype(BF16))[0:1, :]
        for h in range(ATT_HPG):
            o_g, lse_g = [], []
            for g in range(ATT_GROUPS):
                c0 = g * ATT_GW + h * ATT_HD
                vv = jnp.where(is_new, qkv_ref[i, :, 2 * ATT_W + c0:2 * ATT_W + c0 + ATT_HD], v_tiles[g][h])
                gh = g * ATT_HPG + h
                s = scores[g, h] * scale + slot_bias[gh:gh + 1, :]
                m = jnp.max(s, axis=-1, keepdims=True)
                pr = jnp.exp(s - m)
                den = jnp.sum(pr, axis=-1, keepdims=True)
                o_g.append(_dot(one_row(pr), vv.astype(BF16))[0:1, :] / den)
                lse_g.append(m + jnp.log(den))
            mx = jnp.maximum(jnp.maximum(lse_g[0], lse_g[1]), lse_g[2])
            e = [jnp.exp(l - mx) for l in lse_g]
            tot = e[0] + e[1] + e[2]
            att_ref[i, :, h * ATT_HD:(h + 1) * ATT_HD] = ((e[0] / tot) * o_g[0] + (e[1] / tot) * o_g[1]
                                                          + (e[2] / tot) * o_g[2])


def _decode(q, k, v, g, gn, state, qkv, rel_bias_t, caches, cast):
    batch = q.shape[0]
    gam = jnp.asarray(np.ascontiguousarray(np.broadcast_to(
        np.repeat(np.exp(_ret_log_decay()), RET_DK).reshape(RET_PAIRS, 2 * RET_DK, 1),
        (RET_PAIRS, 2 * RET_DK, RET_DV))), F32)
    vec = lambda a: a.reshape(batch, 1, a.shape[1])
    seqs = DECODE_SEQS
    assert batch % seqs == 0
    vspec = lambda width: pl.BlockSpec((seqs, 1, width), lambda b: (b, 0, 0))
    const = lambda a: pl.BlockSpec(a.shape, lambda b: (0,) * a.ndim)
    st_spec = pl.BlockSpec((seqs, RET_PAIRS, 2 * RET_DK, RET_DV), lambda b: (b, 0, 0, 0))
    cache_in, cache_bufs = [], []
    for gi in range(ATT_GROUPS):
        dil = ATT_DILATIONS[gi]
        for c in caches[2 * gi:2 * gi + 2]:
            tiles = ATT_NK if dil > 1 else ATT_NK // 2
            cache_in.append(c.reshape(batch, tiles, c.shape[1] * ATT_HPG // tiles, ATT_HD))
            cache_bufs.append(pltpu.VMEM((DECODE_RING, seqs, tiles, 2 * ATT_HPG, ATT_HD), F32))
    cache_specs = [pl.BlockSpec(memory_space=pl.ANY)] * len(cache_in)
    st_pairs = state.reshape(batch, RET_PAIRS, 2 * RET_DK, RET_DV)
    slot_sel = _slot_select()
    steps = batch // seqs
    assert steps >= DECODE_RING - 1
    slab = lambda w: pl.BlockSpec((w.shape[0] // steps, w.shape[1]), lambda b: (b, 0))
    assert all(w.shape[0] % (16 * steps) == 0 for w in cast)
    res = pl.pallas_call(
        functools.partial(_decode_body, seqs=seqs, n_cast=len(cast)),
        grid=(steps,),
        in_specs=[vspec(RET_QK_W), vspec(RET_QK_W), vspec(RET_V_W), vspec(RET_V_W), const(gn), st_spec, const(gam),
                  vspec(3 * ATT_W), const(rel_bias_t), const(slot_sel)] + cache_specs + [slab(w) for w in cast],
        out_specs=[vspec(RET_V_W), vspec(ATT_GW), st_spec] + [slab(w) for w in cast],
        out_shape=[jax.ShapeDtypeStruct((batch, 1, RET_V_W), F32),
                   jax.ShapeDtypeStruct((batch, 1, ATT_GW), F32),
                   jax.ShapeDtypeStruct(st_pairs.shape, F32)]
        + [jax.ShapeDtypeStruct(w.shape, BF16) for w in cast],
        scratch_shapes=[pltpu.VMEM((rel_bias_t.shape[0], ATT_NK), F32)] + cache_bufs
        + [pltpu.SemaphoreType.DMA((len(cache_in), DECODE_RING))],
        compiler_params=_params(1),
        name="decode",
    )(vec(q), vec(k), vec(v), vec(g), gn, st_pairs, gam, vec(qkv), rel_bias_t, slot_sel, *cache_in, *cast)
    ret, att, nst = res[:3]
    return ret.reshape(batch, RET_V_W), att.reshape(batch, ATT_GW), nst.reshape(state.shape), tuple(res[3:])


def kernel(x_prompt, x_sample, state_ret, cache_k_w128, cache_v_w128, cache_k_w512, cache_v_w512,
           cache_k_w2048, cache_v_w2048, p_prompt, p_sample, ln1_g, w_in, ret_gn_g, w_ret_br, w_att_br,
           w_out, ln2_g, w_up, w_down, w_ple, w_ple_gate, rel_bias, lnf_g):
    depth, _, n_in = w_in.shape
    assert depth == 1 and n_in == N_IN
    batch, seq, _ = x_prompt.shape
    dec_batch, dec_seq, _ = x_sample.shape
    assert dec_seq == 1
    l = 0
    ln1 = ln1_g[l][None, :]
    ln2 = ln2_g[l][None, :]
    lnf = lnf_g[None, :]
    gn = ret_gn_g[l][None, :]
    w_ret, *w_gates = _cast_ret_weights(w_in[l])
    inv_row = _rope_inv_row()

    caches = (cache_k_w128[l], cache_v_w128[l], cache_k_w512[l], cache_v_w512[l],
              cache_k_w2048[l], cache_v_w2048[l])

    xs = x_sample.reshape(dec_batch, D_MODEL)
    xp = x_prompt.reshape(batch * seq, D_MODEL)
    (rq, rk, rv, rg, gr, ga, sq, sk, sv, sg, sgr, sga), w_qkv = _inproj_ret(
        xp, xs, ln1, w_ret, w_gates, w_in[l], inv_row, TM_INPROJ, seq, PAST_LEN)
    att_o = _inproj_att(xp, xs, ln1, w_qkv, TM_INPROJ, seq)
    aqs, aks, avs = att_o[0:3], att_o[3:6], att_o[6:9]
    kfull, vfull = att_o[9:12], att_o[12:15]
    s_qkv = att_o[15]

    rel_bias_t = rel_bias.T
    s_ret, s_attn, new_st, (wpl, wpg) = _decode(
        sq, sk, sv, sg, gn, state_ret[l], s_qkv, rel_bias_t, caches, (w_ple[l], w_ple_gate[l]))
    ple_s = p_sample[l].reshape(dec_batch, D_PLE)

    ret_out, st_p, (wu, wd) = _retention_prompt(rq, rk, rv, rg, gn, batch, seq, TS_RETENTION, (w_up[l], w_down[l]))
    outs, lses, branch_w = [], [], []
    band_onehot, band_neg = _band_select()
    assert ATT_GROUPS == 3
    for g, w in enumerate((w_ret_br[l], w_att_br[l], w_out[l])):
        o, lse, wb = _attention_prompt(aqs[g], aks[g], avs[g], rel_bias_t, band_onehot, band_neg, g, w)
        outs.append(o)
        lses.append(lse)
        branch_w.append(wb)
    wrb, wab, wo = branch_w
    tail_w = (wrb, wab, wo, wu, wd, wpl, wpg, ln2, lnf)
    ple_p = p_prompt[l].reshape(batch * seq, D_PLE)
    tm_tail = batch * seq // dec_batch
    y_p, y_s, new_caches = _tail_shift([ret_out] + outs + lses + [gr, ga, xp, ple_p],
                                       [s_ret, s_attn, sgr, sga, xs, ple_s], tail_w, tm_tail, s_qkv, caches)
    y_prompt = y_p.reshape(batch, seq, D_MODEL)
    y_sample = y_s.reshape(dec_batch, 1, D_MODEL)
    new_state_p = st_p[None]
    kv_p = []
    for g in range(ATT_GROUPS):
        shape = (1, batch, min(ATT_WINDOWS[g], seq), ATT_HPG, ATT_HD)
        kv_p.append(kfull[g].reshape(shape))
        kv_p.append(vfull[g].reshape(shape))
    kv_s = [c[None] for c in new_caches]

    return (y_prompt, y_sample, new_state_p, *kv_p, new_st[None], *kv_s)
```
